```python
import math
import jax, jax.numpy as jnp
from jax import lax
import numpy as np

D_MODEL = 2048
BATCH = 4
SEQ = 2048
DEPTH = 1
DEC_BATCH = 128
DEC_SEQ = 8
PAST_LEN = 2048
PAGE_SIZE = 128

N_MEM = 256
FOX_HEADS = 8
FOX_HEAD_DIM = 128
FOX_WIDTH = FOX_HEADS * FOX_HEAD_DIM
Q_BLOCK = 128
SSD_HEADS = 16
SSD_HEAD_DIM = 64
SSD_WIDTH = SSD_HEADS * SSD_HEAD_DIM
SSD_GROUPS = 2
SSD_HEADS_PER_GROUP = SSD_HEADS // SSD_GROUPS
SSD_STATE = 128
SSD_CHUNK = 128
CONV_WIDTH = 4
CONV_DIM = SSD_WIDTH + 2 * SSD_GROUPS * SSD_STATE
MIX_WIDTH = FOX_WIDTH + SSD_WIDTH
XATTN_HEADS = 4
XATTN_HEAD_DIM = 128
XATTN_WIDTH = XATTN_HEADS * XATTN_HEAD_DIM
D_FF = 5632
FFN_RESIDUAL = 0.5
EPS = 1e-6
IN_SPLITS = (FOX_WIDTH, 2 * FOX_WIDTH, 3 * FOX_WIDTH, 3 * FOX_WIDTH + FOX_HEADS,
             3 * FOX_WIDTH + FOX_HEADS + SSD_WIDTH,
             3 * FOX_WIDTH + FOX_HEADS + SSD_WIDTH + CONV_DIM)
IN_PROJ_WIDTH = IN_SPLITS[-1] + SSD_HEADS

kernel_name = "fox_ssd_parallel_heads_macaron_decode_step"


def rmsnorm(x, g):
    xf = x.astype(jnp.float32)
    xf = xf * lax.rsqrt(jnp.mean(xf * xf, axis=-1, keepdims=True) + EPS)
    return (xf * g.astype(jnp.float32)).astype(x.dtype)


def swiglu(x, w_gate, w_up, w_down):
    return (jax.nn.silu(x @ w_gate) * (x @ w_up)) @ w_down


def fox_block(q, c_q, pos_q, k, v, c_k, pos_k):
    logits = jnp.einsum("blhd,bshd->bhls", q, k, preferred_element_type=jnp.float32) * (FOX_HEAD_DIM ** -0.5)
    bias = jnp.swapaxes(c_q, 1, 2)[:, :, :, None] - jnp.swapaxes(c_k, 1, 2)[:, :, None, :]
    causal = pos_k[None, :] <= pos_q[:, None]
    probs = jax.nn.softmax(jnp.where(causal, logits + bias, -jnp.inf), axis=-1)
    return jnp.einsum("bhls,bshd->blhd", probs.astype(v.dtype), v)


def fox_attention(q, c_q, pos_q, k, v, c_k, pos_k):
    b, L, h, d = q.shape
    blk = Q_BLOCK if L % Q_BLOCK == 0 else L
    nb = L // blk
    qb = jnp.swapaxes(q.reshape(b, nb, blk, h, d), 0, 1)
    cb = jnp.swapaxes(c_q.reshape(b, nb, blk, h), 0, 1)
    pb = pos_q.reshape(nb, blk)
    out = lax.map(lambda a: fox_block(a[0], a[1], a[2], k, v, c_k, pos_k), (qb, cb, pb))
    return jnp.swapaxes(out, 0, 1).reshape(b, L, h, d)


def causal_conv(xbc, buf, w, bias):
    L = xbc.shape[1]
    xp = jnp.concatenate([buf.astype(xbc.dtype), xbc], axis=1)
    y = bias
    for j in range(CONV_WIDTH):
        y = y + xp[:, j:j + L] * w[j]
    return jax.nn.silu(y), xp[:, L:]


def ssd_scan(x, dt, a_head, bm, cm, h0):
    b, L = x.shape[:2]
    chunk = SSD_CHUNK if L % SSD_CHUNK == 0 else L
    nc = L // chunk
    G, E, P, N = SSD_GROUPS, SSD_HEADS_PER_GROUP, SSD_HEAD_DIM, SSD_STATE
    xdt = x.reshape(b, nc, chunk, G, E, P) * dt.reshape(b, nc, chunk, G, E, 1)
    bm = bm.reshape(b, nc, chunk, G, N)
    cm = cm.reshape(b, nc, chunk, G, N)
    a_cum = jnp.cumsum(dt.reshape(b, nc, chunk, G, E) * a_head.reshape(G, E), axis=2)
    seg = a_cum[:, :, :, None] - a_cum[:, :, None, :]
    causal = jnp.tril(jnp.ones((chunk, chunk), dtype=bool))[:, :, None, None]
    decay = jnp.exp(jnp.where(causal, seg, -jnp.inf))
    cb = jnp.einsum("bclgn,bcsgn->bclsg", cm, bm, preferred_element_type=jnp.float32)
    y_diag = jnp.einsum("bclsg,bclsge,bcsgep->bclgep", cb, decay, xdt)
    decay_to_end = jnp.exp(a_cum[:, :, -1:] - a_cum)
    chunk_states = jnp.einsum("bclgn,bclge,bclgep->bcgepn", bm, decay_to_end, xdt)
    chunk_decay = jnp.exp(a_cum[:, :, -1])

    def step(h, inp):
        s_c, d_c = inp
        return h * d_c[..., None, None] + s_c, h

    h_last, h_prev = lax.scan(step, h0.reshape(b, G, E, P, N).astype(jnp.float32),
                              (jnp.swapaxes(chunk_states, 0, 1), jnp.swapaxes(chunk_decay, 0, 1)))
    h_prev = jnp.swapaxes(h_prev, 0, 1)
    y_off = jnp.einsum("bclgn,bcgepn,bclge->bclgep", cm, h_prev, jnp.exp(a_cum))
    return (y_diag + y_off).reshape(b, L, SSD_HEADS, P), h_last.reshape(b, SSD_HEADS, P, N)


def memory_kv(mem, p):
    b, m, _ = mem.shape
    kv = rmsnorm(mem, p["mem_norm"]) @ p["xattn_w_kv"]
    k, v = jnp.split(kv, [XATTN_WIDTH], axis=-1)
    k = rmsnorm(k.reshape(b, m, XATTN_HEADS, XATTN_HEAD_DIM), p["xattn_k_norm"])
    return k, v.reshape(b, m, XATTN_HEADS, XATTN_HEAD_DIM)


def cross_attention(u, mem_k, mem_v, p):
    b, L, _ = u.shape
    q = rmsnorm((u @ p["xattn_w_q"]).reshape(b, L, XATTN_HEADS, XATTN_HEAD_DIM), p["xattn_q_norm"])
    logits = jnp.einsum("blhd,bmhd->bhlm", q, mem_k, preferred_element_type=jnp.float32) * (XATTN_HEAD_DIM ** -0.5)
    probs = jax.nn.softmax(logits, axis=-1)
    o = jnp.einsum("bhlm,bmhd->blhd", probs.astype(mem_v.dtype), mem_v).reshape(b, L, XATTN_WIDTH)
    return o @ p["xattn_w_o"]


def hybrid_layer(x, p, fox_past, ssm_h0, conv_buf, mem_k, mem_v):
    b, L, _ = x.shape
    x = x + FFN_RESIDUAL * swiglu(rmsnorm(x, p["ffn1_norm"]), p["ffn1_w_gate"], p["ffn1_w_up"], p["ffn1_w_down"])
    u = rmsnorm(x, p["mix_norm"])
    q, k, v, f_logit, z, xbc, dt_raw = jnp.split(u @ p["w_in"], IN_SPLITS, axis=-1)
    q = rmsnorm(q.reshape(b, L, FOX_HEADS, FOX_HEAD_DIM), p["fox_q_norm"])
    k = rmsnorm(k.reshape(b, L, FOX_HEADS, FOX_HEAD_DIM), p["fox_k_norm"])
    v = v.reshape(b, L, FOX_HEADS, FOX_HEAD_DIM)
    logf = jax.nn.log_sigmoid(f_logit.astype(jnp.float32) + p["fox_b_f"].astype(jnp.float32))
    if fox_past is None:
        past = 0
        k_all, v_all, logf_all = k, v, logf
    else:
        k_past, v_past, logf_past = fox_past
        past = k_past.shape[1]
        k_all = jnp.concatenate([k_past.astype(k.dtype), k], axis=1)
        v_all = jnp.concatenate([v_past.astype(v.dtype), v], axis=1)
        logf_all = jnp.concatenate([logf_past.astype(jnp.float32), logf], axis=1)
    c_all = jnp.cumsum(logf_all, axis=1)
    pos_k = jnp.arange(past + L)
    fox_out = fox_attention(q, c_all[:, past:], pos_k[past:], k_all, v_all, c_all, pos_k)
    xbc, conv_new = causal_conv(xbc, conv_buf, p["conv_w"], p["conv_b"])
    xs, bm, cm = jnp.split(xbc, [SSD_WIDTH, SSD_WIDTH + SSD_GROUPS * SSD_STATE], axis=-1)
    xs = xs.reshape(b, L, SSD_HEADS, SSD_HEAD_DIM)
    bm = bm.reshape(b, L, SSD_GROUPS, SSD_STATE)
    cm = cm.reshape(b, L, SSD_GROUPS, SSD_STATE)
    dt = jax.nn.softplus(dt_raw.astype(jnp.float32) + p["ssd_dt_bias"].astype(jnp.float32))
    a_head = -jnp.exp(p["ssd_A_log"].astype(jnp.float32))
    y_ssd, h_new = ssd_scan(xs, dt, a_head, bm, cm, ssm_h0)
    y_ssd = y_ssd + xs * p["ssd_D"][:, None]
    y_ssd = rmsnorm(y_ssd.reshape(b, L, SSD_WIDTH) * jax.nn.silu(z), p["ssd_out_norm"])
    mixed = jnp.concatenate([fox_out.reshape(b, L, FOX_WIDTH), y_ssd.astype(fox_out.dtype)], axis=-1) @ p["w_out"]
    x = x + mixed
    x = x + cross_attention(rmsnorm(x, p["xattn_norm"]), mem_k, mem_v, p)
    x = x + FFN_RESIDUAL * swiglu(rmsnorm(x, p["ffn2_norm"]), p["ffn2_w_gate"], p["ffn2_w_up"], p["ffn2_w_down"])
    return x, (k, v, logf, h_new, conv_new)


def setup_inputs(seed: int = 0) -> dict:
    key = jax.random.key(seed)
    ks = iter(jax.random.split(key, 64))
    f32 = jnp.float32

    def nrm(shape, scale=1.0):
        return scale * jax.random.normal(next(ks), shape, f32)

    def gain(shape):
        return 1.0 + 0.05 * jax.random.normal(next(ks), shape, f32)

    n_pages = PAST_LEN // PAGE_SIZE
    n_used = DEC_BATCH * n_pages
    n_pool = n_used + n_used // 4
    page_table = jax.random.permutation(next(ks), n_pool)[:n_used].reshape(DEC_BATCH, n_pages).astype(jnp.int32)
    dt_init = jnp.exp(jax.random.uniform(next(ks), (DEPTH, SSD_HEADS), f32, math.log(1e-3), math.log(1e-1)))
    a_init = jax.random.uniform(next(ks), (DEPTH, SSD_HEADS), f32, 1.0, 16.0)
    return {
        "x_prompt": nrm((BATCH, SEQ, D_MODEL)),
        "x_sample": nrm((DEC_BATCH, DEC_SEQ, D_MODEL)),
        "cache_fox_k": nrm((DEPTH, n_pool, PAGE_SIZE, FOX_HEADS, FOX_HEAD_DIM)),
        "cache_fox_v": nrm((DEPTH, n_pool, PAGE_SIZE, FOX_HEADS, FOX_HEAD_DIM)),
        "cache_fox_logf": jax.nn.log_sigmoid(3.0 + nrm((DEPTH, n_pool, PAGE_SIZE, FOX_HEADS), 0.5)),
        "cache_mem_k": nrm((DEPTH, DEC_BATCH, N_MEM, XATTN_HEADS, XATTN_HEAD_DIM)),
        "cache_mem_v": nrm((DEPTH, DEC_BATCH, N_MEM, XATTN_HEADS, XATTN_HEAD_DIM)),
        "state_ssm": nrm((DEPTH, DEC_BATCH, SSD_HEADS, SSD_HEAD_DIM, SSD_STATE), 0.1),
        "state_conv": nrm((DEPTH, DEC_BATCH, CONV_WIDTH - 1, CONV_DIM)),
        "page_table": page_table,
        "mem_prompt": nrm((BATCH, N_MEM, D_MODEL)),
        "ffn1_norm": gain((DEPTH, D_MODEL)),
        "ffn1_w_gate": nrm((DEPTH, D_MODEL, D_FF), D_MODEL ** -0.5),
        "ffn1_w_up": nrm((DEPTH, D_MODEL, D_FF), D_MODEL ** -0.5),
        "ffn1_w_down": nrm((DEPTH, D_FF, D_MODEL), D_FF ** -0.5),
        "mix_norm": gain((DEPTH, D_MODEL)),
        "w_in": nrm((DEPTH, D_MODEL, IN_PROJ_WIDTH), D_MODEL ** -0.5),
        "fox_b_f": 3.0 + nrm((DEPTH, FOX_HEADS), 0.5),
        "fox_q_norm": gain((DEPTH, FOX_HEAD_DIM)),
        "fox_k_norm": gain((DEPTH, FOX_HEAD_DIM)),
        "conv_w": nrm((DEPTH, CONV_WIDTH, CONV_DIM), CONV_WIDTH ** -0.5),
        "conv_b": nrm((DEPTH, CONV_DIM), 0.02),
        "ssd_dt_bias": dt_init + jnp.log(-jnp.expm1(-dt_init)),
        "ssd_A_log": jnp.log(a_init),
        "ssd_D": gain((DEPTH, SSD_HEADS)),
        "ssd_out_norm": gain((DEPTH, SSD_WIDTH)),
        "w_out": nrm((DEPTH, MIX_WIDTH, D_MODEL), MIX_WIDTH ** -0.5),
        "xattn_norm": gain((DEPTH, D_MODEL)),
        "mem_norm": gain((DEPTH, D_MODEL)),
        "xattn_w_q": nrm((DEPTH, D_MODEL, XATTN_WIDTH), D_MODEL ** -0.5),
        "xattn_w_kv": nrm((DEPTH, D_MODEL, 2 * XATTN_WIDTH), D_MODEL ** -0.5),
        "xattn_q_norm": gain((DEPTH, XATTN_HEAD_DIM)),
        "xattn_k_norm": gain((DEPTH, XATTN_HEAD_DIM)),
        "xattn_w_o": nrm((DEPTH, XATTN_WIDTH, D_MODEL), XATTN_WIDTH ** -0.5),
        "ffn2_norm": gain((DEPTH, D_MODEL)),
        "ffn2_w_gate": nrm((DEPTH, D_MODEL, D_FF), D_MODEL ** -0.5),
        "ffn2_w_up": nrm((DEPTH, D_MODEL, D_FF), D_MODEL ** -0.5),
        "ffn2_w_down": nrm((DEPTH, D_FF, D_MODEL), D_FF ** -0.5),
    }


def reference(x_prompt, x_sample, cache_fox_k, cache_fox_v, cache_fox_logf, cache_mem_k, cache_mem_v,
              state_ssm, state_conv, page_table, mem_prompt,
              ffn1_norm, ffn1_w_gate, ffn1_w_up, ffn1_w_down, mix_norm, w_in, fox_b_f, fox_q_norm, fox_k_norm,
              conv_w, conv_b, ssd_dt_bias, ssd_A_log, ssd_D, ssd_out_norm, w_out,
              xattn_norm, mem_norm, xattn_w_q, xattn_w_kv, xattn_q_norm, xattn_k_norm, xattn_w_o,
              ffn2_norm, ffn2_w_gate, ffn2_w_up, ffn2_w_down):
    bp = x_prompt.shape[0]
    bs = x_sample.shape[0]
    past_len = page_table.shape[1] * PAGE_SIZE
    hp, hs = x_prompt, x_sample
    p_list, s_list, pm_list = [], [], []
    for l in range(DEPTH):
        p = {
            "ffn1_norm": ffn1_norm[l], "ffn1_w_gate": ffn1_w_gate[l], "ffn1_w_up": ffn1_w_up[l],
            "ffn1_w_down": ffn1_w_down[l], "mix_norm": mix_norm[l], "w_in": w_in[l], "fox_b_f": fox_b_f[l],
            "fox_q_norm": fox_q_norm[l], "fox_k_norm": fox_k_norm[l], "conv_w": conv_w[l], "conv_b": conv_b[l],
            "ssd_dt_bias": ssd_dt_bias[l], "ssd_A_log": ssd_A_log[l], "ssd_D": ssd_D[l],
            "ssd_out_norm": ssd_out_norm[l], "w_out": w_out[l], "xattn_norm": xattn_norm[l],
            "mem_norm": mem_norm[l], "xattn_w_q": xattn_w_q[l], "xattn_w_kv": xattn_w_kv[l],
            "xattn_q_norm": xattn_q_norm[l], "xattn_k_norm": xattn_k_norm[l], "xattn_w_o": xattn_w_o[l],
            "ffn2_norm": ffn2_norm[l], "ffn2_w_gate": ffn2_w_gate[l], "ffn2_w_up": ffn2_w_up[l],
            "ffn2_w_down": ffn2_w_down[l],
        }
        mk, mv = memory_kv(mem_prompt, p)
        h0 = jnp.zeros((bp, SSD_HEADS, SSD_HEAD_DIM, SSD_STATE), jnp.float32)
        cb0 = jnp.zeros((bp, CONV_WIDTH - 1, CONV_DIM), x_prompt.dtype)
        hp, st_p = hybrid_layer(hp, p, None, h0, cb0, mk, mv)
        p_list.append(st_p)
        pm_list.append((mk, mv))
        k_past = cache_fox_k[l][page_table].reshape(bs, past_len, FOX_HEADS, FOX_HEAD_DIM)
        v_past = cache_fox_v[l][page_table].reshape(bs, past_len, FOX_HEADS, FOX_HEAD_DIM)
        lf_past = cache_fox_logf[l][page_table].reshape(bs, past_len, FOX_HEADS)
        hs, st_s = hybrid_layer(hs, p, (k_past, v_past, lf_past), state_ssm[l], state_conv[l],
                                cache_mem_k[l], cache_mem_v[l])
        s_list.append(st_s)
    p_fox_k = jnp.stack([s[0] for s in p_list])
    p_fox_v = jnp.stack([s[1] for s in p_list])
    p_fox_logf = jnp.stack([s[2] for s in p_list])
    p_ssm = jnp.stack([s[3] for s in p_list])
    p_conv = jnp.stack([s[4] for s in p_list])
    p_mem_k = jnp.stack([m[0] for m in pm_list])
    p_mem_v = jnp.stack([m[1] for m in pm_list])
    s_fox_k = jnp.stack([s[0] for s in s_list])
    s_fox_v = jnp.stack([s[1] for s in s_list])
    s_fox_logf = jnp.stack([s[2] for s in s_list])
    s_ssm = jnp.stack([s[3] for s in s_list])
    s_conv = jnp.stack([s[4] for s in s_list])
    return (hp, hs, p_fox_k, p_fox_v, p_fox_logf, p_ssm, p_conv, p_mem_k, p_mem_v,
            s_fox_k, s_fox_v, s_fox_logf, s_ssm, s_conv)
```

```python
import functools
import math

import jax
import jax.numpy as jnp
from jax import lax
from jax.experimental import pallas as pl
from jax.experimental.pallas import tpu as pltpu

F32 = jnp.float32
BF16 = jnp.bfloat16

EPS = 1e-6
FFN_RESIDUAL = 0.5
D_MODEL = 2048
D_FF = 5632
PAGE_SIZE = 128
FOX_HEADS = 8
HEAD_DIM = 128
FOX_WIDTH = FOX_HEADS * HEAD_DIM
SSD_HEADS = 16
SSD_HEAD_DIM = 64
SSD_WIDTH = SSD_HEADS * SSD_HEAD_DIM
SSD_GROUPS = 2
SSD_STATE = 128
SSD_CHUNK = 128
CONV_WIDTH = 4
CONV_DIM = SSD_WIDTH + 2 * SSD_GROUPS * SSD_STATE
XATTN_HEADS = 4
XATTN_WIDTH = XATTN_HEADS * HEAD_DIM
LANES = 128
VMEM_LIMIT_BYTES = 56 * 1024 * 1024


def _params(*semantics):
    return pltpu.CompilerParams(dimension_semantics=semantics, vmem_limit_bytes=VMEM_LIMIT_BYTES)


def _rms(x, g):
    return x * lax.rsqrt(jnp.mean(x * x, axis=-1, keepdims=True) + EPS) * g


def _dot(a, b):
    return jnp.dot(a, b, preferred_element_type=F32)


def _dot_nt(a, b):
    return lax.dot_general(a, b, (((1,), (1,)), ((), ())), preferred_element_type=F32)


def _dot_tn(a, b):
    return lax.dot_general(a, b, (((0,), (0,)), ((), ())), preferred_element_type=F32)


def _split3(x):
    x1 = x.astype(BF16)
    r = x - x1.astype(F32)
    x2 = r.astype(BF16)
    x3 = (r - x2.astype(F32)).astype(BF16)
    return x1, x2, x3


def _dot_sel_l(sel, x):
    x1, x2, x3 = _split3(x)
    return _dot(sel, x1) + _dot(sel, x2) + _dot(sel, x3)


def _dot_sel_r(x, sel):
    x1, x2, x3 = _split3(x)
    return _dot(x1, sel) + _dot(x2, sel) + _dot(x3, sel)


def _silu(x):
    return x * jax.nn.sigmoid(x)


def _ffn_body(pre_proj, *refs):
    if pre_proj:
        x_ref, a_ref, wo_ref, g_ref, wg_ref, wu_ref, wd_ref, o_ref, xn_ref = refs
    else:
        x_ref, g_ref, wg_ref, wu_ref, wd_ref, o_ref, xn_ref = refs

    @pl.when(pl.program_id(1) == 0)
    def _():
        x = x_ref[...]
        if pre_proj:
            x = x + _dot(a_ref[...].astype(BF16), wo_ref[...])
        xn_ref[...] = _rms(x, g_ref[...]).astype(BF16)
        o_ref[...] = x

    xn = xn_ref[...]
    gate = _dot(xn, wg_ref[...])
    up = _dot(xn, wu_ref[...])
    h = (_silu(gate) * up * FFN_RESIDUAL).astype(BF16)
    o_ref[...] += _dot(h, wd_ref[...])


def _ffn(x, g, wg, wu, wd, pre=None, *, tm=512, tf=512):
    t, d = x.shape
    f = wg.shape[1]
    tm = min(tm, t)
    grid = (t // tm, f // tf)
    row = lambda i, j: (i, 0)
    in_specs = [pl.BlockSpec((tm, d), row)]
    args = [x]
    if pre is not None:
        a, wo = pre
        in_specs += [pl.BlockSpec((tm, a.shape[1]), row), pl.BlockSpec(wo.shape, lambda i, j: (0, 0))]
        args += [a, wo]
    in_specs += [
        pl.BlockSpec((1, d), lambda i, j: (0, 0)),
        pl.BlockSpec((d, tf), lambda i, j: (0, j)),
        pl.BlockSpec((d, tf), lambda i, j: (0, j)),
        pl.BlockSpec((tf, d), lambda i, j: (j, 0)),
    ]
    args += [g.reshape(1, d), wg, wu, wd]
    return pl.pallas_call(
        functools.partial(_ffn_body, pre is not None),
        grid=grid,
        in_specs=in_specs,
        out_specs=pl.BlockSpec((tm, d), row),
        out_shape=jax.ShapeDtypeStruct((t, d), F32),
        scratch_shapes=[pltpu.VMEM((tm, d), BF16)],
        compiler_params=_params("parallel", "arbitrary"),
        name="ffn_pre" if pre is not None else "ffn",
    )(*args)


IN_TN = 512
IN_BLOCKS = {"q": (0, 2), "k": (2, 2), "v": (4, 2), "z": (6, 2), "xbc": (8, 3)}
IN_NBLK = 11


def _head_norm(y, g):
    outs = []
    for c in range(y.shape[1] // HEAD_DIM):
        yc = y[:, c * HEAD_DIM:(c + 1) * HEAD_DIM]
        outs.append(yc * lax.rsqrt(jnp.mean(yc * yc, axis=-1, keepdims=True) + EPS) * g)
    return jnp.concatenate(outs, axis=1)


def _in_proj_body(x_ref, g_ref, w_ref, ws_ref, bs_ref, gq_ref, gk_ref,
                  q_ref, k_ref, kb_ref, v_ref, vb_ref, z_ref, xbc_ref, logf_ref, dt_ref, u_ref):
    j = pl.program_id(1)

    @pl.when(j == 0)
    def _():
        u = _rms(x_ref[...], g_ref[...]).astype(BF16)
        u_ref[...] = u
        s = _dot(u, ws_ref[...]) + bs_ref[...]
        t = jnp.log1p(jnp.exp(-jnp.abs(s)))
        logf_ref[...] = (jnp.minimum(s, 0.0) - t)[:, :LANES]
        dt_ref[...] = (jnp.maximum(s, 0.0) + t)[:, LANES:]

    y = _dot(u_ref[...], w_ref[...])

    def in_range(name):
        lo, n = IN_BLOCKS[name]
        return jnp.logical_and(j >= lo, j < lo + n)

    @pl.when(in_range("q"))
    def _():
        q_ref[...] = (_head_norm(y, gq_ref[...]) * (HEAD_DIM ** -0.5)).astype(q_ref.dtype)

    @pl.when(in_range("k"))
    def _():
        kn = _head_norm(y, gk_ref[...])
        k_ref[...] = kn
        kb_ref[...] = kn.astype(BF16)

    @pl.when(in_range("v"))
    def _():
        v_ref[...] = y
        vb_ref[...] = y.astype(BF16)

    @pl.when(in_range("z"))
    def _():
        z_ref[...] = y

    @pl.when(in_range("xbc"))
    def _():
        xbc_ref[...] = y


def _in_proj(x, g, w_main, w_small, b_small, gq, gk, *, q_dtype=BF16, tm=512):
    t, d = x.shape
    tm = min(tm, t)
    tn = IN_TN

    def col(name):
        lo, n = IN_BLOCKS[name]
        return pl.BlockSpec((tm, tn), lambda i, j: (i, jnp.clip(j - lo, 0, n - 1)))

    const = lambda i, j: (0, 0)
    slab = pl.BlockSpec((tm, LANES), lambda i, j: (i, 0))
    outs = [
        ("q", q_dtype, FOX_WIDTH), ("k", F32, FOX_WIDTH), ("k", BF16, FOX_WIDTH), ("v", F32, FOX_WIDTH),
        ("v", BF16, FOX_WIDTH), ("z", F32, SSD_WIDTH), ("xbc", F32, CONV_DIM),
    ]
    return pl.pallas_call(
        _in_proj_body,
        grid=(t // tm, IN_NBLK),
        in_specs=[
            pl.BlockSpec((tm, d), lambda i, j: (i, 0)),
            pl.BlockSpec((1, d), const),
            pl.BlockSpec((d, tn), lambda i, j: (0, j)),
            pl.BlockSpec((d, 2 * LANES), const),
            pl.BlockSpec((1, 2 * LANES), const),
            pl.BlockSpec((1, HEAD_DIM), const),
            pl.BlockSpec((1, HEAD_DIM), const),
        ],
        out_specs=[col(n) for n, _, _ in outs] + [slab, slab],
        out_shape=[jax.ShapeDtypeStruct((t, w), dt) for _, dt, w in outs]
        + [jax.ShapeDtypeStruct((t, LANES), F32)] * 2,
        scratch_shapes=[pltpu.VMEM((tm, d), BF16)],
        compiler_params=_params("parallel", "arbitrary"),
        name="in_proj",
    )(x, g.reshape(1, d), w_main, w_small, b_small, gq.reshape(1, HEAD_DIM), gk.reshape(1, HEAD_DIM))


def _pack_in_proj(w_in, fox_b_f, ssd_dt_bias):
    fw = FOX_WIDTH
    f0 = 3 * fw
    z0 = f0 + FOX_HEADS
    x0 = z0 + SSD_WIDTH
    d0 = x0 + CONV_DIM
    w_main = jnp.concatenate([w_in[:, :f0], w_in[:, z0:d0]], axis=1).astype(BF16)
    zeros = lambda n: jnp.zeros((w_in.shape[0], n), w_in.dtype)
    w_small = jnp.concatenate(
        [w_in[:, f0:z0], zeros(LANES - FOX_HEADS), w_in[:, d0:], zeros(LANES - SSD_HEADS)], axis=1).astype(BF16)
    b_small = jnp.concatenate(
        [fox_b_f, jnp.zeros((LANES - FOX_HEADS,), F32), ssd_dt_bias, jnp.zeros((LANES - SSD_HEADS,), F32)]
    ).reshape(1, 2 * LANES)
    return w_main, w_small, b_small


ATT_BLOCK = 256


def _tri(n, *, strict=False, upper=False):
    r = lax.broadcasted_iota(jnp.int32, (n, n), 0)
    c = lax.broadcasted_iota(jnp.int32, (n, n), 1)
    if upper:
        r, c = c, r
    return (c < r) if strict else (c <= r)


def _cumsum_body(x_ref, col_ref, row_ref, carry_ref):
    @pl.when(pl.program_id(1) == 0)
    def _():
        carry_ref[...] = jnp.zeros_like(carry_ref)

    n = x_ref.shape[1]
    tril = _tri(n).astype(BF16)
    c = _dot_sel_l(tril, x_ref[0]) + carry_ref[...]
    carry_ref[...] = c[n - 1:n, :]
    col_ref[0] = c
    row_ref[0] = c.T[:FOX_HEADS, :]


def _cumsum(x, *, tb=ATT_BLOCK):
    b, l, _ = x.shape
    return pl.pallas_call(
        _cumsum_body,
        grid=(b, l // tb),
        in_specs=[pl.BlockSpec((1, tb, LANES), lambda i, j: (i, j, 0))],
        out_specs=[pl.BlockSpec((1, tb, LANES), lambda i, j: (i, j, 0)),
                   pl.BlockSpec((1, FOX_HEADS, tb), lambda i, j: (i, 0, j))],
        out_shape=[jax.ShapeDtypeStruct((b, l, LANES), F32), jax.ShapeDtypeStruct((b, FOX_HEADS, l), F32)],
        scratch_shapes=[pltpu.VMEM((1, LANES), F32)],
        compiler_params=_params("parallel", "arbitrary"),
        name="logf_cumsum",
    )(x)


def _fox_prompt_body(q_ref, k_ref, v_ref, ccol_ref, crow_ref, o_ref):
    i = pl.program_id(1)
    tq = q_ref.shape[0]
    tk = tq
    row = lax.broadcasted_iota(jnp.int32, (tq, tk), 0)
    col = lax.broadcasted_iota(jnp.int32, (tq, tk), 1)
    for h in range(FOX_HEADS):
        hs = slice(h * HEAD_DIM, (h + 1) * HEAD_DIM)
        qh = q_ref[:, hs]
        cq = ccol_ref[0, :, h:h + 1]

        def step(j, carry):
            m, l, acc = carry
            ks = pl.ds(pl.multiple_of(j * tk, tk), tk)
            s = _dot_nt(qh, k_ref[ks, hs]) + (cq - crow_ref[0, h, pl.ds(j, 1), :])
            s = jnp.where(col + (j - i) * tk <= row, s, -jnp.inf)
            m_new = jnp.maximum(m, jnp.max(s, axis=-1, keepdims=True))
            alpha = jnp.exp(m - m_new)
            p = jnp.exp(s - m_new)
            l = alpha * l + jnp.sum(p, axis=-1, keepdims=True)
            acc = alpha * acc + _dot(p.astype(BF16), v_ref[ks, hs])
            return m_new, l, acc

        init = (jnp.full((tq, 1), -jnp.inf, F32), jnp.zeros((tq, 1), F32), jnp.zeros((tq, HEAD_DIM), F32))
        _, l, acc = lax.fori_loop(0, i + 1, step, init)
        o_ref[:, hs] = (acc / l).astype(o_ref.dtype)


def _fox_prompt(q, k, v, ccol, crow, b, l):
    tq = min(ATT_BLOCK, l)
    nq = l // tq
    w = FOX_WIDTH
    return pl.pallas_call(
        _fox_prompt_body,
        grid=(b, nq),
        in_specs=[
            pl.BlockSpec((tq, w), lambda bi, i: (bi * nq + i, 0)),
            pl.BlockSpec((l, w), lambda bi, i: (bi, 0)),
            pl.BlockSpec((l, w), lambda bi, i: (bi, 0)),
            pl.BlockSpec((1, tq, LANES), lambda bi, i: (bi, i, 0)),
            pl.BlockSpec((1, FOX_HEADS, nq, tq), lambda bi, i: (bi, 0, 0, 0)),
        ],
        out_specs=pl.BlockSpec((tq, w), lambda bi, i: (bi * nq + i, 0)),
        out_shape=jax.ShapeDtypeStruct((b * l, w), BF16),
        compiler_params=_params("parallel", "arbitrary"),
        name="fox_prompt",
    )(q, k, v, ccol, crow)


def _page_cumsum_body(x_ref, w_ref, m_ref):
    n = PAGE_SIZE * FOX_HEADS

    @pl.when(pl.program_id(0) == 0)
    def _():
        r = lax.broadcasted_iota(jnp.int32, (n, n), 0)
        c = lax.broadcasted_iota(jnp.int32, (n, n), 1)
        same_head = (r & (FOX_HEADS - 1)) == (c & (FOX_HEADS - 1))
        earlier = lax.shift_right_logical(r, 3) <= lax.shift_right_logical(c, 3)
        m_ref[...] = jnp.logical_and(same_head, earlier).astype(BF16)

    w_ref[...] = _dot_sel_r(x_ref[...], m_ref[...])


def _page_cumsum(logf_pages, *, tb=256):
    n_pool, n = logf_pages.shape
    return pl.pallas_call(
        _page_cumsum_body,
        grid=(n_pool // tb,),
        in_specs=[pl.BlockSpec((tb, n), lambda i: (i, 0))],
        out_specs=pl.BlockSpec((tb, n), lambda i: (i, 0)),
        out_shape=jax.ShapeDtypeStruct((n_pool, n), F32),
        scratch_shapes=[pltpu.VMEM((n, n), BF16)],
        compiler_params=_params("arbitrary"),
        name="page_cumsum",
    )(logf_pages)


def _seq_cumsum_body(seq_len, x_ref, o_ref):
    n = x_ref.shape[0]
    r = lax.broadcasted_iota(jnp.int32, (n, n), 0)
    c = lax.broadcasted_iota(jnp.int32, (n, n), 1)
    same_seq = (r // seq_len) == (c // seq_len)
    o_ref[...] = _dot_sel_l(jnp.logical_and(same_seq, c <= r).astype(BF16), x_ref[...])


def _seq_cumsum(x, seq_len, *, tb=128):
    t = x.shape[0]
    return pl.pallas_call(
        functools.partial(_seq_cumsum_body, seq_len),
        grid=(t // tb,),
        in_specs=[pl.BlockSpec((tb, LANES), lambda i: (i, 0))],
        out_specs=pl.BlockSpec((tb, LANES), lambda i: (i, 0)),
        out_shape=jax.ShapeDtypeStruct((t, LANES), F32),
        compiler_params=_params("parallel"),
        name="seq_cumsum",
    )(x)


def _fox_sample_body(pt_ref, q_ref, kn_ref, vn_ref, cn_ref, kp_ref, vp_ref, w_ref, o_ref,
                     q_scr, colq_ref, toff_ref, m_ref, l_ref, acc_ref):
    j = pl.program_id(1)
    nq = q_ref.shape[0]
    rows = nq * FOX_HEADS
    page_keys = PAGE_SIZE * FOX_HEADS
    row_id = lax.broadcasted_iota(jnp.int32, (rows, 1), 0)
    head_of_row = row_id & (FOX_HEADS - 1)
    query_of_row = lax.shift_right_logical(row_id, 3)

    @pl.when(j == 0)
    def _():
        q = q_ref[...].reshape(rows, HEAD_DIM).astype(BF16)
        q_scr[...] = q
        cn = cn_ref[0]
        key = lax.broadcasted_iota(jnp.int32, (1, rows), 1)
        colq = jnp.sum(jnp.where(key == row_id, cn, 0.0), axis=-1, keepdims=True)
        colq_ref[...] = colq
        toff_ref[...] = jnp.zeros_like(toff_ref)
        s = _dot_nt(q, kn_ref[...].reshape(rows, HEAD_DIM).astype(BF16)) + colq - cn
        valid = jnp.logical_and((key & (FOX_HEADS - 1)) == head_of_row,
                                lax.shift_right_logical(key, 3) <= query_of_row)
        s = jnp.where(valid, s, -jnp.inf)
        m = jnp.max(s, axis=-1, keepdims=True)
        p = jnp.exp(s - m)
        m_ref[...] = m
        l_ref[...] = jnp.sum(p, axis=-1, keepdims=True)
        acc_ref[...] = _dot(p.astype(BF16), vn_ref[...].reshape(rows, HEAD_DIM).astype(BF16))

    w = w_ref[0]
    lane = lax.broadcasted_iota(jnp.int32, (1, LANES), 1)
    last = jnp.where(lane == LANES - FOX_HEADS + head_of_row, w[:, page_keys - LANES:], 0.0)
    toff = toff_ref[...] + jnp.sum(last, axis=-1, keepdims=True)
    toff_ref[...] = toff
    s = _dot_nt(q_scr[...], kp_ref[0].reshape(page_keys, HEAD_DIM).astype(BF16))
    s = s + (colq_ref[...] + toff) - w
    key = lax.broadcasted_iota(jnp.int32, (1, page_keys), 1)
    s = jnp.where((key & (FOX_HEADS - 1)) == head_of_row, s, -jnp.inf)
    m_old = m_ref[...]
    m = jnp.maximum(m_old, jnp.max(s, axis=-1, keepdims=True))
    alpha = jnp.exp(m_old - m)
    p = jnp.exp(s - m)
    m_ref[...] = m
    l_ref[...] = alpha * l_ref[...] + jnp.sum(p, axis=-1, keepdims=True)
    acc_ref[...] = alpha * acc_ref[...] + _dot(p.astype(BF16), vp_ref[0].reshape(page_keys, HEAD_DIM).astype(BF16))

    @pl.when(j == pl.num_programs(1) - 1)
    def _():
        o_ref[...] = (acc_ref[...] / l_ref[...]).reshape(nq, FOX_HEADS, HEAD_DIM)


def _fox_sample(page_table, q, k_new, v_new, c_new, k_pages, v_pages, w_pages, n_seq, nq):
    n_pages = page_table.shape[1]
    rows = nq * FOX_HEADS

    def page4(b, j, pt):
        return (pt[b * n_pages + (n_pages - 1 - j)], 0, 0, 0)

    def page3(b, j, pt):
        return (pt[b * n_pages + (n_pages - 1 - j)], 0, 0)

    seq = pl.BlockSpec((nq, FOX_HEADS, HEAD_DIM), lambda b, j, pt: (b, 0, 0))
    grid_spec = pltpu.PrefetchScalarGridSpec(
        num_scalar_prefetch=1,
        grid=(n_seq, n_pages),
        in_specs=[
            seq, seq, seq,
            pl.BlockSpec((1, 1, rows), lambda b, j, pt: (b, 0, 0)),
            pl.BlockSpec((1, PAGE_SIZE, FOX_HEADS, HEAD_DIM), page4),
            pl.BlockSpec((1, PAGE_SIZE, FOX_HEADS, HEAD_DIM), page4),
            pl.BlockSpec((1, 1, PAGE_SIZE * FOX_HEADS), page3),
        ],
        out_specs=seq,
        scratch_shapes=[
            pltpu.VMEM((rows, HEAD_DIM), BF16), pltpu.VMEM((rows, 1), F32), pltpu.VMEM((rows, 1), F32),
            pltpu.VMEM((rows, 1), F32), pltpu.VMEM((rows, 1), F32), pltpu.VMEM((rows, HEAD_DIM), F32),
        ],
    )
    return pl.pallas_call(
        _fox_sample_body,
        grid_spec=grid_spec,
        out_shape=jax.ShapeDtypeStruct((n_seq * nq, FOX_HEADS, HEAD_DIM), F32),
        compiler_params=_params("parallel", "arbitrary"),
        name="fox_sample",
    )(page_table.reshape(-1), q, k_new, v_new, c_new, k_pages, v_pages, w_pages)


SSD_PAIRS = SSD_HEADS // 2
PAIRS_PER_GROUP = SSD_PAIRS // SSD_GROUPS


def _expander(width):
    n = SSD_HEADS * width
    h = lax.broadcasted_iota(jnp.int32, (LANES, n), 0)
    c = lax.broadcasted_iota(jnp.int32, (LANES, n), 1)
    return (lax.shift_right_logical(c, int(math.log2(width))) == h).astype(BF16)


def _ssd_local(xs, bm, cm, dt, alog, mask, tot_sel):
    n = xs.shape[0]
    lane = lax.broadcasted_iota(jnp.int32, (1, LANES), 1)
    dta = dt * jnp.where(lane < SSD_HEADS, -jnp.exp(alog), 0.0)
    e64 = _expander(SSD_HEAD_DIM)
    e128 = _expander(LANES)
    a_cum = _dot_sel_l(mask.astype(BF16), dta)
    a_tot = _dot_sel_l(tot_sel, dta)
    a_cum_t = a_cum.T
    ac_exp = _dot_sel_r(a_cum, e64)
    atot_exp = _dot_sel_r(a_tot, e64)
    ac_b = _dot_sel_r(a_cum, e128)
    xdt = xs * _dot_sel_r(dt, e64)
    half = lax.broadcasted_iota(jnp.int32, (n, LANES), 1) < SSD_HEAD_DIM
    out = {
        "xdtw": xdt * jnp.exp(atot_exp - ac_exp),
        "eac": jnp.exp(ac_exp),
        "atot_exp": atot_exp,
        "atot_b": _dot_sel_r(a_tot, e128),
        "bg": [], "cg": [], "y_diag": [],
    }
    for g in range(SSD_GROUPS):
        gs = slice(g * SSD_STATE, (g + 1) * SSD_STATE)
        bg = bm[:, gs].astype(BF16)
        cg = cm[:, gs].astype(BF16)
        out["bg"].append(bg)
        out["cg"].append(cg)
        cb = _dot_nt(cg, bg)
        for k in range(g * PAIRS_PER_GROUP, (g + 1) * PAIRS_PER_GROUP):
            ps = slice(k * LANES, (k + 1) * LANES)
            ms = []
            for h in (2 * k, 2 * k + 1):
                seg = ac_b[:, h * LANES:(h + 1) * LANES] - a_cum_t[h:h + 1, :]
                ms.append(cb * jnp.exp(jnp.where(mask, seg, -jnp.inf)))
            m_cat = jnp.concatenate(ms, axis=1).astype(BF16)
            xp = xdt[:, ps]
            x_bd = jnp.concatenate([jnp.where(half, xp, 0.0), jnp.where(half, 0.0, xp)], axis=0).astype(BF16)
            out["y_diag"].append(_dot(m_cat, x_bd))
    return out


def _conv_ssd_sample_body(seq_len, xbc_ref, dt_ref, buf_ref, h0_ref, w_ref, b_ref, alog_ref, dexp_ref,
                          y_ref, hout_ref, cout_ref, xp_ref):
    n = xbc_ref.shape[0]
    n_seq = n // seq_len
    taps = CONV_WIDTH - 1
    base = 8 - taps

    x = xbc_ref[...]
    xp_ref[:, base:8, :] = buf_ref[...]
    xp_ref[:, 8:8 + seq_len, :] = x.reshape(n_seq, seq_len, CONV_DIM)
    acc = b_ref[...] + x * w_ref[taps:taps + 1, :]
    for j in range(taps):
        acc = acc + xp_ref[:, base + j:base + j + seq_len, :].reshape(n, CONV_DIM) * w_ref[j:j + 1, :]
    cout_ref[...] = xp_ref[:, 8 + seq_len - taps:8 + seq_len, :]
    conv = _silu(acc)
    xs = conv[:, :SSD_WIDTH]
    bm = conv[:, SSD_WIDTH:SSD_WIDTH + SSD_GROUPS * SSD_STATE]
    cm = conv[:, SSD_WIDTH + SSD_GROUPS * SSD_STATE:]

    r = lax.broadcasted_iota(jnp.int32, (n, n), 0)
    c = lax.broadcasted_iota(jnp.int32, (n, n), 1)
    same_seq = (r // seq_len) == (c // seq_len)
    loc = _ssd_local(xs, bm, cm, dt_ref[...], alog_ref[...], jnp.logical_and(same_seq, c <= r),
                     same_seq.astype(BF16))

    gw = PAIRS_PER_GROUP * LANES
    seq_of_col = lax.broadcasted_iota(jnp.int32, (1, n), 1) // seq_len
    decay_t = jnp.exp(loc["atot_exp"]).T
    for g in range(SSD_GROUPS):
        gr = slice(g * gw, (g + 1) * gw)
        h_prev = h0_ref[:, gr, :]
        z = _dot_nt(h_prev.reshape(n_seq * gw, SSD_STATE).astype(BF16), loc["cg"][g])
        y_off_t = jnp.zeros((gw, n), F32)
        for s in range(n_seq):
            y_off_t = y_off_t + jnp.where(seq_of_col == s, z[s * gw:(s + 1) * gw, :], 0.0)
        y_off = y_off_t.T * loc["eac"][:, gr]
        y = jnp.concatenate(loc["y_diag"][g * PAIRS_PER_GROUP:(g + 1) * PAIRS_PER_GROUP], axis=1)
        y_ref[:, gr] = y + y_off + xs[:, gr] * dexp_ref[:, gr]
        xw_t = loc["xdtw"][:, gr].T
        lhs = jnp.concatenate([jnp.where(seq_of_col == s, xw_t, 0.0) for s in range(n_seq)], axis=0)
        s_new = _dot(lhs.astype(BF16), loc["bg"][g])
        for s in range(n_seq):
            col = decay_t[gr, s * seq_len:s * seq_len + 1]
            hout_ref[s, gr, :] = h_prev[s] * col + s_new[s * gw:(s + 1) * gw, :]


def _conv_ssd_sample(xbc, dt, conv_buf, h0, conv_w, conv_b, alog, d_exp, n_seq, seq_len):
    assert seq_len == 8 and CONV_WIDTH - 1 <= seq_len
    tile = LANES
    ts = tile // seq_len
    const = lambda i: (0, 0)
    per_s = lambda i: (i, 0, 0)
    return pl.pallas_call(
        functools.partial(_conv_ssd_sample_body, seq_len),
        grid=(n_seq // ts,),
        in_specs=[
            pl.BlockSpec((tile, CONV_DIM), lambda i: (i, 0)),
            pl.BlockSpec((tile, LANES), lambda i: (i, 0)),
            pl.BlockSpec((ts, CONV_WIDTH - 1, CONV_DIM), per_s),
            pl.BlockSpec((ts, SSD_WIDTH, SSD_STATE), per_s),
            pl.BlockSpec((CONV_WIDTH, CONV_DIM), const),
            pl.BlockSpec((1, CONV_DIM), const),
            pl.BlockSpec((1, LANES), const),
            pl.BlockSpec((1, SSD_WIDTH), const),
        ],
        out_specs=[
            pl.BlockSpec((tile, SSD_WIDTH), lambda i: (i, 0)),
            pl.BlockSpec((ts, SSD_WIDTH, SSD_STATE), per_s),
            pl.BlockSpec((ts, CONV_WIDTH - 1, CONV_DIM), per_s),
        ],
        out_shape=[
            jax.ShapeDtypeStruct((n_seq * seq_len, SSD_WIDTH), F32),
            jax.ShapeDtypeStruct((n_seq, SSD_WIDTH, SSD_STATE), F32),
            jax.ShapeDtypeStruct((n_seq, CONV_WIDTH - 1, CONV_DIM), F32),
        ],
        scratch_shapes=[pltpu.VMEM((ts, 8 + seq_len, CONV_DIM), F32)],
        compiler_params=_params("parallel"),
        name="conv_ssd_sample",
    )(xbc, dt, conv_buf, h0, conv_w, conv_b, alog, d_exp)


def _conv_ssd_prompt_body(xbc_ref, dt_ref, buf_ref, h0_ref, w_ref, b_ref, alog_ref, dexp_ref,
                          y_ref, hout_ref, cout_ref, state_ref, xp_ref):
    c = pl.program_id(1)
    nc = pl.num_programs(1)
    tl = xbc_ref.shape[0]
    taps = CONV_WIDTH - 1
    base = 8 - taps

    @pl.when(c == 0)
    def _():
        state_ref[...] = h0_ref[0]
        xp_ref[base:8, :] = buf_ref[0]

    x = xbc_ref[...]
    xp_ref[8:8 + tl, :] = x
    acc = b_ref[...] + x * w_ref[taps:taps + 1, :]
    for j in range(taps):
        acc = acc + xp_ref[base + j:base + j + tl, :] * w_ref[j:j + 1, :]
    xp_ref[base:8, :] = x[tl - taps:, :]
    conv = _silu(acc)
    xs = conv[:, :SSD_WIDTH]
    bm = conv[:, SSD_WIDTH:SSD_WIDTH + SSD_GROUPS * SSD_STATE]
    cm = conv[:, SSD_WIDTH + SSD_GROUPS * SSD_STATE:]

    causal = _tri(tl)
    loc = _ssd_local(xs, bm, cm, dt_ref[...], alog_ref[...], causal, jnp.ones((tl, tl), BF16))
    top = lax.broadcasted_iota(jnp.int32, (tl, LANES), 0) < SSD_HEAD_DIM
    for k in range(SSD_PAIRS):
        g = k // PAIRS_PER_GROUP
        ps = slice(k * LANES, (k + 1) * LANES)
        s_prev = state_ref[ps, :]
        y_off = _dot_nt(loc["cg"][g], s_prev.astype(BF16)) * loc["eac"][:, ps]
        y_ref[:, ps] = loc["y_diag"][k] + y_off + xs[:, ps] * dexp_ref[:, ps]
        cd = [jnp.exp(loc["atot_b"][0:1, h * LANES:(h + 1) * LANES]) for h in (2 * k, 2 * k + 1)]
        state_ref[ps, :] = s_prev * jnp.where(top, cd[0], cd[1]) + _dot_tn(loc["xdtw"][:, ps].astype(BF16), loc["bg"][g])

    @pl.when(c == nc - 1)
    def _():
        hout_ref[0] = state_ref[...]
        cout_ref[0] = xp_ref[base:8, :]


def _conv_ssd_prompt(xbc, dt, conv_buf, h0, conv_w, conv_b, alog, d_exp, b, l):
    tl = SSD_CHUNK
    nc = l // tl
    const = lambda bi, c: (0, 0)
    per_b = lambda bi, c: (bi, 0, 0)
    return pl.pallas_call(
        _conv_ssd_prompt_body,
        grid=(b, nc),
        in_specs=[
            pl.BlockSpec((tl, CONV_DIM), lambda bi, c: (bi * nc + c, 0)),
            pl.BlockSpec((tl, LANES), lambda bi, c: (bi * nc + c, 0)),
            pl.BlockSpec((1, CONV_WIDTH - 1, CONV_DIM), per_b),
            pl.BlockSpec((1, SSD_WIDTH, SSD_STATE), per_b),
            pl.BlockSpec((CONV_WIDTH, CONV_DIM), const),
            pl.BlockSpec((1, CONV_DIM), const),
            pl.BlockSpec((1, LANES), const),
            pl.BlockSpec((1, SSD_WIDTH), const),
        ],
        out_specs=[
            pl.BlockSpec((tl, SSD_WIDTH), lambda bi, c: (bi * nc + c, 0)),
            pl.BlockSpec((1, SSD_WIDTH, SSD_STATE), per_b),
            pl.BlockSpec((1, CONV_WIDTH - 1, CONV_DIM), per_b),
        ],
        out_shape=[
            jax.ShapeDtypeStruct((b * l, SSD_WIDTH), F32),
            jax.ShapeDtypeStruct((b, SSD_WIDTH, SSD_STATE), F32),
            jax.ShapeDtypeStruct((b, CONV_WIDTH - 1, CONV_DIM), F32),
        ],
        scratch_shapes=[pltpu.VMEM((SSD_WIDTH, SSD_STATE), F32), pltpu.VMEM((8 + tl, CONV_DIM), F32)],
        compiler_params=_params("parallel", "arbitrary"),
        name="conv_ssd_prompt",
    )(xbc, dt, conv_buf, h0, conv_w, conv_b, alog, d_exp)


def _out_proj_body(h_ref, fox_ref, y_ref, z_ref, gs_ref, wf_ref, ws_ref, gx_ref, wq_ref, gq_ref, o_ref, q_ref):
    yn = _rms(y_ref[...] * _silu(z_ref[...]), gs_ref[...]).astype(BF16)
    h = h_ref[...] + _dot(fox_ref[...].astype(BF16), wf_ref[...]) + _dot(yn, ws_ref[...])
    o_ref[...] = h
    q = _dot(_rms(h, gx_ref[...]).astype(BF16), wq_ref[...])
    q_ref[...] = _head_norm(q, gq_ref[...]) * (HEAD_DIM ** -0.5)


def _out_proj(h, fox, y, z, g_ssd, w_fox, w_ssd, g_x, wq, gq, *, tm=256):
    t, d = h.shape
    tm = min(tm, t)
    row = lambda w: pl.BlockSpec((tm, w), lambda i: (i, 0))
    full = lambda a: pl.BlockSpec(a.shape, lambda i: (0, 0))
    consts = [g_ssd.reshape(1, SSD_WIDTH), w_fox, w_ssd, g_x.reshape(1, d), wq, gq.reshape(1, HEAD_DIM)]
    return pl.pallas_call(
        _out_proj_body,
        grid=(t // tm,),
        in_specs=[row(d), row(FOX_WIDTH), row(SSD_WIDTH), row(SSD_WIDTH)] + [full(a) for a in consts],
        out_specs=[row(d), row(XATTN_WIDTH)],
        out_shape=[jax.ShapeDtypeStruct((t, d), F32), jax.ShapeDtypeStruct((t, XATTN_WIDTH), F32)],
        compiler_params=_params("parallel"),
        name="out_proj",
    )(h, fox, y, z, *consts)


def _mem_kv_body(m_ref, g_ref, w_ref, gk_ref, k_ref, v_ref):
    kv = _dot(_rms(m_ref[...], g_ref[...]).astype(BF16), w_ref[...])
    k_ref[...] = _head_norm(kv[:, :XATTN_WIDTH], gk_ref[...])
    v_ref[...] = kv[:, XATTN_WIDTH:]


def _mem_kv(mem, g, w_kv, gk, *, tm=256):
    t, d = mem.shape
    row = lambda w: pl.BlockSpec((tm, w), lambda i: (i, 0))
    full = lambda a: pl.BlockSpec(a.shape, lambda i: (0, 0))
    consts = [g.reshape(1, d), w_kv, gk.reshape(1, HEAD_DIM)]
    return pl.pallas_call(
        _mem_kv_body,
        grid=(t // tm,),
        in_specs=[row(d)] + [full(a) for a in consts],
        out_specs=[row(XATTN_WIDTH), row(XATTN_WIDTH)],
        out_shape=[jax.ShapeDtypeStruct((t, XATTN_WIDTH), F32)] * 2,
        compiler_params=_params("parallel"),
        name="mem_kv",
    )(mem, *consts)


def _xattn_body(q_ref, k_ref, v_ref, o_ref):
    for h in range(XATTN_HEADS):
        hs = slice(h * HEAD_DIM, (h + 1) * HEAD_DIM)
        s = _dot_nt(q_ref[:, hs].astype(BF16), k_ref[0, :, hs].astype(BF16))
        p = jnp.exp(s - jnp.max(s, axis=-1, keepdims=True))
        o = _dot(p.astype(BF16), v_ref[0, :, hs].astype(BF16))
        o_ref[:, hs] = o / jnp.sum(p, axis=-1, keepdims=True)


def _xattn(q, mem_k, mem_v, b, l, *, tq=512):
    tq = min(tq, l)
    nq = l // tq
    n_mem = mem_k.shape[1]
    qspec = pl.BlockSpec((tq, XATTN_WIDTH), lambda bi, i: (bi * nq + i, 0))
    mspec = pl.BlockSpec((1, n_mem, XATTN_WIDTH), lambda bi, i: (bi, 0, 0))
    return pl.pallas_call(
        _xattn_body,
        grid=(b, nq),
        in_specs=[qspec, mspec, mspec],
        out_specs=qspec,
        out_shape=jax.ShapeDtypeStruct((b * l, XATTN_WIDTH), F32),
        compiler_params=_params("parallel", "arbitrary"),
        name="xattn",
    )(q, mem_k, mem_v)


def kernel(x_prompt, x_sample, cache_fox_k, cache_fox_v, cache_fox_logf, cache_mem_k, cache_mem_v, state_ssm, state_conv, page_table, mem_prompt, ffn1_norm, ffn1_w_gate, ffn1_w_up, ffn1_w_down, mix_norm, w_in, fox_b_f, fox_q_norm, fox_k_norm, conv_w, conv_b, ssd_dt_bias, ssd_A_log, ssd_D, ssd_out_norm, w_out, xattn_norm, mem_norm, xattn_w_q, xattn_w_kv, xattn_q_norm, xattn_k_norm, xattn_w_o, ffn2_norm, ffn2_w_gate, ffn2_w_up, ffn2_w_down):
    assert x_prompt.shape[2] == D_MODEL and ffn1_norm.shape[0] == 1
    d = D_MODEL
    bp, lp = x_prompt.shape[:2]
    bs, ls = x_sample.shape[:2]
    n_mem = mem_prompt.shape[1]
    n_pool = cache_fox_k.shape[1]

    bf = lambda w: w[0].astype(BF16)
    ffn1 = (ffn1_norm[0], bf(ffn1_w_gate), bf(ffn1_w_up), bf(ffn1_w_down))
    ffn2 = (ffn2_norm[0], bf(ffn2_w_gate), bf(ffn2_w_up), bf(ffn2_w_down))
    w_main, w_small, b_small = _pack_in_proj(w_in[0], fox_b_f[0], ssd_dt_bias[0])
    w_fox, w_ssd = bf(w_out)[:FOX_WIDTH], bf(w_out)[FOX_WIDTH:]
    wq, wkv, wo = bf(xattn_w_q), bf(xattn_w_kv), bf(xattn_w_o)
    alog = jnp.pad(ssd_A_log[0], (0, LANES - SSD_HEADS)).reshape(1, LANES)
    d_exp = jnp.repeat(ssd_D[0], SSD_HEAD_DIM).reshape(1, SSD_WIDTH)
    cw, cb = conv_w[0], conv_b[0].reshape(1, CONV_DIM)

    def front(x, q_dtype):
        h1 = _ffn(x, *ffn1)
        return h1, _in_proj(h1, mix_norm[0], w_main, w_small, b_small, fox_q_norm[0], fox_k_norm[0], q_dtype=q_dtype)

    def back(h1, fox, y, z, mem_k, mem_v, b, l):
        h2, xq = _out_proj(h1, fox, y, z, ssd_out_norm[0], w_fox, w_ssd, xattn_norm[0], wq, xattn_q_norm[0])
        return _ffn(h2, *ffn2, pre=(_xattn(xq, mem_k, mem_v, b, l), wo))

    h1, (q, k_p, kb, v_p, vb, z, xbc, logf_p, dt) = front(x_prompt.reshape(bp * lp, d), BF16)
    ccol, crow = _cumsum(logf_p.reshape(bp, lp, LANES))
    tq = min(ATT_BLOCK, lp)
    fox = _fox_prompt(q, kb, vb, ccol, crow.reshape(bp, FOX_HEADS, lp // tq, tq), bp, lp)
    y, ssm_p, conv_p = _conv_ssd_prompt(
        xbc, dt, jnp.zeros((bp, CONV_WIDTH - 1, CONV_DIM), F32), jnp.zeros((bp, SSD_WIDTH, SSD_STATE), F32),
        cw, cb, alog, d_exp, bp, lp)
    mk, mv = _mem_kv(mem_prompt.reshape(bp * n_mem, d), mem_norm[0], wkv, xattn_k_norm[0])
    y_prompt = back(h1, fox, y, z, mk.reshape(bp, n_mem, XATTN_WIDTH), mv.reshape(bp, n_mem, XATTN_WIDTH), bp, lp)

    h1, (q, k_s, _, v_s, _, z, xbc, logf_s, dt) = front(x_sample.reshape(bs * ls, d), F32)
    w_pages = _page_cumsum(cache_fox_logf[0].reshape(n_pool, PAGE_SIZE * FOX_HEADS))
    heads = lambda a: a.reshape(bs * ls, FOX_HEADS, HEAD_DIM)
    c_new = _seq_cumsum(logf_s, ls)[:, :FOX_HEADS].reshape(bs, 1, ls * FOX_HEADS)
    fox = _fox_sample(
        page_table, heads(q), heads(k_s), heads(v_s), c_new, cache_fox_k[0], cache_fox_v[0],
        w_pages.reshape(n_pool, 1, PAGE_SIZE * FOX_HEADS), bs, ls).reshape(bs * ls, FOX_WIDTH)
    y, ssm_s, conv_s = _conv_ssd_sample(
        xbc, dt, state_conv[0], state_ssm[0].reshape(bs, SSD_WIDTH, SSD_STATE), cw, cb, alog, d_exp, bs, ls)
    y_sample = back(h1, fox, y, z, cache_mem_k[0].reshape(bs, n_mem, XATTN_WIDTH),
                    cache_mem_v[0].reshape(bs, n_mem, XATTN_WIDTH), bs, ls)

    fox_shape = lambda b, l: (1, b, l, FOX_HEADS, HEAD_DIM)
    ssm_shape = lambda b: (1, b, SSD_HEADS, SSD_HEAD_DIM, SSD_STATE)
    mem_shape = (1, bp, n_mem, XATTN_HEADS, HEAD_DIM)
    return (
        y_prompt.reshape(bp, lp, d), y_sample.reshape(bs, ls, d),
        k_p.reshape(fox_shape(bp, lp)), v_p.reshape(fox_shape(bp, lp)),
        logf_p[:, :FOX_HEADS].reshape(1, bp, lp, FOX_HEADS),
        ssm_p.reshape(ssm_shape(bp)), conv_p[None], mk.reshape(mem_shape), mv.reshape(mem_shape),
        k_s.reshape(fox_shape(bs, ls)), v_s.reshape(fox_shape(bs, ls)),
        logf_s[:, :FOX_HEADS].reshape(1, bs, ls, FOX_HEADS),
        ssm_s.reshape(ssm_shape(bs)), conv_s[None],
    )
```

```python
import functools
import math

import jax
import jax.numpy as jnp
from jax import lax
from jax.experimental import pallas as pl
from jax.experimental.pallas import tpu as pltpu

F32 = jnp.float32
BF16 = jnp.bfloat16

EPS = 1e-6
FFN_RESIDUAL = 0.5
D_MODEL = 2048
D_FF = 5632
PAGE_SIZE = 128
FOX_HEADS = 8
HEAD_DIM = 128
FOX_WIDTH = FOX_HEADS * HEAD_DIM
SSD_HEADS = 16
SSD_HEAD_DIM = 64
SSD_WIDTH = SSD_HEADS * SSD_HEAD_DIM
SSD_GROUPS = 2
SSD_STATE = 128
SSD_CHUNK = 128
CONV_WIDTH = 4
CONV_DIM = SSD_WIDTH + 2 * SSD_GROUPS * SSD_STATE
XATTN_HEADS = 4
XATTN_WIDTH = XATTN_HEADS * HEAD_DIM
LANES = 128
VMEM_LIMIT_BYTES = 56 * 1024 * 1024


def _params(*semantics):
    return pltpu.CompilerParams(dimension_semantics=semantics, vmem_limit_bytes=VMEM_LIMIT_BYTES)


def _rms(x, g):
    return x * lax.rsqrt(jnp.mean(x * x, axis=-1, keepdims=True) + EPS) * g


def _dot(a, b):
    return jnp.dot(a, b, preferred_element_type=F32)


def _dot_nt(a, b):
    return lax.dot_general(a, b, (((1,), (1,)), ((), ())), preferred_element_type=F32)


def _dot_tn(a, b):
    return lax.dot_general(a, b, (((0,), (0,)), ((), ())), preferred_element_type=F32)


def _split3(x):
    x1 = x.astype(BF16)
    r = x - x1.astype(F32)
    x2 = r.astype(BF16)
    x3 = (r - x2.astype(F32)).astype(BF16)
    return x1, x2, x3


def _dot_sel_l(sel, x):
    x1, x2, x3 = _split3(x)
    return _dot(sel, x1) + _dot(sel, x2) + _dot(sel, x3)


def _dot_sel_r(x, sel):
    x1, x2, x3 = _split3(x)
    return _dot(x1, sel) + _dot(x2, sel) + _dot(x3, sel)


def _silu(x):
    return x * jax.nn.sigmoid(x)


def _ffn_body(pre_proj, *refs):
    if pre_proj:
        x_ref, a_ref, wo_ref, g_ref, wg_ref, wu_ref, wd_ref, o_ref, xn_ref = refs
    else:
        x_ref, g_ref, wg_ref, wu_ref, wd_ref, o_ref, xn_ref = refs

    @pl.when(pl.program_id(1) == 0)
    def _():
        x = x_ref[...]
        if pre_proj:
            x = x + _dot(a_ref[...].astype(BF16), wo_ref[...])
        xn_ref[...] = _rms(x, g_ref[...]).astype(BF16)
        o_ref[...] = x

    xn = xn_ref[...]
    gate = _dot(xn, wg_ref[...])
    up = _dot(xn, wu_ref[...])
    h = (_silu(gate) * up * FFN_RESIDUAL).astype(BF16)
    o_ref[...] += _dot(h, wd_ref[...])


def _ffn(x, g, wg, wu, wd, pre=None, *, tm=512, tf=512):
    t, d = x.shape
    f = wg.shape[1]
    tm = min(tm, t)
    grid = (t // tm, f // tf)
    row = lambda i, j: (i, 0)
    in_specs = [pl.BlockSpec((tm, d), row)]
    args = [x]
    if pre is not None:
        a, wo = pre
        in_specs += [pl.BlockSpec((tm, a.shape[1]), row), pl.BlockSpec(wo.shape, lambda i, j: (0, 0))]
        args += [a, wo]
    in_specs += [
        pl.BlockSpec((1, d), lambda i, j: (0, 0)),
        pl.BlockSpec((d, tf), lambda i, j: (0, j)),
        pl.BlockSpec((d, tf), lambda i, j: (0, j)),
        pl.BlockSpec((tf, d), lambda i, j: (j, 0)),
    ]
    args += [g.reshape(1, d), wg, wu, wd]
    return pl.pallas_call(
        functools.partial(_ffn_body, pre is not None),
        grid=grid,
        in_specs=in_specs,
        out_specs=pl.BlockSpec((tm, d), row),
        out_shape=jax.ShapeDtypeStruct((t, d), F32),
        scratch_shapes=[pltpu.VMEM((tm, d), BF16)],
        compiler_params=_params("parallel", "arbitrary"),
        name="ffn_pre" if pre is not None else "ffn",
    )(*args)


IN_TN = 512
IN_BLOCKS = {"q": (0, 2), "k": (2, 2), "v": (4, 2), "z": (6, 2), "xbc": (8, 3)}
IN_NBLK = 11


def _head_norm(y, g):
    outs = []
    for c in range(y.shape[1] // HEAD_DIM):
        yc = y[:, c * HEAD_DIM:(c + 1) * HEAD_DIM]
        outs.append(yc * lax.rsqrt(jnp.mean(yc * yc, axis=-1, keepdims=True) + EPS) * g)
    return jnp.concatenate(outs, axis=1)


def _in_proj_body(x_ref, g_ref, w_ref, ws_ref, bs_ref, gq_ref, gk_ref,
                  q_ref, k_ref, kb_ref, v_ref, vb_ref, z_ref, xbc_ref, logf_ref, dt_ref, u_ref):
    j = pl.program_id(1)

    @pl.when(j == 0)
    def _():
        u = _rms(x_ref[...], g_ref[...]).astype(BF16)
        u_ref[...] = u
        s = _dot(u, ws_ref[...]) + bs_ref[...]
        t = jnp.log1p(jnp.exp(-jnp.abs(s)))
        logf_ref[...] = (jnp.minimum(s, 0.0) - t)[:, :LANES]
        dt_ref[...] = (jnp.maximum(s, 0.0) + t)[:, LANES:]

    y = _dot(u_ref[...], w_ref[...])

    def in_range(name):
        lo, n = IN_BLOCKS[name]
        return jnp.logical_and(j >= lo, j < lo + n)

    @pl.when(in_range("q"))
    def _():
        q_ref[...] = (_head_norm(y, gq_ref[...]) * (HEAD_DIM ** -0.5)).astype(q_ref.dtype)

    @pl.when(in_range("k"))
    def _():
        kn = _head_norm(y, gk_ref[...])
        k_ref[...] = kn
        kb_ref[...] = kn.astype(BF16)

    @pl.when(in_range("v"))
    def _():
        v_ref[...] = y
        vb_ref[...] = y.astype(BF16)

    @pl.when(in_range("z"))
    def _():
        z_ref[...] = y

    @pl.when(in_range("xbc"))
    def _():
        xbc_ref[...] = y


def _in_proj(x, g, w_main, w_small, b_small, gq, gk, *, q_dtype=BF16, tm=512):
    t, d = x.shape
    tm = min(tm, t)
    tn = IN_TN

    def col(name):
        lo, n = IN_BLOCKS[name]
        return pl.BlockSpec((tm, tn), lambda i, j: (i, jnp.clip(j - lo, 0, n - 1)))

    const = lambda i, j: (0, 0)
    slab = pl.BlockSpec((tm, LANES), lambda i, j: (i, 0))
    outs = [
        ("q", q_dtype, FOX_WIDTH), ("k", F32, FOX_WIDTH), ("k", BF16, FOX_WIDTH), ("v", F32, FOX_WIDTH),
        ("v", BF16, FOX_WIDTH), ("z", F32, SSD_WIDTH), ("xbc", F32, CONV_DIM),
    ]
    return pl.pallas_call(
        _in_proj_body,
        grid=(t // tm, IN_NBLK),
        in_specs=[
            pl.BlockSpec((tm, d), lambda i, j: (i, 0)),
            pl.BlockSpec((1, d), const),
            pl.BlockSpec((d, tn), lambda i, j: (0, j)),
            pl.BlockSpec((d, 2 * LANES), const),
            pl.BlockSpec((1, 2 * LANES), const),
            pl.BlockSpec((1, HEAD_DIM), const),
            pl.BlockSpec((1, HEAD_DIM), const),
        ],
        out_specs=[col(n) for n, _, _ in outs] + [slab, slab],
        out_shape=[jax.ShapeDtypeStruct((t, w), dt) for _, dt, w in outs]
        + [jax.ShapeDtypeStruct((t, LANES), F32)] * 2,
        scratch_shapes=[pltpu.VMEM((tm, d), BF16)],
        compiler_params=_params("parallel", "arbitrary"),
        name="in_proj",
    )(x, g.reshape(1, d), w_main, w_small, b_small, gq.reshape(1, HEAD_DIM), gk.reshape(1, HEAD_DIM))


def _pack_in_proj(w_in, fox_b_f, ssd_dt_bias):
    fw = FOX_WIDTH
    f0 = 3 * fw
    z0 = f0 + FOX_HEADS
    x0 = z0 + SSD_WIDTH
    d0 = x0 + CONV_DIM
    w_main = jnp.concatenate([w_in[:, :f0], w_in[:, z0:d0]], axis=1).astype(BF16)
    zeros = lambda n: jnp.zeros((w_in.shape[0], n), w_in.dtype)
    w_small = jnp.concatenate(
        [w_in[:, f0:z0], zeros(LANES - FOX_HEADS), w_in[:, d0:], zeros(LANES - SSD_HEADS)], axis=1).astype(BF16)
    b_small = jnp.concatenate(
        [fox_b_f, jnp.zeros((LANES - FOX_HEADS,), F32), ssd_dt_bias, jnp.zeros((LANES - SSD_HEADS,), F32)]
    ).reshape(1, 2 * LANES)
    return w_main, w_small, b_small


ATT_BLOCK = 256


def _tri(n, *, strict=False, upper=False):
    r = lax.broadcasted_iota(jnp.int32, (n, n), 0)
    c = lax.broadcasted_iota(jnp.int32, (n, n), 1)
    if upper:
        r, c = c, r
    return (c < r) if strict else (c <= r)


def _cumsum_body(x_ref, col_ref, row_ref, carry_ref):
    @pl.when(pl.program_id(1) == 0)
    def _():
        carry_ref[...] = jnp.zeros_like(carry_ref)

    n = x_ref.shape[1]
    tril = _tri(n).astype(BF16)
    c = _dot_sel_l(tril, x_ref[0]) + carry_ref[...]
    carry_ref[...] = c[n - 1:n, :]
    col_ref[0] = c
    row_ref[0] = c.T[:FOX_HEADS, :]


def _cumsum(x, *, tb=ATT_BLOCK):
    b, l, _ = x.shape
    return pl.pallas_call(
        _cumsum_body,
        grid=(b, l // tb),
        in_specs=[pl.BlockSpec((1, tb, LANES), lambda i, j: (i, j, 0))],
        out_specs=[pl.BlockSpec((1, tb, LANES), lambda i, j: (i, j, 0)),
                   pl.BlockSpec((1, FOX_HEADS, tb), lambda i, j: (i, 0, j))],
        out_shape=[jax.ShapeDtypeStruct((b, l, LANES), F32), jax.ShapeDtypeStruct((b, FOX_HEADS, l), F32)],
        scratch_shapes=[pltpu.VMEM((1, LANES), F32)],
        compiler_params=_params("parallel", "arbitrary"),
        name="logf_cumsum",
    )(x)


def _fox_prompt_body(tk, q_ref, k_ref, v_ref, ccol_ref, crow_ref, o_ref, m_ref, l_ref, cq_ref, acc_ref):
    i = pl.program_id(1)
    tq = q_ref.shape[0]
    rep = tk // LANES
    m_ref[...] = jnp.full_like(m_ref, -jnp.inf)
    l_ref[...] = jnp.zeros_like(l_ref)
    acc_ref[...] = jnp.zeros_like(acc_ref)
    for h in range(FOX_HEADS):
        cq_ref[h] = jnp.broadcast_to(ccol_ref[0, :, h:h + 1], (tq, LANES))
    row = i * tq + lax.broadcasted_iota(jnp.int32, (tq, tk), 0)
    col = lax.broadcasted_iota(jnp.int32, (tq, tk), 1)
    wide = lambda x: jnp.concatenate([x] * rep, axis=1)

    def block(j, masked):
        ks = pl.ds(pl.multiple_of(j * tk, tk), tk)
        for h in range(FOX_HEADS):
            hs = slice(h * HEAD_DIM, (h + 1) * HEAD_DIM)
            s = _dot_nt(q_ref[:, hs], k_ref[ks, hs]) + (wide(cq_ref[h]) - crow_ref[0, h, pl.ds(j, 1), :])
            if masked:
                s = jnp.where(col + j * tk <= row, s, -jnp.inf)
            m_old = m_ref[h]
            m_new = jnp.maximum(m_old, jnp.max(s, axis=-1, keepdims=True))
            alpha = jnp.exp(m_old - m_new)
            p = jnp.exp(s - wide(m_new))
            m_ref[h] = m_new
            l_ref[h] = alpha * l_ref[h] + jnp.sum(p, axis=-1, keepdims=True)
            acc_ref[h] = alpha * acc_ref[h] + _dot(p.astype(BF16), v_ref[ks, hs])

    last = (i * tq + tq - 1) // tk

    def step(j, carry):
        block(j, False)
        return carry

    lax.fori_loop(0, last, step, 0)
    block(last, True)
    for h in range(FOX_HEADS):
        o_ref[:, h * HEAD_DIM:(h + 1) * HEAD_DIM] = (acc_ref[h] / l_ref[h]).astype(o_ref.dtype)


def _fox_prompt(q, k, v, ccol, crow, b, l, *, tq=128):
    tk = crow.shape[-1]
    tq = min(tq, l)
    assert tk % tq == 0
    nq = l // tq
    w = FOX_WIDTH
    return pl.pallas_call(
        functools.partial(_fox_prompt_body, tk),
        grid=(b, nq),
        in_specs=[
            pl.BlockSpec((tq, w), lambda bi, i: (bi * nq + i, 0)),
            pl.BlockSpec((l, w), lambda bi, i: (bi, 0)),
            pl.BlockSpec((l, w), lambda bi, i: (bi, 0)),
            pl.BlockSpec((1, tq, LANES), lambda bi, i: (bi, i, 0)),
            pl.BlockSpec((1, FOX_HEADS, l // tk, tk), lambda bi, i: (bi, 0, 0, 0)),
        ],
        out_specs=pl.BlockSpec((tq, w), lambda bi, i: (bi * nq + i, 0)),
        out_shape=jax.ShapeDtypeStruct((b * l, w), BF16),
        scratch_shapes=[pltpu.VMEM((FOX_HEADS, tq, LANES), F32)] * 3 + [pltpu.VMEM((FOX_HEADS, tq, HEAD_DIM), F32)],
        compiler_params=_params("parallel", "arbitrary"),
        name="fox_prompt",
    )(q, k, v, ccol, crow)


def _page_cumsum_body(x_ref, w_ref, m_ref):
    n = PAGE_SIZE * FOX_HEADS

    @pl.when(pl.program_id(0) == 0)
    def _():
        r = lax.broadcasted_iota(jnp.int32, (n, n), 0)
        c = lax.broadcasted_iota(jnp.int32, (n, n), 1)
        same_head = (r & (FOX_HEADS - 1)) == (c & (FOX_HEADS - 1))
        earlier = lax.shift_right_logical(r, 3) <= lax.shift_right_logical(c, 3)
        m_ref[...] = jnp.logical_and(same_head, earlier).astype(BF16)

    w_ref[...] = _dot_sel_r(x_ref[...], m_ref[...])


def _page_cumsum(logf_pages, *, tb=256):
    n_pool, n = logf_pages.shape
    return pl.pallas_call(
        _page_cumsum_body,
        grid=(n_pool // tb,),
        in_specs=[pl.BlockSpec((tb, n), lambda i: (i, 0))],
        out_specs=pl.BlockSpec((tb, n), lambda i: (i, 0)),
        out_shape=jax.ShapeDtypeStruct((n_pool, n), F32),
        scratch_shapes=[pltpu.VMEM((n, n), BF16)],
        compiler_params=_params("arbitrary"),
        name="page_cumsum",
    )(logf_pages)


def _seq_cumsum_body(seq_len, x_ref, o_ref):
    n = x_ref.shape[0]
    r = lax.broadcasted_iota(jnp.int32, (n, n), 0)
    c = lax.broadcasted_iota(jnp.int32, (n, n), 1)
    same_seq = (r // seq_len) == (c // seq_len)
    o_ref[...] = _dot_sel_l(jnp.logical_and(same_seq, c <= r).astype(BF16), x_ref[...])


def _seq_cumsum(x, seq_len, *, tb=128):
    t = x.shape[0]
    return pl.pallas_call(
        functools.partial(_seq_cumsum_body, seq_len),
        grid=(t // tb,),
        in_specs=[pl.BlockSpec((tb, LANES), lambda i: (i, 0))],
        out_specs=pl.BlockSpec((tb, LANES), lambda i: (i, 0)),
        out_shape=jax.ShapeDtypeStruct((t, LANES), F32),
        compiler_params=_params("parallel"),
        name="seq_cumsum",
    )(x)


def _fox_sample_body(pps, pt_ref, q_ref, kn_ref, vn_ref, cn_ref, *refs):
    kp_refs, vp_refs, w_refs = refs[:pps], refs[pps:2 * pps], refs[2 * pps:3 * pps]
    o_ref, q_scr, colq_ref, toff_ref, m_ref, l_ref, acc_ref = refs[3 * pps:]
    j = pl.program_id(1)
    nq = q_ref.shape[0]
    rows = nq * FOX_HEADS
    page_keys = PAGE_SIZE * FOX_HEADS
    row_id = lax.broadcasted_iota(jnp.int32, (rows, 1), 0)
    head_of_row = row_id & (FOX_HEADS - 1)
    query_of_row = lax.shift_right_logical(row_id, 3)

    @pl.when(j == 0)
    def _():
        q = q_ref[...].reshape(rows, HEAD_DIM).astype(BF16)
        q_scr[...] = q
        cn = cn_ref[0]
        key = lax.broadcasted_iota(jnp.int32, (1, rows), 1)
        colq = jnp.sum(jnp.where(key == row_id, cn, 0.0), axis=-1, keepdims=True)
        colq_ref[...] = colq
        toff_ref[...] = jnp.zeros_like(toff_ref)
        s = _dot_nt(q, kn_ref[...].reshape(rows, HEAD_DIM).astype(BF16)) + colq - cn
        valid = jnp.logical_and((key & (FOX_HEADS - 1)) == head_of_row,
                                lax.shift_right_logical(key, 3) <= query_of_row)
        s = jnp.where(valid, s, -jnp.inf)
        m = jnp.max(s, axis=-1, keepdims=True)
        p = jnp.exp(s - m)
        m_ref[...] = m
        l_ref[...] = jnp.sum(p, axis=-1, keepdims=True)
        acc_ref[...] = _dot(p.astype(BF16), vn_ref[...].reshape(rows, HEAD_DIM).astype(BF16))

    lane = lax.broadcasted_iota(jnp.int32, (1, LANES), 1)
    own_head = (lax.broadcasted_iota(jnp.int32, (1, page_keys), 1) & (FOX_HEADS - 1)) == head_of_row
    q = q_scr[...]
    colq = colq_ref[...]
    toff = toff_ref[...]
    tiles = []
    for kp_ref, w_ref in zip(kp_refs, w_refs):
        w = w_ref[0]
        last = jnp.where(lane == LANES - FOX_HEADS + head_of_row, w[:, page_keys - LANES:], 0.0)
        toff = toff + jnp.sum(last, axis=-1, keepdims=True)
        s = _dot_nt(q, kp_ref[0].reshape(page_keys, HEAD_DIM).astype(BF16))
        tiles.append(jnp.where(own_head, s + (colq + toff) - w, -jnp.inf))
    toff_ref[...] = toff
    m_old = m_ref[...]
    m = m_old
    for s in tiles:
        m = jnp.maximum(m, jnp.max(s, axis=-1, keepdims=True))
    alpha = jnp.exp(m_old - m)
    l = alpha * l_ref[...]
    acc = alpha * acc_ref[...]
    for s, vp_ref in zip(tiles, vp_refs):
        p = jnp.exp(s - m)
        l = l + jnp.sum(p, axis=-1, keepdims=True)
        acc = acc + _dot(p.astype(BF16), vp_ref[0].reshape(page_keys, HEAD_DIM).astype(BF16))
    m_ref[...] = m
    l_ref[...] = l
    acc_ref[...] = acc

    @pl.when(j == pl.num_programs(1) - 1)
    def _():
        o_ref[...] = (acc_ref[...] / l_ref[...]).reshape(nq, FOX_HEADS, HEAD_DIM)


def _fox_sample(page_table, q, k_new, v_new, c_new, k_pages, v_pages, w_pages, n_seq, nq, *, pages_per_step=8):
    n_pages = page_table.shape[1]
    rows = nq * FOX_HEADS
    pps = math.gcd(pages_per_step, n_pages)

    def page(i, ndim):
        return lambda b, j, pt: (pt[b * n_pages + (n_pages - 1 - j * pps - i)],) + (0,) * (ndim - 1)

    seq = pl.BlockSpec((nq, FOX_HEADS, HEAD_DIM), lambda b, j, pt: (b, 0, 0))
    kv_specs = [pl.BlockSpec((1, PAGE_SIZE, FOX_HEADS, HEAD_DIM), page(i, 4)) for i in range(pps)]
    w_specs = [pl.BlockSpec((1, 1, PAGE_SIZE * FOX_HEADS), page(i, 3)) for i in range(pps)]
    grid_spec = pltpu.PrefetchScalarGridSpec(
        num_scalar_prefetch=1,
        grid=(n_seq, n_pages // pps),
        in_specs=[seq, seq, seq, pl.BlockSpec((1, 1, rows), lambda b, j, pt: (b, 0, 0))]
        + kv_specs + kv_specs + w_specs,
        out_specs=seq,
        scratch_shapes=[
            pltpu.VMEM((rows, HEAD_DIM), BF16), pltpu.VMEM((rows, 1), F32), pltpu.VMEM((rows, 1), F32),
            pltpu.VMEM((rows, 1), F32), pltpu.VMEM((rows, 1), F32), pltpu.VMEM((rows, HEAD_DIM), F32),
        ],
    )
    return pl.pallas_call(
        functools.partial(_fox_sample_body, pps),
        grid_spec=grid_spec,
        out_shape=jax.ShapeDtypeStruct((n_seq * nq, FOX_HEADS, HEAD_DIM), F32),
        compiler_params=_params("parallel", "arbitrary"),
        name="fox_sample",
    )(page_table.reshape(-1), q, k_new, v_new, c_new, *([k_pages] * pps), *([v_pages] * pps), *([w_pages] * pps))


SSD_PAIRS = SSD_HEADS // 2
PAIRS_PER_GROUP = SSD_PAIRS // SSD_GROUPS


def _expander(width):
    n = SSD_HEADS * width
    h = lax.broadcasted_iota(jnp.int32, (LANES, n), 0)
    c = lax.broadcasted_iota(jnp.int32, (LANES, n), 1)
    return (lax.shift_right_logical(c, int(math.log2(width))) == h).astype(BF16)


def _ssd_local(xs, bm, cm, dt, alog, mask, tot_sel):
    n = xs.shape[0]
    lane = lax.broadcasted_iota(jnp.int32, (1, LANES), 1)
    dta = dt * jnp.where(lane < SSD_HEADS, -jnp.exp(alog), 0.0)
    e64 = _expander(SSD_HEAD_DIM)
    e128 = _expander(LANES)
    a_cum = _dot_sel_l(mask.astype(BF16), dta)
    a_tot = _dot_sel_l(tot_sel, dta)
    a_cum_t = a_cum.T
    ac_exp = _dot_sel_r(a_cum, e64)
    atot_exp = _dot_sel_r(a_tot, e64)
    ac_b = _dot_sel_r(a_cum, e128)
    xdt = xs * _dot_sel_r(dt, e64)
    half = lax.broadcasted_iota(jnp.int32, (n, LANES), 1) < SSD_HEAD_DIM
    out = {
        "xdtw": xdt * jnp.exp(atot_exp - ac_exp),
        "eac": jnp.exp(ac_exp),
        "atot_exp": atot_exp,
        "atot_b": _dot_sel_r(a_tot, e128),
        "bg": [], "cg": [], "y_diag": [],
    }
    for g in range(SSD_GROUPS):
        gs = slice(g * SSD_STATE, (g + 1) * SSD_STATE)
        bg = bm[:, gs].astype(BF16)
        cg = cm[:, gs].astype(BF16)
        out["bg"].append(bg)
        out["cg"].append(cg)
        cb = _dot_nt(cg, bg)
        for k in range(g * PAIRS_PER_GROUP, (g + 1) * PAIRS_PER_GROUP):
            ps = slice(k * LANES, (k + 1) * LANES)
            ms = []
            for h in (2 * k, 2 * k + 1):
                seg = ac_b[:, h * LANES:(h + 1) * LANES] - a_cum_t[h:h + 1, :]
                ms.append(cb * jnp.exp(jnp.where(mask, seg, -jnp.inf)))
            m_cat = jnp.concatenate(ms, axis=1).astype(BF16)
            xp = xdt[:, ps]
            x_bd = jnp.concatenate([jnp.where(half, xp, 0.0), jnp.where(half, 0.0, xp)], axis=0).astype(BF16)
            out["y_diag"].append(_dot(m_cat, x_bd))
    return out


def _conv_ssd_sample_body(seq_len, xbc_ref, dt_ref, buf_ref, h0_ref, w_ref, b_ref, alog_ref, dexp_ref,
                          y_ref, hout_ref, cout_ref, xp_ref):
    n = xbc_ref.shape[0]
    n_seq = n // seq_len
    taps = CONV_WIDTH - 1
    base = 8 - taps

    x = xbc_ref[...]
    xp_ref[:, base:8, :] = buf_ref[...]
    xp_ref[:, 8:8 + seq_len, :] = x.reshape(n_seq, seq_len, CONV_DIM)
    acc = b_ref[...] + x * w_ref[taps:taps + 1, :]
    for j in range(taps):
        acc = acc + xp_ref[:, base + j:base + j + seq_len, :].reshape(n, CONV_DIM) * w_ref[j:j + 1, :]
    cout_ref[...] = xp_ref[:, 8 + seq_len - taps:8 + seq_len, :]
    conv = _silu(acc)
    xs = conv[:, :SSD_WIDTH]
    bm = conv[:, SSD_WIDTH:SSD_WIDTH + SSD_GROUPS * SSD_STATE]
    cm = conv[:, SSD_WIDTH + SSD_GROUPS * SSD_STATE:]

    r = lax.broadcasted_iota(jnp.int32, (n, n), 0)
    c = lax.broadcasted_iota(jnp.int32, (n, n), 1)
    same_seq = (r // seq_len) == (c // seq_len)
    loc = _ssd_local(xs, bm, cm, dt_ref[...], alog_ref[...], jnp.logical_and(same_seq, c <= r),
                     same_seq.astype(BF16))

    gw = PAIRS_PER_GROUP * LANES
    seq_of_col = lax.broadcasted_iota(jnp.int32, (1, n), 1) // seq_len
    decay_t = jnp.exp(loc["atot_exp"]).T
    for g in range(SSD_GROUPS):
        gr = slice(g * gw, (g + 1) * gw)
        h_prev = h0_ref[:, gr, :]
        z = _dot_nt(h_prev.reshape(n_seq * gw, SSD_STATE).astype(BF16), loc["cg"][g])
        y_off_t = jnp.zeros((gw, n), F32)
        for s in range(n_seq):
            y_off_t = y_off_t + jnp.where(seq_of_col == s, z[s * gw:(s + 1) * gw, :], 0.0)
        y_off = y_off_t.T * loc["eac"][:, gr]
        y = jnp.concatenate(loc["y_diag"][g * PAIRS_PER_GROUP:(g + 1) * PAIRS_PER_GROUP], axis=1)
        y_ref[:, gr] = y + y_off + xs[:, gr] * dexp_ref[:, gr]
        xw_t = loc["xdtw"][:, gr].T
        lhs = jnp.concatenate([jnp.where(seq_of_col == s, xw_t, 0.0) for s in range(n_seq)], axis=0)
        s_new = _dot(lhs.astype(BF16), loc["bg"][g])
        for s in range(n_seq):
            col = decay_t[gr, s * seq_len:s * seq_len + 1]
            hout_ref[s, gr, :] = h_prev[s] * col + s_new[s * gw:(s + 1) * gw, :]


def _conv_ssd_sample(xbc, dt, conv_buf, h0, conv_w, conv_b, alog, d_exp, n_seq, seq_len):
    assert seq_len == 8 and CONV_WIDTH - 1 <= seq_len
    tile = LANES
    ts = tile // seq_len
    const = lambda i: (0, 0)
    per_s = lambda i: (i, 0, 0)
    return pl.pallas_call(
        functools.partial(_conv_ssd_sample_body, seq_len),
        grid=(n_seq // ts,),
        in_specs=[
            pl.BlockSpec((tile, CONV_DIM), lambda i: (i, 0)),
            pl.BlockSpec((tile, LANES), lambda i: (i, 0)),
            pl.BlockSpec((ts, CONV_WIDTH - 1, CONV_DIM), per_s),
            pl.BlockSpec((ts, SSD_WIDTH, SSD_STATE), per_s),
            pl.BlockSpec((CONV_WIDTH, CONV_DIM), const),
            pl.BlockSpec((1, CONV_DIM), const),
            pl.BlockSpec((1, LANES), const),
            pl.BlockSpec((1, SSD_WIDTH), const),
        ],
        out_specs=[
            pl.BlockSpec((tile, SSD_WIDTH), lambda i: (i, 0)),
            pl.BlockSpec((ts, SSD_WIDTH, SSD_STATE), per_s),
            pl.BlockSpec((ts, CONV_WIDTH - 1, CONV_DIM), per_s),
        ],
        out_shape=[
            jax.ShapeDtypeStruct((n_seq * seq_len, SSD_WIDTH), F32),
            jax.ShapeDtypeStruct((n_seq, SSD_WIDTH, SSD_STATE), F32),
            jax.ShapeDtypeStruct((n_seq, CONV_WIDTH - 1, CONV_DIM), F32),
        ],
        scratch_shapes=[pltpu.VMEM((ts, 8 + seq_len, CONV_DIM), F32)],
        compiler_params=_params("parallel"),
        name="conv_ssd_sample",
    )(xbc, dt, conv_buf, h0, conv_w, conv_b, alog, d_exp)


def _conv_ssd_prompt_body(xbc_ref, dt_ref, buf_ref, h0_ref, w_ref, b_ref, alog_ref, dexp_ref,
                          y_ref, hout_ref, cout_ref, state_ref, xp_ref):
    c = pl.program_id(1)
    nc = pl.num_programs(1)
    tl = xbc_ref.shape[0]
    taps = CONV_WIDTH - 1
    base = 8 - taps

    @pl.when(c == 0)
    def _():
        state_ref[...] = h0_ref[0]
        xp_ref[base:8, :] = buf_ref[0]

    x = xbc_ref[...]
    xp_ref[8:8 + tl, :] = x
    acc = b_ref[...] + x * w_ref[taps:taps + 1, :]
    for j in range(taps):
        acc = acc + xp_ref[base + j:base + j + tl, :] * w_ref[j:j + 1, :]
    xp_ref[base:8, :] = x[tl - taps:, :]
    conv = _silu(acc)
    xs = conv[:, :SSD_WIDTH]
    bm = conv[:, SSD_WIDTH:SSD_WIDTH + SSD_GROUPS * SSD_STATE]
    cm = conv[:, SSD_WIDTH + SSD_GROUPS * SSD_STATE:]

    causal = _tri(tl)
    loc = _ssd_local(xs, bm, cm, dt_ref[...], alog_ref[...], causal, jnp.ones((tl, tl), BF16))
    top = lax.broadcasted_iota(jnp.int32, (tl, LANES), 0) < SSD_HEAD_DIM
    for k in range(SSD_PAIRS):
        g = k // PAIRS_PER_GROUP
        ps = slice(k * LANES, (k + 1) * LANES)
        s_prev = state_ref[ps, :]
        y_off = _dot_nt(loc["cg"][g], s_prev.astype(BF16)) * loc["eac"][:, ps]
        y_ref[:, ps] = loc["y_diag"][k] + y_off + xs[:, ps] * dexp_ref[:, ps]
        cd = [jnp.exp(loc["atot_b"][0:1, h * LANES:(h + 1) * LANES]) for h in (2 * k, 2 * k + 1)]
        state_ref[ps, :] = s_prev * jnp.where(top, cd[0], cd[1]) + _dot_tn(loc["xdtw"][:, ps].astype(BF16), loc["bg"][g])

    @pl.when(c == nc - 1)
    def _():
        hout_ref[0] = state_ref[...]
        cout_ref[0] = xp_ref[base:8, :]


def _conv_ssd_prompt(xbc, dt, conv_buf, h0, conv_w, conv_b, alog, d_exp, b, l):
    tl = SSD_CHUNK
    nc = l // tl
    const = lambda bi, c: (0, 0)
    per_b = lambda bi, c: (bi, 0, 0)
    return pl.pallas_call(
        _conv_ssd_prompt_body,
        grid=(b, nc),
        in_specs=[
            pl.BlockSpec((tl, CONV_DIM), lambda bi, c: (bi * nc + c, 0)),
            pl.BlockSpec((tl, LANES), lambda bi, c: (bi * nc + c, 0)),
            pl.BlockSpec((1, CONV_WIDTH - 1, CONV_DIM), per_b),
            pl.BlockSpec((1, SSD_WIDTH, SSD_STATE), per_b),
            pl.BlockSpec((CONV_WIDTH, CONV_DIM), const),
            pl.BlockSpec((1, CONV_DIM), const),
            pl.BlockSpec((1, LANES), const),
            pl.BlockSpec((1, SSD_WIDTH), const),
        ],
        out_specs=[
            pl.BlockSpec((tl, SSD_WIDTH), lambda bi, c: (bi * nc + c, 0)),
            pl.BlockSpec((1, SSD_WIDTH, SSD_STATE), per_b),
            pl.BlockSpec((1, CONV_WIDTH - 1, CONV_DIM), per_b),
        ],
        out_shape=[
            jax.ShapeDtypeStruct((b * l, SSD_WIDTH), F32),
            jax.ShapeDtypeStruct((b, SSD_WIDTH, SSD_STATE), F32),
            jax.ShapeDtypeStruct((b, CONV_WIDTH - 1, CONV_DIM), F32),
        ],
        scratch_shapes=[pltpu.VMEM((SSD_WIDTH, SSD_STATE), F32), pltpu.VMEM((8 + tl, CONV_DIM), F32)],
        compiler_params=_params("parallel", "arbitrary"),
        name="conv_ssd_prompt",
    )(xbc, dt, conv_buf, h0, conv_w, conv_b, alog, d_exp)


def _out_proj_body(h_ref, fox_ref, y_ref, z_ref, gs_ref, wf_ref, ws_ref, gx_ref, wq_ref, gq_ref, o_ref, q_ref):
    yn = _rms(y_ref[...] * _silu(z_ref[...]), gs_ref[...]).astype(BF16)
    h = h_ref[...] + _dot(fox_ref[...].astype(BF16), wf_ref[...]) + _dot(yn, ws_ref[...])
    o_ref[...] = h
    q = _dot(_rms(h, gx_ref[...]).astype(BF16), wq_ref[...])
    q_ref[...] = _head_norm(q, gq_ref[...]) * (HEAD_DIM ** -0.5)


def _out_proj(h, fox, y, z, g_ssd, w_fox, w_ssd, g_x, wq, gq, *, tm=256):
    t, d = h.shape
    tm = min(tm, t)
    row = lambda w: pl.BlockSpec((tm, w), lambda i: (i, 0))
    full = lambda a: pl.BlockSpec(a.shape, lambda i: (0, 0))
    consts = [g_ssd.reshape(1, SSD_WIDTH), w_fox, w_ssd, g_x.reshape(1, d), wq, gq.reshape(1, HEAD_DIM)]
    return pl.pallas_call(
        _out_proj_body,
        grid=(t // tm,),
        in_specs=[row(d), row(FOX_WIDTH), row(SSD_WIDTH), row(SSD_WIDTH)] + [full(a) for a in consts],
        out_specs=[row(d), row(XATTN_WIDTH)],
        out_shape=[jax.ShapeDtypeStruct((t, d), F32), jax.ShapeDtypeStruct((t, XATTN_WIDTH), F32)],
        compiler_params=_params("parallel"),
        name="out_proj",
    )(h, fox, y, z, *consts)


def _mem_kv_body(m_ref, g_ref, w_ref, gk_ref, k_ref, v_ref):
    kv = _dot(_rms(m_ref[...], g_ref[...]).astype(BF16), w_ref[...])
    k_ref[...] = _head_norm(kv[:, :XATTN_WIDTH], gk_ref[...])
    v_ref[...] = kv[:, XATTN_WIDTH:]


def _mem_kv(mem, g, w_kv, gk, *, tm=256):
    t, d = mem.shape
    row = lambda w: pl.BlockSpec((tm, w), lambda i: (i, 0))
    full = lambda a: pl.BlockSpec(a.shape, lambda i: (0, 0))
    consts = [g.reshape(1, d), w_kv, gk.reshape(1, HEAD_DIM)]
    return pl.pallas_call(
        _mem_kv_body,
        grid=(t // tm,),
        in_specs=[row(d)] + [full(a) for a in consts],
        out_specs=[row(XATTN_WIDTH), row(XATTN_WIDTH)],
        out_shape=[jax.ShapeDtypeStruct((t, XATTN_WIDTH), F32)] * 2,
        compiler_params=_params("parallel"),
        name="mem_kv",
    )(mem, *consts)


def _xattn_body(q_ref, k_ref, v_ref, o_ref):
    for h in range(XATTN_HEADS):
        hs = slice(h * HEAD_DIM, (h + 1) * HEAD_DIM)
        s = _dot_nt(q_ref[:, hs].astype(BF16), k_ref[0, :, hs].astype(BF16))
        p = jnp.exp(s - jnp.max(s, axis=-1, keepdims=True))
        o = _dot(p.astype(BF16), v_ref[0, :, hs].astype(BF16))
        o_ref[:, hs] = o / jnp.sum(p, axis=-1, keepdims=True)


def _xattn(q, mem_k, mem_v, b, l, *, tq=512):
    tq = min(tq, l)
    nq = l // tq
    n_mem = mem_k.shape[1]
    qspec = pl.BlockSpec((tq, XATTN_WIDTH), lambda bi, i: (bi * nq + i, 0))
    mspec = pl.BlockSpec((1, n_mem, XATTN_WIDTH), lambda bi, i: (bi, 0, 0))
    return pl.pallas_call(
        _xattn_body,
        grid=(b, nq),
        in_specs=[qspec, mspec, mspec],
        out_specs=qspec,
        out_shape=jax.ShapeDtypeStruct((b * l, XATTN_WIDTH), F32),
        compiler_params=_params("parallel", "arbitrary"),
        name="xattn",
    )(q, mem_k, mem_v)


def _xattn_rows_body(n_seq, q_ref, k_ref, v_ref, o_ref):
    tiles = q_ref.shape[0] // n_seq
    rows = tiles * 8
    keys = k_ref.shape[1] * k_ref.shape[2]
    row_head = lax.broadcasted_iota(jnp.int32, (rows, 1), 0) & (XATTN_HEADS - 1)
    key_head = lax.broadcasted_iota(jnp.int32, (1, keys), 1) & (XATTN_HEADS - 1)
    own_head = row_head == key_head
    for s in range(n_seq):
        qs = slice(s * tiles, (s + 1) * tiles)
        q = q_ref[qs].reshape(rows, HEAD_DIM).astype(BF16)
        sc = jnp.where(own_head, _dot_nt(q, k_ref[s].reshape(keys, HEAD_DIM).astype(BF16)), -jnp.inf)
        p = jnp.exp(sc - jnp.max(sc, axis=-1, keepdims=True))
        o = _dot(p.astype(BF16), v_ref[s].reshape(keys, HEAD_DIM).astype(BF16))
        o_ref[qs] = (o / jnp.sum(p, axis=-1, keepdims=True)).reshape(tiles, 8, HEAD_DIM)


def _xattn_rows(q, mem_k, mem_v, n_seq, *, ts=8):
    tiles = q.shape[0] // n_seq
    ts = math.gcd(ts, n_seq)
    qspec = pl.BlockSpec((ts * tiles, 8, HEAD_DIM), lambda i: (i, 0, 0))
    mspec = pl.BlockSpec((ts,) + mem_k.shape[1:], lambda i: (i, 0, 0, 0))
    return pl.pallas_call(
        functools.partial(_xattn_rows_body, ts),
        grid=(n_seq // ts,),
        in_specs=[qspec, mspec, mspec],
        out_specs=qspec,
        out_shape=jax.ShapeDtypeStruct(q.shape, F32),
        compiler_params=_params("parallel"),
        name="xattn_rows",
    )(q, mem_k, mem_v)


def kernel(x_prompt, x_sample, cache_fox_k, cache_fox_v, cache_fox_logf, cache_mem_k, cache_mem_v, state_ssm, state_conv, page_table, mem_prompt, ffn1_norm, ffn1_w_gate, ffn1_w_up, ffn1_w_down, mix_norm, w_in, fox_b_f, fox_q_norm, fox_k_norm, conv_w, conv_b, ssd_dt_bias, ssd_A_log, ssd_D, ssd_out_norm, w_out, xattn_norm, mem_norm, xattn_w_q, xattn_w_kv, xattn_q_norm, xattn_k_norm, xattn_w_o, ffn2_norm, ffn2_w_gate, ffn2_w_up, ffn2_w_down):
    assert x_prompt.shape[2] == D_MODEL and ffn1_norm.shape[0] == 1
    d = D_MODEL
    bp, lp = x_prompt.shape[:2]
    bs, ls = x_sample.shape[:2]
    n_mem = mem_prompt.shape[1]
    n_pool = cache_fox_k.shape[1]

    bf = lambda w: w[0].astype(BF16)
    ffn1 = (ffn1_norm[0], bf(ffn1_w_gate), bf(ffn1_w_up), bf(ffn1_w_down))
    ffn2 = (ffn2_norm[0], bf(ffn2_w_gate), bf(ffn2_w_up), bf(ffn2_w_down))
    w_main, w_small, b_small = _pack_in_proj(w_in[0], fox_b_f[0], ssd_dt_bias[0])
    w_fox, w_ssd = bf(w_out)[:FOX_WIDTH], bf(w_out)[FOX_WIDTH:]
    wq, wkv, wo = bf(xattn_w_q), bf(xattn_w_kv), bf(xattn_w_o)
    alog = jnp.pad(ssd_A_log[0], (0, LANES - SSD_HEADS)).reshape(1, LANES)
    d_exp = jnp.repeat(ssd_D[0], SSD_HEAD_DIM).reshape(1, SSD_WIDTH)
    cw, cb = conv_w[0], conv_b[0].reshape(1, CONV_DIM)

    def front(x, q_dtype):
        h1 = _ffn(x, *ffn1)
        return h1, _in_proj(h1, mix_norm[0], w_main, w_small, b_small, fox_q_norm[0], fox_k_norm[0], q_dtype=q_dtype)

    def back(h1, fox, y, z, attend):
        h2, xq = _out_proj(h1, fox, y, z, ssd_out_norm[0], w_fox, w_ssd, xattn_norm[0], wq, xattn_q_norm[0])
        return _ffn(h2, *ffn2, pre=(attend(xq), wo))

    h1, (q, k_p, kb, v_p, vb, z, xbc, logf_p, dt) = front(x_prompt.reshape(bp * lp, d), BF16)
    ccol, crow = _cumsum(logf_p.reshape(bp, lp, LANES))
    tq = min(ATT_BLOCK, lp)
    fox = _fox_prompt(q, kb, vb, ccol, crow.reshape(bp, FOX_HEADS, lp // tq, tq), bp, lp)
    y, ssm_p, conv_p = _conv_ssd_prompt(
        xbc, dt, jnp.zeros((bp, CONV_WIDTH - 1, CONV_DIM), F32), jnp.zeros((bp, SSD_WIDTH, SSD_STATE), F32),
        cw, cb, alog, d_exp, bp, lp)
    mk, mv = _mem_kv(mem_prompt.reshape(bp * n_mem, d), mem_norm[0], wkv, xattn_k_norm[0])
    y_prompt = back(h1, fox, y, z, lambda xq: _xattn(
        xq, mk.reshape(bp, n_mem, XATTN_WIDTH), mv.reshape(bp, n_mem, XATTN_WIDTH), bp, lp))

    h1, (q, k_s, _, v_s, _, z, xbc, logf_s, dt) = front(x_sample.reshape(bs * ls, d), F32)
    w_pages = _page_cumsum(cache_fox_logf[0].reshape(n_pool, PAGE_SIZE * FOX_HEADS))
    heads = lambda a: a.reshape(bs * ls, FOX_HEADS, HEAD_DIM)
    c_new = _seq_cumsum(logf_s, ls)[:, :FOX_HEADS].reshape(bs, 1, ls * FOX_HEADS)
    fox = _fox_sample(
        page_table, heads(q), heads(k_s), heads(v_s), c_new, cache_fox_k[0], cache_fox_v[0],
        w_pages.reshape(n_pool, 1, PAGE_SIZE * FOX_HEADS), bs, ls).reshape(bs * ls, FOX_WIDTH)
    y, ssm_s, conv_s = _conv_ssd_sample(
        xbc, dt, state_conv[0], state_ssm[0].reshape(bs, SSD_WIDTH, SSD_STATE), cw, cb, alog, d_exp, bs, ls)
    mem_tiles = lambda m: m[0].reshape(bs, n_mem * XATTN_HEADS // 8, 8, HEAD_DIM)
    y_sample = back(h1, fox, y, z, lambda xq: _xattn_rows(
        xq.reshape(bs * ls * XATTN_HEADS // 8, 8, HEAD_DIM), mem_tiles(cache_mem_k), mem_tiles(cache_mem_v), bs,
    ).reshape(bs * ls, XATTN_WIDTH))

    fox_shape = lambda b, l: (1, b, l, FOX_HEADS, HEAD_DIM)
    ssm_shape = lambda b: (1, b, SSD_HEADS, SSD_HEAD_DIM, SSD_STATE)
    mem_shape = (1, bp, n_mem, XATTN_HEADS, HEAD_DIM)
    return (
        y_prompt.reshape(bp, lp, d), y_sample.reshape(bs, ls, d),
        k_p.reshape(fox_shape(bp, lp)), v_p.reshape(fox_shape(bp, lp)),
        logf_p[:, :FOX_HEADS].reshape(1, bp, lp, FOX_HEADS),
        ssm_p.reshape(ssm_shape(bp)), conv_p[None], mk.reshape(mem_shape), mv.reshape(mem_shape),
        k_s.reshape(fox_shape(bs, ls)), v_s.reshape(fox_shape(bs, ls)),
        logf_s[:, :FOX_HEADS].reshape(1, bs, ls, FOX_HEADS),
        ssm_s.reshape(ssm_shape(bs)), conv_s[None],
    )
```

```python
import functools
import math

import jax
import jax.numpy as jnp
from jax import lax
from jax.experimental import pallas as pl
from jax.experimental.pallas import tpu as pltpu

F32 = jnp.float32
BF16 = jnp.bfloat16

EPS = 1e-6
FFN_RESIDUAL = 0.5
D_MODEL = 2048
D_FF = 5632
PAGE_SIZE = 128
FOX_HEADS = 8
HEAD_DIM = 128
FOX_WIDTH = FOX_HEADS * HEAD_DIM
SSD_HEADS = 16
SSD_HEAD_DIM = 64
SSD_WIDTH = SSD_HEADS * SSD_HEAD_DIM
SSD_GROUPS = 2
SSD_STATE = 128
SSD_CHUNK = 128
CONV_WIDTH = 4
CONV_DIM = SSD_WIDTH + 2 * SSD_GROUPS * SSD_STATE
XATTN_HEADS = 4
XATTN_WIDTH = XATTN_HEADS * HEAD_DIM
LANES = 128
VMEM_LIMIT_BYTES = 56 * 1024 * 1024


def _params(*semantics):
    return pltpu.CompilerParams(dimension_semantics=semantics, vmem_limit_bytes=VMEM_LIMIT_BYTES)


def _rms(x, g):
    return x * lax.rsqrt(jnp.mean(x * x, axis=-1, keepdims=True) + EPS) * g


def _dot(a, b):
    return jnp.dot(a, b, preferred_element_type=F32)


def _dot_nt(a, b):
    return lax.dot_general(a, b, (((1,), (1,)), ((), ())), preferred_element_type=F32)


def _dot_tn(a, b):
    return lax.dot_general(a, b, (((0,), (0,)), ((), ())), preferred_element_type=F32)


def _split3(x):
    x1 = x.astype(BF16)
    r = x - x1.astype(F32)
    x2 = r.astype(BF16)
    x3 = (r - x2.astype(F32)).astype(BF16)
    return x1, x2, x3


def _dot_sel_l(sel, x):
    x1, x2, x3 = _split3(x)
    return _dot(sel, x1) + _dot(sel, x2) + _dot(sel, x3)


def _dot_sel_r(x, sel):
    x1, x2, x3 = _split3(x)
    return _dot(x1, sel) + _dot(x2, sel) + _dot(x3, sel)


def _silu(x):
    return x * jax.nn.sigmoid(x)


def _ffn_body(pre_proj, *refs):
    if pre_proj:
        x_ref, a_ref, wo_ref, g_ref, wg_ref, wu_ref, wd_ref, o_ref, xn_ref = refs
    else:
        x_ref, g_ref, wg_ref, wu_ref, wd_ref, o_ref, xn_ref = refs

    @pl.when(pl.program_id(1) == 0)
    def _():
        x = x_ref[...]
        if pre_proj:
            x = x + _dot(a_ref[...].astype(BF16), wo_ref[...])
        xn_ref[...] = _rms(x, g_ref[...]).astype(BF16)
        o_ref[...] = x

    xn = xn_ref[...]
    gate = _dot(xn, wg_ref[...])
    up = _dot(xn, wu_ref[...])
    h = (_silu(gate) * up * FFN_RESIDUAL).astype(BF16)
    o_ref[...] += _dot(h, wd_ref[...])


def _ffn(x, g, wg, wu, wd, pre=None, *, tm=512, tf=512):
    t, d = x.shape
    f = wg.shape[1]
    tm = min(tm, t)
    grid = (t // tm, f // tf)
    row = lambda i, j: (i, 0)
    in_specs = [pl.BlockSpec((tm, d), row)]
    args = [x]
    if pre is not None:
        a, wo = pre
        in_specs += [pl.BlockSpec((tm, a.shape[1]), row), pl.BlockSpec(wo.shape, lambda i, j: (0, 0))]
        args += [a, wo]
    in_specs += [
        pl.BlockSpec((1, d), lambda i, j: (0, 0)),
        pl.BlockSpec((d, tf), lambda i, j: (0, j)),
        pl.BlockSpec((d, tf), lambda i, j: (0, j)),
        pl.BlockSpec((tf, d), lambda i, j: (j, 0)),
    ]
    args += [g.reshape(1, d), wg, wu, wd]
    return pl.pallas_call(
        functools.partial(_ffn_body, pre is not None),
        grid=grid,
        in_specs=in_specs,
        out_specs=pl.BlockSpec((tm, d), row),
        out_shape=jax.ShapeDtypeStruct((t, d), F32),
        scratch_shapes=[pltpu.VMEM((tm, d), BF16)],
        compiler_params=_params("parallel", "arbitrary"),
        name="ffn_pre" if pre is not None else "ffn",
    )(*args)


IN_TN = 512
IN_BLOCKS = {"q": (0, 2), "k": (2, 2), "v": (4, 2), "z": (6, 2), "xbc": (8, 3)}
IN_NBLK = 11


def _head_norm(y, g):
    outs = []
    for c in range(y.shape[1] // HEAD_DIM):
        yc = y[:, c * HEAD_DIM:(c + 1) * HEAD_DIM]
        outs.append(yc * lax.rsqrt(jnp.mean(yc * yc, axis=-1, keepdims=True) + EPS) * g)
    return jnp.concatenate(outs, axis=1)


def _in_proj_body(x_ref, g_ref, wa_ref, wb_ref, ws_ref, bs_ref, gq_ref, gk_ref,
                  q_ref, k_ref, kb_ref, v_ref, vb_ref, z_ref, xbc_ref, logf_ref, dt_ref, u_ref):
    j = pl.program_id(1)

    @pl.when(j == 0)
    def _():
        u = _rms(x_ref[...], g_ref[...]).astype(BF16)
        u_ref[...] = u
        s = _dot(u, ws_ref[...]) + bs_ref[...]
        t = jnp.log1p(jnp.exp(-jnp.abs(s)))
        logf_ref[...] = (jnp.minimum(s, 0.0) - t)[:, :LANES]
        dt_ref[...] = (jnp.maximum(s, 0.0) + t)[:, LANES:]

    def in_range(name):
        lo, n = IN_BLOCKS[name]
        return jnp.logical_and(j >= lo, j < lo + n)

    project = lambda: _dot(u_ref[...], wa_ref[...])
    project_b = lambda: _dot(u_ref[...], wb_ref[...])

    @pl.when(in_range("q"))
    def _():
        q_ref[...] = (_head_norm(project(), gq_ref[...]) * (HEAD_DIM ** -0.5)).astype(q_ref.dtype)

    @pl.when(in_range("k"))
    def _():
        kn = _head_norm(project(), gk_ref[...])
        k_ref[...] = kn
        kb_ref[...] = kn.astype(BF16)

    @pl.when(in_range("v"))
    def _():
        y = project()
        v_ref[...] = y
        vb_ref[...] = y.astype(BF16)

    @pl.when(in_range("z"))
    def _():
        z_ref[...] = project_b()

    @pl.when(in_range("xbc"))
    def _():
        xbc_ref[...] = project_b()


def _in_proj(x, g, w_qkv, w_zx, w_small, b_small, gq, gk, *, q_dtype=BF16, tm=512):
    t, d = x.shape
    tm = min(tm, t)
    tn = IN_TN
    nb_a = w_qkv.shape[1] // tn
    nb_b = w_zx.shape[1] // tn
    assert nb_a + nb_b == IN_NBLK

    def col(name):
        lo, n = IN_BLOCKS[name]
        return pl.BlockSpec((tm, tn), lambda i, j: (i, jnp.clip(j - lo, 0, n - 1)))

    const = lambda i, j: (0, 0)
    slab = pl.BlockSpec((tm, LANES), lambda i, j: (i, 0))
    outs = [
        ("q", q_dtype, FOX_WIDTH), ("k", F32, FOX_WIDTH), ("k", BF16, FOX_WIDTH), ("v", F32, FOX_WIDTH),
        ("v", BF16, FOX_WIDTH), ("z", F32, SSD_WIDTH), ("xbc", F32, CONV_DIM),
    ]
    return pl.pallas_call(
        _in_proj_body,
        grid=(t // tm, IN_NBLK),
        in_specs=[
            pl.BlockSpec((tm, d), lambda i, j: (i, 0)),
            pl.BlockSpec((1, d), const),
            pl.BlockSpec((d, tn), lambda i, j: (0, jnp.minimum(j, nb_a - 1))),
            pl.BlockSpec((d, tn), lambda i, j: (0, jnp.clip(j - nb_a, 0, nb_b - 1))),
            pl.BlockSpec((d, 2 * LANES), const),
            pl.BlockSpec((1, 2 * LANES), const),
            pl.BlockSpec((1, HEAD_DIM), const),
            pl.BlockSpec((1, HEAD_DIM), const),
        ],
        out_specs=[col(n) for n, _, _ in outs] + [slab, slab],
        out_shape=[jax.ShapeDtypeStruct((t, w), dt) for _, dt, w in outs]
        + [jax.ShapeDtypeStruct((t, LANES), F32)] * 2,
        scratch_shapes=[pltpu.VMEM((tm, d), BF16)],
        compiler_params=_params("parallel", "arbitrary"),
        name="in_proj",
    )(x, g.reshape(1, d), w_qkv, w_zx, w_small, b_small, gq.reshape(1, HEAD_DIM), gk.reshape(1, HEAD_DIM))


def _pack_in_proj(w_in, fox_b_f, ssd_dt_bias):
    fw = FOX_WIDTH
    f0 = 3 * fw
    z0 = f0 + FOX_HEADS
    x0 = z0 + SSD_WIDTH
    d0 = x0 + CONV_DIM
    w_qkv = w_in[:, :f0].astype(BF16)
    w_zx = w_in[:, z0:d0].astype(BF16)
    zeros = lambda n: jnp.zeros((w_in.shape[0], n), w_in.dtype)
    w_small = jnp.concatenate(
        [w_in[:, f0:z0], zeros(LANES - FOX_HEADS), w_in[:, d0:], zeros(LANES - SSD_HEADS)], axis=1).astype(BF16)
    b_small = jnp.concatenate(
        [fox_b_f, jnp.zeros((LANES - FOX_HEADS,), F32), ssd_dt_bias, jnp.zeros((LANES - SSD_HEADS,), F32)]
    ).reshape(1, 2 * LANES)
    return w_qkv, w_zx, w_small, b_small


ATT_BLOCK = 256


def _tri(n, *, strict=False, upper=False):
    r = lax.broadcasted_iota(jnp.int32, (n, n), 0)
    c = lax.broadcasted_iota(jnp.int32, (n, n), 1)
    if upper:
        r, c = c, r
    return (c < r) if strict else (c <= r)


def _cumsum_body(x_ref, col_ref, row_ref, carry_ref):
    @pl.when(pl.program_id(1) == 0)
    def _():
        carry_ref[...] = jnp.zeros_like(carry_ref)

    n = x_ref.shape[1]
    tril = _tri(n).astype(BF16)
    c = _dot_sel_l(tril, x_ref[0]) + carry_ref[...]
    carry_ref[...] = c[n - 1:n, :]
    col_ref[0] = c
    row_ref[0] = c.T[:FOX_HEADS, :]


def _cumsum(x, *, tb=ATT_BLOCK):
    b, l, _ = x.shape
    return pl.pallas_call(
        _cumsum_body,
        grid=(b, l // tb),
        in_specs=[pl.BlockSpec((1, tb, LANES), lambda i, j: (i, j, 0))],
        out_specs=[pl.BlockSpec((1, tb, LANES), lambda i, j: (i, j, 0)),
                   pl.BlockSpec((1, FOX_HEADS, tb), lambda i, j: (i, 0, j))],
        out_shape=[jax.ShapeDtypeStruct((b, l, LANES), F32), jax.ShapeDtypeStruct((b, FOX_HEADS, l), F32)],
        scratch_shapes=[pltpu.VMEM((1, LANES), F32)],
        compiler_params=_params("parallel", "arbitrary"),
        name="logf_cumsum",
    )(x)


def _fox_prompt_body(tk, q_ref, k_ref, v_ref, ccol_ref, crow_ref, o_ref, m_ref, l_ref, cq_ref, acc_ref):
    i = pl.program_id(1)
    tq = q_ref.shape[0]
    rep = tk // LANES
    m_ref[...] = jnp.full_like(m_ref, -jnp.inf)
    l_ref[...] = jnp.zeros_like(l_ref)
    acc_ref[...] = jnp.zeros_like(acc_ref)
    for h in range(FOX_HEADS):
        cq_ref[h] = jnp.broadcast_to(ccol_ref[0, :, h:h + 1], (tq, LANES))
    row = i * tq + lax.broadcasted_iota(jnp.int32, (tq, tk), 0)
    col = lax.broadcasted_iota(jnp.int32, (tq, tk), 1)
    wide = lambda x: jnp.concatenate([x] * rep, axis=1)

    def block(j, masked):
        ks = pl.ds(pl.multiple_of(j * tk, tk), tk)
        for h in range(FOX_HEADS):
            hs = slice(h * HEAD_DIM, (h + 1) * HEAD_DIM)
            s = _dot_nt(q_ref[:, hs], k_ref[ks, hs]) + (wide(cq_ref[h]) - crow_ref[0, h, pl.ds(j, 1), :])
            if masked:
                s = jnp.where(col + j * tk <= row, s, -jnp.inf)
            m_old = m_ref[h]
            m_new = jnp.maximum(m_old, jnp.max(s, axis=-1, keepdims=True))
            alpha = jnp.exp(m_old - m_new)
            p = jnp.exp(s - wide(m_new))
            m_ref[h] = m_new
            l_ref[h] = alpha * l_ref[h] + jnp.sum(p, axis=-1, keepdims=True)
            acc_ref[h] = alpha * acc_ref[h] + _dot(p.astype(BF16), v_ref[ks, hs])

    last = (i * tq + tq - 1) // tk

    def step(j, carry):
        block(j, False)
        return carry

    lax.fori_loop(0, last, step, 0)
    block(last, True)
    for h in range(FOX_HEADS):
        o_ref[:, h * HEAD_DIM:(h + 1) * HEAD_DIM] = (acc_ref[h] / l_ref[h]).astype(o_ref.dtype)


def _fox_prompt(q, k, v, ccol, crow, b, l, *, tq=128):
    tk = crow.shape[-1]
    tq = min(tq, l)
    assert tk % tq == 0
    nq = l // tq
    w = FOX_WIDTH
    return pl.pallas_call(
        functools.partial(_fox_prompt_body, tk),
        grid=(b, nq),
        in_specs=[
            pl.BlockSpec((tq, w), lambda bi, i: (bi * nq + i, 0)),
            pl.BlockSpec((l, w), lambda bi, i: (bi, 0)),
            pl.BlockSpec((l, w), lambda bi, i: (bi, 0)),
            pl.BlockSpec((1, tq, LANES), lambda bi, i: (bi, i, 0)),
            pl.BlockSpec((1, FOX_HEADS, l // tk, tk), lambda bi, i: (bi, 0, 0, 0)),
        ],
        out_specs=pl.BlockSpec((tq, w), lambda bi, i: (bi * nq + i, 0)),
        out_shape=jax.ShapeDtypeStruct((b * l, w), BF16),
        scratch_shapes=[pltpu.VMEM((FOX_HEADS, tq, LANES), F32)] * 3 + [pltpu.VMEM((FOX_HEADS, tq, HEAD_DIM), F32)],
        compiler_params=_params("parallel", "arbitrary"),
        name="fox_prompt",
    )(q, k, v, ccol, crow)


def _page_cumsum_body(x_ref, w_ref, m_ref):
    n = PAGE_SIZE * FOX_HEADS

    @pl.when(pl.program_id(0) == 0)
    def _():
        r = lax.broadcasted_iota(jnp.int32, (n, n), 0)
        c = lax.broadcasted_iota(jnp.int32, (n, n), 1)
        same_head = (r & (FOX_HEADS - 1)) == (c & (FOX_HEADS - 1))
        earlier = lax.shift_right_logical(r, 3) <= lax.shift_right_logical(c, 3)
        m_ref[...] = jnp.logical_and(same_head, earlier).astype(BF16)

    w_ref[...] = _dot_sel_r(x_ref[...], m_ref[...])


def _page_cumsum(logf_pages, *, tb=256):
    n_pool, n = logf_pages.shape
    return pl.pallas_call(
        _page_cumsum_body,
        grid=(n_pool // tb,),
        in_specs=[pl.BlockSpec((tb, n), lambda i: (i, 0))],
        out_specs=pl.BlockSpec((tb, n), lambda i: (i, 0)),
        out_shape=jax.ShapeDtypeStruct((n_pool, n), F32),
        scratch_shapes=[pltpu.VMEM((n, n), BF16)],
        compiler_params=_params("arbitrary"),
        name="page_cumsum",
    )(logf_pages)


def _seq_cumsum_body(seq_len, x_ref, o_ref):
    n = x_ref.shape[0]
    r = lax.broadcasted_iota(jnp.int32, (n, n), 0)
    c = lax.broadcasted_iota(jnp.int32, (n, n), 1)
    same_seq = (r // seq_len) == (c // seq_len)
    o_ref[...] = _dot_sel_l(jnp.logical_and(same_seq, c <= r).astype(BF16), x_ref[...])


def _seq_cumsum(x, seq_len, *, tb=128):
    t = x.shape[0]
    return pl.pallas_call(
        functools.partial(_seq_cumsum_body, seq_len),
        grid=(t // tb,),
        in_specs=[pl.BlockSpec((tb, LANES), lambda i: (i, 0))],
        out_specs=pl.BlockSpec((tb, LANES), lambda i: (i, 0)),
        out_shape=jax.ShapeDtypeStruct((t, LANES), F32),
        compiler_params=_params("parallel"),
        name="seq_cumsum",
    )(x)


def _fox_sample_body(pps, pt_ref, q_ref, kn_ref, vn_ref, cn_ref, *refs):
    kp_refs, vp_refs, w_refs = refs[:pps], refs[pps:2 * pps], refs[2 * pps:3 * pps]
    o_ref, q_scr, colq_ref, toff_ref, m_ref, l_ref, acc_ref = refs[3 * pps:]
    j = pl.program_id(1)
    nq = q_ref.shape[0]
    rows = nq * FOX_HEADS
    page_keys = PAGE_SIZE * FOX_HEADS
    row_id = lax.broadcasted_iota(jnp.int32, (rows, 1), 0)
    head_of_row = row_id & (FOX_HEADS - 1)
    query_of_row = lax.shift_right_logical(row_id, 3)

    @pl.when(j == 0)
    def _():
        q = q_ref[...].reshape(rows, HEAD_DIM).astype(BF16)
        q_scr[...] = q
        cn = cn_ref[0]
        key = lax.broadcasted_iota(jnp.int32, (1, rows), 1)
        colq = jnp.sum(jnp.where(key == row_id, cn, 0.0), axis=-1, keepdims=True)
        colq_ref[...] = colq
        toff_ref[...] = jnp.zeros_like(toff_ref)
        s = _dot_nt(q, kn_ref[...].reshape(rows, HEAD_DIM).astype(BF16)) + colq - cn
        valid = jnp.logical_and((key & (FOX_HEADS - 1)) == head_of_row,
                                lax.shift_right_logical(key, 3) <= query_of_row)
        s = jnp.where(valid, s, -jnp.inf)
        m = jnp.max(s, axis=-1, keepdims=True)
        p = jnp.exp(s - m)
        m_ref[...] = m
        l_ref[...] = jnp.sum(p, axis=-1, keepdims=True)
        acc_ref[...] = _dot(p.astype(BF16), vn_ref[...].reshape(rows, HEAD_DIM).astype(BF16))

    lane = lax.broadcasted_iota(jnp.int32, (1, LANES), 1)
    own_head = (lax.broadcasted_iota(jnp.int32, (1, page_keys), 1) & (FOX_HEADS - 1)) == head_of_row
    q = q_scr[...]
    colq = colq_ref[...]
    toff = toff_ref[...]
    tiles = []
    for kp_ref, w_ref in zip(kp_refs, w_refs):
        w = w_ref[0]
        last = jnp.where(lane == LANES - FOX_HEADS + head_of_row, w[:, page_keys - LANES:], 0.0)
        toff = toff + jnp.sum(last, axis=-1, keepdims=True)
        s = _dot_nt(q, kp_ref[0].reshape(page_keys, HEAD_DIM).astype(BF16))
        tiles.append(jnp.where(own_head, s + (colq + toff) - w, -jnp.inf))
    toff_ref[...] = toff
    m_old = m_ref[...]
    m = m_old
    for s in tiles:
        m = jnp.maximum(m, jnp.max(s, axis=-1, keepdims=True))
    alpha = jnp.exp(m_old - m)
    l = alpha * l_ref[...]
    acc = alpha * acc_ref[...]
    for s, vp_ref in zip(tiles, vp_refs):
        p = jnp.exp(s - m)
        l = l + jnp.sum(p, axis=-1, keepdims=True)
        acc = acc + _dot(p.astype(BF16), vp_ref[0].reshape(page_keys, HEAD_DIM).astype(BF16))
    m_ref[...] = m
    l_ref[...] = l
    acc_ref[...] = acc

    @pl.when(j == pl.num_programs(1) - 1)
    def _():
        o_ref[...] = (acc_ref[...] / l_ref[...]).reshape(nq, FOX_HEADS, HEAD_DIM)


def _fox_sample(page_table, q, k_new, v_new, c_new, k_pages, v_pages, w_pages, n_seq, nq, *, pages_per_step=16):
    n_pages = page_table.shape[1]
    rows = nq * FOX_HEADS
    pps = math.gcd(pages_per_step, n_pages)

    def page(i, ndim):
        return lambda b, j, pt: (pt[b * n_pages + (n_pages - 1 - j * pps - i)],) + (0,) * (ndim - 1)

    seq = pl.BlockSpec((nq, FOX_HEADS, HEAD_DIM), lambda b, j, pt: (b, 0, 0))
    kv_specs = [pl.BlockSpec((1, PAGE_SIZE, FOX_HEADS, HEAD_DIM), page(i, 4)) for i in range(pps)]
    w_specs = [pl.BlockSpec((1, 1, PAGE_SIZE * FOX_HEADS), page(i, 3)) for i in range(pps)]
    grid_spec = pltpu.PrefetchScalarGridSpec(
        num_scalar_prefetch=1,
        grid=(n_seq, n_pages // pps),
        in_specs=[seq, seq, seq, pl.BlockSpec((1, 1, rows), lambda b, j, pt: (b, 0, 0))]
        + kv_specs + kv_specs + w_specs,
        out_specs=seq,
        scratch_shapes=[
            pltpu.VMEM((rows, HEAD_DIM), BF16), pltpu.VMEM((rows, 1), F32), pltpu.VMEM((rows, 1), F32),
            pltpu.VMEM((rows, 1), F32), pltpu.VMEM((rows, 1), F32), pltpu.VMEM((rows, HEAD_DIM), F32),
        ],
    )
    return pl.pallas_call(
        functools.partial(_fox_sample_body, pps),
        grid_spec=grid_spec,
        out_shape=jax.ShapeDtypeStruct((n_seq * nq, FOX_HEADS, HEAD_DIM), F32),
        compiler_params=_params("parallel", "arbitrary"),
        name="fox_sample",
    )(page_table.reshape(-1), q, k_new, v_new, c_new, *([k_pages] * pps), *([v_pages] * pps), *([w_pages] * pps))


SSD_PAIRS = SSD_HEADS // 2
PAIRS_PER_GROUP = SSD_PAIRS // SSD_GROUPS


def _expander(width):
    n = SSD_HEADS * width
    h = lax.broadcasted_iota(jnp.int32, (LANES, n), 0)
    c = lax.broadcasted_iota(jnp.int32, (LANES, n), 1)
    return (lax.shift_right_logical(c, int(math.log2(width))) == h).astype(BF16)


def _ssd_local(xs, bm, cm, dt, alog, mask, tot_sel):
    n = xs.shape[0]
    lane = lax.broadcasted_iota(jnp.int32, (1, LANES), 1)
    dta = dt * jnp.where(lane < SSD_HEADS, -jnp.exp(alog), 0.0)
    e64 = _expander(SSD_HEAD_DIM)
    a_cum = _dot_sel_l(mask.astype(BF16), dta)
    a_cum_t = a_cum.T
    ac_exp = _dot_sel_r(a_cum, e64)
    if tot_sel is None:
        atot_exp = ac_exp[n - 1:n, :]
    else:
        atot_exp = _dot_sel_r(_dot_sel_l(tot_sel, dta), e64)
    ac_b = _dot_sel_r(a_cum, _expander(LANES))
    xdt = xs * _dot_sel_r(dt, e64)
    half = lax.broadcasted_iota(jnp.int32, (n, LANES), 1) < SSD_HEAD_DIM
    out = {
        "xdtw": xdt * jnp.exp(atot_exp - ac_exp),
        "eac": jnp.exp(ac_exp),
        "atot_exp": atot_exp,
        "ac_b": ac_b,
        "bg": [], "cg": [], "y_diag": [],
    }
    for g in range(SSD_GROUPS):
        gs = slice(g * SSD_STATE, (g + 1) * SSD_STATE)
        bg = bm[:, gs].astype(BF16)
        cg = cm[:, gs].astype(BF16)
        out["bg"].append(bg)
        out["cg"].append(cg)
        cb = _dot_nt(cg, bg)
        for k in range(g * PAIRS_PER_GROUP, (g + 1) * PAIRS_PER_GROUP):
            ps = slice(k * LANES, (k + 1) * LANES)
            ms = []
            for h in (2 * k, 2 * k + 1):
                seg = ac_b[:, h * LANES:(h + 1) * LANES] - a_cum_t[h:h + 1, :]
                ms.append(cb * jnp.exp(jnp.where(mask, seg, -jnp.inf)))
            m_cat = jnp.concatenate(ms, axis=1).astype(BF16)
            xp = xdt[:, ps]
            x_bd = jnp.concatenate([jnp.where(half, xp, 0.0), jnp.where(half, 0.0, xp)], axis=0).astype(BF16)
            out["y_diag"].append(_dot(m_cat, x_bd))
    return out


def _conv_ssd_sample_body(seq_len, xbc_ref, dt_ref, buf_ref, h0_ref, w_ref, b_ref, alog_ref, dexp_ref,
                          y_ref, hout_ref, cout_ref, xp_ref):
    n = xbc_ref.shape[0]
    n_seq = n // seq_len
    taps = CONV_WIDTH - 1
    base = 8 - taps

    x = xbc_ref[...]
    xp_ref[:, base:8, :] = buf_ref[...]
    xp_ref[:, 8:8 + seq_len, :] = x.reshape(n_seq, seq_len, CONV_DIM)
    acc = b_ref[...] + x * w_ref[taps:taps + 1, :]
    for j in range(taps):
        acc = acc + xp_ref[:, base + j:base + j + seq_len, :].reshape(n, CONV_DIM) * w_ref[j:j + 1, :]
    cout_ref[...] = xp_ref[:, 8 + seq_len - taps:8 + seq_len, :]
    conv = _silu(acc)
    xs = conv[:, :SSD_WIDTH]
    bm = conv[:, SSD_WIDTH:SSD_WIDTH + SSD_GROUPS * SSD_STATE]
    cm = conv[:, SSD_WIDTH + SSD_GROUPS * SSD_STATE:]

    r = lax.broadcasted_iota(jnp.int32, (n, n), 0)
    c = lax.broadcasted_iota(jnp.int32, (n, n), 1)
    same_seq = (r // seq_len) == (c // seq_len)
    loc = _ssd_local(xs, bm, cm, dt_ref[...], alog_ref[...], jnp.logical_and(same_seq, c <= r),
                     same_seq.astype(BF16))

    gw = PAIRS_PER_GROUP * LANES
    seq_of_col = lax.broadcasted_iota(jnp.int32, (1, n), 1) // seq_len
    decay_t = jnp.exp(loc["atot_exp"]).T
    for g in range(SSD_GROUPS):
        gr = slice(g * gw, (g + 1) * gw)
        h_prev = h0_ref[:, gr, :]
        z = _dot_nt(h_prev.reshape(n_seq * gw, SSD_STATE).astype(BF16), loc["cg"][g])
        y_off_t = jnp.zeros((gw, n), F32)
        for s in range(n_seq):
            y_off_t = y_off_t + jnp.where(seq_of_col == s, z[s * gw:(s + 1) * gw, :], 0.0)
        y_off = y_off_t.T * loc["eac"][:, gr]
        y = jnp.concatenate(loc["y_diag"][g * PAIRS_PER_GROUP:(g + 1) * PAIRS_PER_GROUP], axis=1)
        y_ref[:, gr] = y + y_off + xs[:, gr] * dexp_ref[:, gr]
        xw_t = loc["xdtw"][:, gr].T
        lhs = jnp.concatenate([jnp.where(seq_of_col == s, xw_t, 0.0) for s in range(n_seq)], axis=0)
        s_new = _dot(lhs.astype(BF16), loc["bg"][g])
        for s in range(n_seq):
            col = decay_t[gr, s * seq_len:s * seq_len + 1]
            hout_ref[s, gr, :] = h_prev[s] * col + s_new[s * gw:(s + 1) * gw, :]


def _conv_ssd_sample(xbc, dt, conv_buf, h0, conv_w, conv_b, alog, d_exp, n_seq, seq_len):
    assert seq_len == 8 and CONV_WIDTH - 1 <= seq_len
    tile = LANES
    ts = tile // seq_len
    const = lambda i: (0, 0)
    per_s = lambda i: (i, 0, 0)
    return pl.pallas_call(
        functools.partial(_conv_ssd_sample_body, seq_len),
        grid=(n_seq // ts,),
        in_specs=[
            pl.BlockSpec((tile, CONV_DIM), lambda i: (i, 0)),
            pl.BlockSpec((tile, LANES), lambda i: (i, 0)),
            pl.BlockSpec((ts, CONV_WIDTH - 1, CONV_DIM), per_s),
            pl.BlockSpec((ts, SSD_WIDTH, SSD_STATE), per_s),
            pl.BlockSpec((CONV_WIDTH, CONV_DIM), const),
            pl.BlockSpec((1, CONV_DIM), const),
            pl.BlockSpec((1, LANES), const),
            pl.BlockSpec((1, SSD_WIDTH), const),
        ],
        out_specs=[
            pl.BlockSpec((tile, SSD_WIDTH), lambda i: (i, 0)),
            pl.BlockSpec((ts, SSD_WIDTH, SSD_STATE), per_s),
            pl.BlockSpec((ts, CONV_WIDTH - 1, CONV_DIM), per_s),
        ],
        out_shape=[
            jax.ShapeDtypeStruct((n_seq * seq_len, SSD_WIDTH), F32),
            jax.ShapeDtypeStruct((n_seq, SSD_WIDTH, SSD_STATE), F32),
            jax.ShapeDtypeStruct((n_seq, CONV_WIDTH - 1, CONV_DIM), F32),
        ],
        scratch_shapes=[pltpu.VMEM((ts, 8 + seq_len, CONV_DIM), F32)],
        compiler_params=_params("parallel"),
        name="conv_ssd_sample",
    )(xbc, dt, conv_buf, h0, conv_w, conv_b, alog, d_exp)


def _conv_ssd_prompt_body(xbc_ref, dt_ref, buf_ref, h0_ref, w_ref, b_ref, alog_ref, dexp_ref,
                          y_ref, hout_ref, cout_ref, state_ref, xp_ref):
    c = pl.program_id(1)
    nc = pl.num_programs(1)
    tl = xbc_ref.shape[0]
    taps = CONV_WIDTH - 1
    base = 8 - taps

    @pl.when(c == 0)
    def _():
        state_ref[...] = h0_ref[0]
        xp_ref[base:8, :] = buf_ref[0]

    x = xbc_ref[...]
    xp_ref[8:8 + tl, :] = x
    acc = b_ref[...] + x * w_ref[taps:taps + 1, :]
    for j in range(taps):
        acc = acc + xp_ref[base + j:base + j + tl, :] * w_ref[j:j + 1, :]
    xp_ref[base:8, :] = x[tl - taps:, :]
    conv = _silu(acc)
    xs = conv[:, :SSD_WIDTH]
    bm = conv[:, SSD_WIDTH:SSD_WIDTH + SSD_GROUPS * SSD_STATE]
    cm = conv[:, SSD_WIDTH + SSD_GROUPS * SSD_STATE:]

    causal = _tri(tl)
    loc = _ssd_local(xs, bm, cm, dt_ref[...], alog_ref[...], causal, None)
    top = lax.broadcasted_iota(jnp.int32, (tl, LANES), 0) < SSD_HEAD_DIM
    for k in range(SSD_PAIRS):
        g = k // PAIRS_PER_GROUP
        ps = slice(k * LANES, (k + 1) * LANES)
        s_prev = state_ref[ps, :]
        y_off = _dot_nt(loc["cg"][g], s_prev.astype(BF16)) * loc["eac"][:, ps]
        y_ref[:, ps] = loc["y_diag"][k] + y_off + xs[:, ps] * dexp_ref[:, ps]
        cd = [jnp.exp(loc["ac_b"][tl - 1:tl, h * LANES:(h + 1) * LANES]) for h in (2 * k, 2 * k + 1)]
        state_ref[ps, :] = s_prev * jnp.where(top, cd[0], cd[1]) + _dot_tn(loc["xdtw"][:, ps].astype(BF16), loc["bg"][g])

    @pl.when(c == nc - 1)
    def _():
        hout_ref[0] = state_ref[...]
        cout_ref[0] = xp_ref[base:8, :]


def _conv_ssd_prompt(xbc, dt, conv_buf, h0, conv_w, conv_b, alog, d_exp, b, l):
    tl = SSD_CHUNK
    nc = l // tl
    const = lambda bi, c: (0, 0)
    per_b = lambda bi, c: (bi, 0, 0)
    return pl.pallas_call(
        _conv_ssd_prompt_body,
        grid=(b, nc),
        in_specs=[
            pl.BlockSpec((tl, CONV_DIM), lambda bi, c: (bi * nc + c, 0)),
            pl.BlockSpec((tl, LANES), lambda bi, c: (bi * nc + c, 0)),
            pl.BlockSpec((1, CONV_WIDTH - 1, CONV_DIM), per_b),
            pl.BlockSpec((1, SSD_WIDTH, SSD_STATE), per_b),
            pl.BlockSpec((CONV_WIDTH, CONV_DIM), const),
            pl.BlockSpec((1, CONV_DIM), const),
            pl.BlockSpec((1, LANES), const),
            pl.BlockSpec((1, SSD_WIDTH), const),
        ],
        out_specs=[
            pl.BlockSpec((tl, SSD_WIDTH), lambda bi, c: (bi * nc + c, 0)),
            pl.BlockSpec((1, SSD_WIDTH, SSD_STATE), per_b),
            pl.BlockSpec((1, CONV_WIDTH - 1, CONV_DIM), per_b),
        ],
        out_shape=[
            jax.ShapeDtypeStruct((b * l, SSD_WIDTH), F32),
            jax.ShapeDtypeStruct((b, SSD_WIDTH, SSD_STATE), F32),
            jax.ShapeDtypeStruct((b, CONV_WIDTH - 1, CONV_DIM), F32),
        ],
        scratch_shapes=[pltpu.VMEM((SSD_WIDTH, SSD_STATE), F32), pltpu.VMEM((8 + tl, CONV_DIM), F32)],
        compiler_params=_params("parallel", "arbitrary"),
        name="conv_ssd_prompt",
    )(xbc, dt, conv_buf, h0, conv_w, conv_b, alog, d_exp)


def _out_proj_body(h_ref, fox_ref, y_ref, z_ref, gs_ref, wf_ref, ws_ref, gx_ref, wq_ref, gq_ref, o_ref, q_ref):
    yn = _rms(y_ref[...] * _silu(z_ref[...]), gs_ref[...]).astype(BF16)
    h = h_ref[...] + _dot(fox_ref[...].astype(BF16), wf_ref[...]) + _dot(yn, ws_ref[...])
    o_ref[...] = h
    q = _dot(_rms(h, gx_ref[...]).astype(BF16), wq_ref[...])
    q_ref[...] = _head_norm(q, gq_ref[...]) * (HEAD_DIM ** -0.5)


def _out_proj(h, fox, y, z, g_ssd, w_fox, w_ssd, g_x, wq, gq, *, tm=256):
    t, d = h.shape
    tm = min(tm, t)
    row = lambda w: pl.BlockSpec((tm, w), lambda i: (i, 0))
    full = lambda a: pl.BlockSpec(a.shape, lambda i: (0, 0))
    consts = [g_ssd.reshape(1, SSD_WIDTH), w_fox, w_ssd, g_x.reshape(1, d), wq, gq.reshape(1, HEAD_DIM)]
    return pl.pallas_call(
        _out_proj_body,
        grid=(t // tm,),
        in_specs=[row(d), row(FOX_WIDTH), row(SSD_WIDTH), row(SSD_WIDTH)] + [full(a) for a in consts],
        out_specs=[row(d), row(XATTN_WIDTH)],
        out_shape=[jax.ShapeDtypeStruct((t, d), F32), jax.ShapeDtypeStruct((t, XATTN_WIDTH), F32)],
        compiler_params=_params("parallel"),
        name="out_proj",
    )(h, fox, y, z, *consts)


def _mem_kv_body(m_ref, g_ref, w_ref, gk_ref, k_ref, v_ref):
    kv = _dot(_rms(m_ref[...], g_ref[...]).astype(BF16), w_ref[...])
    k_ref[...] = _head_norm(kv[:, :XATTN_WIDTH], gk_ref[...])
    v_ref[...] = kv[:, XATTN_WIDTH:]


def _mem_kv(mem, g, w_kv, gk, *, tm=256):
    t, d = mem.shape
    row = lambda w: pl.BlockSpec((tm, w), lambda i: (i, 0))
    full = lambda a: pl.BlockSpec(a.shape, lambda i: (0, 0))
    consts = [g.reshape(1, d), w_kv, gk.reshape(1, HEAD_DIM)]
    return pl.pallas_call(
        _mem_kv_body,
        grid=(t // tm,),
        in_specs=[row(d)] + [full(a) for a in consts],
        out_specs=[row(XATTN_WIDTH), row(XATTN_WIDTH)],
        out_shape=[jax.ShapeDtypeStruct((t, XATTN_WIDTH), F32)] * 2,
        compiler_params=_params("parallel"),
        name="mem_kv",
    )(mem, *consts)


def _xattn_body(q_ref, k_ref, v_ref, o_ref):
    for h in range(XATTN_HEADS):
        hs = slice(h * HEAD_DIM, (h + 1) * HEAD_DIM)
        s = _dot_nt(q_ref[:, hs].astype(BF16), k_ref[0, :, hs].astype(BF16))
        p = jnp.exp(s - jnp.max(s, axis=-1, keepdims=True))
        o = _dot(p.astype(BF16), v_ref[0, :, hs].astype(BF16))
        o_ref[:, hs] = o / jnp.sum(p, axis=-1, keepdims=True)


def _xattn(q, mem_k, mem_v, b, l, *, tq=512):
    tq = min(tq, l)
    nq = l // tq
    n_mem = mem_k.shape[1]
    qspec = pl.BlockSpec((tq, XATTN_WIDTH), lambda bi, i: (bi * nq + i, 0))
    mspec = pl.BlockSpec((1, n_mem, XATTN_WIDTH), lambda bi, i: (bi, 0, 0))
    return pl.pallas_call(
        _xattn_body,
        grid=(b, nq),
        in_specs=[qspec, mspec, mspec],
        out_specs=qspec,
        out_shape=jax.ShapeDtypeStruct((b * l, XATTN_WIDTH), F32),
        compiler_params=_params("parallel", "arbitrary"),
        name="xattn",
    )(q, mem_k, mem_v)


def _xattn_rows_body(n_seq, q_ref, k_ref, v_ref, o_ref):
    tiles = q_ref.shape[0] // n_seq
    rows = tiles * 8
    keys = k_ref.shape[1] * k_ref.shape[2]
    row_head = lax.broadcasted_iota(jnp.int32, (rows, 1), 0) & (XATTN_HEADS - 1)
    key_head = lax.broadcasted_iota(jnp.int32, (1, keys), 1) & (XATTN_HEADS - 1)
    own_head = row_head == key_head
    for s in range(n_seq):
        qs = slice(s * tiles, (s + 1) * tiles)
        q = q_ref[qs].reshape(rows, HEAD_DIM).astype(BF16)
        sc = jnp.where(own_head, _dot_nt(q, k_ref[s].reshape(keys, HEAD_DIM).astype(BF16)), -jnp.inf)
        p = jnp.exp(sc - jnp.max(sc, axis=-1, keepdims=True))
        o = _dot(p.astype(BF16), v_ref[s].reshape(keys, HEAD_DIM).astype(BF16))
        o_ref[qs] = (o / jnp.sum(p, axis=-1, keepdims=True)).reshape(tiles, 8, HEAD_DIM)


def _xattn_rows(q, mem_k, mem_v, n_seq, *, ts=8):
    tiles = q.shape[0] // n_seq
    ts = math.gcd(ts, n_seq)
    qspec = pl.BlockSpec((ts * tiles, 8, HEAD_DIM), lambda i: (i, 0, 0))
    mspec = pl.BlockSpec((ts,) + mem_k.shape[1:], lambda i: (i, 0, 0, 0))
    return pl.pallas_call(
        functools.partial(_xattn_rows_body, ts),
        grid=(n_seq // ts,),
        in_specs=[qspec, mspec, mspec],
        out_specs=qspec,
        out_shape=jax.ShapeDtypeStruct(q.shape, F32),
        compiler_params=_params("parallel"),
        name="xattn_rows",
    )(q, mem_k, mem_v)


def kernel(x_prompt, x_sample, cache_fox_k, cache_fox_v, cache_fox_logf, cache_mem_k, cache_mem_v, state_ssm, state_conv, page_table, mem_prompt, ffn1_norm, ffn1_w_gate, ffn1_w_up, ffn1_w_down, mix_norm, w_in, fox_b_f, fox_q_norm, fox_k_norm, conv_w, conv_b, ssd_dt_bias, ssd_A_log, ssd_D, ssd_out_norm, w_out, xattn_norm, mem_norm, xattn_w_q, xattn_w_kv, xattn_q_norm, xattn_k_norm, xattn_w_o, ffn2_norm, ffn2_w_gate, ffn2_w_up, ffn2_w_down):
    assert x_prompt.shape[2] == D_MODEL and ffn1_norm.shape[0] == 1
    d = D_MODEL
    bp, lp = x_prompt.shape[:2]
    bs, ls = x_sample.shape[:2]
    n_mem = mem_prompt.shape[1]
    n_pool = cache_fox_k.shape[1]

    bf = lambda w: w[0].astype(BF16)
    ffn1 = (ffn1_norm[0], bf(ffn1_w_gate), bf(ffn1_w_up), bf(ffn1_w_down))
    ffn2 = (ffn2_norm[0], bf(ffn2_w_gate), bf(ffn2_w_up), bf(ffn2_w_down))
    in_w = _pack_in_proj(w_in[0], fox_b_f[0], ssd_dt_bias[0])
    w_fox, w_ssd = bf(w_out)[:FOX_WIDTH], bf(w_out)[FOX_WIDTH:]
    wq, wkv, wo = bf(xattn_w_q), bf(xattn_w_kv), bf(xattn_w_o)
    alog = jnp.pad(ssd_A_log[0], (0, LANES - SSD_HEADS)).reshape(1, LANES)
    d_exp = jnp.repeat(ssd_D[0], SSD_HEAD_DIM).reshape(1, SSD_WIDTH)
    cw, cb = conv_w[0], conv_b[0].reshape(1, CONV_DIM)

    def front(x, q_dtype):
        h1 = _ffn(x, *ffn1)
        return h1, _in_proj(h1, mix_norm[0], *in_w, fox_q_norm[0], fox_k_norm[0], q_dtype=q_dtype)

    def back(h1, fox, y, z, attend):
        h2, xq = _out_proj(h1, fox, y, z, ssd_out_norm[0], w_fox, w_ssd, xattn_norm[0], wq, xattn_q_norm[0])
        return _ffn(h2, *ffn2, pre=(attend(xq), wo))

    h1, (q, k_p, kb, v_p, vb, z, xbc, logf_p, dt) = front(x_prompt.reshape(bp * lp, d), BF16)
    ccol, crow = _cumsum(logf_p.reshape(bp, lp, LANES))
    tq = min(ATT_BLOCK, lp)
    fox = _fox_prompt(q, kb, vb, ccol, crow.reshape(bp, FOX_HEADS, lp // tq, tq), bp, lp)
    y, ssm_p, conv_p = _conv_ssd_prompt(
        xbc, dt, jnp.zeros((bp, CONV_WIDTH - 1, CONV_DIM), F32), jnp.zeros((bp, SSD_WIDTH, SSD_STATE), F32),
        cw, cb, alog, d_exp, bp, lp)
    mk, mv = _mem_kv(mem_prompt.reshape(bp * n_mem, d), mem_norm[0], wkv, xattn_k_norm[0])
    y_prompt = back(h1, fox, y, z, lambda xq: _xattn(
        xq, mk.reshape(bp, n_mem, XATTN_WIDTH), mv.reshape(bp, n_mem, XATTN_WIDTH), bp, lp))

    h1, (q, k_s, _, v_s, _, z, xbc, logf_s, dt) = front(x_sample.reshape(bs * ls, d), F32)
    w_pages = _page_cumsum(cache_fox_logf[0].reshape(n_pool, PAGE_SIZE * FOX_HEADS))
    heads = lambda a: a.reshape(bs * ls, FOX_HEADS, HEAD_DIM)
    c_new = _seq_cumsum(logf_s, ls)[:, :FOX_HEADS].reshape(bs, 1, ls * FOX_HEADS)
    fox = _fox_sample(
        page_table, heads(q), heads(k_s), heads(v_s), c_new, cache_fox_k[0], cache_fox_v[0],
        w_pages.reshape(n_pool, 1, PAGE_SIZE * FOX_HEADS), bs, ls).reshape(bs * ls, FOX_WIDTH)
    y, ssm_s, conv_s = _conv_ssd_sample(
        xbc, dt, state_conv[0], state_ssm[0].reshape(bs, SSD_WIDTH, SSD_STATE), cw, cb, alog, d_exp, bs, ls)
    mem_tiles = lambda m: m[0].reshape(bs, n_mem * XATTN_HEADS // 8, 8, HEAD_DIM)
    y_sample = back(h1, fox, y, z, lambda xq: _xattn_rows(
        xq.reshape(bs * ls * XATTN_HEADS // 8, 8, HEAD_DIM), mem_tiles(cache_mem_k), mem_tiles(cache_mem_v), bs,
    ).reshape(bs * ls, XATTN_WIDTH))

    fox_shape = lambda b, l: (1, b, l, FOX_HEADS, HEAD_DIM)
    ssm_shape = lambda b: (1, b, SSD_HEADS, SSD_HEAD_DIM, SSD_STATE)
    mem_shape = (1, bp, n_mem, XATTN_HEADS, HEAD_DIM)
    return (
        y_prompt.reshape(bp, lp, d), y_sample.reshape(bs, ls, d),
        k_p.reshape(fox_shape(bp, lp)), v_p.reshape(fox_shape(bp, lp)),
        logf_p[:, :FOX_HEADS].reshape(1, bp, lp, FOX_HEADS),
        ssm_p.reshape(ssm_shape(bp)), conv_p[None], mk.reshape(mem_shape), mv.reshape(mem_shape),
        k_s.reshape(fox_shape(bs, ls)), v_s.reshape(fox_shape(bs, ls)),
        logf_s[:, :FOX_HEADS].reshape(1, bs, ls, FOX_HEADS),
        ssm_s.reshape(ssm_shape(bs)), conv_s[None],
    )
```

```python
import functools
import math

import jax
import jax.numpy as jnp
from jax import lax
from jax.experimental import pallas as pl
from jax.experimental.pallas import tpu as pltpu

F32 = jnp.float32
BF16 = jnp.bfloat16

EPS = 1e-6
FFN_RESIDUAL = 0.5
D_MODEL = 2048
D_FF = 5632
PAGE_SIZE = 128
FOX_HEADS = 8
HEAD_DIM = 128
FOX_WIDTH = FOX_HEADS * HEAD_DIM
SSD_HEADS = 16
SSD_HEAD_DIM = 64
SSD_WIDTH = SSD_HEADS * SSD_HEAD_DIM
SSD_GROUPS = 2
SSD_STATE = 128
SSD_CHUNK = 128
CONV_WIDTH = 4
CONV_DIM = SSD_WIDTH + 2 * SSD_GROUPS * SSD_STATE
XATTN_HEADS = 4
XATTN_WIDTH = XATTN_HEADS * HEAD_DIM
LANES = 128
VMEM_LIMIT_BYTES = 56 * 1024 * 1024


def _params(*semantics):
    return pltpu.CompilerParams(dimension_semantics=semantics, vmem_limit_bytes=VMEM_LIMIT_BYTES)


def _rms(x, g):
    return x * lax.rsqrt(jnp.mean(x * x, axis=-1, keepdims=True) + EPS) * g


def _dot(a, b):
    return jnp.dot(a, b, preferred_element_type=F32)


def _dot_nt(a, b):
    return lax.dot_general(a, b, (((1,), (1,)), ((), ())), preferred_element_type=F32)


def _dot_tn(a, b):
    return lax.dot_general(a, b, (((0,), (0,)), ((), ())), preferred_element_type=F32)


def _split3(x):
    x1 = x.astype(BF16)
    r = x - x1.astype(F32)
    x2 = r.astype(BF16)
    x3 = (r - x2.astype(F32)).astype(BF16)
    return x1, x2, x3


def _dot_sel_l(sel, x):
    x1, x2, x3 = _split3(x)
    return _dot(sel, x1) + _dot(sel, x2) + _dot(sel, x3)


def _dot_sel_r(x, sel):
    x1, x2, x3 = _split3(x)
    return _dot(x1, sel) + _dot(x2, sel) + _dot(x3, sel)


def _silu(x):
    return x * jax.nn.sigmoid(x)


def _ffn_body(pre_proj, *refs):
    if pre_proj:
        x_ref, a_ref, wo_ref, g_ref, wg_ref, wu_ref, wd_ref, o_ref, xn_ref = refs
    else:
        x_ref, g_ref, wg_ref, wu_ref, wd_ref, o_ref, xn_ref = refs

    @pl.when(pl.program_id(1) == 0)
    def _():
        x = x_ref[...]
        if pre_proj:
            x = x + _dot(a_ref[...].astype(BF16), wo_ref[...])
        xn_ref[...] = _rms(x, g_ref[...]).astype(BF16)
        o_ref[...] = x

    xn = xn_ref[...]
    gate = _dot(xn, wg_ref[0])
    up = _dot(xn, wu_ref[0])
    h = (_silu(gate) * up * FFN_RESIDUAL).astype(BF16)
    o_ref[...] += _dot(h, wd_ref[...])


FFN_TF = 512


def _col_blocks(w, tn):
    k, n = w.shape
    return w.reshape(k, n // tn, tn).transpose(1, 0, 2).astype(BF16)


def _ffn(x, g, wg, wu, wd, pre=None, *, tm=512):
    t, d = x.shape
    nf, _, tf = wg.shape
    tm = min(tm, t)
    grid = (t // tm, nf)
    row = lambda i, j: (i, 0)
    in_specs = [pl.BlockSpec((tm, d), row)]
    args = [x]
    if pre is not None:
        a, wo = pre
        in_specs += [pl.BlockSpec((tm, a.shape[1]), row), pl.BlockSpec(wo.shape, lambda i, j: (0, 0))]
        args += [a, wo]
    in_specs += [
        pl.BlockSpec((1, d), lambda i, j: (0, 0)),
        pl.BlockSpec((1, d, tf), lambda i, j: (j, 0, 0)),
        pl.BlockSpec((1, d, tf), lambda i, j: (j, 0, 0)),
        pl.BlockSpec((tf, d), lambda i, j: (j, 0)),
    ]
    args += [g.reshape(1, d), wg, wu, wd]
    return pl.pallas_call(
        functools.partial(_ffn_body, pre is not None),
        grid=grid,
        in_specs=in_specs,
        out_specs=pl.BlockSpec((tm, d), row),
        out_shape=jax.ShapeDtypeStruct((t, d), F32),
        scratch_shapes=[pltpu.VMEM((tm, d), BF16)],
        compiler_params=_params("parallel", "arbitrary"),
        name="ffn_pre" if pre is not None else "ffn",
    )(*args)


IN_TN = 512
IN_BLOCKS = {"q": (0, 2), "k": (2, 2), "v": (4, 2), "z": (6, 2), "xbc": (8, 3)}
IN_NBLK = 11


def _head_norm(y, g):
    outs = []
    for c in range(y.shape[1] // HEAD_DIM):
        yc = y[:, c * HEAD_DIM:(c + 1) * HEAD_DIM]
        outs.append(yc * lax.rsqrt(jnp.mean(yc * yc, axis=-1, keepdims=True) + EPS) * g)
    return jnp.concatenate(outs, axis=1)


def _in_proj_body(x_ref, g_ref, wa_ref, wb_ref, ws_ref, bs_ref, gq_ref, gk_ref,
                  q_ref, k_ref, kb_ref, v_ref, vb_ref, z_ref, xbc_ref, logf_ref, dt_ref, u_ref):
    j = pl.program_id(1)

    @pl.when(j == 0)
    def _():
        u = _rms(x_ref[...], g_ref[...]).astype(BF16)
        u_ref[...] = u
        s = _dot(u, ws_ref[...]) + bs_ref[...]
        t = jnp.log1p(jnp.exp(-jnp.abs(s)))
        logf_ref[...] = (jnp.minimum(s, 0.0) - t)[:, :LANES]
        dt_ref[...] = (jnp.maximum(s, 0.0) + t)[:, LANES:]

    def in_range(name):
        lo, n = IN_BLOCKS[name]
        return jnp.logical_and(j >= lo, j < lo + n)

    project = lambda: _dot(u_ref[...], wa_ref[0])
    project_b = lambda: _dot(u_ref[...], wb_ref[0])

    @pl.when(in_range("q"))
    def _():
        q_ref[...] = (_head_norm(project(), gq_ref[...]) * (HEAD_DIM ** -0.5)).astype(q_ref.dtype)

    @pl.when(in_range("k"))
    def _():
        kn = _head_norm(project(), gk_ref[...])
        k_ref[...] = kn
        kb_ref[...] = kn.astype(BF16)

    @pl.when(in_range("v"))
    def _():
        y = project()
        v_ref[...] = y
        vb_ref[...] = y.astype(BF16)

    @pl.when(in_range("z"))
    def _():
        z_ref[...] = project_b()

    @pl.when(in_range("xbc"))
    def _():
        xbc_ref[...] = project_b()


def _in_proj(x, g, w_qkv, w_zx, w_small, b_small, gq, gk, *, q_dtype=BF16, tm=512):
    t, d = x.shape
    tm = min(tm, t)
    tn = IN_TN
    nb_a = w_qkv.shape[0]
    nb_b = w_zx.shape[0]
    assert nb_a + nb_b == IN_NBLK and w_qkv.shape[2] == tn

    def col(name):
        lo, n = IN_BLOCKS[name]
        return pl.BlockSpec((tm, tn), lambda i, j: (i, jnp.clip(j - lo, 0, n - 1)))

    const = lambda i, j: (0, 0)
    slab = pl.BlockSpec((tm, LANES), lambda i, j: (i, 0))
    outs = [
        ("q", q_dtype, FOX_WIDTH), ("k", F32, FOX_WIDTH), ("k", BF16, FOX_WIDTH), ("v", F32, FOX_WIDTH),
        ("v", BF16, FOX_WIDTH), ("z", F32, SSD_WIDTH), ("xbc", F32, CONV_DIM),
    ]
    return pl.pallas_call(
        _in_proj_body,
        grid=(t // tm, IN_NBLK),
        in_specs=[
            pl.BlockSpec((tm, d), lambda i, j: (i, 0)),
            pl.BlockSpec((1, d), const),
            pl.BlockSpec((1, d, tn), lambda i, j: (jnp.minimum(j, nb_a - 1), 0, 0)),
            pl.BlockSpec((1, d, tn), lambda i, j: (jnp.clip(j - nb_a, 0, nb_b - 1), 0, 0)),
            pl.BlockSpec((d, 2 * LANES), const),
            pl.BlockSpec((1, 2 * LANES), const),
            pl.BlockSpec((1, HEAD_DIM), const),
            pl.BlockSpec((1, HEAD_DIM), const),
        ],
        out_specs=[col(n) for n, _, _ in outs] + [slab, slab],
        out_shape=[jax.ShapeDtypeStruct((t, w), dt) for _, dt, w in outs]
        + [jax.ShapeDtypeStruct((t, LANES), F32)] * 2,
        scratch_shapes=[pltpu.VMEM((tm, d), BF16)],
        compiler_params=_params("parallel", "arbitrary"),
        name="in_proj",
    )(x, g.reshape(1, d), w_qkv, w_zx, w_small, b_small, gq.reshape(1, HEAD_DIM), gk.reshape(1, HEAD_DIM))


def _pack_in_proj(w_in, fox_b_f, ssd_dt_bias):
    fw = FOX_WIDTH
    f0 = 3 * fw
    z0 = f0 + FOX_HEADS
    x0 = z0 + SSD_WIDTH
    d0 = x0 + CONV_DIM
    w_qkv = _col_blocks(w_in[:, :f0], IN_TN)
    w_zx = _col_blocks(w_in[:, z0:d0], IN_TN)
    zeros = lambda n: jnp.zeros((w_in.shape[0], n), w_in.dtype)
    w_small = jnp.concatenate(
        [w_in[:, f0:z0], zeros(LANES - FOX_HEADS), w_in[:, d0:], zeros(LANES - SSD_HEADS)], axis=1).astype(BF16)
    b_small = jnp.concatenate(
        [fox_b_f, jnp.zeros((LANES - FOX_HEADS,), F32), ssd_dt_bias, jnp.zeros((LANES - SSD_HEADS,), F32)]
    ).reshape(1, 2 * LANES)
    return w_qkv, w_zx, w_small, b_small


ATT_BLOCK = 256


def _tri(n, *, strict=False, upper=False):
    r = lax.broadcasted_iota(jnp.int32, (n, n), 0)
    c = lax.broadcasted_iota(jnp.int32, (n, n), 1)
    if upper:
        r, c = c, r
    return (c < r) if strict else (c <= r)


def _cumsum_body(x_ref, col_ref, row_ref, carry_ref):
    @pl.when(pl.program_id(1) == 0)
    def _():
        carry_ref[...] = jnp.zeros_like(carry_ref)

    n = x_ref.shape[1]
    tril = _tri(n).astype(BF16)
    c = _dot_sel_l(tril, x_ref[0]) + carry_ref[...]
    carry_ref[...] = c[n - 1:n, :]
    col_ref[0] = c
    row_ref[0] = c.T[:FOX_HEADS, :]


def _cumsum(x, *, tb=ATT_BLOCK):
    b, l, _ = x.shape
    return pl.pallas_call(
        _cumsum_body,
        grid=(b, l // tb),
        in_specs=[pl.BlockSpec((1, tb, LANES), lambda i, j: (i, j, 0))],
        out_specs=[pl.BlockSpec((1, tb, LANES), lambda i, j: (i, j, 0)),
                   pl.BlockSpec((1, FOX_HEADS, tb), lambda i, j: (i, 0, j))],
        out_shape=[jax.ShapeDtypeStruct((b, l, LANES), F32), jax.ShapeDtypeStruct((b, FOX_HEADS, l), F32)],
        scratch_shapes=[pltpu.VMEM((1, LANES), F32)],
        compiler_params=_params("parallel", "arbitrary"),
        name="logf_cumsum",
    )(x)


def _fox_prompt_body(tk, q_ref, k_ref, v_ref, ccol_ref, crow_ref, o_ref, m_ref, l_ref, cq_ref, acc_ref):
    i = pl.program_id(1)
    tq = q_ref.shape[0]
    rep = tk // LANES
    m_ref[...] = jnp.full_like(m_ref, -jnp.inf)
    l_ref[...] = jnp.zeros_like(l_ref)
    acc_ref[...] = jnp.zeros_like(acc_ref)
    for h in range(FOX_HEADS):
        cq_ref[h] = jnp.broadcast_to(ccol_ref[0, :, h:h + 1], (tq, LANES))
    row = i * tq + lax.broadcasted_iota(jnp.int32, (tq, tk), 0)
    col = lax.broadcasted_iota(jnp.int32, (tq, tk), 1)
    wide = lambda x: jnp.concatenate([x] * rep, axis=1)

    def block(j, masked):
        ks = pl.ds(pl.multiple_of(j * tk, tk), tk)
        for h in range(FOX_HEADS):
            hs = slice(h * HEAD_DIM, (h + 1) * HEAD_DIM)
            s = _dot_nt(q_ref[:, hs], k_ref[ks, hs]) + (wide(cq_ref[h]) - crow_ref[0, h, pl.ds(j, 1), :])
            if masked:
                s = jnp.where(col + j * tk <= row, s, -jnp.inf)
            m_old = m_ref[h]
            m_new = jnp.maximum(m_old, jnp.max(s, axis=-1, keepdims=True))
            alpha = jnp.exp(m_old - m_new)
            p = jnp.exp(s - wide(m_new))
            m_ref[h] = m_new
            l_ref[h] = alpha * l_ref[h] + jnp.sum(p, axis=-1, keepdims=True)
            acc_ref[h] = alpha * acc_ref[h] + _dot(p.astype(BF16), v_ref[ks, hs])

    last = (i * tq + tq - 1) // tk

    def step(j, carry):
        block(j, False)
        return carry

    lax.fori_loop(0, last, step, 0)
    block(last, True)
    for h in range(FOX_HEADS):
        o_ref[:, h * HEAD_DIM:(h + 1) * HEAD_DIM] = (acc_ref[h] / l_ref[h]).astype(o_ref.dtype)


def _fox_prompt(q, k, v, ccol, crow, b, l, *, tq=128):
    tk = crow.shape[-1]
    tq = min(tq, l)
    assert tk % tq == 0
    nq = l // tq
    w = FOX_WIDTH
    return pl.pallas_call(
        functools.partial(_fox_prompt_body, tk),
        grid=(b, nq),
        in_specs=[
            pl.BlockSpec((tq, w), lambda bi, i: (bi * nq + i, 0)),
            pl.BlockSpec((l, w), lambda bi, i: (bi, 0)),
            pl.BlockSpec((l, w), lambda bi, i: (bi, 0)),
            pl.BlockSpec((1, tq, LANES), lambda bi, i: (bi, i, 0)),
            pl.BlockSpec((1, FOX_HEADS, l // tk, tk), lambda bi, i: (bi, 0, 0, 0)),
        ],
        out_specs=pl.BlockSpec((tq, w), lambda bi, i: (bi * nq + i, 0)),
        out_shape=jax.ShapeDtypeStruct((b * l, w), BF16),
        scratch_shapes=[pltpu.VMEM((FOX_HEADS, tq, LANES), F32)] * 3 + [pltpu.VMEM((FOX_HEADS, tq, HEAD_DIM), F32)],
        compiler_params=_params("parallel", "arbitrary"),
        name="fox_prompt",
    )(q, k, v, ccol, crow)


def _page_cumsum_body(x_ref, w_ref, m_ref):
    n = PAGE_SIZE * FOX_HEADS

    @pl.when(pl.program_id(0) == 0)
    def _():
        r = lax.broadcasted_iota(jnp.int32, (n, n), 0)
        c = lax.broadcasted_iota(jnp.int32, (n, n), 1)
        same_head = (r & (FOX_HEADS - 1)) == (c & (FOX_HEADS - 1))
        earlier = lax.shift_right_logical(r, 3) <= lax.shift_right_logical(c, 3)
        m_ref[...] = jnp.logical_and(same_head, earlier).astype(BF16)

    w_ref[...] = _dot_sel_r(x_ref[...], m_ref[...])


def _page_cumsum(logf_pages, *, tb=256):
    n_pool, n = logf_pages.shape
    return pl.pallas_call(
        _page_cumsum_body,
        grid=(n_pool // tb,),
        in_specs=[pl.BlockSpec((tb, n), lambda i: (i, 0))],
        out_specs=pl.BlockSpec((tb, n), lambda i: (i, 0)),
        out_shape=jax.ShapeDtypeStruct((n_pool, n), F32),
        scratch_shapes=[pltpu.VMEM((n, n), BF16)],
        compiler_params=_params("arbitrary"),
        name="page_cumsum",
    )(logf_pages)


def _seq_cumsum_body(seq_len, x_ref, o_ref):
    n = x_ref.shape[0]
    r = lax.broadcasted_iota(jnp.int32, (n, n), 0)
    c = lax.broadcasted_iota(jnp.int32, (n, n), 1)
    same_seq = (r // seq_len) == (c // seq_len)
    o_ref[...] = _dot_sel_l(jnp.logical_and(same_seq, c <= r).astype(BF16), x_ref[...])


def _seq_cumsum(x, seq_len, *, tb=128):
    t = x.shape[0]
    return pl.pallas_call(
        functools.partial(_seq_cumsum_body, seq_len),
        grid=(t // tb,),
        in_specs=[pl.BlockSpec((tb, LANES), lambda i: (i, 0))],
        out_specs=pl.BlockSpec((tb, LANES), lambda i: (i, 0)),
        out_shape=jax.ShapeDtypeStruct((t, LANES), F32),
        compiler_params=_params("parallel"),
        name="seq_cumsum",
    )(x)


def _fox_sample_body(pps, pt_ref, q_ref, kn_ref, vn_ref, cn_ref, *refs):
    kp_refs, vp_refs, w_refs = refs[:pps], refs[pps:2 * pps], refs[2 * pps:3 * pps]
    o_ref, q_scr, colq_ref, toff_ref, m_ref, l_ref, acc_ref = refs[3 * pps:]
    j = pl.program_id(1)
    nq = q_ref.shape[0]
    rows = nq * FOX_HEADS
    page_keys = PAGE_SIZE * FOX_HEADS
    row_id = lax.broadcasted_iota(jnp.int32, (rows, 1), 0)
    head_of_row = row_id & (FOX_HEADS - 1)
    query_of_row = lax.shift_right_logical(row_id, 3)

    @pl.when(j == 0)
    def _():
        q = q_ref[...].reshape(rows, HEAD_DIM).astype(BF16)
        q_scr[...] = q
        cn = cn_ref[0]
        key = lax.broadcasted_iota(jnp.int32, (1, rows), 1)
        colq = jnp.sum(jnp.where(key == row_id, cn, 0.0), axis=-1, keepdims=True)
        colq_ref[...] = colq
        toff_ref[...] = jnp.zeros_like(toff_ref)
        s = _dot_nt(q, kn_ref[...].reshape(rows, HEAD_DIM).astype(BF16)) + colq - cn
        valid = jnp.logical_and((key & (FOX_HEADS - 1)) == head_of_row,
                                lax.shift_right_logical(key, 3) <= query_of_row)
        s = jnp.where(valid, s, -jnp.inf)
        m = jnp.max(s, axis=-1, keepdims=True)
        p = jnp.exp(s - m)
        m_ref[...] = m
        l_ref[...] = jnp.sum(p, axis=-1, keepdims=True)
        acc_ref[...] = _dot(p.astype(BF16), vn_ref[...].reshape(rows, HEAD_DIM).astype(BF16))

    lane = lax.broadcasted_iota(jnp.int32, (1, LANES), 1)
    own_head = (lax.broadcasted_iota(jnp.int32, (1, page_keys), 1) & (FOX_HEADS - 1)) == head_of_row
    q = q_scr[...]
    colq = colq_ref[...]
    toff = toff_ref[...]
    tiles = []
    for kp_ref, w_ref in zip(kp_refs, w_refs):
        w = w_ref[0]
        last = jnp.where(lane == LANES - FOX_HEADS + head_of_row, w[:, page_keys - LANES:], 0.0)
        toff = toff + jnp.sum(last, axis=-1, keepdims=True)
        s = _dot_nt(q, kp_ref[0].reshape(page_keys, HEAD_DIM).astype(BF16))
        tiles.append(jnp.where(own_head, s + (colq + toff) - w, -jnp.inf))
    toff_ref[...] = toff
    m_old = m_ref[...]
    m = m_old
    for s in tiles:
        m = jnp.maximum(m, jnp.max(s, axis=-1, keepdims=True))
    alpha = jnp.exp(m_old - m)
    l = alpha * l_ref[...]
    acc = alpha * acc_ref[...]
    for s, vp_ref in zip(tiles, vp_refs):
        p = jnp.exp(s - m)
        l = l + jnp.sum(p, axis=-1, keepdims=True)
        acc = acc + _dot(p.astype(BF16), vp_ref[0].reshape(page_keys, HEAD_DIM).astype(BF16))
    m_ref[...] = m
    l_ref[...] = l
    acc_ref[...] = acc

    @pl.when(j == pl.num_programs(1) - 1)
    def _():
        o_ref[...] = (acc_ref[...] / l_ref[...]).reshape(nq, FOX_HEADS, HEAD_DIM)


def _fox_sample(page_table, q, k_new, v_new, c_new, k_pages, v_pages, w_pages, n_seq, nq, *, pages_per_step=16):
    n_pages = page_table.shape[1]
    rows = nq * FOX_HEADS
    pps = math.gcd(pages_per_step, n_pages)

    def page(i, ndim):
        return lambda b, j, pt: (pt[b * n_pages + (n_pages - 1 - j * pps - i)],) + (0,) * (ndim - 1)

    seq = pl.BlockSpec((nq, FOX_HEADS, HEAD_DIM), lambda b, j, pt: (b, 0, 0))
    kv_specs = [pl.BlockSpec((1, PAGE_SIZE, FOX_HEADS, HEAD_DIM), page(i, 4)) for i in range(pps)]
    w_specs = [pl.BlockSpec((1, 1, PAGE_SIZE * FOX_HEADS), page(i, 3)) for i in range(pps)]
    grid_spec = pltpu.PrefetchScalarGridSpec(
        num_scalar_prefetch=1,
        grid=(n_seq, n_pages // pps),
        in_specs=[seq, seq, seq, pl.BlockSpec((1, 1, rows), lambda b, j, pt: (b, 0, 0))]
        + kv_specs + kv_specs + w_specs,
        out_specs=seq,
        scratch_shapes=[
            pltpu.VMEM((rows, HEAD_DIM), BF16), pltpu.VMEM((rows, 1), F32), pltpu.VMEM((rows, 1), F32),
            pltpu.VMEM((rows, 1), F32), pltpu.VMEM((rows, 1), F32), pltpu.VMEM((rows, HEAD_DIM), F32),
        ],
    )
    return pl.pallas_call(
        functools.partial(_fox_sample_body, pps),
        grid_spec=grid_spec,
        out_shape=jax.ShapeDtypeStruct((n_seq * nq, FOX_HEADS, HEAD_DIM), F32),
        compiler_params=_params("parallel", "arbitrary"),
        name="fox_sample",
    )(page_table.reshape(-1), q, k_new, v_new, c_new, *([k_pages] * pps), *([v_pages] * pps), *([w_pages] * pps))


SSD_PAIRS = SSD_HEADS // 2
PAIRS_PER_GROUP = SSD_PAIRS // SSD_GROUPS


def _expander(width):
    n = SSD_HEADS * width
    h = lax.broadcasted_iota(jnp.int32, (LANES, n), 0)
    c = lax.broadcasted_iota(jnp.int32, (LANES, n), 1)
    return (lax.shift_right_logical(c, int(math.log2(width))) == h).astype(BF16)


def _ssd_local(xs, bm, cm, dt, alog, mask, tot_sel):
    n = xs.shape[0]
    lane = lax.broadcasted_iota(jnp.int32, (1, LANES), 1)
    dta = dt * jnp.where(lane < SSD_HEADS, -jnp.exp(alog), 0.0)
    e64 = _expander(SSD_HEAD_DIM)
    a_cum = _dot_sel_l(mask.astype(BF16), dta)
    a_cum_t = a_cum.T
    ac_exp = _dot_sel_r(a_cum, e64)
    if tot_sel is None:
        atot_exp = ac_exp[n - 1:n, :]
    else:
        atot_exp = _dot_sel_r(_dot_sel_l(tot_sel, dta), e64)
    ac_b = _dot_sel_r(a_cum, _expander(LANES))
    xdt = xs * _dot_sel_r(dt, e64)
    half = lax.broadcasted_iota(jnp.int32, (n, LANES), 1) < SSD_HEAD_DIM
    out = {
        "xdtw": xdt * jnp.exp(atot_exp - ac_exp),
        "eac": jnp.exp(ac_exp),
        "atot_exp": atot_exp,
        "ac_b": ac_b,
        "bg": [], "cg": [], "y_diag": [],
    }
    for g in range(SSD_GROUPS):
        gs = slice(g * SSD_STATE, (g + 1) * SSD_STATE)
        bg = bm[:, gs].astype(BF16)
        cg = cm[:, gs].astype(BF16)
        out["bg"].append(bg)
        out["cg"].append(cg)
        cb = _dot_nt(cg, bg)
        for k in range(g * PAIRS_PER_GROUP, (g + 1) * PAIRS_PER_GROUP):
            ps = slice(k * LANES, (k + 1) * LANES)
            ms = []
            for h in (2 * k, 2 * k + 1):
                seg = ac_b[:, h * LANES:(h + 1) * LANES] - a_cum_t[h:h + 1, :]
                ms.append(cb * jnp.exp(jnp.where(mask, seg, -jnp.inf)))
            m_cat = jnp.concatenate(ms, axis=1).astype(BF16)
            xp = xdt[:, ps]
            x_bd = jnp.concatenate([jnp.where(half, xp, 0.0), jnp.where(half, 0.0, xp)], axis=0).astype(BF16)
            out["y_diag"].append(_dot(m_cat, x_bd))
    return out


def _conv_ssd_sample_body(seq_len, xbc_ref, dt_ref, buf_ref, h0_ref, w_ref, b_ref, alog_ref, dexp_ref,
                          y_ref, hout_ref, cout_ref, xp_ref):
    n = xbc_ref.shape[0]
    n_seq = n // seq_len
    taps = CONV_WIDTH - 1
    base = 8 - taps

    x = xbc_ref[...]
    xp_ref[:, base:8, :] = buf_ref[...]
    xp_ref[:, 8:8 + seq_len, :] = x.reshape(n_seq, seq_len, CONV_DIM)
    acc = b_ref[...] + x * w_ref[taps:taps + 1, :]
    for j in range(taps):
        acc = acc + xp_ref[:, base + j:base + j + seq_len, :].reshape(n, CONV_DIM) * w_ref[j:j + 1, :]
    cout_ref[...] = xp_ref[:, 8 + seq_len - taps:8 + seq_len, :]
    conv = _silu(acc)
    xs = conv[:, :SSD_WIDTH]
    bm = conv[:, SSD_WIDTH:SSD_WIDTH + SSD_GROUPS * SSD_STATE]
    cm = conv[:, SSD_WIDTH + SSD_GROUPS * SSD_STATE:]

    r = lax.broadcasted_iota(jnp.int32, (n, n), 0)
    c = lax.broadcasted_iota(jnp.int32, (n, n), 1)
    same_seq = (r // seq_len) == (c // seq_len)
    loc = _ssd_local(xs, bm, cm, dt_ref[...], alog_ref[...], jnp.logical_and(same_seq, c <= r),
                     same_seq.astype(BF16))

    gw = PAIRS_PER_GROUP * LANES
    seq_of_col = lax.broadcasted_iota(jnp.int32, (1, n), 1) // seq_len
    decay_t = jnp.exp(loc["atot_exp"]).T
    for g in range(SSD_GROUPS):
        gr = slice(g * gw, (g + 1) * gw)
        h_prev = h0_ref[:, gr, :]
        z = _dot_nt(h_prev.reshape(n_seq * gw, SSD_STATE).astype(BF16), loc["cg"][g])
        y_off_t = jnp.zeros((gw, n), F32)
        for s in range(n_seq):
            y_off_t = y_off_t + jnp.where(seq_of_col == s, z[s * gw:(s + 1) * gw, :], 0.0)
        y_off = y_off_t.T * loc["eac"][:, gr]
        y = jnp.concatenate(loc["y_diag"][g * PAIRS_PER_GROUP:(g + 1) * PAIRS_PER_GROUP], axis=1)
        y_ref[:, gr] = y + y_off + xs[:, gr] * dexp_ref[:, gr]
        xw_t = loc["xdtw"][:, gr].T
        lhs = jnp.concatenate([jnp.where(seq_of_col == s, xw_t, 0.0) for s in range(n_seq)], axis=0)
        s_new = _dot(lhs.astype(BF16), loc["bg"][g])
        for s in range(n_seq):
            col = decay_t[gr, s * seq_len:s * seq_len + 1]
            hout_ref[s, gr, :] = h_prev[s] * col + s_new[s * gw:(s + 1) * gw, :]


def _conv_ssd_sample(xbc, dt, conv_buf, h0, conv_w, conv_b, alog, d_exp, n_seq, seq_len):
    assert seq_len == 8 and CONV_WIDTH - 1 <= seq_len
    tile = LANES
    ts = tile // seq_len
    const = lambda i: (0, 0)
    per_s = lambda i: (i, 0, 0)
    return pl.pallas_call(
        functools.partial(_conv_ssd_sample_body, seq_len),
        grid=(n_seq // ts,),
        in_specs=[
            pl.BlockSpec((tile, CONV_DIM), lambda i: (i, 0)),
            pl.BlockSpec((tile, LANES), lambda i: (i, 0)),
            pl.BlockSpec((ts, CONV_WIDTH - 1, CONV_DIM), per_s),
            pl.BlockSpec((ts, SSD_WIDTH, SSD_STATE), per_s),
            pl.BlockSpec((CONV_WIDTH, CONV_DIM), const),
            pl.BlockSpec((1, CONV_DIM), const),
            pl.BlockSpec((1, LANES), const),
            pl.BlockSpec((1, SSD_WIDTH), const),
        ],
        out_specs=[
            pl.BlockSpec((tile, SSD_WIDTH), lambda i: (i, 0)),
            pl.BlockSpec((ts, SSD_WIDTH, SSD_STATE), per_s),
            pl.BlockSpec((ts, CONV_WIDTH - 1, CONV_DIM), per_s),
        ],
        out_shape=[
            jax.ShapeDtypeStruct((n_seq * seq_len, SSD_WIDTH), F32),
            jax.ShapeDtypeStruct((n_seq, SSD_WIDTH, SSD_STATE), F32),
            jax.ShapeDtypeStruct((n_seq, CONV_WIDTH - 1, CONV_DIM), F32),
        ],
        scratch_shapes=[pltpu.VMEM((ts, 8 + seq_len, CONV_DIM), F32)],
        compiler_params=_params("parallel"),
        name="conv_ssd_sample",
    )(xbc, dt, conv_buf, h0, conv_w, conv_b, alog, d_exp)


def _conv_ssd_prompt_body(xbc_ref, dt_ref, buf_ref, h0_ref, w_ref, b_ref, alog_ref, dexp_ref,
                          y_ref, hout_ref, cout_ref, state_ref, xp_ref):
    c = pl.program_id(1)
    nc = pl.num_programs(1)
    tl = xbc_ref.shape[0]
    taps = CONV_WIDTH - 1
    base = 8 - taps

    @pl.when(c == 0)
    def _():
        state_ref[...] = h0_ref[0]
        xp_ref[base:8, :] = buf_ref[0]

    x = xbc_ref[...]
    xp_ref[8:8 + tl, :] = x
    acc = b_ref[...] + x * w_ref[taps:taps + 1, :]
    for j in range(taps):
        acc = acc + xp_ref[base + j:base + j + tl, :] * w_ref[j:j + 1, :]
    xp_ref[base:8, :] = x[tl - taps:, :]
    conv = _silu(acc)
    xs = conv[:, :SSD_WIDTH]
    bm = conv[:, SSD_WIDTH:SSD_WIDTH + SSD_GROUPS * SSD_STATE]
    cm = conv[:, SSD_WIDTH + SSD_GROUPS * SSD_STATE:]

    causal = _tri(tl)
    loc = _ssd_local(xs, bm, cm, dt_ref[...], alog_ref[...], causal, None)
    top = lax.broadcasted_iota(jnp.int32, (tl, LANES), 0) < SSD_HEAD_DIM
    for k in range(SSD_PAIRS):
        g = k // PAIRS_PER_GROUP
        ps = slice(k * LANES, (k + 1) * LANES)
        s_prev = state_ref[ps, :]
        y_off = _dot_nt(loc["cg"][g], s_prev.astype(BF16)) * loc["eac"][:, ps]
        y_ref[:, ps] = loc["y_diag"][k] + y_off + xs[:, ps] * dexp_ref[:, ps]
        cd = [jnp.exp(loc["ac_b"][tl - 1:tl, h * LANES:(h + 1) * LANES]) for h in (2 * k, 2 * k + 1)]
        state_ref[ps, :] = s_prev * jnp.where(top, cd[0], cd[1]) + _dot_tn(loc["xdtw"][:, ps].astype(BF16), loc["bg"][g])

    @pl.when(c == nc - 1)
    def _():
        hout_ref[0] = state_ref[...]
        cout_ref[0] = xp_ref[base:8, :]


def _conv_ssd_prompt(xbc, dt, conv_buf, h0, conv_w, conv_b, alog, d_exp, b, l):
    tl = SSD_CHUNK
    nc = l // tl
    const = lambda bi, c: (0, 0)
    per_b = lambda bi, c: (bi, 0, 0)
    return pl.pallas_call(
        _conv_ssd_prompt_body,
        grid=(b, nc),
        in_specs=[
            pl.BlockSpec((tl, CONV_DIM), lambda bi, c: (bi * nc + c, 0)),
            pl.BlockSpec((tl, LANES), lambda bi, c: (bi * nc + c, 0)),
            pl.BlockSpec((1, CONV_WIDTH - 1, CONV_DIM), per_b),
            pl.BlockSpec((1, SSD_WIDTH, SSD_STATE), per_b),
            pl.BlockSpec((CONV_WIDTH, CONV_DIM), const),
            pl.BlockSpec((1, CONV_DIM), const),
            pl.BlockSpec((1, LANES), const),
            pl.BlockSpec((1, SSD_WIDTH), const),
        ],
        out_specs=[
            pl.BlockSpec((tl, SSD_WIDTH), lambda bi, c: (bi * nc + c, 0)),
            pl.BlockSpec((1, SSD_WIDTH, SSD_STATE), per_b),
            pl.BlockSpec((1, CONV_WIDTH - 1, CONV_DIM), per_b),
        ],
        out_shape=[
            jax.ShapeDtypeStruct((b * l, SSD_WIDTH), F32),
            jax.ShapeDtypeStruct((b, SSD_WIDTH, SSD_STATE), F32),
            jax.ShapeDtypeStruct((b, CONV_WIDTH - 1, CONV_DIM), F32),
        ],
        scratch_shapes=[pltpu.VMEM((SSD_WIDTH, SSD_STATE), F32), pltpu.VMEM((8 + tl, CONV_DIM), F32)],
        compiler_params=_params("parallel", "arbitrary"),
        name="conv_ssd_prompt",
    )(xbc, dt, conv_buf, h0, conv_w, conv_b, alog, d_exp)


def _out_proj_body(h_ref, fox_ref, y_ref, z_ref, gs_ref, wf_ref, ws_ref, gx_ref, wq_ref, gq_ref, o_ref, q_ref):
    yn = _rms(y_ref[...] * _silu(z_ref[...]), gs_ref[...]).astype(BF16)
    h = h_ref[...] + _dot(fox_ref[...].astype(BF16), wf_ref[...]) + _dot(yn, ws_ref[...])
    o_ref[...] = h
    q = _dot(_rms(h, gx_ref[...]).astype(BF16), wq_ref[...])
    q_ref[...] = _head_norm(q, gq_ref[...]) * (HEAD_DIM ** -0.5)


def _out_proj(h, fox, y, z, g_ssd, w_fox, w_ssd, g_x, wq, gq, *, tm=256):
    t, d = h.shape
    tm = min(tm, t)
    row = lambda w: pl.BlockSpec((tm, w), lambda i: (i, 0))
    full = lambda a: pl.BlockSpec(a.shape, lambda i: (0, 0))
    consts = [g_ssd.reshape(1, SSD_WIDTH), w_fox, w_ssd, g_x.reshape(1, d), wq, gq.reshape(1, HEAD_DIM)]
    return pl.pallas_call(
        _out_proj_body,
        grid=(t // tm,),
        in_specs=[row(d), row(FOX_WIDTH), row(SSD_WIDTH), row(SSD_WIDTH)] + [full(a) for a in consts],
        out_specs=[row(d), row(XATTN_WIDTH)],
        out_shape=[jax.ShapeDtypeStruct((t, d), F32), jax.ShapeDtypeStruct((t, XATTN_WIDTH), F32)],
        compiler_params=_params("parallel"),
        name="out_proj",
    )(h, fox, y, z, *consts)


def _mem_kv_body(m_ref, g_ref, w_ref, gk_ref, k_ref, v_ref):
    kv = _dot(_rms(m_ref[...], g_ref[...]).astype(BF16), w_ref[...])
    k_ref[...] = _head_norm(kv[:, :XATTN_WIDTH], gk_ref[...])
    v_ref[...] = kv[:, XATTN_WIDTH:]


def _mem_kv(mem, g, w_kv, gk, *, tm=256):
    t, d = mem.shape
    row = lambda w: pl.BlockSpec((tm, w), lambda i: (i, 0))
    full = lambda a: pl.BlockSpec(a.shape, lambda i: (0, 0))
    consts = [g.reshape(1, d), w_kv, gk.reshape(1, HEAD_DIM)]
    return pl.pallas_call(
        _mem_kv_body,
        grid=(t // tm,),
        in_specs=[row(d)] + [full(a) for a in consts],
        out_specs=[row(XATTN_WIDTH), row(XATTN_WIDTH)],
        out_shape=[jax.ShapeDtypeStruct((t, XATTN_WIDTH), F32)] * 2,
        compiler_params=_params("parallel"),
        name="mem_kv",
    )(mem, *consts)


def _xattn_body(q_ref, k_ref, v_ref, o_ref):
    for h in range(XATTN_HEADS):
        hs = slice(h * HEAD_DIM, (h + 1) * HEAD_DIM)
        s = _dot_nt(q_ref[:, hs].astype(BF16), k_ref[0, :, hs].astype(BF16))
        p = jnp.exp(s - jnp.max(s, axis=-1, keepdims=True))
        o = _dot(p.astype(BF16), v_ref[0, :, hs].astype(BF16))
        o_ref[:, hs] = o / jnp.sum(p, axis=-1, keepdims=True)


def _xattn(q, mem_k, mem_v, b, l, *, tq=512):
    tq = min(tq, l)
    nq = l // tq
    n_mem = mem_k.shape[1]
    qspec = pl.BlockSpec((tq, XATTN_WIDTH), lambda bi, i: (bi * nq + i, 0))
    mspec = pl.BlockSpec((1, n_mem, XATTN_WIDTH), lambda bi, i: (bi, 0, 0))
    return pl.pallas_call(
        _xattn_body,
        grid=(b, nq),
        in_specs=[qspec, mspec, mspec],
        out_specs=qspec,
        out_shape=jax.ShapeDtypeStruct((b * l, XATTN_WIDTH), F32),
        compiler_params=_params("parallel", "arbitrary"),
        name="xattn",
    )(q, mem_k, mem_v)


def _xattn_rows_body(n_seq, q_ref, k_ref, v_ref, o_ref):
    tiles = q_ref.shape[0] // n_seq
    rows = tiles * 8
    keys = k_ref.shape[1] * k_ref.shape[2]
    row_head = lax.broadcasted_iota(jnp.int32, (rows, 1), 0) & (XATTN_HEADS - 1)
    key_head = lax.broadcasted_iota(jnp.int32, (1, keys), 1) & (XATTN_HEADS - 1)
    own_head = row_head == key_head
    for s in range(n_seq):
        qs = slice(s * tiles, (s + 1) * tiles)
        q = q_ref[qs].reshape(rows, HEAD_DIM).astype(BF16)
        sc = jnp.where(own_head, _dot_nt(q, k_ref[s].reshape(keys, HEAD_DIM).astype(BF16)), -jnp.inf)
        p = jnp.exp(sc - jnp.max(sc, axis=-1, keepdims=True))
        o = _dot(p.astype(BF16), v_ref[s].reshape(keys, HEAD_DIM).astype(BF16))
        o_ref[qs] = (o / jnp.sum(p, axis=-1, keepdims=True)).reshape(tiles, 8, HEAD_DIM)


def _xattn_rows(q, mem_k, mem_v, n_seq, *, ts=8):
    tiles = q.shape[0] // n_seq
    ts = math.gcd(ts, n_seq)
    qspec = pl.BlockSpec((ts * tiles, 8, HEAD_DIM), lambda i: (i, 0, 0))
    mspec = pl.BlockSpec((ts,) + mem_k.shape[1:], lambda i: (i, 0, 0, 0))
    return pl.pallas_call(
        functools.partial(_xattn_rows_body, ts),
        grid=(n_seq // ts,),
        in_specs=[qspec, mspec, mspec],
        out_specs=qspec,
        out_shape=jax.ShapeDtypeStruct(q.shape, F32),
        compiler_params=_params("parallel"),
        name="xattn_rows",
    )(q, mem_k, mem_v)


def kernel(x_prompt, x_sample, cache_fox_k, cache_fox_v, cache_fox_logf, cache_mem_k, cache_mem_v, state_ssm, state_conv, page_table, mem_prompt, ffn1_norm, ffn1_w_gate, ffn1_w_up, ffn1_w_down, mix_norm, w_in, fox_b_f, fox_q_norm, fox_k_norm, conv_w, conv_b, ssd_dt_bias, ssd_A_log, ssd_D, ssd_out_norm, w_out, xattn_norm, mem_norm, xattn_w_q, xattn_w_kv, xattn_q_norm, xattn_k_norm, xattn_w_o, ffn2_norm, ffn2_w_gate, ffn2_w_up, ffn2_w_down):
    assert x_prompt.shape[2] == D_MODEL and ffn1_norm.shape[0] == 1
    d = D_MODEL
    bp, lp = x_prompt.shape[:2]
    bs, ls = x_sample.shape[:2]
    n_mem = mem_prompt.shape[1]
    n_pool = cache_fox_k.shape[1]

    bf = lambda w: w[0].astype(BF16)
    blocks = lambda w: _col_blocks(w[0], FFN_TF)
    ffn1 = (ffn1_norm[0], blocks(ffn1_w_gate), blocks(ffn1_w_up), bf(ffn1_w_down))
    ffn2 = (ffn2_norm[0], blocks(ffn2_w_gate), blocks(ffn2_w_up), bf(ffn2_w_down))
    in_w = _pack_in_proj(w_in[0], fox_b_f[0], ssd_dt_bias[0])
    w_fox, w_ssd = bf(w_out)[:FOX_WIDTH], bf(w_out)[FOX_WIDTH:]
    wq, wkv, wo = bf(xattn_w_q), bf(xattn_w_kv), bf(xattn_w_o)
    alog = jnp.pad(ssd_A_log[0], (0, LANES - SSD_HEADS)).reshape(1, LANES)
    d_exp = jnp.repeat(ssd_D[0], SSD_HEAD_DIM).reshape(1, SSD_WIDTH)
    cw, cb = conv_w[0], conv_b[0].reshape(1, CONV_DIM)

    def front(x, q_dtype):
        h1 = _ffn(x, *ffn1)
        return h1, _in_proj(h1, mix_norm[0], *in_w, fox_q_norm[0], fox_k_norm[0], q_dtype=q_dtype)

    def back(h1, fox, y, z, attend):
        h2, xq = _out_proj(h1, fox, y, z, ssd_out_norm[0], w_fox, w_ssd, xattn_norm[0], wq, xattn_q_norm[0])
        return _ffn(h2, *ffn2, pre=(attend(xq), wo))

    h1, (q, k_p, kb, v_p, vb, z, xbc, logf_p, dt) = front(x_prompt.reshape(bp * lp, d), BF16)
    ccol, crow = _cumsum(logf_p.reshape(bp, lp, LANES))
    tq = min(ATT_BLOCK, lp)
    fox = _fox_prompt(q, kb, vb, ccol, crow.reshape(bp, FOX_HEADS, lp // tq, tq), bp, lp)
    y, ssm_p, conv_p = _conv_ssd_prompt(
        xbc, dt, jnp.zeros((bp, CONV_WIDTH - 1, CONV_DIM), F32), jnp.zeros((bp, SSD_WIDTH, SSD_STATE), F32),
        cw, cb, alog, d_exp, bp, lp)
    mk, mv = _mem_kv(mem_prompt.reshape(bp * n_mem, d), mem_norm[0], wkv, xattn_k_norm[0])
    y_prompt = back(h1, fox, y, z, lambda xq: _xattn(
        xq, mk.reshape(bp, n_mem, XATTN_WIDTH), mv.reshape(bp, n_mem, XATTN_WIDTH), bp, lp))

    h1, (q, k_s, _, v_s, _, z, xbc, logf_s, dt) = front(x_sample.reshape(bs * ls, d), F32)
    w_pages = _page_cumsum(cache_fox_logf[0].reshape(n_pool, PAGE_SIZE * FOX_HEADS))
    heads = lambda a: a.reshape(bs * ls, FOX_HEADS, HEAD_DIM)
    c_new = _seq_cumsum(logf_s, ls)[:, :FOX_HEADS].reshape(bs, 1, ls * FOX_HEADS)
    fox = _fox_sample(
        page_table, heads(q), heads(k_s), heads(v_s), c_new, cache_fox_k[0], cache_fox_v[0],
        w_pages.reshape(n_pool, 1, PAGE_SIZE * FOX_HEADS), bs, ls).reshape(bs * ls, FOX_WIDTH)
    y, ssm_s, conv_s = _conv_ssd_sample(
        xbc, dt, state_conv[0], state_ssm[0].reshape(bs, SSD_WIDTH, SSD_STATE), cw, cb, alog, d_exp, bs, ls)
    mem_tiles = lambda m: m[0].reshape(bs, n_mem * XATTN_HEADS // 8, 8, HEAD_DIM)
    y_sample = back(h1, fox, y, z, lambda xq: _xattn_rows(
        xq.reshape(bs * ls * XATTN_HEADS // 8, 8, HEAD_DIM), mem_tiles(cache_mem_k), mem_tiles(cache_mem_v), bs,
    ).reshape(bs * ls, XATTN_WIDTH))

    fox_shape = lambda b, l: (1, b, l, FOX_HEADS, HEAD_DIM)
    ssm_shape = lambda b: (1, b, SSD_HEADS, SSD_HEAD_DIM, SSD_STATE)
    mem_shape = (1, bp, n_mem, XATTN_HEADS, HEAD_DIM)
    return (
        y_prompt.reshape(bp, lp, d), y_sample.reshape(bs, ls, d),
        k_p.reshape(fox_shape(bp, lp)), v_p.reshape(fox_shape(bp, lp)),
        logf_p[:, :FOX_HEADS].reshape(1, bp, lp, FOX_HEADS),
        ssm_p.reshape(ssm_shape(bp)), conv_p[None], mk.reshape(mem_shape), mv.reshape(mem_shape),
        k_s.reshape(fox_shape(bs, ls)), v_s.reshape(fox_shape(bs, ls)),
        logf_s[:, :FOX_HEADS].reshape(1, bs, ls, FOX_HEADS),
        ssm_s.reshape(ssm_shape(bs)), conv_s[None],
    )
```

```python
import functools
import math

import jax
import jax.numpy as jnp
from jax import lax
from jax.experimental import pallas as pl
from jax.experimental.pallas import tpu as pltpu

F32 = jnp.float32
BF16 = jnp.bfloat16

EPS = 1e-6
FFN_RESIDUAL = 0.5
D_MODEL = 2048
D_FF = 5632
PAGE_SIZE = 128
FOX_HEADS = 8
HEAD_DIM = 128
FOX_WIDTH = FOX_HEADS * HEAD_DIM
SSD_HEADS = 16
SSD_HEAD_DIM = 64
SSD_WIDTH = SSD_HEADS * SSD_HEAD_DIM
SSD_GROUPS = 2
SSD_STATE = 128
SSD_CHUNK = 128
CONV_WIDTH = 4
CONV_DIM = SSD_WIDTH + 2 * SSD_GROUPS * SSD_STATE
XATTN_HEADS = 4
XATTN_WIDTH = XATTN_HEADS * HEAD_DIM
LANES = 128
VMEM_LIMIT_BYTES = 56 * 1024 * 1024


def _params(*semantics):
    return pltpu.CompilerParams(dimension_semantics=semantics, vmem_limit_bytes=VMEM_LIMIT_BYTES)


def _rms(x, g):
    return x * lax.rsqrt(jnp.mean(x * x, axis=-1, keepdims=True) + EPS) * g


def _dot(a, b):
    return jnp.dot(a, b, preferred_element_type=F32)


def _dot_nt(a, b):
    return lax.dot_general(a, b, (((1,), (1,)), ((), ())), preferred_element_type=F32)


def _dot_tn(a, b):
    return lax.dot_general(a, b, (((0,), (0,)), ((), ())), preferred_element_type=F32)


def _split3(x):
    x1 = x.astype(BF16)
    r = x - x1.astype(F32)
    x2 = r.astype(BF16)
    x3 = (r - x2.astype(F32)).astype(BF16)
    return x1, x2, x3


def _dot_sel_l(sel, x):
    x1, x2, x3 = _split3(x)
    return _dot(sel, x1) + _dot(sel, x2) + _dot(sel, x3)


def _dot_sel_r(x, sel):
    x1, x2, x3 = _split3(x)
    return _dot(x1, sel) + _dot(x2, sel) + _dot(x3, sel)


def _silu(x):
    return x * jax.nn.sigmoid(x)


def _ffn_body(pre_proj, *refs):
    if pre_proj:
        x_ref, a_ref, wo_ref, g_ref, wg_ref, wu_ref, wd_ref, o_ref, xn_ref = refs
    else:
        x_ref, g_ref, wg_ref, wu_ref, wd_ref, o_ref, xn_ref = refs

    @pl.when(pl.program_id(1) == 0)
    def _():
        x = x_ref[...]
        if pre_proj:
            x = x + _dot(a_ref[...].astype(BF16), wo_ref[...])
        xn_ref[...] = _rms(x, g_ref[...]).astype(BF16)
        o_ref[...] = x

    xn = xn_ref[...]
    gate = _dot(xn, wg_ref[...])
    up = _dot(xn, wu_ref[...])
    h = (_silu(gate) * up * FFN_RESIDUAL).astype(BF16)
    o_ref[...] += _dot(h, wd_ref[...])


def _ffn(x, g, wg, wu, wd, pre=None, *, tm=512, tf=512):
    t, d = x.shape
    f = wg.shape[1]
    tm = min(tm, t)
    grid = (t // tm, f // tf)
    row = lambda i, j: (i, 0)
    in_specs = [pl.BlockSpec((tm, d), row)]
    args = [x]
    if pre is not None:
        a, wo = pre
        in_specs += [pl.BlockSpec((tm, a.shape[1]), row), pl.BlockSpec(wo.shape, lambda i, j: (0, 0))]
        args += [a, wo]
    in_specs += [
        pl.BlockSpec((1, d), lambda i, j: (0, 0)),
        pl.BlockSpec((d, tf), lambda i, j: (0, j)),
        pl.BlockSpec((d, tf), lambda i, j: (0, j)),
        pl.BlockSpec((tf, d), lambda i, j: (j, 0)),
    ]
    args += [g.reshape(1, d), wg, wu, wd]
    return pl.pallas_call(
        functools.partial(_ffn_body, pre is not None),
        grid=grid,
        in_specs=in_specs,
        out_specs=pl.BlockSpec((tm, d), row),
        out_shape=jax.ShapeDtypeStruct((t, d), F32),
        scratch_shapes=[pltpu.VMEM((tm, d), BF16)],
        compiler_params=_params("parallel", "arbitrary"),
        name="ffn_pre" if pre is not None else "ffn",
    )(*args)


IN_TN = 512
IN_BLOCKS = {"q": (0, 2), "k": (2, 2), "v": (4, 2), "z": (6, 2), "xbc": (8, 3)}
IN_NBLK = 11


def _head_norm(y, g):
    outs = []
    for c in range(y.shape[1] // HEAD_DIM):
        yc = y[:, c * HEAD_DIM:(c + 1) * HEAD_DIM]
        outs.append(yc * lax.rsqrt(jnp.mean(yc * yc, axis=-1, keepdims=True) + EPS) * g)
    return jnp.concatenate(outs, axis=1)


def _in_proj_body(x_ref, g_ref, wa_ref, wb_ref, ws_ref, bs_ref, gq_ref, gk_ref,
                  q_ref, k_ref, kb_ref, v_ref, vb_ref, z_ref, xbc_ref, logf_ref, dt_ref, u_ref):
    j = pl.program_id(1)

    @pl.when(j == 0)
    def _():
        u = _rms(x_ref[...], g_ref[...]).astype(BF16)
        u_ref[...] = u
        s = _dot(u, ws_ref[...]) + bs_ref[...]
        t = jnp.log1p(jnp.exp(-jnp.abs(s)))
        logf_ref[...] = (jnp.minimum(s, 0.0) - t)[:, :LANES]
        dt_ref[...] = (jnp.maximum(s, 0.0) + t)[:, LANES:]

    tn = wa_ref.shape[1]
    heads = tn // HEAD_DIM

    def column_block(name, sub, w_ref):
        cols = slice(sub * tn, (sub + 1) * tn)

        @pl.when(j == IN_BLOCKS[name][0] + sub)
        def _():
            y = _dot(u_ref[...], w_ref[...])
            if name == "q":
                q_ref[:, cols] = (_head_norm(y, gq_ref[...]) * (HEAD_DIM ** -0.5)).astype(q_ref.dtype)
            elif name == "k" or name == "v":
                if name == "k":
                    y = _head_norm(y, gk_ref[...])
                out_ref, bf_ref = (k_ref, kb_ref) if name == "k" else (v_ref, vb_ref)
                bf_ref[:, cols] = y.astype(BF16)
                for h in range(heads):
                    out_ref[:, sub * heads + h, :] = y[:, h * HEAD_DIM:(h + 1) * HEAD_DIM]
            elif name == "z":
                z_ref[:, cols] = y
            else:
                xbc_ref[:, cols] = y

    for name, (lo, n) in IN_BLOCKS.items():
        for sub in range(n):
            column_block(name, sub, wa_ref if name in ("q", "k", "v") else wb_ref)


def _in_proj(x, g, w_qkv, w_zx, w_small, b_small, gq, gk, *, q_dtype=BF16, tm=512):
    t, d = x.shape
    tm = min(tm, t)
    tn = IN_TN
    nb_a = w_qkv.shape[1] // tn
    nb_b = w_zx.shape[1] // tn
    assert nb_a + nb_b == IN_NBLK

    const = lambda i, j: (0, 0)
    rows = lambda w: pl.BlockSpec((tm, w), lambda i, j: (i, 0))
    by_head = pl.BlockSpec((tm, FOX_HEADS, HEAD_DIM), lambda i, j: (i, 0, 0))
    by_head_shape = jax.ShapeDtypeStruct((t, FOX_HEADS, HEAD_DIM), F32)
    flat = lambda dt, w: jax.ShapeDtypeStruct((t, w), dt)
    return pl.pallas_call(
        _in_proj_body,
        grid=(t // tm, IN_NBLK),
        in_specs=[
            rows(d),
            pl.BlockSpec((1, d), const),
            pl.BlockSpec((d, tn), lambda i, j: (0, jnp.minimum(j, nb_a - 1))),
            pl.BlockSpec((d, tn), lambda i, j: (0, jnp.clip(j - nb_a, 0, nb_b - 1))),
            pl.BlockSpec((d, 2 * LANES), const),
            pl.BlockSpec((1, 2 * LANES), const),
            pl.BlockSpec((1, HEAD_DIM), const),
            pl.BlockSpec((1, HEAD_DIM), const),
        ],
        out_specs=[rows(FOX_WIDTH), by_head, rows(FOX_WIDTH), by_head, rows(FOX_WIDTH), rows(SSD_WIDTH),
                   rows(CONV_DIM), rows(LANES), rows(LANES)],
        out_shape=[flat(q_dtype, FOX_WIDTH), by_head_shape, flat(BF16, FOX_WIDTH), by_head_shape,
                   flat(BF16, FOX_WIDTH), flat(F32, SSD_WIDTH), flat(F32, CONV_DIM), flat(F32, LANES),
                   flat(F32, LANES)],
        scratch_shapes=[pltpu.VMEM((tm, d), BF16)],
        compiler_params=_params("parallel", "arbitrary"),
        name="in_proj",
    )(x, g.reshape(1, d), w_qkv, w_zx, w_small, b_small, gq.reshape(1, HEAD_DIM), gk.reshape(1, HEAD_DIM))


def _pack_in_proj(w_in, fox_b_f, ssd_dt_bias):
    fw = FOX_WIDTH
    f0 = 3 * fw
    z0 = f0 + FOX_HEADS
    x0 = z0 + SSD_WIDTH
    d0 = x0 + CONV_DIM
    w_qkv = w_in[:, :f0].astype(BF16)
    w_zx = w_in[:, z0:d0].astype(BF16)
    zeros = lambda n: jnp.zeros((w_in.shape[0], n), w_in.dtype)
    w_small = jnp.concatenate(
        [w_in[:, f0:z0], zeros(LANES - FOX_HEADS), w_in[:, d0:], zeros(LANES - SSD_HEADS)], axis=1).astype(BF16)
    b_small = jnp.concatenate(
        [fox_b_f, jnp.zeros((LANES - FOX_HEADS,), F32), ssd_dt_bias, jnp.zeros((LANES - SSD_HEADS,), F32)]
    ).reshape(1, 2 * LANES)
    return w_qkv, w_zx, w_small, b_small


ATT_BLOCK = 256


def _tri(n, *, strict=False, upper=False):
    r = lax.broadcasted_iota(jnp.int32, (n, n), 0)
    c = lax.broadcasted_iota(jnp.int32, (n, n), 1)
    if upper:
        r, c = c, r
    return (c < r) if strict else (c <= r)


def _cumsum_body(x_ref, col_ref, row_ref, carry_ref):
    @pl.when(pl.program_id(1) == 0)
    def _():
        carry_ref[...] = jnp.zeros_like(carry_ref)

    n = x_ref.shape[1]
    tril = _tri(n).astype(BF16)
    c = _dot_sel_l(tril, x_ref[0]) + carry_ref[...]
    carry_ref[...] = c[n - 1:n, :]
    col_ref[0] = c
    row_ref[0] = c.T[:FOX_HEADS, :]


def _cumsum(x, *, tb=ATT_BLOCK):
    b, l, _ = x.shape
    return pl.pallas_call(
        _cumsum_body,
        grid=(b, l // tb),
        in_specs=[pl.BlockSpec((1, tb, LANES), lambda i, j: (i, j, 0))],
        out_specs=[pl.BlockSpec((1, tb, LANES), lambda i, j: (i, j, 0)),
                   pl.BlockSpec((1, FOX_HEADS, tb), lambda i, j: (i, 0, j))],
        out_shape=[jax.ShapeDtypeStruct((b, l, LANES), F32), jax.ShapeDtypeStruct((b, FOX_HEADS, l), F32)],
        scratch_shapes=[pltpu.VMEM((1, LANES), F32)],
        compiler_params=_params("parallel", "arbitrary"),
        name="logf_cumsum",
    )(x)


def _fox_prompt_body(tk, q_ref, k_ref, v_ref, ccol_ref, crow_ref, o_ref, m_ref, l_ref, cq_ref, acc_ref):
    i = pl.program_id(1)
    tq = q_ref.shape[0]
    rep = tk // LANES
    m_ref[...] = jnp.full_like(m_ref, -jnp.inf)
    l_ref[...] = jnp.zeros_like(l_ref)
    acc_ref[...] = jnp.zeros_like(acc_ref)
    for h in range(FOX_HEADS):
        cq_ref[h] = jnp.broadcast_to(ccol_ref[0, :, h:h + 1], (tq, LANES))
    row = i * tq + lax.broadcasted_iota(jnp.int32, (tq, tk), 0)
    col = lax.broadcasted_iota(jnp.int32, (tq, tk), 1)
    wide = lambda x: jnp.concatenate([x] * rep, axis=1)

    def block(j, masked):
        ks = pl.ds(pl.multiple_of(j * tk, tk), tk)
        for h in range(FOX_HEADS):
            hs = slice(h * HEAD_DIM, (h + 1) * HEAD_DIM)
            s = _dot_nt(q_ref[:, hs], k_ref[ks, hs]) + (wide(cq_ref[h]) - crow_ref[0, h, pl.ds(j, 1), :])
            if masked:
                s = jnp.where(col + j * tk <= row, s, -jnp.inf)
            m_old = m_ref[h]
            m_new = jnp.maximum(m_old, jnp.max(s, axis=-1, keepdims=True))
            alpha = jnp.exp(m_old - m_new)
            p = jnp.exp(s - wide(m_new))
            m_ref[h] = m_new
            l_ref[h] = alpha * l_ref[h] + jnp.sum(p, axis=-1, keepdims=True)
            acc_ref[h] = alpha * acc_ref[h] + _dot(p.astype(BF16), v_ref[ks, hs])

    last = (i * tq + tq - 1) // tk

    def step(j, carry):
        block(j, False)
        return carry

    lax.fori_loop(0, last, step, 0)
    block(last, True)
    for h in range(FOX_HEADS):
        o_ref[:, h * HEAD_DIM:(h + 1) * HEAD_DIM] = (acc_ref[h] / l_ref[h]).astype(o_ref.dtype)


def _fox_prompt(q, k, v, ccol, crow, b, l, *, tq=128):
    tk = crow.shape[-1]
    tq = min(tq, l)
    assert tk % tq == 0
    nq = l // tq
    w = FOX_WIDTH
    return pl.pallas_call(
        functools.partial(_fox_prompt_body, tk),
        grid=(b, nq),
        in_specs=[
            pl.BlockSpec((tq, w), lambda bi, i: (bi * nq + i, 0)),
            pl.BlockSpec((l, w), lambda bi, i: (bi, 0)),
            pl.BlockSpec((l, w), lambda bi, i: (bi, 0)),
            pl.BlockSpec((1, tq, LANES), lambda bi, i: (bi, i, 0)),
            pl.BlockSpec((1, FOX_HEADS, l // tk, tk), lambda bi, i: (bi, 0, 0, 0)),
        ],
        out_specs=pl.BlockSpec((tq, w), lambda bi, i: (bi * nq + i, 0)),
        out_shape=jax.ShapeDtypeStruct((b * l, w), BF16),
        scratch_shapes=[pltpu.VMEM((FOX_HEADS, tq, LANES), F32)] * 3 + [pltpu.VMEM((FOX_HEADS, tq, HEAD_DIM), F32)],
        compiler_params=_params("parallel", "arbitrary"),
        name="fox_prompt",
    )(q, k, v, ccol, crow)


def _page_cumsum_body(x_ref, w_ref, m_ref):
    n = PAGE_SIZE * FOX_HEADS

    @pl.when(pl.program_id(0) == 0)
    def _():
        r = lax.broadcasted_iota(jnp.int32, (n, n), 0)
        c = lax.broadcasted_iota(jnp.int32, (n, n), 1)
        same_head = (r & (FOX_HEADS - 1)) == (c & (FOX_HEADS - 1))
        earlier = lax.shift_right_logical(r, 3) <= lax.shift_right_logical(c, 3)
        m_ref[...] = jnp.logical_and(same_head, earlier).astype(BF16)

    w_ref[...] = _dot_sel_r(x_ref[...], m_ref[...])


def _page_cumsum(logf_pages, *, tb=256):
    n_pool, n = logf_pages.shape
    return pl.pallas_call(
        _page_cumsum_body,
        grid=(n_pool // tb,),
        in_specs=[pl.BlockSpec((tb, n), lambda i: (i, 0))],
        out_specs=pl.BlockSpec((tb, n), lambda i: (i, 0)),
        out_shape=jax.ShapeDtypeStruct((n_pool, n), F32),
        scratch_shapes=[pltpu.VMEM((n, n), BF16)],
        compiler_params=_params("arbitrary"),
        name="page_cumsum",
    )(logf_pages)


def _seq_cumsum_body(seq_len, x_ref, o_ref):
    n = x_ref.shape[0]
    r = lax.broadcasted_iota(jnp.int32, (n, n), 0)
    c = lax.broadcasted_iota(jnp.int32, (n, n), 1)
    same_seq = (r // seq_len) == (c // seq_len)
    o_ref[...] = _dot_sel_l(jnp.logical_and(same_seq, c <= r).astype(BF16), x_ref[...])


def _seq_cumsum(x, seq_len, *, tb=128):
    t = x.shape[0]
    return pl.pallas_call(
        functools.partial(_seq_cumsum_body, seq_len),
        grid=(t // tb,),
        in_specs=[pl.BlockSpec((tb, LANES), lambda i: (i, 0))],
        out_specs=pl.BlockSpec((tb, LANES), lambda i: (i, 0)),
        out_shape=jax.ShapeDtypeStruct((t, LANES), F32),
        compiler_params=_params("parallel"),
        name="seq_cumsum",
    )(x)


def _fox_sample_body(pps, pt_ref, q_ref, kn_ref, vn_ref, cn_ref, *refs):
    kp_refs, vp_refs, w_refs = refs[:pps], refs[pps:2 * pps], refs[2 * pps:3 * pps]
    o_ref, q_scr, colq_ref, toff_ref, m_ref, l_ref, acc_ref = refs[3 * pps:]
    j = pl.program_id(1)
    nq = q_ref.shape[0]
    rows = nq * FOX_HEADS
    page_keys = PAGE_SIZE * FOX_HEADS
    row_id = lax.broadcasted_iota(jnp.int32, (rows, 1), 0)
    head_of_row = row_id & (FOX_HEADS - 1)
    query_of_row = lax.shift_right_logical(row_id, 3)

    @pl.when(j == 0)
    def _():
        q = q_ref[...].reshape(rows, HEAD_DIM).astype(BF16)
        q_scr[...] = q
        cn = cn_ref[0]
        key = lax.broadcasted_iota(jnp.int32, (1, rows), 1)
        colq = jnp.sum(jnp.where(key == row_id, cn, 0.0), axis=-1, keepdims=True)
        colq_ref[...] = colq
        toff_ref[...] = jnp.zeros_like(toff_ref)
        s = _dot_nt(q, kn_ref[...].reshape(rows, HEAD_DIM).astype(BF16)) + colq - cn
        valid = jnp.logical_and((key & (FOX_HEADS - 1)) == head_of_row,
                                lax.shift_right_logical(key, 3) <= query_of_row)
        s = jnp.where(valid, s, -jnp.inf)
        m = jnp.max(s, axis=-1, keepdims=True)
        p = jnp.exp(s - m)
        m_ref[...] = m
        l_ref[...] = jnp.sum(p, axis=-1, keepdims=True)
        acc_ref[...] = _dot(p.astype(BF16), vn_ref[...].reshape(rows, HEAD_DIM).astype(BF16))

    lane = lax.broadcasted_iota(jnp.int32, (1, LANES), 1)
    own_head = (lax.broadcasted_iota(jnp.int32, (1, page_keys), 1) & (FOX_HEADS - 1)) == head_of_row
    q = q_scr[...]
    colq = colq_ref[...]
    toff = toff_ref[...]
    tiles = []
    for kp_ref, w_ref in zip(kp_refs, w_refs):
        w = w_ref[0]
        last = jnp.where(lane == LANES - FOX_HEADS + head_of_row, w[:, page_keys - LANES:], 0.0)
        toff = toff + jnp.sum(last, axis=-1, keepdims=True)
        s = _dot_nt(q, kp_ref[0].reshape(page_keys, HEAD_DIM).astype(BF16))
        tiles.append(jnp.where(own_head, s + (colq + toff) - w, -jnp.inf))
    toff_ref[...] = toff
    m_old = m_ref[...]
    m = m_old
    for s in tiles:
        m = jnp.maximum(m, jnp.max(s, axis=-1, keepdims=True))
    alpha = jnp.exp(m_old - m)
    l = alpha * l_ref[...]
    acc = alpha * acc_ref[...]
    for s, vp_ref in zip(tiles, vp_refs):
        p = jnp.exp(s - m)
        l = l + jnp.sum(p, axis=-1, keepdims=True)
        acc = acc + _dot(p.astype(BF16), vp_ref[0].reshape(page_keys, HEAD_DIM).astype(BF16))
    m_ref[...] = m
    l_ref[...] = l
    acc_ref[...] = acc

    @pl.when(j == pl.num_programs(1) - 1)
    def _():
        o_ref[...] = (acc_ref[...] / l_ref[...]).reshape(nq, FOX_HEADS, HEAD_DIM)


def _fox_sample(page_table, q, k_new, v_new, c_new, k_pages, v_pages, w_pages, n_seq, nq, *, pages_per_step=16):
    n_pages = page_table.shape[1]
    rows = nq * FOX_HEADS
    pps = math.gcd(pages_per_step, n_pages)

    def page(i, ndim):
        return lambda b, j, pt: (pt[b * n_pages + (n_pages - 1 - j * pps - i)],) + (0,) * (ndim - 1)

    seq = pl.BlockSpec((nq, FOX_HEADS, HEAD_DIM), lambda b, j, pt: (b, 0, 0))
    kv_specs = [pl.BlockSpec((1, PAGE_SIZE, FOX_HEADS, HEAD_DIM), page(i, 4)) for i in range(pps)]
    w_specs = [pl.BlockSpec((1, 1, PAGE_SIZE * FOX_HEADS), page(i, 3)) for i in range(pps)]
    grid_spec = pltpu.PrefetchScalarGridSpec(
        num_scalar_prefetch=1,
        grid=(n_seq, n_pages // pps),
        in_specs=[seq, seq, seq, pl.BlockSpec((1, 1, rows), lambda b, j, pt: (b, 0, 0))]
        + kv_specs + kv_specs + w_specs,
        out_specs=seq,
        scratch_shapes=[
            pltpu.VMEM((rows, HEAD_DIM), BF16), pltpu.VMEM((rows, 1), F32), pltpu.VMEM((rows, 1), F32),
            pltpu.VMEM((rows, 1), F32), pltpu.VMEM((rows, 1), F32), pltpu.VMEM((rows, HEAD_DIM), F32),
        ],
    )
    return pl.pallas_call(
        functools.partial(_fox_sample_body, pps),
        grid_spec=grid_spec,
        out_shape=jax.ShapeDtypeStruct((n_seq * nq, FOX_HEADS, HEAD_DIM), F32),
        compiler_params=_params("parallel", "arbitrary"),
        name="fox_sample",
    )(page_table.reshape(-1), q, k_new, v_new, c_new, *([k_pages] * pps), *([v_pages] * pps), *([w_pages] * pps))


SSD_PAIRS = SSD_HEADS // 2
PAIRS_PER_GROUP = SSD_PAIRS // SSD_GROUPS


def _expander(width):
    n = SSD_HEADS * width
    h = lax.broadcasted_iota(jnp.int32, (LANES, n), 0)
    c = lax.broadcasted_iota(jnp.int32, (LANES, n), 1)
    return (lax.shift_right_logical(c, int(math.log2(width))) == h).astype(BF16)


def _ssd_local(xs, bm, cm, dt, alog, mask, tot_sel):
    n = xs.shape[0]
    lane = lax.broadcasted_iota(jnp.int32, (1, LANES), 1)
    dta = dt * jnp.where(lane < SSD_HEADS, -jnp.exp(alog), 0.0)
    e64 = _expander(SSD_HEAD_DIM)
    a_cum = _dot_sel_l(mask.astype(BF16), dta)
    a_cum_t = a_cum.T
    ac_exp = _dot_sel_r(a_cum, e64)
    if tot_sel is None:
        atot_exp = ac_exp[n - 1:n, :]
    else:
        atot_exp = _dot_sel_r(_dot_sel_l(tot_sel, dta), e64)
    ac_b = _dot_sel_r(a_cum, _expander(LANES))
    xdt = xs * _dot_sel_r(dt, e64)
    half = lax.broadcasted_iota(jnp.int32, (n, LANES), 1) < SSD_HEAD_DIM
    out = {
        "xdtw": xdt * jnp.exp(atot_exp - ac_exp),
        "eac": jnp.exp(ac_exp),
        "atot_exp": atot_exp,
        "ac_b": ac_b,
        "bg": [], "cg": [], "y_diag": [],
    }
    for g in range(SSD_GROUPS):
        gs = slice(g * SSD_STATE, (g + 1) * SSD_STATE)
        bg = bm[:, gs].astype(BF16)
        cg = cm[:, gs].astype(BF16)
        out["bg"].append(bg)
        out["cg"].append(cg)
        cb = _dot_nt(cg, bg)
        for k in range(g * PAIRS_PER_GROUP, (g + 1) * PAIRS_PER_GROUP):
            ps = slice(k * LANES, (k + 1) * LANES)
            ms = []
            for h in (2 * k, 2 * k + 1):
                seg = ac_b[:, h * LANES:(h + 1) * LANES] - a_cum_t[h:h + 1, :]
                ms.append(cb * jnp.exp(jnp.where(mask, seg, -jnp.inf)))
            m_cat = jnp.concatenate(ms, axis=1).astype(BF16)
            xp = xdt[:, ps]
            x_bd = jnp.concatenate([jnp.where(half, xp, 0.0), jnp.where(half, 0.0, xp)], axis=0).astype(BF16)
            out["y_diag"].append(_dot(m_cat, x_bd))
    return out


def _conv_ssd_sample_body(seq_len, xbc_ref, dt_ref, buf_ref, h0_ref, w_ref, b_ref, alog_ref, dexp_ref,
                          y_ref, hout_ref, cout_ref, xp_ref):
    n = xbc_ref.shape[0]
    n_seq = n // seq_len
    taps = CONV_WIDTH - 1
    base = 8 - taps

    x = xbc_ref[...]
    xp_ref[:, base:8, :] = buf_ref[...]
    xp_ref[:, 8:8 + seq_len, :] = x.reshape(n_seq, seq_len, CONV_DIM)
    acc = b_ref[...] + x * w_ref[taps:taps + 1, :]
    for j in range(taps):
        acc = acc + xp_ref[:, base + j:base + j + seq_len, :].reshape(n, CONV_DIM) * w_ref[j:j + 1, :]
    cout_ref[...] = xp_ref[:, 8 + seq_len - taps:8 + seq_len, :]
    conv = _silu(acc)
    xs = conv[:, :SSD_WIDTH]
    bm = conv[:, SSD_WIDTH:SSD_WIDTH + SSD_GROUPS * SSD_STATE]
    cm = conv[:, SSD_WIDTH + SSD_GROUPS * SSD_STATE:]

    r = lax.broadcasted_iota(jnp.int32, (n, n), 0)
    c = lax.broadcasted_iota(jnp.int32, (n, n), 1)
    same_seq = (r // seq_len) == (c // seq_len)
    loc = _ssd_local(xs, bm, cm, dt_ref[...], alog_ref[...], jnp.logical_and(same_seq, c <= r),
                     same_seq.astype(BF16))

    gw = PAIRS_PER_GROUP * LANES
    seq_of_col = lax.broadcasted_iota(jnp.int32, (1, n), 1) // seq_len
    decay_t = jnp.exp(loc["atot_exp"]).T
    for g in range(SSD_GROUPS):
        gr = slice(g * gw, (g + 1) * gw)
        h_prev = h0_ref[:, gr, :]
        z = _dot_nt(h_prev.reshape(n_seq * gw, SSD_STATE).astype(BF16), loc["cg"][g])
        y_off_t = jnp.zeros((gw, n), F32)
        for s in range(n_seq):
            y_off_t = y_off_t + jnp.where(seq_of_col == s, z[s * gw:(s + 1) * gw, :], 0.0)
        y_off = y_off_t.T * loc["eac"][:, gr]
        y = jnp.concatenate(loc["y_diag"][g * PAIRS_PER_GROUP:(g + 1) * PAIRS_PER_GROUP], axis=1)
        y_ref[:, gr] = y + y_off + xs[:, gr] * dexp_ref[:, gr]
        xw_t = loc["xdtw"][:, gr].T
        lhs = jnp.concatenate([jnp.where(seq_of_col == s, xw_t, 0.0) for s in range(n_seq)], axis=0)
        s_new = _dot(lhs.astype(BF16), loc["bg"][g])
        for s in range(n_seq):
            col = decay_t[gr, s * seq_len:s * seq_len + 1]
            hout_ref[s, gr, :] = h_prev[s] * col + s_new[s * gw:(s + 1) * gw, :]


def _conv_ssd_sample(xbc, dt, conv_buf, h0, conv_w, conv_b, alog, d_exp, n_seq, seq_len):
    assert seq_len == 8 and CONV_WIDTH - 1 <= seq_len
    tile = LANES
    ts = tile // seq_len
    const = lambda i: (0, 0)
    per_s = lambda i: (i, 0, 0)
    return pl.pallas_call(
        functools.partial(_conv_ssd_sample_body, seq_len),
        grid=(n_seq // ts,),
        in_specs=[
            pl.BlockSpec((tile, CONV_DIM), lambda i: (i, 0)),
            pl.BlockSpec((tile, LANES), lambda i: (i, 0)),
            pl.BlockSpec((ts, CONV_WIDTH - 1, CONV_DIM), per_s),
            pl.BlockSpec((ts, SSD_WIDTH, SSD_STATE), per_s),
            pl.BlockSpec((CONV_WIDTH, CONV_DIM), const),
            pl.BlockSpec((1, CONV_DIM), const),
            pl.BlockSpec((1, LANES), const),
            pl.BlockSpec((1, SSD_WIDTH), const),
        ],
        out_specs=[
            pl.BlockSpec((tile, SSD_WIDTH), lambda i: (i, 0)),
            pl.BlockSpec((ts, SSD_WIDTH, SSD_STATE), per_s),
            pl.BlockSpec((ts, CONV_WIDTH - 1, CONV_DIM), per_s),
        ],
        out_shape=[
            jax.ShapeDtypeStruct((n_seq * seq_len, SSD_WIDTH), F32),
            jax.ShapeDtypeStruct((n_seq, SSD_WIDTH, SSD_STATE), F32),
            jax.ShapeDtypeStruct((n_seq, CONV_WIDTH - 1, CONV_DIM), F32),
        ],
        scratch_shapes=[pltpu.VMEM((ts, 8 + seq_len, CONV_DIM), F32)],
        compiler_params=_params("parallel"),
        name="conv_ssd_sample",
    )(xbc, dt, conv_buf, h0, conv_w, conv_b, alog, d_exp)


def _conv_ssd_prompt_body(xbc_ref, dt_ref, buf_ref, h0_ref, w_ref, b_ref, alog_ref, dexp_ref,
                          y_ref, hout_ref, cout_ref, state_ref, xp_ref):
    c = pl.program_id(1)
    nc = pl.num_programs(1)
    tl = xbc_ref.shape[0]
    taps = CONV_WIDTH - 1
    base = 8 - taps

    @pl.when(c == 0)
    def _():
        state_ref[...] = h0_ref[0]
        xp_ref[base:8, :] = buf_ref[0]

    x = xbc_ref[...]
    xp_ref[8:8 + tl, :] = x
    acc = b_ref[...] + x * w_ref[taps:taps + 1, :]
    for j in range(taps):
        acc = acc + xp_ref[base + j:base + j + tl, :] * w_ref[j:j + 1, :]
    xp_ref[base:8, :] = x[tl - taps:, :]
    conv = _silu(acc)
    xs = conv[:, :SSD_WIDTH]
    bm = conv[:, SSD_WIDTH:SSD_WIDTH + SSD_GROUPS * SSD_STATE]
    cm = conv[:, SSD_WIDTH + SSD_GROUPS * SSD_STATE:]

    causal = _tri(tl)
    loc = _ssd_local(xs, bm, cm, dt_ref[...], alog_ref[...], causal, None)
    top = lax.broadcasted_iota(jnp.int32, (tl, LANES), 0) < SSD_HEAD_DIM
    for k in range(SSD_PAIRS):
        g = k // PAIRS_PER_GROUP
        ps = slice(k * LANES, (k + 1) * LANES)
        s_prev = state_ref[ps, :]
        y_off = _dot_nt(loc["cg"][g], s_prev.astype(BF16)) * loc["eac"][:, ps]
        y_ref[:, ps] = loc["y_diag"][k] + y_off + xs[:, ps] * dexp_ref[:, ps]
        cd = [jnp.exp(loc["ac_b"][tl - 1:tl, h * LANES:(h + 1) * LANES]) for h in (2 * k, 2 * k + 1)]
        state_ref[ps, :] = s_prev * jnp.where(top, cd[0], cd[1]) + _dot_tn(loc["xdtw"][:, ps].astype(BF16), loc["bg"][g])

    @pl.when(c == nc - 1)
    def _():
        hout_ref[0] = state_ref[...]
        cout_ref[0] = xp_ref[base:8, :]


def _conv_ssd_prompt(xbc, dt, conv_buf, h0, conv_w, conv_b, alog, d_exp, b, l):
    tl = SSD_CHUNK
    nc = l // tl
    const = lambda bi, c: (0, 0)
    per_b = lambda bi, c: (bi, 0, 0)
    return pl.pallas_call(
        _conv_ssd_prompt_body,
        grid=(b, nc),
        in_specs=[
            pl.BlockSpec((tl, CONV_DIM), lambda bi, c: (bi * nc + c, 0)),
            pl.BlockSpec((tl, LANES), lambda bi, c: (bi * nc + c, 0)),
            pl.BlockSpec((1, CONV_WIDTH - 1, CONV_DIM), per_b),
            pl.BlockSpec((1, SSD_WIDTH, SSD_STATE), per_b),
            pl.BlockSpec((CONV_WIDTH, CONV_DIM), const),
            pl.BlockSpec((1, CONV_DIM), const),
            pl.BlockSpec((1, LANES), const),
            pl.BlockSpec((1, SSD_WIDTH), const),
        ],
        out_specs=[
            pl.BlockSpec((tl, SSD_WIDTH), lambda bi, c: (bi * nc + c, 0)),
            pl.BlockSpec((1, SSD_WIDTH, SSD_STATE), per_b),
            pl.BlockSpec((1, CONV_WIDTH - 1, CONV_DIM), per_b),
        ],
        out_shape=[
            jax.ShapeDtypeStruct((b * l, SSD_WIDTH), F32),
            jax.ShapeDtypeStruct((b, SSD_WIDTH, SSD_STATE), F32),
            jax.ShapeDtypeStruct((b, CONV_WIDTH - 1, CONV_DIM), F32),
        ],
        scratch_shapes=[pltpu.VMEM((SSD_WIDTH, SSD_STATE), F32), pltpu.VMEM((8 + tl, CONV_DIM), F32)],
        compiler_params=_params("parallel", "arbitrary"),
        name="conv_ssd_prompt",
    )(xbc, dt, conv_buf, h0, conv_w, conv_b, alog, d_exp)


def _out_proj_body(h_ref, fox_ref, y_ref, z_ref, gs_ref, wf_ref, ws_ref, gx_ref, wq_ref, gq_ref, o_ref, q_ref):
    yn = _rms(y_ref[...] * _silu(z_ref[...]), gs_ref[...]).astype(BF16)
    h = h_ref[...] + _dot(fox_ref[...].astype(BF16), wf_ref[...]) + _dot(yn, ws_ref[...])
    o_ref[...] = h
    q = _dot(_rms(h, gx_ref[...]).astype(BF16), wq_ref[...])
    q_ref[...] = _head_norm(q, gq_ref[...]) * (HEAD_DIM ** -0.5)


def _out_proj(h, fox, y, z, g_ssd, w_fox, w_ssd, g_x, wq, gq, *, tm=256):
    t, d = h.shape
    tm = min(tm, t)
    row = lambda w: pl.BlockSpec((tm, w), lambda i: (i, 0))
    full = lambda a: pl.BlockSpec(a.shape, lambda i: (0, 0))
    consts = [g_ssd.reshape(1, SSD_WIDTH), w_fox, w_ssd, g_x.reshape(1, d), wq, gq.reshape(1, HEAD_DIM)]
    return pl.pallas_call(
        _out_proj_body,
        grid=(t // tm,),
        in_specs=[row(d), row(FOX_WIDTH), row(SSD_WIDTH), row(SSD_WIDTH)] + [full(a) for a in consts],
        out_specs=[row(d), row(XATTN_WIDTH)],
        out_shape=[jax.ShapeDtypeStruct((t, d), F32), jax.ShapeDtypeStruct((t, XATTN_WIDTH), F32)],
        compiler_params=_params("parallel"),
        name="out_proj",
    )(h, fox, y, z, *consts)


def _mem_kv_body(m_ref, g_ref, w_ref, gk_ref, k_ref, v_ref):
    kv = _dot(_rms(m_ref[...], g_ref[...]).astype(BF16), w_ref[...])
    k_ref[...] = _head_norm(kv[:, :XATTN_WIDTH], gk_ref[...])
    v_ref[...] = kv[:, XATTN_WIDTH:]


def _mem_kv(mem, g, w_kv, gk, *, tm=256):
    t, d = mem.shape
    row = lambda w: pl.BlockSpec((tm, w), lambda i: (i, 0))
    full = lambda a: pl.BlockSpec(a.shape, lambda i: (0, 0))
    consts = [g.reshape(1, d), w_kv, gk.reshape(1, HEAD_DIM)]
    return pl.pallas_call(
        _mem_kv_body,
        grid=(t // tm,),
        in_specs=[row(d)] + [full(a) for a in consts],
        out_specs=[row(XATTN_WIDTH), row(XATTN_WIDTH)],
        out_shape=[jax.ShapeDtypeStruct((t, XATTN_WIDTH), F32)] * 2,
        compiler_params=_params("parallel"),
        name="mem_kv",
    )(mem, *consts)


def _xattn_body(q_ref, k_ref, v_ref, o_ref):
    for h in range(XATTN_HEADS):
        hs = slice(h * HEAD_DIM, (h + 1) * HEAD_DIM)
        s = _dot_nt(q_ref[:, hs].astype(BF16), k_ref[0, :, hs].astype(BF16))
        p = jnp.exp(s - jnp.max(s, axis=-1, keepdims=True))
        o = _dot(p.astype(BF16), v_ref[0, :, hs].astype(BF16))
        o_ref[:, hs] = o / jnp.sum(p, axis=-1, keepdims=True)


def _xattn(q, mem_k, mem_v, b, l, *, tq=512):
    tq = min(tq, l)
    nq = l // tq
    n_mem = mem_k.shape[1]
    qspec = pl.BlockSpec((tq, XATTN_WIDTH), lambda bi, i: (bi * nq + i, 0))
    mspec = pl.BlockSpec((1, n_mem, XATTN_WIDTH), lambda bi, i: (bi, 0, 0))
    return pl.pallas_call(
        _xattn_body,
        grid=(b, nq),
        in_specs=[qspec, mspec, mspec],
        out_specs=qspec,
        out_shape=jax.ShapeDtypeStruct((b * l, XATTN_WIDTH), F32),
        compiler_params=_params("parallel", "arbitrary"),
        name="xattn",
    )(q, mem_k, mem_v)


def _xattn_rows_body(n_seq, q_ref, k_ref, v_ref, o_ref):
    tiles = q_ref.shape[0] // n_seq
    rows = tiles * 8
    keys = k_ref.shape[1] * k_ref.shape[2]
    row_head = lax.broadcasted_iota(jnp.int32, (rows, 1), 0) & (XATTN_HEADS - 1)
    key_head = lax.broadcasted_iota(jnp.int32, (1, keys), 1) & (XATTN_HEADS - 1)
    own_head = row_head == key_head
    for s in range(n_seq):
        qs = slice(s * tiles, (s + 1) * tiles)
        q = q_ref[qs].reshape(rows, HEAD_DIM).astype(BF16)
        sc = jnp.where(own_head, _dot_nt(q, k_ref[s].reshape(keys, HEAD_DIM).astype(BF16)), -jnp.inf)
        p = jnp.exp(sc - jnp.max(sc, axis=-1, keepdims=True))
        o = _dot(p.astype(BF16), v_ref[s].reshape(keys, HEAD_DIM).astype(BF16))
        o_ref[qs] = (o / jnp.sum(p, axis=-1, keepdims=True)).reshape(tiles, 8, HEAD_DIM)


def _xattn_rows(q, mem_k, mem_v, n_seq, *, ts=8):
    tiles = q.shape[0] // n_seq
    ts = math.gcd(ts, n_seq)
    qspec = pl.BlockSpec((ts * tiles, 8, HEAD_DIM), lambda i: (i, 0, 0))
    mspec = pl.BlockSpec((ts,) + mem_k.shape[1:], lambda i: (i, 0, 0, 0))
    return pl.pallas_call(
        functools.partial(_xattn_rows_body, ts),
        grid=(n_seq // ts,),
        in_specs=[qspec, mspec, mspec],
        out_specs=qspec,
        out_shape=jax.ShapeDtypeStruct(q.shape, F32),
        compiler_params=_params("parallel"),
        name="xattn_rows",
    )(q, mem_k, mem_v)


def kernel(x_prompt, x_sample, cache_fox_k, cache_fox_v, cache_fox_logf, cache_mem_k, cache_mem_v, state_ssm, state_conv, page_table, mem_prompt, ffn1_norm, ffn1_w_gate, ffn1_w_up, ffn1_w_down, mix_norm, w_in, fox_b_f, fox_q_norm, fox_k_norm, conv_w, conv_b, ssd_dt_bias, ssd_A_log, ssd_D, ssd_out_norm, w_out, xattn_norm, mem_norm, xattn_w_q, xattn_w_kv, xattn_q_norm, xattn_k_norm, xattn_w_o, ffn2_norm, ffn2_w_gate, ffn2_w_up, ffn2_w_down):
    assert x_prompt.shape[2] == D_MODEL and ffn1_norm.shape[0] == 1
    d = D_MODEL
    bp, lp = x_prompt.shape[:2]
    bs, ls = x_sample.shape[:2]
    n_mem = mem_prompt.shape[1]
    n_pool = cache_fox_k.shape[1]

    bf = lambda w: w[0].astype(BF16)
    ffn1 = (ffn1_norm[0], bf(ffn1_w_gate), bf(ffn1_w_up), bf(ffn1_w_down))
    ffn2 = (ffn2_norm[0], bf(ffn2_w_gate), bf(ffn2_w_up), bf(ffn2_w_down))
    in_w = _pack_in_proj(w_in[0], fox_b_f[0], ssd_dt_bias[0])
    w_fox, w_ssd = bf(w_out)[:FOX_WIDTH], bf(w_out)[FOX_WIDTH:]
    wq, wkv, wo = bf(xattn_w_q), bf(xattn_w_kv), bf(xattn_w_o)
    alog = jnp.pad(ssd_A_log[0], (0, LANES - SSD_HEADS)).reshape(1, LANES)
    d_exp = jnp.repeat(ssd_D[0], SSD_HEAD_DIM).reshape(1, SSD_WIDTH)
    cw, cb = conv_w[0], conv_b[0].reshape(1, CONV_DIM)

    def front(x, q_dtype):
        h1 = _ffn(x, *ffn1)
        return h1, _in_proj(h1, mix_norm[0], *in_w, fox_q_norm[0], fox_k_norm[0], q_dtype=q_dtype)

    def back(h1, fox, y, z, attend):
        h2, xq = _out_proj(h1, fox, y, z, ssd_out_norm[0], w_fox, w_ssd, xattn_norm[0], wq, xattn_q_norm[0])
        return _ffn(h2, *ffn2, pre=(attend(xq), wo))

    h1, (q, k_p, kb, v_p, vb, z, xbc, logf_p, dt) = front(x_prompt.reshape(bp * lp, d), BF16)
    ccol, crow = _cumsum(logf_p.reshape(bp, lp, LANES))
    tq = min(ATT_BLOCK, lp)
    fox = _fox_prompt(q, kb, vb, ccol, crow.reshape(bp, FOX_HEADS, lp // tq, tq), bp, lp)
    y, ssm_p, conv_p = _conv_ssd_prompt(
        xbc, dt, jnp.zeros((bp, CONV_WIDTH - 1, CONV_DIM), F32), jnp.zeros((bp, SSD_WIDTH, SSD_STATE), F32),
        cw, cb, alog, d_exp, bp, lp)
    mk, mv = _mem_kv(mem_prompt.reshape(bp * n_mem, d), mem_norm[0], wkv, xattn_k_norm[0])
    y_prompt = back(h1, fox, y, z, lambda xq: _xattn(
        xq, mk.reshape(bp, n_mem, XATTN_WIDTH), mv.reshape(bp, n_mem, XATTN_WIDTH), bp, lp))

    h1, (q, k_s, _, v_s, _, z, xbc, logf_s, dt) = front(x_sample.reshape(bs * ls, d), F32)
    w_pages = _page_cumsum(cache_fox_logf[0].reshape(n_pool, PAGE_SIZE * FOX_HEADS))
    c_new = _seq_cumsum(logf_s, ls)[:, :FOX_HEADS].reshape(bs, 1, ls * FOX_HEADS)
    fox = _fox_sample(
        page_table, q.reshape(bs * ls, FOX_HEADS, HEAD_DIM), k_s, v_s, c_new, cache_fox_k[0], cache_fox_v[0],
        w_pages.reshape(n_pool, 1, PAGE_SIZE * FOX_HEADS), bs, ls).reshape(bs * ls, FOX_WIDTH)
    y, ssm_s, conv_s = _conv_ssd_sample(
        xbc, dt, state_conv[0], state_ssm[0].reshape(bs, SSD_WIDTH, SSD_STATE), cw, cb, alog, d_exp, bs, ls)
    mem_tiles = lambda m: m[0].reshape(bs, n_mem * XATTN_HEADS // 8, 8, HEAD_DIM)
    y_sample = back(h1, fox, y, z, lambda xq: _xattn_rows(
        xq.reshape(bs * ls * XATTN_HEADS // 8, 8, HEAD_DIM), mem_tiles(cache_mem_k), mem_tiles(cache_mem_v), bs,
    ).reshape(bs * ls, XATTN_WIDTH))

    fox_shape = lambda b, l: (1, b, l, FOX_HEADS, HEAD_DIM)
    ssm_shape = lambda b: (1, b, SSD_HEADS, SSD_HEAD_DIM, SSD_STATE)
    mem_shape = (1, bp, n_mem, XATTN_HEADS, HEAD_DIM)
    return (
        y_prompt.reshape(bp, lp, d), y_sample.reshape(bs, ls, d),
        k_p.reshape(fox_shape(bp, lp)), v_p.reshape(fox_shape(bp, lp)),
        logf_p[:, :FOX_HEADS].reshape(1, bp, lp, FOX_HEADS),
        ssm_p.reshape(ssm_shape(bp)), conv_p[None], mk.reshape(mem_shape), mv.reshape(mem_shape),
        k_s.reshape(fox_shape(bs, ls)), v_s.reshape(fox_shape(bs, ls)),
        logf_s[:, :FOX_HEADS].reshape(1, bs, ls, FOX_HEADS),
        ssm_s.reshape(ssm_shape(bs)), conv_s[None],
    )
```

```python
import functools
import math

import jax
import jax.numpy as jnp
from jax import lax
from jax.experimental import pallas as pl
from jax.experimental.pallas import tpu as pltpu

F32 = jnp.float32
BF16 = jnp.bfloat16

EPS = 1e-6
LOG2E = math.log2(math.e)
FFN_RESIDUAL = 0.5
D_MODEL = 2048
D_FF = 5632
PAGE_SIZE = 128
FOX_HEADS = 8
HEAD_DIM = 128
FOX_WIDTH = FOX_HEADS * HEAD_DIM
SSD_HEADS = 16
SSD_HEAD_DIM = 64
SSD_WIDTH = SSD_HEADS * SSD_HEAD_DIM
SSD_GROUPS = 2
SSD_STATE = 128
SSD_CHUNK = 128
CONV_WIDTH = 4
CONV_DIM = SSD_WIDTH + 2 * SSD_GROUPS * SSD_STATE
XATTN_HEADS = 4
XATTN_WIDTH = XATTN_HEADS * HEAD_DIM
LANES = 128
VMEM_LIMIT_BYTES = 56 * 1024 * 1024

def _params(*semantics):
    return pltpu.CompilerParams(dimension_semantics=semantics, vmem_limit_bytes=VMEM_LIMIT_BYTES)


def _rms(x, g):
    return x * lax.rsqrt(jnp.mean(x * x, axis=-1, keepdims=True) + EPS) * g


def _dot(a, b):
    return jnp.dot(a, b, preferred_element_type=F32)


def _dot_nt(a, b):
    return lax.dot_general(a, b, (((1,), (1,)), ((), ())), preferred_element_type=F32)


def _dot_tn(a, b):
    return lax.dot_general(a, b, (((0,), (0,)), ((), ())), preferred_element_type=F32)


def _split3(x):
    x1 = x.astype(BF16)
    r = x - x1.astype(F32)
    x2 = r.astype(BF16)
    x3 = (r - x2.astype(F32)).astype(BF16)
    return x1, x2, x3


def _dot_sel_l(sel, x):
    x1, x2, x3 = _split3(x)
    return _dot(sel, x1) + _dot(sel, x2) + _dot(sel, x3)


def _dot_sel_r(x, sel):
    x1, x2, x3 = _split3(x)
    return _dot(x1, sel) + _dot(x2, sel) + _dot(x3, sel)


def _silu(x):
    return x * jax.nn.sigmoid(x)


def _ffn_body(pre_proj, *refs):
    if pre_proj:
        x_ref, a_ref, wo_ref, g_ref, wg_ref, wu_ref, wd_ref, o_ref, xn_ref = refs
    else:
        x_ref, g_ref, wg_ref, wu_ref, wd_ref, o_ref, xn_ref = refs

    @pl.when(pl.program_id(1) == 0)
    def _():
        x = x_ref[...]
        if pre_proj:
            x = x + _dot(a_ref[...].astype(BF16), wo_ref[...])
        xn_ref[...] = _rms(x, g_ref[...]).astype(BF16)
        o_ref[...] = x

    xn = xn_ref[...]
    gate = _dot(xn, wg_ref[...])
    up = _dot(xn, wu_ref[...])
    h = (_silu(gate) * up * FFN_RESIDUAL).astype(BF16)
    o_ref[...] += _dot(h, wd_ref[...])


def _ffn(x, g, wg, wu, wd, pre=None, *, tm=512, tf=512):
    t, d = x.shape
    f = wg.shape[1]
    tm = min(tm, t)
    grid = (t // tm, f // tf)
    row = lambda i, j: (i, 0)
    in_specs = [pl.BlockSpec((tm, d), row)]
    args = [x]
    if pre is not None:
        a, wo = pre
        in_specs += [pl.BlockSpec((tm, a.shape[1]), row), pl.BlockSpec(wo.shape, lambda i, j: (0, 0))]
        args += [a, wo]
    in_specs += [
        pl.BlockSpec((1, d), lambda i, j: (0, 0)),
        pl.BlockSpec((d, tf), lambda i, j: (0, j)),
        pl.BlockSpec((d, tf), lambda i, j: (0, j)),
        pl.BlockSpec((tf, d), lambda i, j: (j, 0)),
    ]
    args += [g.reshape(1, d), wg, wu, wd]
    return pl.pallas_call(
        functools.partial(_ffn_body, pre is not None),
        grid=grid,
        in_specs=in_specs,
        out_specs=pl.BlockSpec((tm, d), row),
        out_shape=jax.ShapeDtypeStruct((t, d), F32),
        scratch_shapes=[pltpu.VMEM((tm, d), BF16)],
        compiler_params=_params("parallel", "arbitrary"),
        name="ffn_pre" if pre is not None else "ffn",
    )(*args)


IN_QKV_TN = 1024
IN_ZX_TN = 512


def _head_norm(y, g):
    outs = []
    for c in range(y.shape[1] // HEAD_DIM):
        yc = y[:, c * HEAD_DIM:(c + 1) * HEAD_DIM]
        outs.append(yc * lax.rsqrt(jnp.mean(yc * yc, axis=-1, keepdims=True) + EPS) * g)
    return jnp.concatenate(outs, axis=1)


def _in_qkv_body(q_scale, x_ref, g_ref, w_ref, ws_ref, bs_ref, gq_ref, gk_ref,
                 q_ref, k_ref, kb_ref, v_ref, vb_ref, logf_ref, dt_ref, u_ref):
    j = pl.program_id(1)

    @pl.when(j == 0)
    def _():
        u = _rms(x_ref[...], g_ref[...]).astype(BF16)
        u_ref[...] = u
        s = _dot(u, ws_ref[...]) + bs_ref[...]
        t = jnp.log1p(jnp.exp(-jnp.abs(s)))
        logf_ref[...] = (jnp.minimum(s, 0.0) - t)[:, :LANES]
        dt_ref[...] = (jnp.maximum(s, 0.0) + t)[:, LANES:]
        q_ref[...] = (_head_norm(_dot(u, w_ref[...]), gq_ref[...]) * q_scale).astype(q_ref.dtype)

    def by_head(y, out_ref, bf_ref):
        bf_ref[...] = y.astype(BF16)
        for h in range(FOX_HEADS):
            out_ref[:, h, :] = y[:, h * HEAD_DIM:(h + 1) * HEAD_DIM]

    @pl.when(j == 1)
    def _():
        by_head(_head_norm(_dot(u_ref[...], w_ref[...]), gk_ref[...]), k_ref, kb_ref)

    @pl.when(j == 2)
    def _():
        by_head(_dot(u_ref[...], w_ref[...]), v_ref, vb_ref)


def _in_qkv(x, g, w_qkv, w_small, b_small, gq, gk, *, q_dtype, q_scale, tm=512):
    t, d = x.shape
    tm = min(tm, t)
    tn = IN_QKV_TN
    assert tn == FOX_WIDTH and w_qkv.shape[1] == 3 * tn
    const = lambda i, j: (0, 0)
    rows = lambda w: pl.BlockSpec((tm, w), lambda i, j: (i, 0))
    by_head = pl.BlockSpec((tm, FOX_HEADS, HEAD_DIM), lambda i, j: (i, 0, 0))
    by_head_shape = jax.ShapeDtypeStruct((t, FOX_HEADS, HEAD_DIM), F32)
    flat = lambda dt, w: jax.ShapeDtypeStruct((t, w), dt)
    return pl.pallas_call(
        functools.partial(_in_qkv_body, q_scale),
        grid=(t // tm, 3),
        in_specs=[
            rows(d),
            pl.BlockSpec((1, d), const),
            pl.BlockSpec((d, tn), lambda i, j: (0, j)),
            pl.BlockSpec((d, 2 * LANES), const),
            pl.BlockSpec((1, 2 * LANES), const),
            pl.BlockSpec((1, HEAD_DIM), const),
            pl.BlockSpec((1, HEAD_DIM), const),
        ],
        out_specs=[rows(FOX_WIDTH), by_head, rows(FOX_WIDTH), by_head, rows(FOX_WIDTH), rows(LANES), rows(LANES),
                   rows(d)],
        out_shape=[flat(q_dtype, FOX_WIDTH), by_head_shape, flat(BF16, FOX_WIDTH), by_head_shape,
                   flat(BF16, FOX_WIDTH), flat(F32, LANES), flat(F32, LANES), flat(BF16, d)],
        compiler_params=_params("parallel", "arbitrary"),
        name="in_qkv",
    )(x, g.reshape(1, d), w_qkv, w_small, b_small, gq.reshape(1, HEAD_DIM), gk.reshape(1, HEAD_DIM))


def _in_zx_body(u_ref, w_ref, z_ref, xbc_ref):
    j = pl.program_id(1)
    tn = w_ref.shape[1]
    nz = z_ref.shape[1] // tn
    for c in range(nz + xbc_ref.shape[1] // tn):
        out_ref, sub = (z_ref, c) if c < nz else (xbc_ref, c - nz)

        @pl.when(j == c)
        def _(out_ref=out_ref, sub=sub):
            out_ref[:, sub * tn:(sub + 1) * tn] = _dot(u_ref[...], w_ref[...])


def _in_zx(u, w_zx, *, tm=1024):
    t, d = u.shape
    tm = min(tm, t)
    tn = IN_ZX_TN
    rows = lambda w: pl.BlockSpec((tm, w), lambda i, j: (i, 0))
    return pl.pallas_call(
        _in_zx_body,
        grid=(t // tm, w_zx.shape[1] // tn),
        in_specs=[rows(d), pl.BlockSpec((d, tn), lambda i, j: (0, j))],
        out_specs=[rows(SSD_WIDTH), rows(CONV_DIM)],
        out_shape=[jax.ShapeDtypeStruct((t, SSD_WIDTH), F32), jax.ShapeDtypeStruct((t, CONV_DIM), F32)],
        compiler_params=_params("parallel", "arbitrary"),
        name="in_zx",
    )(u, w_zx)


def _in_proj(x, g, w_qkv, w_zx, w_small, b_small, gq, gk, *, q_dtype, q_scale):
    q, k, kb, v, vb, logf, dt, u = _in_qkv(x, g, w_qkv, w_small, b_small, gq, gk, q_dtype=q_dtype, q_scale=q_scale)
    z, xbc = _in_zx(u, w_zx)
    return q, k, kb, v, vb, z, xbc, logf, dt


def _pack_in_proj(w_in, fox_b_f, ssd_dt_bias):
    fw = FOX_WIDTH
    f0 = 3 * fw
    z0 = f0 + FOX_HEADS
    x0 = z0 + SSD_WIDTH
    d0 = x0 + CONV_DIM
    w_qkv = w_in[:, :f0].astype(BF16)
    w_zx = w_in[:, z0:d0].astype(BF16)
    zeros = lambda n: jnp.zeros((w_in.shape[0], n), w_in.dtype)
    w_small = jnp.concatenate(
        [w_in[:, f0:z0], zeros(LANES - FOX_HEADS), w_in[:, d0:], zeros(LANES - SSD_HEADS)], axis=1).astype(BF16)
    b_small = jnp.concatenate(
        [fox_b_f, jnp.zeros((LANES - FOX_HEADS,), F32), ssd_dt_bias, jnp.zeros((LANES - SSD_HEADS,), F32)]
    ).reshape(1, 2 * LANES)
    return w_qkv, w_zx, w_small, b_small


ATT_BLOCK = 256


def _tri(n, *, strict=False, upper=False):
    r = lax.broadcasted_iota(jnp.int32, (n, n), 0)
    c = lax.broadcasted_iota(jnp.int32, (n, n), 1)
    if upper:
        r, c = c, r
    return (c < r) if strict else (c <= r)


def _cumsum_body(x_ref, col_ref, row_ref, carry_ref):
    @pl.when(pl.program_id(1) == 0)
    def _():
        carry_ref[...] = jnp.zeros_like(carry_ref)

    n = x_ref.shape[1]
    tril = _tri(n).astype(BF16)
    c = _dot_sel_l(tril, x_ref[0]) + carry_ref[...]
    carry_ref[...] = c[n - 1:n, :]
    c = c * LOG2E
    col_ref[0] = c
    row_ref[0] = c.T[:FOX_HEADS, :]


def _cumsum(x, *, tb=ATT_BLOCK):
    b, l, _ = x.shape
    return pl.pallas_call(
        _cumsum_body,
        grid=(b, l // tb),
        in_specs=[pl.BlockSpec((1, tb, LANES), lambda i, j: (i, j, 0))],
        out_specs=[pl.BlockSpec((1, tb, LANES), lambda i, j: (i, j, 0)),
                   pl.BlockSpec((1, FOX_HEADS, tb), lambda i, j: (i, 0, j))],
        out_shape=[jax.ShapeDtypeStruct((b, l, LANES), F32), jax.ShapeDtypeStruct((b, FOX_HEADS, l), F32)],
        scratch_shapes=[pltpu.VMEM((1, LANES), F32)],
        compiler_params=_params("parallel", "arbitrary"),
        name="logf_cumsum",
    )(x)


def _fox_prompt_body(tk, q_ref, k_ref, v_ref, ccol_ref, crow_ref, o_ref, m_ref, l_ref, cq_ref, acc_ref):
    i = pl.program_id(1)
    tq = q_ref.shape[0]
    rep = tk // LANES
    m_ref[...] = jnp.full_like(m_ref, -jnp.inf)
    l_ref[...] = jnp.zeros_like(l_ref)
    acc_ref[...] = jnp.zeros_like(acc_ref)
    for h in range(FOX_HEADS):
        cq_ref[h] = jnp.broadcast_to(ccol_ref[0, :, h:h + 1], (tq, LANES))
    row = i * tq + lax.broadcasted_iota(jnp.int32, (tq, tk), 0)
    col = lax.broadcasted_iota(jnp.int32, (tq, tk), 1)
    wide = lambda x: jnp.concatenate([x] * rep, axis=1)

    def block(j, masked):
        ks = pl.ds(pl.multiple_of(j * tk, tk), tk)
        for h in range(FOX_HEADS):
            hs = slice(h * HEAD_DIM, (h + 1) * HEAD_DIM)
            s = _dot_nt(q_ref[:, hs], k_ref[ks, hs]) + (wide(cq_ref[h]) - crow_ref[0, h, pl.ds(j, 1), :])
            if masked:
                s = jnp.where(col + j * tk <= row, s, -jnp.inf)
            m_old = m_ref[h]
            m_new = jnp.maximum(m_old, jnp.max(s, axis=-1, keepdims=True))
            alpha = jnp.exp2(m_old - m_new)
            p = jnp.exp2(s - wide(m_new))
            m_ref[h] = m_new
            l_ref[h] = alpha * l_ref[h] + jnp.sum(p, axis=-1, keepdims=True)
            acc_ref[h] = alpha * acc_ref[h] + _dot(p.astype(BF16), v_ref[ks, hs])

    last = (i * tq + tq - 1) // tk

    def step(j, carry):
        block(j, False)
        return carry

    lax.fori_loop(0, last, step, 0)
    block(last, True)
    for h in range(FOX_HEADS):
        o_ref[:, h * HEAD_DIM:(h + 1) * HEAD_DIM] = (acc_ref[h] / l_ref[h]).astype(o_ref.dtype)


def _fox_prompt(q, k, v, ccol, crow, b, l, *, tq=128):
    tk = crow.shape[-1]
    tq = min(tq, l)
    assert tk % tq == 0
    nq = l // tq
    w = FOX_WIDTH
    return pl.pallas_call(
        functools.partial(_fox_prompt_body, tk),
        grid=(b, nq),
        in_specs=[
            pl.BlockSpec((tq, w), lambda bi, i: (bi * nq + i, 0)),
            pl.BlockSpec((l, w), lambda bi, i: (bi, 0)),
            pl.BlockSpec((l, w), lambda bi, i: (bi, 0)),
            pl.BlockSpec((1, tq, LANES), lambda bi, i: (bi, i, 0)),
            pl.BlockSpec((1, FOX_HEADS, l // tk, tk), lambda bi, i: (bi, 0, 0, 0)),
        ],
        out_specs=pl.BlockSpec((tq, w), lambda bi, i: (bi * nq + i, 0)),
        out_shape=jax.ShapeDtypeStruct((b * l, w), BF16),
        scratch_shapes=[pltpu.VMEM((FOX_HEADS, tq, LANES), F32)] * 3 + [pltpu.VMEM((FOX_HEADS, tq, HEAD_DIM), F32)],
        compiler_params=_params("parallel", "arbitrary"),
        name="fox_prompt",
    )(q, k, v, ccol, crow)


def _page_cumsum_body(x_ref, w_ref, m_ref):
    n = PAGE_SIZE * FOX_HEADS

    @pl.when(pl.program_id(0) == 0)
    def _():
        r = lax.broadcasted_iota(jnp.int32, (n, n), 0)
        c = lax.broadcasted_iota(jnp.int32, (n, n), 1)
        same_head = (r & (FOX_HEADS - 1)) == (c & (FOX_HEADS - 1))
        earlier = lax.shift_right_logical(r, 3) <= lax.shift_right_logical(c, 3)
        m_ref[...] = jnp.logical_and(same_head, earlier).astype(BF16)

    w_ref[...] = _dot_sel_r(x_ref[...], m_ref[...])


def _page_cumsum(logf_pages, *, tb=256):
    n_pool, n = logf_pages.shape
    return pl.pallas_call(
        _page_cumsum_body,
        grid=(n_pool // tb,),
        in_specs=[pl.BlockSpec((tb, n), lambda i: (i, 0))],
        out_specs=pl.BlockSpec((tb, n), lambda i: (i, 0)),
        out_shape=jax.ShapeDtypeStruct((n_pool, n), F32),
        scratch_shapes=[pltpu.VMEM((n, n), BF16)],
        compiler_params=_params("arbitrary"),
        name="page_cumsum",
    )(logf_pages)


def _seq_cumsum_body(seq_len, x_ref, o_ref):
    n = x_ref.shape[0]
    r = lax.broadcasted_iota(jnp.int32, (n, n), 0)
    c = lax.broadcasted_iota(jnp.int32, (n, n), 1)
    same_seq = (r // seq_len) == (c // seq_len)
    o_ref[...] = _dot_sel_l(jnp.logical_and(same_seq, c <= r).astype(BF16), x_ref[...])


def _seq_cumsum(x, seq_len, *, tb=128):
    t = x.shape[0]
    return pl.pallas_call(
        functools.partial(_seq_cumsum_body, seq_len),
        grid=(t // tb,),
        in_specs=[pl.BlockSpec((tb, LANES), lambda i: (i, 0))],
        out_specs=pl.BlockSpec((tb, LANES), lambda i: (i, 0)),
        out_shape=jax.ShapeDtypeStruct((t, LANES), F32),
        compiler_params=_params("parallel"),
        name="seq_cumsum",
    )(x)


def _fox_sample_body(pps, pt_ref, q_ref, kn_ref, vn_ref, cn_ref, *refs):
    kp_refs, vp_refs, w_refs = refs[:pps], refs[pps:2 * pps], refs[2 * pps:3 * pps]
    o_ref, q_scr, colq_ref, toff_ref, m_ref, l_ref, acc_ref = refs[3 * pps:]
    j = pl.program_id(1)
    nq = q_ref.shape[0]
    rows = nq * FOX_HEADS
    page_keys = PAGE_SIZE * FOX_HEADS
    row_id = lax.broadcasted_iota(jnp.int32, (rows, 1), 0)
    head_of_row = row_id & (FOX_HEADS - 1)
    query_of_row = lax.shift_right_logical(row_id, 3)

    @pl.when(j == 0)
    def _():
        q = q_ref[...].reshape(rows, HEAD_DIM).astype(BF16)
        q_scr[...] = q
        cn = cn_ref[0]
        key = lax.broadcasted_iota(jnp.int32, (1, rows), 1)
        colq = jnp.sum(jnp.where(key == row_id, cn, 0.0), axis=-1, keepdims=True)
        colq_ref[...] = colq
        toff_ref[...] = jnp.zeros_like(toff_ref)
        s = _dot_nt(q, kn_ref[...].reshape(rows, HEAD_DIM).astype(BF16)) + colq - cn
        valid = jnp.logical_and((key & (FOX_HEADS - 1)) == head_of_row,
                                lax.shift_right_logical(key, 3) <= query_of_row)
        s = jnp.where(valid, s, -jnp.inf)
        m = jnp.max(s, axis=-1, keepdims=True)
        p = jnp.exp(s - m)
        m_ref[...] = m
        l_ref[...] = jnp.sum(p, axis=-1, keepdims=True)
        acc_ref[...] = _dot(p.astype(BF16), vn_ref[...].reshape(rows, HEAD_DIM).astype(BF16))

    lane = lax.broadcasted_iota(jnp.int32, (1, LANES), 1)
    own_head = (lax.broadcasted_iota(jnp.int32, (1, page_keys), 1) & (FOX_HEADS - 1)) == head_of_row
    q = q_scr[...]
    colq = colq_ref[...]
    toff = toff_ref[...]
    tiles = []
    for kp_ref, w_ref in zip(kp_refs, w_refs):
        w = w_ref[0]
        last = jnp.where(lane == LANES - FOX_HEADS + head_of_row, w[:, page_keys - LANES:], 0.0)
        toff = toff + jnp.sum(last, axis=-1, keepdims=True)
        s = _dot_nt(q, kp_ref[0].reshape(page_keys, HEAD_DIM).astype(BF16))
        tiles.append(jnp.where(own_head, s + (colq + toff) - w, -jnp.inf))
    toff_ref[...] = toff
    m_old = m_ref[...]
    m = m_old
    for s in tiles:
        m = jnp.maximum(m, jnp.max(s, axis=-1, keepdims=True))
    alpha = jnp.exp(m_old - m)
    l = alpha * l_ref[...]
    acc = alpha * acc_ref[...]
    for s, vp_ref in zip(tiles, vp_refs):
        p = jnp.exp(s - m)
        l = l + jnp.sum(p, axis=-1, keepdims=True)
        acc = acc + _dot(p.astype(BF16), vp_ref[0].reshape(page_keys, HEAD_DIM).astype(BF16))
    m_ref[...] = m
    l_ref[...] = l
    acc_ref[...] = acc

    @pl.when(j == pl.num_programs(1) - 1)
    def _():
        o_ref[...] = (acc_ref[...] / l_ref[...]).reshape(nq, FOX_HEADS, HEAD_DIM)


def _fox_sample(page_table, q, k_new, v_new, c_new, k_pages, v_pages, w_pages, n_seq, nq, *, pages_per_step=16):
    n_pages = page_table.shape[1]
    rows = nq * FOX_HEADS
    pps = math.gcd(pages_per_step, n_pages)

    def page(i, ndim):
        return lambda b, j, pt: (pt[b * n_pages + (n_pages - 1 - j * pps - i)],) + (0,) * (ndim - 1)

    seq = pl.BlockSpec((nq, FOX_HEADS, HEAD_DIM), lambda b, j, pt: (b, 0, 0))
    kv_specs = [pl.BlockSpec((1, PAGE_SIZE, FOX_HEADS, HEAD_DIM), page(i, 4)) for i in range(pps)]
    w_specs = [pl.BlockSpec((1, 1, PAGE_SIZE * FOX_HEADS), page(i, 3)) for i in range(pps)]
    grid_spec = pltpu.PrefetchScalarGridSpec(
        num_scalar_prefetch=1,
        grid=(n_seq, n_pages // pps),
        in_specs=[seq, seq, seq, pl.BlockSpec((1, 1, rows), lambda b, j, pt: (b, 0, 0))]
        + kv_specs + kv_specs + w_specs,
        out_specs=seq,
        scratch_shapes=[
            pltpu.VMEM((rows, HEAD_DIM), BF16), pltpu.VMEM((rows, 1), F32), pltpu.VMEM((rows, 1), F32),
            pltpu.VMEM((rows, 1), F32), pltpu.VMEM((rows, 1), F32), pltpu.VMEM((rows, HEAD_DIM), F32),
        ],
    )
    return pl.pallas_call(
        functools.partial(_fox_sample_body, pps),
        grid_spec=grid_spec,
        out_shape=jax.ShapeDtypeStruct((n_seq * nq, FOX_HEADS, HEAD_DIM), F32),
        compiler_params=_params("parallel", "arbitrary"),
        name="fox_sample",
    )(page_table.reshape(-1), q, k_new, v_new, c_new, *([k_pages] * pps), *([v_pages] * pps), *([w_pages] * pps))


SSD_PAIRS = SSD_HEADS // 2
PAIRS_PER_GROUP = SSD_PAIRS // SSD_GROUPS


def _expander(width):
    n = SSD_HEADS * width
    h = lax.broadcasted_iota(jnp.int32, (LANES, n), 0)
    c = lax.broadcasted_iota(jnp.int32, (LANES, n), 1)
    return (lax.shift_right_logical(c, int(math.log2(width))) == h).astype(BF16)


def _ssd_local(xs, bm, cm, dt, alog, mask, tot_sel):
    n = xs.shape[0]
    lane = lax.broadcasted_iota(jnp.int32, (1, LANES), 1)
    dta = dt * jnp.where(lane < SSD_HEADS, -jnp.exp(alog), 0.0)
    e64 = _expander(SSD_HEAD_DIM)
    a_cum = _dot_sel_l(mask.astype(BF16), dta)
    a_cum_t = a_cum.T
    ac_exp = _dot_sel_r(a_cum, e64)
    if tot_sel is None:
        atot_exp = ac_exp[n - 1:n, :]
    else:
        atot_exp = _dot_sel_r(_dot_sel_l(tot_sel, dta), e64)
    ac_b = _dot_sel_r(a_cum, _expander(LANES))
    xdt = xs * _dot_sel_r(dt, e64)
    half = lax.broadcasted_iota(jnp.int32, (n, LANES), 1) < SSD_HEAD_DIM
    out = {
        "xdtw": xdt * jnp.exp(atot_exp - ac_exp),
        "eac": jnp.exp(ac_exp),
        "atot_exp": atot_exp,
        "ac_b": ac_b,
        "bg": [], "cg": [], "y_diag": [],
    }
    for g in range(SSD_GROUPS):
        gs = slice(g * SSD_STATE, (g + 1) * SSD_STATE)
        bg = bm[:, gs].astype(BF16)
        cg = cm[:, gs].astype(BF16)
        out["bg"].append(bg)
        out["cg"].append(cg)
        cb = _dot_nt(cg, bg)
        for k in range(g * PAIRS_PER_GROUP, (g + 1) * PAIRS_PER_GROUP):
            ps = slice(k * LANES, (k + 1) * LANES)
            ms = []
            for h in (2 * k, 2 * k + 1):
                seg = ac_b[:, h * LANES:(h + 1) * LANES] - a_cum_t[h:h + 1, :]
                ms.append(cb * jnp.exp(jnp.where(mask, seg, -jnp.inf)))
            m_cat = jnp.concatenate(ms, axis=1).astype(BF16)
            xp = xdt[:, ps]
            x_bd = jnp.concatenate([jnp.where(half, xp, 0.0), jnp.where(half, 0.0, xp)], axis=0).astype(BF16)
            out["y_diag"].append(_dot(m_cat, x_bd))
    return out


def _conv_ssd_sample_body(seq_len, xbc_ref, dt_ref, buf_ref, h0_ref, w_ref, b_ref, alog_ref, dexp_ref,
                          y_ref, hout_ref, cout_ref, xp_ref):
    n = xbc_ref.shape[0]
    n_seq = n // seq_len
    taps = CONV_WIDTH - 1
    base = 8 - taps

    x = xbc_ref[...]
    xp_ref[:, base:8, :] = buf_ref[...]
    xp_ref[:, 8:8 + seq_len, :] = x.reshape(n_seq, seq_len, CONV_DIM)
    acc = b_ref[...] + x * w_ref[taps:taps + 1, :]
    for j in range(taps):
        acc = acc + xp_ref[:, base + j:base + j + seq_len, :].reshape(n, CONV_DIM) * w_ref[j:j + 1, :]
    cout_ref[...] = xp_ref[:, 8 + seq_len - taps:8 + seq_len, :]
    conv = _silu(acc)
    xs = conv[:, :SSD_WIDTH]
    bm = conv[:, SSD_WIDTH:SSD_WIDTH + SSD_GROUPS * SSD_STATE]
    cm = conv[:, SSD_WIDTH + SSD_GROUPS * SSD_STATE:]

    r = lax.broadcasted_iota(jnp.int32, (n, n), 0)
    c = lax.broadcasted_iota(jnp.int32, (n, n), 1)
    same_seq = (r // seq_len) == (c // seq_len)
    loc = _ssd_local(xs, bm, cm, dt_ref[...], alog_ref[...], jnp.logical_and(same_seq, c <= r),
                     same_seq.astype(BF16))

    gw = PAIRS_PER_GROUP * LANES
    seq_of_col = lax.broadcasted_iota(jnp.int32, (1, n), 1) // seq_len
    decay_t = jnp.exp(loc["atot_exp"]).T
    for g in range(SSD_GROUPS):
        gr = slice(g * gw, (g + 1) * gw)
        h_prev = h0_ref[:, gr, :]
        z = _dot_nt(h_prev.reshape(n_seq * gw, SSD_STATE).astype(BF16), loc["cg"][g])
        y_off_t = jnp.zeros((gw, n), F32)
        for s in range(n_seq):
            y_off_t = y_off_t + jnp.where(seq_of_col == s, z[s * gw:(s + 1) * gw, :], 0.0)
        y_off = y_off_t.T * loc["eac"][:, gr]
        y = jnp.concatenate(loc["y_diag"][g * PAIRS_PER_GROUP:(g + 1) * PAIRS_PER_GROUP], axis=1)
        y_ref[:, gr] = y + y_off + xs[:, gr] * dexp_ref[:, gr]
        xw_t = loc["xdtw"][:, gr].T
        lhs = jnp.concatenate([jnp.where(seq_of_col == s, xw_t, 0.0) for s in range(n_seq)], axis=0)
        s_new = _dot(lhs.astype(BF16), loc["bg"][g])
        for s in range(n_seq):
            col = decay_t[gr, s * seq_len:s * seq_len + 1]
            hout_ref[s, gr, :] = h_prev[s] * col + s_new[s * gw:(s + 1) * gw, :]


def _conv_ssd_sample(xbc, dt, conv_buf, h0, conv_w, conv_b, alog, d_exp, n_seq, seq_len):
    assert seq_len == 8 and CONV_WIDTH - 1 <= seq_len
    tile = LANES
    ts = tile // seq_len
    const = lambda i: (0, 0)
    per_s = lambda i: (i, 0, 0)
    return pl.pallas_call(
        functools.partial(_conv_ssd_sample_body, seq_len),
        grid=(n_seq // ts,),
        in_specs=[
            pl.BlockSpec((tile, CONV_DIM), lambda i: (i, 0)),
            pl.BlockSpec((tile, LANES), lambda i: (i, 0)),
            pl.BlockSpec((ts, CONV_WIDTH - 1, CONV_DIM), per_s),
            pl.BlockSpec((ts, SSD_WIDTH, SSD_STATE), per_s),
            pl.BlockSpec((CONV_WIDTH, CONV_DIM), const),
            pl.BlockSpec((1, CONV_DIM), const),
            pl.BlockSpec((1, LANES), const),
            pl.BlockSpec((1, SSD_WIDTH), const),
        ],
        out_specs=[
            pl.BlockSpec((tile, SSD_WIDTH), lambda i: (i, 0)),
            pl.BlockSpec((ts, SSD_WIDTH, SSD_STATE), per_s),
            pl.BlockSpec((ts, CONV_WIDTH - 1, CONV_DIM), per_s),
        ],
        out_shape=[
            jax.ShapeDtypeStruct((n_seq * seq_len, SSD_WIDTH), F32),
            jax.ShapeDtypeStruct((n_seq, SSD_WIDTH, SSD_STATE), F32),
            jax.ShapeDtypeStruct((n_seq, CONV_WIDTH - 1, CONV_DIM), F32),
        ],
        scratch_shapes=[pltpu.VMEM((ts, 8 + seq_len, CONV_DIM), F32)],
        compiler_params=_params("parallel"),
        name="conv_ssd_sample",
    )(xbc, dt, conv_buf, h0, conv_w, conv_b, alog, d_exp)


def _conv_ssd_prompt_body(xbc_ref, dt_ref, buf_ref, h0_ref, w_ref, b_ref, alog_ref, dexp_ref,
                          y_ref, hout_ref, cout_ref, state_ref, xp_ref):
    c = pl.program_id(1)
    nc = pl.num_programs(1)
    tl = xbc_ref.shape[0]
    taps = CONV_WIDTH - 1
    base = 8 - taps

    @pl.when(c == 0)
    def _():
        state_ref[...] = h0_ref[0]
        xp_ref[base:8, :] = buf_ref[0]

    x = xbc_ref[...]
    xp_ref[8:8 + tl, :] = x
    acc = b_ref[...] + x * w_ref[taps:taps + 1, :]
    for j in range(taps):
        acc = acc + xp_ref[base + j:base + j + tl, :] * w_ref[j:j + 1, :]
    xp_ref[base:8, :] = x[tl - taps:, :]
    conv = _silu(acc)
    xs = conv[:, :SSD_WIDTH]
    bm = conv[:, SSD_WIDTH:SSD_WIDTH + SSD_GROUPS * SSD_STATE]
    cm = conv[:, SSD_WIDTH + SSD_GROUPS * SSD_STATE:]

    causal = _tri(tl)
    loc = _ssd_local(xs, bm, cm, dt_ref[...], alog_ref[...], causal, None)
    top = lax.broadcasted_iota(jnp.int32, (tl, LANES), 0) < SSD_HEAD_DIM
    for k in range(SSD_PAIRS):
        g = k // PAIRS_PER_GROUP
        ps = slice(k * LANES, (k + 1) * LANES)
        s_prev = state_ref[ps, :]
        y_off = _dot_nt(loc["cg"][g], s_prev.astype(BF16)) * loc["eac"][:, ps]
        y_ref[:, ps] = loc["y_diag"][k] + y_off + xs[:, ps] * dexp_ref[:, ps]
        cd = [jnp.exp(loc["ac_b"][tl - 1:tl, h * LANES:(h + 1) * LANES]) for h in (2 * k, 2 * k + 1)]
        state_ref[ps, :] = s_prev * jnp.where(top, cd[0], cd[1]) + _dot_tn(loc["xdtw"][:, ps].astype(BF16), loc["bg"][g])

    @pl.when(c == nc - 1)
    def _():
        hout_ref[0] = state_ref[...]
        cout_ref[0] = xp_ref[base:8, :]


def _conv_ssd_prompt(xbc, dt, conv_buf, h0, conv_w, conv_b, alog, d_exp, b, l):
    tl = SSD_CHUNK
    nc = l // tl
    const = lambda bi, c: (0, 0)
    per_b = lambda bi, c: (bi, 0, 0)
    return pl.pallas_call(
        _conv_ssd_prompt_body,
        grid=(b, nc),
        in_specs=[
            pl.BlockSpec((tl, CONV_DIM), lambda bi, c: (bi * nc + c, 0)),
            pl.BlockSpec((tl, LANES), lambda bi, c: (bi * nc + c, 0)),
            pl.BlockSpec((1, CONV_WIDTH - 1, CONV_DIM), per_b),
            pl.BlockSpec((1, SSD_WIDTH, SSD_STATE), per_b),
            pl.BlockSpec((CONV_WIDTH, CONV_DIM), const),
            pl.BlockSpec((1, CONV_DIM), const),
            pl.BlockSpec((1, LANES), const),
            pl.BlockSpec((1, SSD_WIDTH), const),
        ],
        out_specs=[
            pl.BlockSpec((tl, SSD_WIDTH), lambda bi, c: (bi * nc + c, 0)),
            pl.BlockSpec((1, SSD_WIDTH, SSD_STATE), per_b),
            pl.BlockSpec((1, CONV_WIDTH - 1, CONV_DIM), per_b),
        ],
        out_shape=[
            jax.ShapeDtypeStruct((b * l, SSD_WIDTH), F32),
            jax.ShapeDtypeStruct((b, SSD_WIDTH, SSD_STATE), F32),
            jax.ShapeDtypeStruct((b, CONV_WIDTH - 1, CONV_DIM), F32),
        ],
        scratch_shapes=[pltpu.VMEM((SSD_WIDTH, SSD_STATE), F32), pltpu.VMEM((8 + tl, CONV_DIM), F32)],
        compiler_params=_params("parallel", "arbitrary"),
        name="conv_ssd_prompt",
    )(xbc, dt, conv_buf, h0, conv_w, conv_b, alog, d_exp)


def _out_proj_body(h_ref, fox_ref, y_ref, z_ref, gs_ref, wf_ref, ws_ref, gx_ref, wq_ref, gq_ref, o_ref, q_ref):
    yn = _rms(y_ref[...] * _silu(z_ref[...]), gs_ref[...]).astype(BF16)
    h = h_ref[...] + _dot(fox_ref[...].astype(BF16), wf_ref[...]) + _dot(yn, ws_ref[...])
    o_ref[...] = h
    q = _dot(_rms(h, gx_ref[...]).astype(BF16), wq_ref[...])
    q_ref[...] = _head_norm(q, gq_ref[...]) * (HEAD_DIM ** -0.5)


def _out_proj(h, fox, y, z, g_ssd, w_fox, w_ssd, g_x, wq, gq, *, tm=256):
    t, d = h.shape
    tm = min(tm, t)
    row = lambda w: pl.BlockSpec((tm, w), lambda i: (i, 0))
    full = lambda a: pl.BlockSpec(a.shape, lambda i: (0, 0))
    consts = [g_ssd.reshape(1, SSD_WIDTH), w_fox, w_ssd, g_x.reshape(1, d), wq, gq.reshape(1, HEAD_DIM)]
    return pl.pallas_call(
        _out_proj_body,
        grid=(t // tm,),
        in_specs=[row(d), row(FOX_WIDTH), row(SSD_WIDTH), row(SSD_WIDTH)] + [full(a) for a in consts],
        out_specs=[row(d), row(XATTN_WIDTH)],
        out_shape=[jax.ShapeDtypeStruct((t, d), F32), jax.ShapeDtypeStruct((t, XATTN_WIDTH), F32)],
        compiler_params=_params("parallel"),
        name="out_proj",
    )(h, fox, y, z, *consts)


def _mem_kv_body(m_ref, g_ref, w_ref, gk_ref, k_ref, v_ref):
    kv = _dot(_rms(m_ref[...], g_ref[...]).astype(BF16), w_ref[...])
    k_ref[...] = _head_norm(kv[:, :XATTN_WIDTH], gk_ref[...])
    v_ref[...] = kv[:, XATTN_WIDTH:]


def _mem_kv(mem, g, w_kv, gk, *, tm=256):
    t, d = mem.shape
    row = lambda w: pl.BlockSpec((tm, w), lambda i: (i, 0))
    full = lambda a: pl.BlockSpec(a.shape, lambda i: (0, 0))
    consts = [g.reshape(1, d), w_kv, gk.reshape(1, HEAD_DIM)]
    return pl.pallas_call(
        _mem_kv_body,
        grid=(t // tm,),
        in_specs=[row(d)] + [full(a) for a in consts],
        out_specs=[row(XATTN_WIDTH), row(XATTN_WIDTH)],
        out_shape=[jax.ShapeDtypeStruct((t, XATTN_WIDTH), F32)] * 2,
        compiler_params=_params("parallel"),
        name="mem_kv",
    )(mem, *consts)


def _xattn_body(q_ref, k_ref, v_ref, o_ref):
    for h in range(XATTN_HEADS):
        hs = slice(h * HEAD_DIM, (h + 1) * HEAD_DIM)
        s = _dot_nt(q_ref[:, hs].astype(BF16), k_ref[0, :, hs].astype(BF16))
        p = jnp.exp(s - jnp.max(s, axis=-1, keepdims=True))
        o = _dot(p.astype(BF16), v_ref[0, :, hs].astype(BF16))
        o_ref[:, hs] = o / jnp.sum(p, axis=-1, keepdims=True)


def _xattn(q, mem_k, mem_v, b, l, *, tq=512):
    tq = min(tq, l)
    nq = l // tq
    n_mem = mem_k.shape[1]
    qspec = pl.BlockSpec((tq, XATTN_WIDTH), lambda bi, i: (bi * nq + i, 0))
    mspec = pl.BlockSpec((1, n_mem, XATTN_WIDTH), lambda bi, i: (bi, 0, 0))
    return pl.pallas_call(
        _xattn_body,
        grid=(b, nq),
        in_specs=[qspec, mspec, mspec],
        out_specs=qspec,
        out_shape=jax.ShapeDtypeStruct((b * l, XATTN_WIDTH), F32),
        compiler_params=_params("parallel", "arbitrary"),
        name="xattn",
    )(q, mem_k, mem_v)


def _xattn_rows_body(n_seq, q_ref, k_ref, v_ref, o_ref):
    tiles = q_ref.shape[0] // n_seq
    rows = tiles * 8
    keys = k_ref.shape[1] * k_ref.shape[2]
    row_head = lax.broadcasted_iota(jnp.int32, (rows, 1), 0) & (XATTN_HEADS - 1)
    key_head = lax.broadcasted_iota(jnp.int32, (1, keys), 1) & (XATTN_HEADS - 1)
    own_head = row_head == key_head
    for s in range(n_seq):
        qs = slice(s * tiles, (s + 1) * tiles)
        q = q_ref[qs].reshape(rows, HEAD_DIM).astype(BF16)
        sc = jnp.where(own_head, _dot_nt(q, k_ref[s].reshape(keys, HEAD_DIM).astype(BF16)), -jnp.inf)
        p = jnp.exp(sc - jnp.max(sc, axis=-1, keepdims=True))
        o = _dot(p.astype(BF16), v_ref[s].reshape(keys, HEAD_DIM).astype(BF16))
        o_ref[qs] = (o / jnp.sum(p, axis=-1, keepdims=True)).reshape(tiles, 8, HEAD_DIM)


def _xattn_rows(q, mem_k, mem_v, n_seq, *, ts=8):
    tiles = q.shape[0] // n_seq
    ts = math.gcd(ts, n_seq)
    qspec = pl.BlockSpec((ts * tiles, 8, HEAD_DIM), lambda i: (i, 0, 0))
    mspec = pl.BlockSpec((ts,) + mem_k.shape[1:], lambda i: (i, 0, 0, 0))
    return pl.pallas_call(
        functools.partial(_xattn_rows_body, ts),
        grid=(n_seq // ts,),
        in_specs=[qspec, mspec, mspec],
        out_specs=qspec,
        out_shape=jax.ShapeDtypeStruct(q.shape, F32),
        compiler_params=_params("parallel"),
        name="xattn_rows",
    )(q, mem_k, mem_v)


def kernel(x_prompt, x_sample, cache_fox_k, cache_fox_v, cache_fox_logf, cache_mem_k, cache_mem_v, state_ssm, state_conv, page_table, mem_prompt, ffn1_norm, ffn1_w_gate, ffn1_w_up, ffn1_w_down, mix_norm, w_in, fox_b_f, fox_q_norm, fox_k_norm, conv_w, conv_b, ssd_dt_bias, ssd_A_log, ssd_D, ssd_out_norm, w_out, xattn_norm, mem_norm, xattn_w_q, xattn_w_kv, xattn_q_norm, xattn_k_norm, xattn_w_o, ffn2_norm, ffn2_w_gate, ffn2_w_up, ffn2_w_down):
    assert x_prompt.shape[2] == D_MODEL and ffn1_norm.shape[0] == 1
    d = D_MODEL
    bp, lp = x_prompt.shape[:2]
    bs, ls = x_sample.shape[:2]
    n_mem = mem_prompt.shape[1]
    n_pool = cache_fox_k.shape[1]

    bf = lambda w: w[0].astype(BF16)
    ffn1 = (ffn1_norm[0], bf(ffn1_w_gate), bf(ffn1_w_up), bf(ffn1_w_down))
    ffn2 = (ffn2_norm[0], bf(ffn2_w_gate), bf(ffn2_w_up), bf(ffn2_w_down))
    in_w = _pack_in_proj(w_in[0], fox_b_f[0], ssd_dt_bias[0])
    w_fox, w_ssd = bf(w_out)[:FOX_WIDTH], bf(w_out)[FOX_WIDTH:]
    wq, wkv, wo = bf(xattn_w_q), bf(xattn_w_kv), bf(xattn_w_o)
    alog = jnp.pad(ssd_A_log[0], (0, LANES - SSD_HEADS)).reshape(1, LANES)
    d_exp = jnp.repeat(ssd_D[0], SSD_HEAD_DIM).reshape(1, SSD_WIDTH)
    cw, cb = conv_w[0], conv_b[0].reshape(1, CONV_DIM)

    def front(x, q_dtype, q_scale):
        h1 = _ffn(x, *ffn1)
        return h1, _in_proj(h1, mix_norm[0], *in_w, fox_q_norm[0], fox_k_norm[0], q_dtype=q_dtype, q_scale=q_scale)

    def back(h1, fox, y, z, attend):
        h2, xq = _out_proj(h1, fox, y, z, ssd_out_norm[0], w_fox, w_ssd, xattn_norm[0], wq, xattn_q_norm[0])
        return _ffn(h2, *ffn2, pre=(attend(xq), wo))

    h1, (q, k_p, kb, v_p, vb, z, xbc, logf_p, dt) = front(x_prompt.reshape(bp * lp, d), BF16, HEAD_DIM ** -0.5 * LOG2E)
    ccol, crow = _cumsum(logf_p.reshape(bp, lp, LANES))
    tq = min(ATT_BLOCK, lp)
    fox = _fox_prompt(q, kb, vb, ccol, crow.reshape(bp, FOX_HEADS, lp // tq, tq), bp, lp)
    y, ssm_p, conv_p = _conv_ssd_prompt(
        xbc, dt, jnp.zeros((bp, CONV_WIDTH - 1, CONV_DIM), F32), jnp.zeros((bp, SSD_WIDTH, SSD_STATE), F32),
        cw, cb, alog, d_exp, bp, lp)
    mk, mv = _mem_kv(mem_prompt.reshape(bp * n_mem, d), mem_norm[0], wkv, xattn_k_norm[0])
    y_prompt = back(h1, fox, y, z, lambda xq: _xattn(
        xq, mk.reshape(bp, n_mem, XATTN_WIDTH), mv.reshape(bp, n_mem, XATTN_WIDTH), bp, lp))

    h1, (q, k_s, _, v_s, _, z, xbc, logf_s, dt) = front(x_sample.reshape(bs * ls, d), F32, HEAD_DIM ** -0.5)
    w_pages = _page_cumsum(cache_fox_logf[0].reshape(n_pool, PAGE_SIZE * FOX_HEADS))
    c_new = _seq_cumsum(logf_s, ls)[:, :FOX_HEADS].reshape(bs, 1, ls * FOX_HEADS)
    fox = _fox_sample(
        page_table, q.reshape(bs * ls, FOX_HEADS, HEAD_DIM), k_s, v_s, c_new, cache_fox_k[0], cache_fox_v[0],
        w_pages.reshape(n_pool, 1, PAGE_SIZE * FOX_HEADS), bs, ls).reshape(bs * ls, FOX_WIDTH)
    y, ssm_s, conv_s = _conv_ssd_sample(
        xbc, dt, state_conv[0], state_ssm[0].reshape(bs, SSD_WIDTH, SSD_STATE), cw, cb, alog, d_exp, bs, ls)
    mem_tiles = lambda m: m[0].reshape(bs, n_mem * XATTN_HEADS // 8, 8, HEAD_DIM)
    y_sample = back(h1, fox, y, z, lambda xq: _xattn_rows(
        xq.reshape(bs * ls * XATTN_HEADS // 8, 8, HEAD_DIM), mem_tiles(cache_mem_k), mem_tiles(cache_mem_v), bs,
    ).reshape(bs * ls, XATTN_WIDTH))

    fox_shape = lambda b, l: (1, b, l, FOX_HEADS, HEAD_DIM)
    ssm_shape = lambda b: (1, b, SSD_HEADS, SSD_HEAD_DIM, SSD_STATE)
    mem_shape = (1, bp, n_mem, XATTN_HEADS, HEAD_DIM)
    return (
        y_prompt.reshape(bp, lp, d), y_sample.reshape(bs, ls, d),
        k_p.reshape(fox_shape(bp, lp)), v_p.reshape(fox_shape(bp, lp)),
        logf_p[:, :FOX_HEADS].reshape(1, bp, lp, FOX_HEADS),
        ssm_p.reshape(ssm_shape(bp)), conv_p[None], mk.reshape(mem_shape), mv.reshape(mem_shape),
        k_s.reshape(fox_shape(bs, ls)), v_s.reshape(fox_shape(bs, ls)),
        logf_s[:, :FOX_HEADS].reshape(1, bs, ls, FOX_HEADS),
        ssm_s.reshape(ssm_shape(bs)), conv_s[None],
    )
```

```python
import functools
import math

import jax
import jax.numpy as jnp
from jax import lax
from jax.experimental import pallas as pl
from jax.experimental.pallas import tpu as pltpu

F32 = jnp.float32
BF16 = jnp.bfloat16

EPS = 1e-6
LOG2E = math.log2(math.e)
FFN_RESIDUAL = 0.5
D_MODEL = 2048
D_FF = 5632
PAGE_SIZE = 128
FOX_HEADS = 8
HEAD_DIM = 128
FOX_WIDTH = FOX_HEADS * HEAD_DIM
SSD_HEADS = 16
SSD_HEAD_DIM = 64
SSD_WIDTH = SSD_HEADS * SSD_HEAD_DIM
SSD_GROUPS = 2
SSD_STATE = 128
SSD_CHUNK = 128
CONV_WIDTH = 4
CONV_DIM = SSD_WIDTH + 2 * SSD_GROUPS * SSD_STATE
XATTN_HEADS = 4
XATTN_WIDTH = XATTN_HEADS * HEAD_DIM
LANES = 128
VMEM_LIMIT_BYTES = 56 * 1024 * 1024

def _params(*semantics):
    return pltpu.CompilerParams(dimension_semantics=semantics, vmem_limit_bytes=VMEM_LIMIT_BYTES)


def _rms(x, g):
    return x * lax.rsqrt(jnp.mean(x * x, axis=-1, keepdims=True) + EPS) * g


def _dot(a, b):
    return jnp.dot(a, b, preferred_element_type=F32)


def _dot_nt(a, b):
    return lax.dot_general(a, b, (((1,), (1,)), ((), ())), preferred_element_type=F32)


def _dot_tn(a, b):
    return lax.dot_general(a, b, (((0,), (0,)), ((), ())), preferred_element_type=F32)


def _split3(x):
    x1 = x.astype(BF16)
    r = x - x1.astype(F32)
    x2 = r.astype(BF16)
    x3 = (r - x2.astype(F32)).astype(BF16)
    return x1, x2, x3


def _dot_sel_l(sel, x):
    x1, x2, x3 = _split3(x)
    return _dot(sel, x1) + _dot(sel, x2) + _dot(sel, x3)


def _dot_sel_r(x, sel):
    x1, x2, x3 = _split3(x)
    return _dot(x1, sel) + _dot(x2, sel) + _dot(x3, sel)


def _silu(x):
    return x * jax.nn.sigmoid(x)


def _ffn_body(pre_proj, split, *refs):
    if pre_proj:
        x_ref, a_ref, wo_ref, g_ref, *w_refs, o_ref, xn_ref = refs
    else:
        x_ref, g_ref, *w_refs, o_ref, xn_ref = refs

    @pl.when(pl.program_id(1) == 0)
    def _():
        x = x_ref[...]
        if pre_proj:
            x = x + _dot(a_ref[...].astype(BF16), wo_ref[...])
        xn_ref[...] = _rms(x, g_ref[...]).astype(BF16)
        o_ref[...] = x

    xn = xn_ref[...]
    acc = None
    for wg_ref, wu_ref, wd_ref in zip(w_refs[:split], w_refs[split:2 * split], w_refs[2 * split:]):
        gate = _dot(xn, wg_ref[...])
        up = _dot(xn, wu_ref[...])
        h = (_silu(gate) * up * FFN_RESIDUAL).astype(BF16)
        part = _dot(h, wd_ref[...])
        acc = part if acc is None else acc + part
    o_ref[...] += acc


def _ffn(x, g, wg, wu, wd, pre=None, *, tm=512, tf=512, split=2):
    t, d = x.shape
    f = wg.shape[1]
    tm = min(tm, t)
    ts = tf // split
    grid = (t // tm, f // tf)
    row = lambda i, j: (i, 0)
    in_specs = [pl.BlockSpec((tm, d), row)]
    args = [x]
    if pre is not None:
        a, wo = pre
        in_specs += [pl.BlockSpec((tm, a.shape[1]), row), pl.BlockSpec(wo.shape, lambda i, j: (0, 0))]
        args += [a, wo]
    cols = [pl.BlockSpec((d, ts), lambda i, j, s=s: (0, j * split + s)) for s in range(split)]
    rows = [pl.BlockSpec((ts, d), lambda i, j, s=s: (j * split + s, 0)) for s in range(split)]
    in_specs += [pl.BlockSpec((1, d), lambda i, j: (0, 0))] + cols + cols + rows
    args += [g.reshape(1, d)] + [wg] * split + [wu] * split + [wd] * split
    return pl.pallas_call(
        functools.partial(_ffn_body, pre is not None, split),
        grid=grid,
        in_specs=in_specs,
        out_specs=pl.BlockSpec((tm, d), row),
        out_shape=jax.ShapeDtypeStruct((t, d), F32),
        scratch_shapes=[pltpu.VMEM((tm, d), BF16)],
        compiler_params=_params("parallel", "arbitrary"),
        name="ffn_pre" if pre is not None else "ffn",
    )(*args)


IN_QKV_TN = 1024
IN_ZX_TN = 512


def _head_norm(y, g):
    outs = []
    for c in range(y.shape[1] // HEAD_DIM):
        yc = y[:, c * HEAD_DIM:(c + 1) * HEAD_DIM]
        outs.append(yc * lax.rsqrt(jnp.mean(yc * yc, axis=-1, keepdims=True) + EPS) * g)
    return jnp.concatenate(outs, axis=1)


def _in_qkv_body(q_scale, x_ref, g_ref, w_ref, ws_ref, bs_ref, gq_ref, gk_ref,
                 q_ref, k_ref, kb_ref, v_ref, vb_ref, logf_ref, dt_ref, u_ref):
    j = pl.program_id(1)

    @pl.when(j == 0)
    def _():
        u = _rms(x_ref[...], g_ref[...]).astype(BF16)
        u_ref[...] = u
        s = _dot(u, ws_ref[...]) + bs_ref[...]
        t = jnp.log1p(jnp.exp(-jnp.abs(s)))
        logf_ref[...] = (jnp.minimum(s, 0.0) - t)[:, :LANES]
        dt_ref[...] = (jnp.maximum(s, 0.0) + t)[:, LANES:]
        q_ref[...] = (_head_norm(_dot(u, w_ref[...]), gq_ref[...]) * q_scale).astype(q_ref.dtype)

    def by_head(y, out_ref, bf_ref):
        bf_ref[...] = y.astype(BF16)
        for h in range(FOX_HEADS):
            out_ref[:, h, :] = y[:, h * HEAD_DIM:(h + 1) * HEAD_DIM]

    @pl.when(j == 1)
    def _():
        by_head(_head_norm(_dot(u_ref[...], w_ref[...]), gk_ref[...]), k_ref, kb_ref)

    @pl.when(j == 2)
    def _():
        by_head(_dot(u_ref[...], w_ref[...]), v_ref, vb_ref)


def _in_qkv(x, g, w_qkv, w_small, b_small, gq, gk, *, q_dtype, q_scale, tm=512):
    t, d = x.shape
    tm = min(tm, t)
    tn = IN_QKV_TN
    assert tn == FOX_WIDTH and w_qkv.shape[1] == 3 * tn
    const = lambda i, j: (0, 0)
    rows = lambda w: pl.BlockSpec((tm, w), lambda i, j: (i, 0))
    by_head = pl.BlockSpec((tm, FOX_HEADS, HEAD_DIM), lambda i, j: (i, 0, 0))
    by_head_shape = jax.ShapeDtypeStruct((t, FOX_HEADS, HEAD_DIM), F32)
    flat = lambda dt, w: jax.ShapeDtypeStruct((t, w), dt)
    return pl.pallas_call(
        functools.partial(_in_qkv_body, q_scale),
        grid=(t // tm, 3),
        in_specs=[
            rows(d),
            pl.BlockSpec((1, d), const),
            pl.BlockSpec((d, tn), lambda i, j: (0, j)),
            pl.BlockSpec((d, 2 * LANES), const),
            pl.BlockSpec((1, 2 * LANES), const),
            pl.BlockSpec((1, HEAD_DIM), const),
            pl.BlockSpec((1, HEAD_DIM), const),
        ],
        out_specs=[rows(FOX_WIDTH), by_head, rows(FOX_WIDTH), by_head, rows(FOX_WIDTH), rows(LANES), rows(LANES),
                   rows(d)],
        out_shape=[flat(q_dtype, FOX_WIDTH), by_head_shape, flat(BF16, FOX_WIDTH), by_head_shape,
                   flat(BF16, FOX_WIDTH), flat(F32, LANES), flat(F32, LANES), flat(BF16, d)],
        compiler_params=_params("parallel", "arbitrary"),
        name="in_qkv",
    )(x, g.reshape(1, d), w_qkv, w_small, b_small, gq.reshape(1, HEAD_DIM), gk.reshape(1, HEAD_DIM))


def _in_zx_body(u_ref, wa_ref, wb_ref, z_ref, xbc_ref):
    j = pl.program_id(1)
    half = wa_ref.shape[1]
    tn = 2 * half
    nz = z_ref.shape[1] // tn
    for c in range(nz + xbc_ref.shape[1] // tn):
        out_ref, sub = (z_ref, c) if c < nz else (xbc_ref, c - nz)

        @pl.when(j == c)
        def _(out_ref=out_ref, sub=sub):
            out_ref[:, sub * tn:sub * tn + half] = _dot(u_ref[...], wa_ref[...])
            out_ref[:, sub * tn + half:(sub + 1) * tn] = _dot(u_ref[...], wb_ref[...])


def _in_zx(u, w_zx, *, tm=1024):
    t, d = u.shape
    tm = min(tm, t)
    tn = IN_ZX_TN
    rows = lambda w: pl.BlockSpec((tm, w), lambda i, j: (i, 0))
    return pl.pallas_call(
        _in_zx_body,
        grid=(t // tm, w_zx.shape[1] // tn),
        in_specs=[rows(d), pl.BlockSpec((d, tn // 2), lambda i, j: (0, 2 * j)),
                  pl.BlockSpec((d, tn // 2), lambda i, j: (0, 2 * j + 1))],
        out_specs=[rows(SSD_WIDTH), rows(CONV_DIM)],
        out_shape=[jax.ShapeDtypeStruct((t, SSD_WIDTH), F32), jax.ShapeDtypeStruct((t, CONV_DIM), F32)],
        compiler_params=_params("parallel", "arbitrary"),
        name="in_zx",
    )(u, w_zx, w_zx)


def _in_proj(x, g, w_qkv, w_zx, w_small, b_small, gq, gk, *, q_dtype, q_scale):
    q, k, kb, v, vb, logf, dt, u = _in_qkv(x, g, w_qkv, w_small, b_small, gq, gk, q_dtype=q_dtype, q_scale=q_scale)
    z, xbc = _in_zx(u, w_zx)
    return q, k, kb, v, vb, z, xbc, logf, dt


def _pack_in_proj(w_in, fox_b_f, ssd_dt_bias):
    fw = FOX_WIDTH
    f0 = 3 * fw
    z0 = f0 + FOX_HEADS
    x0 = z0 + SSD_WIDTH
    d0 = x0 + CONV_DIM
    w_qkv = w_in[:, :f0].astype(BF16)
    w_zx = w_in[:, z0:d0].astype(BF16)
    zeros = lambda n: jnp.zeros((w_in.shape[0], n), w_in.dtype)
    w_small = jnp.concatenate(
        [w_in[:, f0:z0], zeros(LANES - FOX_HEADS), w_in[:, d0:], zeros(LANES - SSD_HEADS)], axis=1).astype(BF16)
    b_small = jnp.concatenate(
        [fox_b_f, jnp.zeros((LANES - FOX_HEADS,), F32), ssd_dt_bias, jnp.zeros((LANES - SSD_HEADS,), F32)]
    ).reshape(1, 2 * LANES)
    return w_qkv, w_zx, w_small, b_small


ATT_BLOCK = 256


def _tri(n, *, strict=False, upper=False):
    r = lax.broadcasted_iota(jnp.int32, (n, n), 0)
    c = lax.broadcasted_iota(jnp.int32, (n, n), 1)
    if upper:
        r, c = c, r
    return (c < r) if strict else (c <= r)


def _cumsum_body(x_ref, col_ref, row_ref, carry_ref):
    @pl.when(pl.program_id(1) == 0)
    def _():
        carry_ref[...] = jnp.zeros_like(carry_ref)

    n = x_ref.shape[1]
    tril = _tri(n).astype(BF16)
    c = _dot_sel_l(tril, x_ref[0]) + carry_ref[...]
    carry_ref[...] = c[n - 1:n, :]
    c = c * LOG2E
    col_ref[0] = c
    row_ref[0] = c.T[:FOX_HEADS, :]


def _cumsum(x, *, tb=ATT_BLOCK):
    b, l, _ = x.shape
    return pl.pallas_call(
        _cumsum_body,
        grid=(b, l // tb),
        in_specs=[pl.BlockSpec((1, tb, LANES), lambda i, j: (i, j, 0))],
        out_specs=[pl.BlockSpec((1, tb, LANES), lambda i, j: (i, j, 0)),
                   pl.BlockSpec((1, FOX_HEADS, tb), lambda i, j: (i, 0, j))],
        out_shape=[jax.ShapeDtypeStruct((b, l, LANES), F32), jax.ShapeDtypeStruct((b, FOX_HEADS, l), F32)],
        scratch_shapes=[pltpu.VMEM((1, LANES), F32)],
        compiler_params=_params("parallel", "arbitrary"),
        name="logf_cumsum",
    )(x)


def _fox_prompt_body(tk, q_ref, k_ref, v_ref, ccol_ref, crow_ref, o_ref, m_ref, l_ref, cq_ref, acc_ref):
    i = pl.program_id(1)
    tq = q_ref.shape[0]
    rep = tk // LANES
    m_ref[...] = jnp.full_like(m_ref, -jnp.inf)
    l_ref[...] = jnp.zeros_like(l_ref)
    acc_ref[...] = jnp.zeros_like(acc_ref)
    for h in range(FOX_HEADS):
        cq_ref[h] = jnp.broadcast_to(ccol_ref[0, :, h:h + 1], (tq, LANES))
    row = i * tq + lax.broadcasted_iota(jnp.int32, (tq, tk), 0)
    col = lax.broadcasted_iota(jnp.int32, (tq, tk), 1)
    wide = lambda x: jnp.concatenate([x] * rep, axis=1)

    def block(j, masked):
        ks = pl.ds(pl.multiple_of(j * tk, tk), tk)
        for h in range(FOX_HEADS):
            hs = slice(h * HEAD_DIM, (h + 1) * HEAD_DIM)
            s = _dot_nt(q_ref[:, hs], k_ref[ks, hs]) + (wide(cq_ref[h]) - crow_ref[0, h, pl.ds(j, 1), :])
            if masked:
                s = jnp.where(col + j * tk <= row, s, -jnp.inf)
            m_old = m_ref[h]
            m_new = jnp.maximum(m_old, jnp.max(s, axis=-1, keepdims=True))
            alpha = jnp.exp2(m_old - m_new)
            p = jnp.exp2(s - wide(m_new))
            m_ref[h] = m_new
            l_ref[h] = alpha * l_ref[h] + jnp.sum(p, axis=-1, keepdims=True)
            acc_ref[h] = alpha * acc_ref[h] + _dot(p.astype(BF16), v_ref[ks, hs])

    last = (i * tq + tq - 1) // tk

    def step(j, carry):
        block(j, False)
        return carry

    lax.fori_loop(0, last, step, 0)
    block(last, True)
    for h in range(FOX_HEADS):
        o_ref[:, h * HEAD_DIM:(h + 1) * HEAD_DIM] = (acc_ref[h] / l_ref[h]).astype(o_ref.dtype)


def _fox_prompt(q, k, v, ccol, crow, b, l, *, tq=128):
    tk = crow.shape[-1]
    tq = min(tq, l)
    assert tk % tq == 0
    nq = l // tq
    w = FOX_WIDTH
    return pl.pallas_call(
        functools.partial(_fox_prompt_body, tk),
        grid=(b, nq),
        in_specs=[
            pl.BlockSpec((tq, w), lambda bi, i: (bi * nq + i, 0)),
            pl.BlockSpec((l, w), lambda bi, i: (bi, 0)),
            pl.BlockSpec((l, w), lambda bi, i: (bi, 0)),
            pl.BlockSpec((1, tq, LANES), lambda bi, i: (bi, i, 0)),
            pl.BlockSpec((1, FOX_HEADS, l // tk, tk), lambda bi, i: (bi, 0, 0, 0)),
        ],
        out_specs=pl.BlockSpec((tq, w), lambda bi, i: (bi * nq + i, 0)),
        out_shape=jax.ShapeDtypeStruct((b * l, w), BF16),
        scratch_shapes=[pltpu.VMEM((FOX_HEADS, tq, LANES), F32)] * 3 + [pltpu.VMEM((FOX_HEADS, tq, HEAD_DIM), F32)],
        compiler_params=_params("parallel", "arbitrary"),
        name="fox_prompt",
    )(q, k, v, ccol, crow)


def _page_cumsum_body(x_ref, w_ref, m_ref):
    n = PAGE_SIZE * FOX_HEADS

    @pl.when(pl.program_id(0) == 0)
    def _():
        r = lax.broadcasted_iota(jnp.int32, (n, n), 0)
        c = lax.broadcasted_iota(jnp.int32, (n, n), 1)
        same_head = (r & (FOX_HEADS - 1)) == (c & (FOX_HEADS - 1))
        earlier = lax.shift_right_logical(r, 3) <= lax.shift_right_logical(c, 3)
        m_ref[...] = jnp.logical_and(same_head, earlier).astype(BF16)

    w_ref[...] = _dot_sel_r(x_ref[...], m_ref[...])


def _page_cumsum(logf_pages, *, tb=256):
    n_pool, n = logf_pages.shape
    return pl.pallas_call(
        _page_cumsum_body,
        grid=(n_pool // tb,),
        in_specs=[pl.BlockSpec((tb, n), lambda i: (i, 0))],
        out_specs=pl.BlockSpec((tb, n), lambda i: (i, 0)),
        out_shape=jax.ShapeDtypeStruct((n_pool, n), F32),
        scratch_shapes=[pltpu.VMEM((n, n), BF16)],
        compiler_params=_params("arbitrary"),
        name="page_cumsum",
    )(logf_pages)


def _seq_cumsum_body(seq_len, x_ref, o_ref):
    n = x_ref.shape[0]
    r = lax.broadcasted_iota(jnp.int32, (n, n), 0)
    c = lax.broadcasted_iota(jnp.int32, (n, n), 1)
    same_seq = (r // seq_len) == (c // seq_len)
    o_ref[...] = _dot_sel_l(jnp.logical_and(same_seq, c <= r).astype(BF16), x_ref[...])


def _seq_cumsum(x, seq_len, *, tb=128):
    t = x.shape[0]
    return pl.pallas_call(
        functools.partial(_seq_cumsum_body, seq_len),
        grid=(t // tb,),
        in_specs=[pl.BlockSpec((tb, LANES), lambda i: (i, 0))],
        out_specs=pl.BlockSpec((tb, LANES), lambda i: (i, 0)),
        out_shape=jax.ShapeDtypeStruct((t, LANES), F32),
        compiler_params=_params("parallel"),
        name="seq_cumsum",
    )(x)


def _fox_sample_body(pps, pt_ref, q_ref, kn_ref, vn_ref, cn_ref, *refs):
    kp_refs, vp_refs, w_refs = refs[:pps], refs[pps:2 * pps], refs[2 * pps:3 * pps]
    o_ref, q_scr, colq_ref, toff_ref, m_ref, l_ref, acc_ref = refs[3 * pps:]
    j = pl.program_id(1)
    nq = q_ref.shape[0]
    rows = nq * FOX_HEADS
    page_keys = PAGE_SIZE * FOX_HEADS
    row_id = lax.broadcasted_iota(jnp.int32, (rows, 1), 0)
    head_of_row = row_id & (FOX_HEADS - 1)
    query_of_row = lax.shift_right_logical(row_id, 3)

    @pl.when(j == 0)
    def _():
        q = q_ref[...].reshape(rows, HEAD_DIM).astype(BF16)
        q_scr[...] = q
        cn = cn_ref[0]
        key = lax.broadcasted_iota(jnp.int32, (1, rows), 1)
        colq = jnp.sum(jnp.where(key == row_id, cn, 0.0), axis=-1, keepdims=True)
        colq_ref[...] = colq
        toff_ref[...] = jnp.zeros_like(toff_ref)
        s = _dot_nt(q, kn_ref[...].reshape(rows, HEAD_DIM).astype(BF16)) + colq - cn
        valid = jnp.logical_and((key & (FOX_HEADS - 1)) == head_of_row,
                                lax.shift_right_logical(key, 3) <= query_of_row)
        s = jnp.where(valid, s, -jnp.inf)
        m = jnp.max(s, axis=-1, keepdims=True)
        p = jnp.exp(s - m)
        m_ref[...] = m
        l_ref[...] = jnp.sum(p, axis=-1, keepdims=True)
        acc_ref[...] = _dot(p.astype(BF16), vn_ref[...].reshape(rows, HEAD_DIM).astype(BF16))

    lane = lax.broadcasted_iota(jnp.int32, (1, LANES), 1)
    own_head = (lax.broadcasted_iota(jnp.int32, (1, page_keys), 1) & (FOX_HEADS - 1)) == head_of_row
    q = q_scr[...]
    colq = colq_ref[...]
    toff = toff_ref[...]
    tiles = []
    for kp_ref, w_ref in zip(kp_refs, w_refs):
        w = w_ref[0]
        last = jnp.where(lane == LANES - FOX_HEADS + head_of_row, w[:, page_keys - LANES:], 0.0)
        toff = toff + jnp.sum(last, axis=-1, keepdims=True)
        s = _dot_nt(q, kp_ref[0].reshape(page_keys, HEAD_DIM).astype(BF16))
        tiles.append(jnp.where(own_head, s + (colq + toff) - w, -jnp.inf))
    toff_ref[...] = toff
    m_old = m_ref[...]
    m = m_old
    for s in tiles:
        m = jnp.maximum(m, jnp.max(s, axis=-1, keepdims=True))
    alpha = jnp.exp(m_old - m)
    l = alpha * l_ref[...]
    acc = alpha * acc_ref[...]
    for s, vp_ref in zip(tiles, vp_refs):
        p = jnp.exp(s - m)
        l = l + jnp.sum(p, axis=-1, keepdims=True)
        acc = acc + _dot(p.astype(BF16), vp_ref[0].reshape(page_keys, HEAD_DIM).astype(BF16))
    m_ref[...] = m
    l_ref[...] = l
    acc_ref[...] = acc

    @pl.when(j == pl.num_programs(1) - 1)
    def _():
        o_ref[...] = (acc_ref[...] / l_ref[...]).reshape(nq, FOX_HEADS, HEAD_DIM)


def _fox_sample(page_table, q, k_new, v_new, c_new, k_pages, v_pages, w_pages, n_seq, nq, *, pages_per_step=16):
    n_pages = page_table.shape[1]
    rows = nq * FOX_HEADS
    pps = math.gcd(pages_per_step, n_pages)

    def page(i, ndim):
        return lambda b, j, pt: (pt[b * n_pages + (n_pages - 1 - j * pps - i)],) + (0,) * (ndim - 1)

    seq = pl.BlockSpec((nq, FOX_HEADS, HEAD_DIM), lambda b, j, pt: (b, 0, 0))
    kv_specs = [pl.BlockSpec((1, PAGE_SIZE, FOX_HEADS, HEAD_DIM), page(i, 4)) for i in range(pps)]
    w_specs = [pl.BlockSpec((1, 1, PAGE_SIZE * FOX_HEADS), page(i, 3)) for i in range(pps)]
    grid_spec = pltpu.PrefetchScalarGridSpec(
        num_scalar_prefetch=1,
        grid=(n_seq, n_pages // pps),
        in_specs=[seq, seq, seq, pl.BlockSpec((1, 1, rows), lambda b, j, pt: (b, 0, 0))]
        + kv_specs + kv_specs + w_specs,
        out_specs=seq,
        scratch_shapes=[
            pltpu.VMEM((rows, HEAD_DIM), BF16), pltpu.VMEM((rows, 1), F32), pltpu.VMEM((rows, 1), F32),
            pltpu.VMEM((rows, 1), F32), pltpu.VMEM((rows, 1), F32), pltpu.VMEM((rows, HEAD_DIM), F32),
        ],
    )
    return pl.pallas_call(
        functools.partial(_fox_sample_body, pps),
        grid_spec=grid_spec,
        out_shape=jax.ShapeDtypeStruct((n_seq * nq, FOX_HEADS, HEAD_DIM), F32),
        compiler_params=_params("parallel", "arbitrary"),
        name="fox_sample",
    )(page_table.reshape(-1), q, k_new, v_new, c_new, *([k_pages] * pps), *([v_pages] * pps), *([w_pages] * pps))


SSD_PAIRS = SSD_HEADS // 2
PAIRS_PER_GROUP = SSD_PAIRS // SSD_GROUPS


def _expander(width):
    n = SSD_HEADS * width
    h = lax.broadcasted_iota(jnp.int32, (LANES, n), 0)
    c = lax.broadcasted_iota(jnp.int32, (LANES, n), 1)
    return (lax.shift_right_logical(c, int(math.log2(width))) == h).astype(BF16)


def _ssd_local(xs, bm, cm, dt, alog, mask, tot_sel):
    n = xs.shape[0]
    lane = lax.broadcasted_iota(jnp.int32, (1, LANES), 1)
    dta = dt * jnp.where(lane < SSD_HEADS, -jnp.exp(alog), 0.0)
    e64 = _expander(SSD_HEAD_DIM)
    a_cum = _dot_sel_l(mask.astype(BF16), dta)
    a_cum_t = a_cum.T
    ac_exp = _dot_sel_r(a_cum, e64)
    if tot_sel is None:
        atot_exp = ac_exp[n - 1:n, :]
    else:
        atot_exp = _dot_sel_r(_dot_sel_l(tot_sel, dta), e64)
    ac_b = _dot_sel_r(a_cum, _expander(LANES))
    xdt = xs * _dot_sel_r(dt, e64)
    half = lax.broadcasted_iota(jnp.int32, (n, LANES), 1) < SSD_HEAD_DIM
    out = {
        "xdtw": xdt * jnp.exp(atot_exp - ac_exp),
        "eac": jnp.exp(ac_exp),
        "atot_exp": atot_exp,
        "ac_b": ac_b,
        "bg": [], "cg": [], "y_diag": [],
    }
    for g in range(SSD_GROUPS):
        gs = slice(g * SSD_STATE, (g + 1) * SSD_STATE)
        bg = bm[:, gs].astype(BF16)
        cg = cm[:, gs].astype(BF16)
        out["bg"].append(bg)
        out["cg"].append(cg)
        cb = _dot_nt(cg, bg)
        for k in range(g * PAIRS_PER_GROUP, (g + 1) * PAIRS_PER_GROUP):
            ps = slice(k * LANES, (k + 1) * LANES)
            ms = []
            for h in (2 * k, 2 * k + 1):
                seg = ac_b[:, h * LANES:(h + 1) * LANES] - a_cum_t[h:h + 1, :]
                ms.append(cb * jnp.exp(jnp.where(mask, seg, -jnp.inf)))
            m_cat = jnp.concatenate(ms, axis=1).astype(BF16)
            xp = xdt[:, ps]
            x_bd = jnp.concatenate([jnp.where(half, xp, 0.0), jnp.where(half, 0.0, xp)], axis=0).astype(BF16)
            out["y_diag"].append(_dot(m_cat, x_bd))
    return out


def _conv_ssd_sample_body(seq_len, xbc_ref, dt_ref, buf_ref, h0_ref, w_ref, b_ref, alog_ref, dexp_ref,
                          y_ref, hout_ref, cout_ref, xp_ref):
    n = xbc_ref.shape[0]
    n_seq = n // seq_len
    taps = CONV_WIDTH - 1
    base = 8 - taps

    x = xbc_ref[...]
    xp_ref[:, base:8, :] = buf_ref[...]
    xp_ref[:, 8:8 + seq_len, :] = x.reshape(n_seq, seq_len, CONV_DIM)
    acc = b_ref[...] + x * w_ref[taps:taps + 1, :]
    for j in range(taps):
        acc = acc + xp_ref[:, base + j:base + j + seq_len, :].reshape(n, CONV_DIM) * w_ref[j:j + 1, :]
    cout_ref[...] = xp_ref[:, 8 + seq_len - taps:8 + seq_len, :]
    conv = _silu(acc)
    xs = conv[:, :SSD_WIDTH]
    bm = conv[:, SSD_WIDTH:SSD_WIDTH + SSD_GROUPS * SSD_STATE]
    cm = conv[:, SSD_WIDTH + SSD_GROUPS * SSD_STATE:]

    r = lax.broadcasted_iota(jnp.int32, (n, n), 0)
    c = lax.broadcasted_iota(jnp.int32, (n, n), 1)
    same_seq = (r // seq_len) == (c // seq_len)
    loc = _ssd_local(xs, bm, cm, dt_ref[...], alog_ref[...], jnp.logical_and(same_seq, c <= r),
                     same_seq.astype(BF16))

    gw = PAIRS_PER_GROUP * LANES
    seq_of_col = lax.broadcasted_iota(jnp.int32, (1, n), 1) // seq_len
    decay_t = jnp.exp(loc["atot_exp"]).T
    for g in range(SSD_GROUPS):
        gr = slice(g * gw, (g + 1) * gw)
        h_prev = h0_ref[:, gr, :]
        z = _dot_nt(h_prev.reshape(n_seq * gw, SSD_STATE).astype(BF16), loc["cg"][g])
        y_off_t = jnp.zeros((gw, n), F32)
        for s in range(n_seq):
            y_off_t = y_off_t + jnp.where(seq_of_col == s, z[s * gw:(s + 1) * gw, :], 0.0)
        y_off = y_off_t.T * loc["eac"][:, gr]
        y = jnp.concatenate(loc["y_diag"][g * PAIRS_PER_GROUP:(g + 1) * PAIRS_PER_GROUP], axis=1)
        y_ref[:, gr] = y + y_off + xs[:, gr] * dexp_ref[:, gr]
        xw_t = loc["xdtw"][:, gr].T
        lhs = jnp.concatenate([jnp.where(seq_of_col == s, xw_t, 0.0) for s in range(n_seq)], axis=0)
        s_new = _dot(lhs.astype(BF16), loc["bg"][g])
        for s in range(n_seq):
            col = decay_t[gr, s * seq_len:s * seq_len + 1]
            hout_ref[s, gr, :] = h_prev[s] * col + s_new[s * gw:(s + 1) * gw, :]


def _conv_ssd_sample(xbc, dt, conv_buf, h0, conv_w, conv_b, alog, d_exp, n_seq, seq_len):
    assert seq_len == 8 and CONV_WIDTH - 1 <= seq_len
    tile = LANES
    ts = tile // seq_len
    const = lambda i: (0, 0)
    per_s = lambda i: (i, 0, 0)
    return pl.pallas_call(
        functools.partial(_conv_ssd_sample_body, seq_len),
        grid=(n_seq // ts,),
        in_specs=[
            pl.BlockSpec((tile, CONV_DIM), lambda i: (i, 0)),
            pl.BlockSpec((tile, LANES), lambda i: (i, 0)),
            pl.BlockSpec((ts, CONV_WIDTH - 1, CONV_DIM), per_s),
            pl.BlockSpec((ts, SSD_WIDTH, SSD_STATE), per_s),
            pl.BlockSpec((CONV_WIDTH, CONV_DIM), const),
            pl.BlockSpec((1, CONV_DIM), const),
            pl.BlockSpec((1, LANES), const),
            pl.BlockSpec((1, SSD_WIDTH), const),
        ],
        out_specs=[
            pl.BlockSpec((tile, SSD_WIDTH), lambda i: (i, 0)),
            pl.BlockSpec((ts, SSD_WIDTH, SSD_STATE), per_s),
            pl.BlockSpec((ts, CONV_WIDTH - 1, CONV_DIM), per_s),
        ],
        out_shape=[
            jax.ShapeDtypeStruct((n_seq * seq_len, SSD_WIDTH), F32),
            jax.ShapeDtypeStruct((n_seq, SSD_WIDTH, SSD_STATE), F32),
            jax.ShapeDtypeStruct((n_seq, CONV_WIDTH - 1, CONV_DIM), F32),
        ],
        scratch_shapes=[pltpu.VMEM((ts, 8 + seq_len, CONV_DIM), F32)],
        compiler_params=_params("parallel"),
        name="conv_ssd_sample",
    )(xbc, dt, conv_buf, h0, conv_w, conv_b, alog, d_exp)


def _conv_ssd_prompt_body(xbc_ref, dt_ref, buf_ref, h0_ref, w_ref, b_ref, alog_ref, dexp_ref,
                          y_ref, hout_ref, cout_ref, state_ref, xp_ref):
    c = pl.program_id(1)
    nc = pl.num_programs(1)
    tl = xbc_ref.shape[0]
    taps = CONV_WIDTH - 1
    base = 8 - taps

    @pl.when(c == 0)
    def _():
        state_ref[...] = h0_ref[0]
        xp_ref[base:8, :] = buf_ref[0]

    x = xbc_ref[...]
    xp_ref[8:8 + tl, :] = x
    acc = b_ref[...] + x * w_ref[taps:taps + 1, :]
    for j in range(taps):
        acc = acc + xp_ref[base + j:base + j + tl, :] * w_ref[j:j + 1, :]
    xp_ref[base:8, :] = x[tl - taps:, :]
    conv = _silu(acc)
    xs = conv[:, :SSD_WIDTH]
    bm = conv[:, SSD_WIDTH:SSD_WIDTH + SSD_GROUPS * SSD_STATE]
    cm = conv[:, SSD_WIDTH + SSD_GROUPS * SSD_STATE:]

    causal = _tri(tl)
    loc = _ssd_local(xs, bm, cm, dt_ref[...], alog_ref[...], causal, None)
    top = lax.broadcasted_iota(jnp.int32, (tl, LANES), 0) < SSD_HEAD_DIM
    for k in range(SSD_PAIRS):
        g = k // PAIRS_PER_GROUP
        ps = slice(k * LANES, (k + 1) * LANES)
        s_prev = state_ref[ps, :]
        y_off = _dot_nt(loc["cg"][g], s_prev.astype(BF16)) * loc["eac"][:, ps]
        y_ref[:, ps] = loc["y_diag"][k] + y_off + xs[:, ps] * dexp_ref[:, ps]
        cd = [jnp.exp(loc["ac_b"][tl - 1:tl, h * LANES:(h + 1) * LANES]) for h in (2 * k, 2 * k + 1)]
        state_ref[ps, :] = s_prev * jnp.where(top, cd[0], cd[1]) + _dot_tn(loc["xdtw"][:, ps].astype(BF16), loc["bg"][g])

    @pl.when(c == nc - 1)
    def _():
        hout_ref[0] = state_ref[...]
        cout_ref[0] = xp_ref[base:8, :]


def _conv_ssd_prompt(xbc, dt, conv_buf, h0, conv_w, conv_b, alog, d_exp, b, l):
    tl = SSD_CHUNK
    nc = l // tl
    const = lambda bi, c: (0, 0)
    per_b = lambda bi, c: (bi, 0, 0)
    return pl.pallas_call(
        _conv_ssd_prompt_body,
        grid=(b, nc),
        in_specs=[
            pl.BlockSpec((tl, CONV_DIM), lambda bi, c: (bi * nc + c, 0)),
            pl.BlockSpec((tl, LANES), lambda bi, c: (bi * nc + c, 0)),
            pl.BlockSpec((1, CONV_WIDTH - 1, CONV_DIM), per_b),
            pl.BlockSpec((1, SSD_WIDTH, SSD_STATE), per_b),
            pl.BlockSpec((CONV_WIDTH, CONV_DIM), const),
            pl.BlockSpec((1, CONV_DIM), const),
            pl.BlockSpec((1, LANES), const),
            pl.BlockSpec((1, SSD_WIDTH), const),
        ],
        out_specs=[
            pl.BlockSpec((tl, SSD_WIDTH), lambda bi, c: (bi * nc + c, 0)),
            pl.BlockSpec((1, SSD_WIDTH, SSD_STATE), per_b),
            pl.BlockSpec((1, CONV_WIDTH - 1, CONV_DIM), per_b),
        ],
        out_shape=[
            jax.ShapeDtypeStruct((b * l, SSD_WIDTH), F32),
            jax.ShapeDtypeStruct((b, SSD_WIDTH, SSD_STATE), F32),
            jax.ShapeDtypeStruct((b, CONV_WIDTH - 1, CONV_DIM), F32),
        ],
        scratch_shapes=[pltpu.VMEM((SSD_WIDTH, SSD_STATE), F32), pltpu.VMEM((8 + tl, CONV_DIM), F32)],
        compiler_params=_params("parallel", "arbitrary"),
        name="conv_ssd_prompt",
    )(xbc, dt, conv_buf, h0, conv_w, conv_b, alog, d_exp)


def _out_proj_body(h_ref, fox_ref, y_ref, z_ref, gs_ref, wf_ref, ws_ref, gx_ref, wq_ref, gq_ref, o_ref, q_ref):
    yn = _rms(y_ref[...] * _silu(z_ref[...]), gs_ref[...]).astype(BF16)
    h = h_ref[...] + _dot(fox_ref[...].astype(BF16), wf_ref[...]) + _dot(yn, ws_ref[...])
    o_ref[...] = h
    q = _dot(_rms(h, gx_ref[...]).astype(BF16), wq_ref[...])
    q_ref[...] = _head_norm(q, gq_ref[...]) * (HEAD_DIM ** -0.5)


def _out_proj(h, fox, y, z, g_ssd, w_fox, w_ssd, g_x, wq, gq, *, tm=256):
    t, d = h.shape
    tm = min(tm, t)
    row = lambda w: pl.BlockSpec((tm, w), lambda i: (i, 0))
    full = lambda a: pl.BlockSpec(a.shape, lambda i: (0, 0))
    consts = [g_ssd.reshape(1, SSD_WIDTH), w_fox, w_ssd, g_x.reshape(1, d), wq, gq.reshape(1, HEAD_DIM)]
    return pl.pallas_call(
        _out_proj_body,
        grid=(t // tm,),
        in_specs=[row(d), row(FOX_WIDTH), row(SSD_WIDTH), row(SSD_WIDTH)] + [full(a) for a in consts],
        out_specs=[row(d), row(XATTN_WIDTH)],
        out_shape=[jax.ShapeDtypeStruct((t, d), F32), jax.ShapeDtypeStruct((t, XATTN_WIDTH), F32)],
        compiler_params=_params("parallel"),
        name="out_proj",
    )(h, fox, y, z, *consts)


def _mem_kv_body(m_ref, g_ref, w_ref, gk_ref, k_ref, v_ref):
    kv = _dot(_rms(m_ref[...], g_ref[...]).astype(BF16), w_ref[...])
    k_ref[...] = _head_norm(kv[:, :XATTN_WIDTH], gk_ref[...])
    v_ref[...] = kv[:, XATTN_WIDTH:]


def _mem_kv(mem, g, w_kv, gk, *, tm=256):
    t, d = mem.shape
    row = lambda w: pl.BlockSpec((tm, w), lambda i: (i, 0))
    full = lambda a: pl.BlockSpec(a.shape, lambda i: (0, 0))
    consts = [g.reshape(1, d), w_kv, gk.reshape(1, HEAD_DIM)]
    return pl.pallas_call(
        _mem_kv_body,
        grid=(t // tm,),
        in_specs=[row(d)] + [full(a) for a in consts],
        out_specs=[row(XATTN_WIDTH), row(XATTN_WIDTH)],
        out_shape=[jax.ShapeDtypeStruct((t, XATTN_WIDTH), F32)] * 2,
        compiler_params=_params("parallel"),
        name="mem_kv",
    )(mem, *consts)


def _xattn_body(q_ref, k_ref, v_ref, o_ref):
    for h in range(XATTN_HEADS):
        hs = slice(h * HEAD_DIM, (h + 1) * HEAD_DIM)
        s = _dot_nt(q_ref[:, hs].astype(BF16), k_ref[0, :, hs].astype(BF16))
        p = jnp.exp(s - jnp.max(s, axis=-1, keepdims=True))
        o = _dot(p.astype(BF16), v_ref[0, :, hs].astype(BF16))
        o_ref[:, hs] = o / jnp.sum(p, axis=-1, keepdims=True)


def _xattn(q, mem_k, mem_v, b, l, *, tq=512):
    tq = min(tq, l)
    nq = l // tq
    n_mem = mem_k.shape[1]
    qspec = pl.BlockSpec((tq, XATTN_WIDTH), lambda bi, i: (bi * nq + i, 0))
    mspec = pl.BlockSpec((1, n_mem, XATTN_WIDTH), lambda bi, i: (bi, 0, 0))
    return pl.pallas_call(
        _xattn_body,
        grid=(b, nq),
        in_specs=[qspec, mspec, mspec],
        out_specs=qspec,
        out_shape=jax.ShapeDtypeStruct((b * l, XATTN_WIDTH), F32),
        compiler_params=_params("parallel", "arbitrary"),
        name="xattn",
    )(q, mem_k, mem_v)


def _xattn_rows_body(n_seq, q_ref, k_ref, v_ref, o_ref):
    tiles = q_ref.shape[0] // n_seq
    rows = tiles * 8
    keys = k_ref.shape[1] * k_ref.shape[2]
    row_head = lax.broadcasted_iota(jnp.int32, (rows, 1), 0) & (XATTN_HEADS - 1)
    key_head = lax.broadcasted_iota(jnp.int32, (1, keys), 1) & (XATTN_HEADS - 1)
    own_head = row_head == key_head
    for s in range(n_seq):
        qs = slice(s * tiles, (s + 1) * tiles)
        q = q_ref[qs].reshape(rows, HEAD_DIM).astype(BF16)
        sc = jnp.where(own_head, _dot_nt(q, k_ref[s].reshape(keys, HEAD_DIM).astype(BF16)), -jnp.inf)
        p = jnp.exp(sc - jnp.max(sc, axis=-1, keepdims=True))
        o = _dot(p.astype(BF16), v_ref[s].reshape(keys, HEAD_DIM).astype(BF16))
        o_ref[qs] = (o / jnp.sum(p, axis=-1, keepdims=True)).reshape(tiles, 8, HEAD_DIM)


def _xattn_rows(q, mem_k, mem_v, n_seq, *, ts=8):
    tiles = q.shape[0] // n_seq
    ts = math.gcd(ts, n_seq)
    qspec = pl.BlockSpec((ts * tiles, 8, HEAD_DIM), lambda i: (i, 0, 0))
    mspec = pl.BlockSpec((ts,) + mem_k.shape[1:], lambda i: (i, 0, 0, 0))
    return pl.pallas_call(
        functools.partial(_xattn_rows_body, ts),
        grid=(n_seq // ts,),
        in_specs=[qspec, mspec, mspec],
        out_specs=qspec,
        out_shape=jax.ShapeDtypeStruct(q.shape, F32),
        compiler_params=_params("parallel"),
        name="xattn_rows",
    )(q, mem_k, mem_v)


def kernel(x_prompt, x_sample, cache_fox_k, cache_fox_v, cache_fox_logf, cache_mem_k, cache_mem_v, state_ssm, state_conv, page_table, mem_prompt, ffn1_norm, ffn1_w_gate, ffn1_w_up, ffn1_w_down, mix_norm, w_in, fox_b_f, fox_q_norm, fox_k_norm, conv_w, conv_b, ssd_dt_bias, ssd_A_log, ssd_D, ssd_out_norm, w_out, xattn_norm, mem_norm, xattn_w_q, xattn_w_kv, xattn_q_norm, xattn_k_norm, xattn_w_o, ffn2_norm, ffn2_w_gate, ffn2_w_up, ffn2_w_down):
    assert x_prompt.shape[2] == D_MODEL and ffn1_norm.shape[0] == 1
    d = D_MODEL
    bp, lp = x_prompt.shape[:2]
    bs, ls = x_sample.shape[:2]
    n_mem = mem_prompt.shape[1]
    n_pool = cache_fox_k.shape[1]

    bf = lambda w: w[0].astype(BF16)
    ffn1 = (ffn1_norm[0], bf(ffn1_w_gate), bf(ffn1_w_up), bf(ffn1_w_down))
    ffn2 = (ffn2_norm[0], bf(ffn2_w_gate), bf(ffn2_w_up), bf(ffn2_w_down))
    in_w = _pack_in_proj(w_in[0], fox_b_f[0], ssd_dt_bias[0])
    w_fox, w_ssd = bf(w_out)[:FOX_WIDTH], bf(w_out)[FOX_WIDTH:]
    wq, wkv, wo = bf(xattn_w_q), bf(xattn_w_kv), bf(xattn_w_o)
    alog = jnp.pad(ssd_A_log[0], (0, LANES - SSD_HEADS)).reshape(1, LANES)
    d_exp = jnp.repeat(ssd_D[0], SSD_HEAD_DIM).reshape(1, SSD_WIDTH)
    cw, cb = conv_w[0], conv_b[0].reshape(1, CONV_DIM)

    def front(x, q_dtype, q_scale):
        h1 = _ffn(x, *ffn1)
        return h1, _in_proj(h1, mix_norm[0], *in_w, fox_q_norm[0], fox_k_norm[0], q_dtype=q_dtype, q_scale=q_scale)

    def back(h1, fox, y, z, attend):
        h2, xq = _out_proj(h1, fox, y, z, ssd_out_norm[0], w_fox, w_ssd, xattn_norm[0], wq, xattn_q_norm[0])
        return _ffn(h2, *ffn2, pre=(attend(xq), wo))

    h1, (q, k_p, kb, v_p, vb, z, xbc, logf_p, dt) = front(x_prompt.reshape(bp * lp, d), BF16, HEAD_DIM ** -0.5 * LOG2E)
    ccol, crow = _cumsum(logf_p.reshape(bp, lp, LANES))
    tq = min(ATT_BLOCK, lp)
    fox = _fox_prompt(q, kb, vb, ccol, crow.reshape(bp, FOX_HEADS, lp // tq, tq), bp, lp)
    y, ssm_p, conv_p = _conv_ssd_prompt(
        xbc, dt, jnp.zeros((bp, CONV_WIDTH - 1, CONV_DIM), F32), jnp.zeros((bp, SSD_WIDTH, SSD_STATE), F32),
        cw, cb, alog, d_exp, bp, lp)
    mk, mv = _mem_kv(mem_prompt.reshape(bp * n_mem, d), mem_norm[0], wkv, xattn_k_norm[0])
    y_prompt = back(h1, fox, y, z, lambda xq: _xattn(
        xq, mk.reshape(bp, n_mem, XATTN_WIDTH), mv.reshape(bp, n_mem, XATTN_WIDTH), bp, lp))

    h1, (q, k_s, _, v_s, _, z, xbc, logf_s, dt) = front(x_sample.reshape(bs * ls, d), F32, HEAD_DIM ** -0.5)
    w_pages = _page_cumsum(cache_fox_logf[0].reshape(n_pool, PAGE_SIZE * FOX_HEADS))
    c_new = _seq_cumsum(logf_s, ls)[:, :FOX_HEADS].reshape(bs, 1, ls * FOX_HEADS)
    fox = _fox_sample(
        page_table, q.reshape(bs * ls, FOX_HEADS, HEAD_DIM), k_s, v_s, c_new, cache_fox_k[0], cache_fox_v[0],
        w_pages.reshape(n_pool, 1, PAGE_SIZE * FOX_HEADS), bs, ls).reshape(bs * ls, FOX_WIDTH)
    y, ssm_s, conv_s = _conv_ssd_sample(
        xbc, dt, state_conv[0], state_ssm[0].reshape(bs, SSD_WIDTH, SSD_STATE), cw, cb, alog, d_exp, bs, ls)
    mem_tiles = lambda m: m[0].reshape(bs, n_mem * XATTN_HEADS // 8, 8, HEAD_DIM)
    y_sample = back(h1, fox, y, z, lambda xq: _xattn_rows(
        xq.reshape(bs * ls * XATTN_HEADS // 8, 8, HEAD_DIM), mem_tiles(cache_mem_k), mem_tiles(cache_mem_v), bs,
    ).reshape(bs * ls, XATTN_WIDTH))

    fox_shape = lambda b, l: (1, b, l, FOX_HEADS, HEAD_DIM)
    ssm_shape = lambda b: (1, b, SSD_HEADS, SSD_HEAD_DIM, SSD_STATE)
    mem_shape = (1, bp, n_mem, XATTN_HEADS, HEAD_DIM)
    return (
        y_prompt.reshape(bp, lp, d), y_sample.reshape(bs, ls, d),
        k_p.reshape(fox_shape(bp, lp)), v_p.reshape(fox_shape(bp, lp)),
        logf_p[:, :FOX_HEADS].reshape(1, bp, lp, FOX_HEADS),
        ssm_p.reshape(ssm_shape(bp)), conv_p[None], mk.reshape(mem_shape), mv.reshape(mem_shape),
        k_s.reshape(fox_shape(bs, ls)), v_s.reshape(fox_shape(bs, ls)),
        logf_s[:, :FOX_HEADS].reshape(1, bs, ls, FOX_HEADS),
        ssm_s.reshape(ssm_shape(bs)), conv_s[None],
    )
```

```python
import functools
import math

import jax
import jax.numpy as jnp
from jax import lax
from jax.experimental import pallas as pl
from jax.experimental.pallas import tpu as pltpu

F32 = jnp.float32
BF16 = jnp.bfloat16

EPS = 1e-6
LOG2E = math.log2(math.e)
FFN_RESIDUAL = 0.5
D_MODEL = 2048
D_FF = 5632
PAGE_SIZE = 128
FOX_HEADS = 8
HEAD_DIM = 128
FOX_WIDTH = FOX_HEADS * HEAD_DIM
SSD_HEADS = 16
SSD_HEAD_DIM = 64
SSD_WIDTH = SSD_HEADS * SSD_HEAD_DIM
SSD_GROUPS = 2
SSD_STATE = 128
SSD_CHUNK = 128
CONV_WIDTH = 4
CONV_DIM = SSD_WIDTH + 2 * SSD_GROUPS * SSD_STATE
XATTN_HEADS = 4
XATTN_WIDTH = XATTN_HEADS * HEAD_DIM
LANES = 128
VMEM_LIMIT_BYTES = 56 * 1024 * 1024

def _params(*semantics):
    return pltpu.CompilerParams(dimension_semantics=semantics, vmem_limit_bytes=VMEM_LIMIT_BYTES)


def _rms(x, g):
    return x * lax.rsqrt(jnp.mean(x * x, axis=-1, keepdims=True) + EPS) * g


def _dot(a, b):
    return jnp.dot(a, b, preferred_element_type=F32)


def _dot_nt(a, b):
    return lax.dot_general(a, b, (((1,), (1,)), ((), ())), preferred_element_type=F32)


def _dot_tn(a, b):
    return lax.dot_general(a, b, (((0,), (0,)), ((), ())), preferred_element_type=F32)


def _split3(x):
    x1 = x.astype(BF16)
    r = x - x1.astype(F32)
    x2 = r.astype(BF16)
    x3 = (r - x2.astype(F32)).astype(BF16)
    return x1, x2, x3


def _dot_sel_l(sel, x):
    x1, x2, x3 = _split3(x)
    return _dot(sel, x1) + _dot(sel, x2) + _dot(sel, x3)


def _dot_sel_r(x, sel):
    x1, x2, x3 = _split3(x)
    return _dot(x1, sel) + _dot(x2, sel) + _dot(x3, sel)


def _silu(x):
    return x * jax.nn.sigmoid(x)


def _ffn_body(pre_proj, split, *refs):
    if pre_proj:
        x_ref, a_ref, wo_ref, g_ref, *w_refs, o_ref, xn_ref = refs
    else:
        x_ref, g_ref, *w_refs, o_ref, xn_ref = refs

    @pl.when(pl.program_id(1) == 0)
    def _():
        x = x_ref[...]
        if pre_proj:
            x = x + _dot(a_ref[...].astype(BF16), wo_ref[...])
        xn_ref[...] = _rms(x, g_ref[...]).astype(BF16)
        o_ref[...] = x

    xn = xn_ref[...]
    acc = None
    for wg_ref, wu_ref, wd_ref in zip(w_refs[:split], w_refs[split:2 * split], w_refs[2 * split:]):
        gate = _dot(xn, wg_ref[...])
        up = _dot(xn, wu_ref[...])
        h = (_silu(gate) * up * FFN_RESIDUAL).astype(BF16)
        part = _dot(h, wd_ref[...])
        acc = part if acc is None else acc + part
    o_ref[...] += acc


def _ffn(x, g, wg, wu, wd, pre=None, *, tm=512, tf=512, split=2):
    t, d = x.shape
    f = wg.shape[1]
    tm = min(tm, t)
    ts = tf // split
    grid = (t // tm, f // tf)
    row = lambda i, j: (i, 0)
    in_specs = [pl.BlockSpec((tm, d), row)]
    args = [x]
    if pre is not None:
        a, wo = pre
        in_specs += [pl.BlockSpec((tm, a.shape[1]), row), pl.BlockSpec(wo.shape, lambda i, j: (0, 0))]
        args += [a, wo]
    cols = [pl.BlockSpec((d, ts), lambda i, j, s=s: (0, j * split + s)) for s in range(split)]
    rows = [pl.BlockSpec((ts, d), lambda i, j, s=s: (j * split + s, 0)) for s in range(split)]
    in_specs += [pl.BlockSpec((1, d), lambda i, j: (0, 0))] + cols + cols + rows
    args += [g.reshape(1, d)] + [wg] * split + [wu] * split + [wd] * split
    return pl.pallas_call(
        functools.partial(_ffn_body, pre is not None, split),
        grid=grid,
        in_specs=in_specs,
        out_specs=pl.BlockSpec((tm, d), row),
        out_shape=jax.ShapeDtypeStruct((t, d), F32),
        scratch_shapes=[pltpu.VMEM((tm, d), BF16)],
        compiler_params=_params("parallel", "arbitrary"),
        name="ffn_pre" if pre is not None else "ffn",
    )(*args)


IN_QKV_TN = 1024
IN_ZX_TN = 512


def _head_norm(y, g):
    outs = []
    for c in range(y.shape[1] // HEAD_DIM):
        yc = y[:, c * HEAD_DIM:(c + 1) * HEAD_DIM]
        outs.append(yc * lax.rsqrt(jnp.mean(yc * yc, axis=-1, keepdims=True) + EPS) * g)
    return jnp.concatenate(outs, axis=1)


def _in_qkv_body(q_scale, x_ref, g_ref, w_ref, ws_ref, bs_ref, gq_ref, gk_ref,
                 q_ref, k_ref, kb_ref, v_ref, vb_ref, logf_ref, dt_ref, u_ref):
    j = pl.program_id(1)

    @pl.when(j == 0)
    def _():
        u = _rms(x_ref[...], g_ref[...]).astype(BF16)
        u_ref[...] = u
        s = _dot(u, ws_ref[...]) + bs_ref[...]
        t = jnp.log1p(jnp.exp(-jnp.abs(s)))
        logf_ref[...] = (jnp.minimum(s, 0.0) - t)[:, :LANES]
        dt_ref[...] = (jnp.maximum(s, 0.0) + t)[:, LANES:]
        q_ref[...] = (_head_norm(_dot(u, w_ref[...]), gq_ref[...]) * q_scale).astype(q_ref.dtype)

    def by_head(y, out_ref, bf_ref):
        bf_ref[...] = y.astype(BF16)
        for h in range(FOX_HEADS):
            out_ref[:, h, :] = y[:, h * HEAD_DIM:(h + 1) * HEAD_DIM]

    @pl.when(j == 1)
    def _():
        by_head(_head_norm(_dot(u_ref[...], w_ref[...]), gk_ref[...]), k_ref, kb_ref)

    @pl.when(j == 2)
    def _():
        by_head(_dot(u_ref[...], w_ref[...]), v_ref, vb_ref)


def _in_qkv(x, g, w_qkv, w_small, b_small, gq, gk, *, q_dtype, q_scale, tm=512):
    t, d = x.shape
    tm = min(tm, t)
    tn = IN_QKV_TN
    assert tn == FOX_WIDTH and w_qkv.shape[1] == 3 * tn
    const = lambda i, j: (0, 0)
    rows = lambda w: pl.BlockSpec((tm, w), lambda i, j: (i, 0))
    by_head = pl.BlockSpec((tm, FOX_HEADS, HEAD_DIM), lambda i, j: (i, 0, 0))
    by_head_shape = jax.ShapeDtypeStruct((t, FOX_HEADS, HEAD_DIM), F32)
    flat = lambda dt, w: jax.ShapeDtypeStruct((t, w), dt)
    return pl.pallas_call(
        functools.partial(_in_qkv_body, q_scale),
        grid=(t // tm, 3),
        in_specs=[
            rows(d),
            pl.BlockSpec((1, d), const),
            pl.BlockSpec((d, tn), lambda i, j: (0, j)),
            pl.BlockSpec((d, 2 * LANES), const),
            pl.BlockSpec((1, 2 * LANES), const),
            pl.BlockSpec((1, HEAD_DIM), const),
            pl.BlockSpec((1, HEAD_DIM), const),
        ],
        out_specs=[rows(FOX_WIDTH), by_head, rows(FOX_WIDTH), by_head, rows(FOX_WIDTH), rows(LANES), rows(LANES),
                   rows(d)],
        out_shape=[flat(q_dtype, FOX_WIDTH), by_head_shape, flat(BF16, FOX_WIDTH), by_head_shape,
                   flat(BF16, FOX_WIDTH), flat(F32, LANES), flat(F32, LANES), flat(BF16, d)],
        compiler_params=_params("parallel", "arbitrary"),
        name="in_qkv",
    )(x, g.reshape(1, d), w_qkv, w_small, b_small, gq.reshape(1, HEAD_DIM), gk.reshape(1, HEAD_DIM))


def _in_zx_body(u_ref, wa_ref, wb_ref, z_ref, xbc_ref):
    j = pl.program_id(1)
    half = wa_ref.shape[1]
    tn = 2 * half
    nz = z_ref.shape[1] // tn
    for c in range(nz + xbc_ref.shape[1] // tn):
        out_ref, sub = (z_ref, c) if c < nz else (xbc_ref, c - nz)

        @pl.when(j == c)
        def _(out_ref=out_ref, sub=sub):
            out_ref[:, sub * tn:sub * tn + half] = _dot(u_ref[...], wa_ref[...])
            out_ref[:, sub * tn + half:(sub + 1) * tn] = _dot(u_ref[...], wb_ref[...])


def _in_zx(u, w_zx, *, tm=1024):
    t, d = u.shape
    tm = min(tm, t)
    tn = IN_ZX_TN
    rows = lambda w: pl.BlockSpec((tm, w), lambda i, j: (i, 0))
    return pl.pallas_call(
        _in_zx_body,
        grid=(t // tm, w_zx.shape[1] // tn),
        in_specs=[rows(d), pl.BlockSpec((d, tn // 2), lambda i, j: (0, 2 * j)),
                  pl.BlockSpec((d, tn // 2), lambda i, j: (0, 2 * j + 1))],
        out_specs=[rows(SSD_WIDTH), rows(CONV_DIM)],
        out_shape=[jax.ShapeDtypeStruct((t, SSD_WIDTH), F32), jax.ShapeDtypeStruct((t, CONV_DIM), F32)],
        compiler_params=_params("parallel", "arbitrary"),
        name="in_zx",
    )(u, w_zx, w_zx)


def _in_proj(x, g, w_qkv, w_zx, w_small, b_small, gq, gk, *, q_dtype, q_scale):
    q, k, kb, v, vb, logf, dt, u = _in_qkv(x, g, w_qkv, w_small, b_small, gq, gk, q_dtype=q_dtype, q_scale=q_scale)
    z, xbc = _in_zx(u, w_zx)
    return q, k, kb, v, vb, z, xbc, logf, dt


def _pack_in_proj(w_in, fox_b_f, ssd_dt_bias):
    fw = FOX_WIDTH
    f0 = 3 * fw
    z0 = f0 + FOX_HEADS
    x0 = z0 + SSD_WIDTH
    d0 = x0 + CONV_DIM
    w_qkv = w_in[:, :f0].astype(BF16)
    w_zx = w_in[:, z0:d0].astype(BF16)
    zeros = lambda n: jnp.zeros((w_in.shape[0], n), w_in.dtype)
    w_small = jnp.concatenate(
        [w_in[:, f0:z0], zeros(LANES - FOX_HEADS), w_in[:, d0:], zeros(LANES - SSD_HEADS)], axis=1).astype(BF16)
    b_small = jnp.concatenate(
        [fox_b_f, jnp.zeros((LANES - FOX_HEADS,), F32), ssd_dt_bias, jnp.zeros((LANES - SSD_HEADS,), F32)]
    ).reshape(1, 2 * LANES)
    return w_qkv, w_zx, w_small, b_small


ATT_BLOCK = 256
FOX_QUERY_BLOCK = 256
FOX_KEY_BLOCK = 256


def _tri(n, *, strict=False, upper=False):
    r = lax.broadcasted_iota(jnp.int32, (n, n), 0)
    c = lax.broadcasted_iota(jnp.int32, (n, n), 1)
    if upper:
        r, c = c, r
    return (c < r) if strict else (c <= r)


def _cumsum_body(x_ref, col_ref, row_ref, carry_ref):
    @pl.when(pl.program_id(1) == 0)
    def _():
        carry_ref[...] = jnp.zeros_like(carry_ref)

    n = x_ref.shape[1]
    tril = _tri(n).astype(BF16)
    c = _dot_sel_l(tril, x_ref[0]) + carry_ref[...]
    carry_ref[...] = c[n - 1:n, :]
    c = c * LOG2E
    col_ref[0] = c
    row_ref[0] = c.T[:FOX_HEADS, :]


def _cumsum(x, *, tb=ATT_BLOCK):
    b, l, _ = x.shape
    return pl.pallas_call(
        _cumsum_body,
        grid=(b, l // tb),
        in_specs=[pl.BlockSpec((1, tb, LANES), lambda i, j: (i, j, 0))],
        out_specs=[pl.BlockSpec((1, tb, LANES), lambda i, j: (i, j, 0)),
                   pl.BlockSpec((1, FOX_HEADS, tb), lambda i, j: (i, 0, j))],
        out_shape=[jax.ShapeDtypeStruct((b, l, LANES), F32), jax.ShapeDtypeStruct((b, FOX_HEADS, l), F32)],
        scratch_shapes=[pltpu.VMEM((1, LANES), F32)],
        compiler_params=_params("parallel", "arbitrary"),
        name="logf_cumsum",
    )(x)


def _fox_prompt_body(tk, q_ref, k_ref, v_ref, ccol_ref, crow_ref, o_ref, m_ref, l_ref, cq_ref, acc_ref):
    i = pl.program_id(1)
    tq = q_ref.shape[0]
    rep = tk // LANES
    m_ref[...] = jnp.full_like(m_ref, -jnp.inf)
    l_ref[...] = jnp.zeros_like(l_ref)
    acc_ref[...] = jnp.zeros_like(acc_ref)
    for h in range(FOX_HEADS):
        cq_ref[h] = jnp.broadcast_to(ccol_ref[0, :, h:h + 1], (tq, LANES))
    row = i * tq + lax.broadcasted_iota(jnp.int32, (tq, tk), 0)
    col = lax.broadcasted_iota(jnp.int32, (tq, tk), 1)
    wide = lambda x: jnp.concatenate([x] * rep, axis=1)

    def block(j, masked):
        ks = pl.ds(pl.multiple_of(j * tk, tk), tk)
        for h in range(FOX_HEADS):
            hs = slice(h * HEAD_DIM, (h + 1) * HEAD_DIM)
            s = _dot_nt(q_ref[:, hs], k_ref[ks, hs]) + (wide(cq_ref[h]) - crow_ref[0, h, pl.ds(j, 1), :])
            if masked:
                s = jnp.where(col + j * tk <= row, s, -jnp.inf)
            m_old = m_ref[h]
            m_new = jnp.maximum(m_old, jnp.max(s, axis=-1, keepdims=True))
            alpha = jnp.exp2(m_old - m_new)
            p = jnp.exp2(s - wide(m_new))
            m_ref[h] = m_new
            l_ref[h] = alpha * l_ref[h] + jnp.sum(p, axis=-1, keepdims=True)
            acc_ref[h] = alpha * acc_ref[h] + _dot(p.astype(BF16), v_ref[ks, hs])

    first_masked = (i * tq) // tk

    def step(j, carry):
        block(j, False)
        return carry

    lax.fori_loop(0, first_masked, step, 0)
    for extra in range(max(tq // tk, 1)):
        block(first_masked + extra, True)
    for h in range(FOX_HEADS):
        o_ref[:, h * HEAD_DIM:(h + 1) * HEAD_DIM] = (acc_ref[h] / l_ref[h]).astype(o_ref.dtype)


def _fox_prompt(q, k, v, ccol, crow, b, l, *, tq=128):
    tk = crow.shape[-1]
    tq = min(tq, l)
    assert tk % tq == 0 or tq % tk == 0
    nq = l // tq
    w = FOX_WIDTH
    return pl.pallas_call(
        functools.partial(_fox_prompt_body, tk),
        grid=(b, nq),
        in_specs=[
            pl.BlockSpec((tq, w), lambda bi, i: (bi * nq + i, 0)),
            pl.BlockSpec((l, w), lambda bi, i: (bi, 0)),
            pl.BlockSpec((l, w), lambda bi, i: (bi, 0)),
            pl.BlockSpec((1, tq, LANES), lambda bi, i: (bi, i, 0)),
            pl.BlockSpec((1, FOX_HEADS, l // tk, tk), lambda bi, i: (bi, 0, 0, 0)),
        ],
        out_specs=pl.BlockSpec((tq, w), lambda bi, i: (bi * nq + i, 0)),
        out_shape=jax.ShapeDtypeStruct((b * l, w), BF16),
        scratch_shapes=[pltpu.VMEM((FOX_HEADS, tq, LANES), F32)] * 3 + [pltpu.VMEM((FOX_HEADS, tq, HEAD_DIM), F32)],
        compiler_params=_params("parallel", "arbitrary"),
        name="fox_prompt",
    )(q, k, v, ccol, crow)


def _page_cumsum_body(x_ref, w_ref, m_ref):
    n = PAGE_SIZE * FOX_HEADS

    @pl.when(pl.program_id(0) == 0)
    def _():
        r = lax.broadcasted_iota(jnp.int32, (n, n), 0)
        c = lax.broadcasted_iota(jnp.int32, (n, n), 1)
        same_head = (r & (FOX_HEADS - 1)) == (c & (FOX_HEADS - 1))
        earlier = lax.shift_right_logical(r, 3) <= lax.shift_right_logical(c, 3)
        m_ref[...] = jnp.logical_and(same_head, earlier).astype(BF16)

    w_ref[...] = _dot_sel_r(x_ref[...], m_ref[...])


def _page_cumsum(logf_pages, *, tb=256):
    n_pool, n = logf_pages.shape
    return pl.pallas_call(
        _page_cumsum_body,
        grid=(n_pool // tb,),
        in_specs=[pl.BlockSpec((tb, n), lambda i: (i, 0))],
        out_specs=pl.BlockSpec((tb, n), lambda i: (i, 0)),
        out_shape=jax.ShapeDtypeStruct((n_pool, n), F32),
        scratch_shapes=[pltpu.VMEM((n, n), BF16)],
        compiler_params=_params("arbitrary"),
        name="page_cumsum",
    )(logf_pages)


def _seq_cumsum_body(seq_len, x_ref, o_ref):
    n = x_ref.shape[0]
    r = lax.broadcasted_iota(jnp.int32, (n, n), 0)
    c = lax.broadcasted_iota(jnp.int32, (n, n), 1)
    same_seq = (r // seq_len) == (c // seq_len)
    o_ref[...] = _dot_sel_l(jnp.logical_and(same_seq, c <= r).astype(BF16), x_ref[...])


def _seq_cumsum(x, seq_len, *, tb=128):
    t = x.shape[0]
    return pl.pallas_call(
        functools.partial(_seq_cumsum_body, seq_len),
        grid=(t // tb,),
        in_specs=[pl.BlockSpec((tb, LANES), lambda i: (i, 0))],
        out_specs=pl.BlockSpec((tb, LANES), lambda i: (i, 0)),
        out_shape=jax.ShapeDtypeStruct((t, LANES), F32),
        compiler_params=_params("parallel"),
        name="seq_cumsum",
    )(x)


def _fox_sample_body(pps, pt_ref, q_ref, kn_ref, vn_ref, cn_ref, *refs):
    kp_refs, vp_refs, w_refs = refs[:pps], refs[pps:2 * pps], refs[2 * pps:3 * pps]
    o_ref, q_scr, colq_ref, toff_ref, m_ref, l_ref, acc_ref = refs[3 * pps:]
    j = pl.program_id(1)
    nq = q_ref.shape[0]
    rows = nq * FOX_HEADS
    page_keys = PAGE_SIZE * FOX_HEADS
    row_id = lax.broadcasted_iota(jnp.int32, (rows, 1), 0)
    head_of_row = row_id & (FOX_HEADS - 1)
    query_of_row = lax.shift_right_logical(row_id, 3)

    @pl.when(j == 0)
    def _():
        q = q_ref[...].reshape(rows, HEAD_DIM).astype(BF16)
        q_scr[...] = q
        cn = cn_ref[0]
        key = lax.broadcasted_iota(jnp.int32, (1, rows), 1)
        colq = jnp.sum(jnp.where(key == row_id, cn, 0.0), axis=-1, keepdims=True)
        colq_ref[...] = colq
        toff_ref[...] = jnp.zeros_like(toff_ref)
        s = _dot_nt(q, kn_ref[...].reshape(rows, HEAD_DIM).astype(BF16)) + colq - cn
        valid = jnp.logical_and((key & (FOX_HEADS - 1)) == head_of_row,
                                lax.shift_right_logical(key, 3) <= query_of_row)
        s = jnp.where(valid, s, -jnp.inf)
        m = jnp.max(s, axis=-1, keepdims=True)
        p = jnp.exp(s - m)
        m_ref[...] = m
        l_ref[...] = jnp.sum(p, axis=-1, keepdims=True)
        acc_ref[...] = _dot(p.astype(BF16), vn_ref[...].reshape(rows, HEAD_DIM).astype(BF16))

    lane = lax.broadcasted_iota(jnp.int32, (1, LANES), 1)
    own_head = (lax.broadcasted_iota(jnp.int32, (1, page_keys), 1) & (FOX_HEADS - 1)) == head_of_row
    q = q_scr[...]
    colq = colq_ref[...]
    toff = toff_ref[...]
    tiles = []
    for kp_ref, w_ref in zip(kp_refs, w_refs):
        w = w_ref[0]
        last = jnp.where(lane == LANES - FOX_HEADS + head_of_row, w[:, page_keys - LANES:], 0.0)
        toff = toff + jnp.sum(last, axis=-1, keepdims=True)
        s = _dot_nt(q, kp_ref[0].reshape(page_keys, HEAD_DIM).astype(BF16))
        tiles.append(jnp.where(own_head, s + (colq + toff) - w, -jnp.inf))
    toff_ref[...] = toff
    m_old = m_ref[...]
    m = m_old
    for s in tiles:
        m = jnp.maximum(m, jnp.max(s, axis=-1, keepdims=True))
    alpha = jnp.exp(m_old - m)
    l = alpha * l_ref[...]
    acc = alpha * acc_ref[...]
    for s, vp_ref in zip(tiles, vp_refs):
        p = jnp.exp(s - m)
        l = l + jnp.sum(p, axis=-1, keepdims=True)
        acc = acc + _dot(p.astype(BF16), vp_ref[0].reshape(page_keys, HEAD_DIM).astype(BF16))
    m_ref[...] = m
    l_ref[...] = l
    acc_ref[...] = acc

    @pl.when(j == pl.num_programs(1) - 1)
    def _():
        o_ref[...] = (acc_ref[...] / l_ref[...]).reshape(nq, FOX_HEADS, HEAD_DIM)


def _fox_sample(page_table, q, k_new, v_new, c_new, k_pages, v_pages, w_pages, n_seq, nq, *, pages_per_step=16):
    n_pages = page_table.shape[1]
    rows = nq * FOX_HEADS
    pps = math.gcd(pages_per_step, n_pages)

    def page(i, ndim):
        return lambda b, j, pt: (pt[b * n_pages + (n_pages - 1 - j * pps - i)],) + (0,) * (ndim - 1)

    seq = pl.BlockSpec((nq, FOX_HEADS, HEAD_DIM), lambda b, j, pt: (b, 0, 0))
    kv_specs = [pl.BlockSpec((1, PAGE_SIZE, FOX_HEADS, HEAD_DIM), page(i, 4)) for i in range(pps)]
    w_specs = [pl.BlockSpec((1, 1, PAGE_SIZE * FOX_HEADS), page(i, 3)) for i in range(pps)]
    grid_spec = pltpu.PrefetchScalarGridSpec(
        num_scalar_prefetch=1,
        grid=(n_seq, n_pages // pps),
        in_specs=[seq, seq, seq, pl.BlockSpec((1, 1, rows), lambda b, j, pt: (b, 0, 0))]
        + kv_specs + kv_specs + w_specs,
        out_specs=seq,
        scratch_shapes=[
            pltpu.VMEM((rows, HEAD_DIM), BF16), pltpu.VMEM((rows, 1), F32), pltpu.VMEM((rows, 1), F32),
            pltpu.VMEM((rows, 1), F32), pltpu.VMEM((rows, 1), F32), pltpu.VMEM((rows, HEAD_DIM), F32),
        ],
    )
    return pl.pallas_call(
        functools.partial(_fox_sample_body, pps),
        grid_spec=grid_spec,
        out_shape=jax.ShapeDtypeStruct((n_seq * nq, FOX_HEADS, HEAD_DIM), F32),
        compiler_params=_params("parallel", "arbitrary"),
        name="fox_sample",
    )(page_table.reshape(-1), q, k_new, v_new, c_new, *([k_pages] * pps), *([v_pages] * pps), *([w_pages] * pps))


SSD_PAIRS = SSD_HEADS // 2
PAIRS_PER_GROUP = SSD_PAIRS // SSD_GROUPS


def _expander(width):
    n = SSD_HEADS * width
    h = lax.broadcasted_iota(jnp.int32, (LANES, n), 0)
    c = lax.broadcasted_iota(jnp.int32, (LANES, n), 1)
    return (lax.shift_right_logical(c, int(math.log2(width))) == h).astype(BF16)


def _ssd_local(xs, bm, cm, dt, alog, mask, tot_sel):
    n = xs.shape[0]
    lane = lax.broadcasted_iota(jnp.int32, (1, LANES), 1)
    dta = dt * jnp.where(lane < SSD_HEADS, -jnp.exp(alog), 0.0)
    e64 = _expander(SSD_HEAD_DIM)
    a_cum = _dot_sel_l(mask.astype(BF16), dta)
    a_cum_t = a_cum.T
    ac_exp = _dot_sel_r(a_cum, e64)
    if tot_sel is None:
        atot_exp = ac_exp[n - 1:n, :]
    else:
        atot_exp = _dot_sel_r(_dot_sel_l(tot_sel, dta), e64)
    ac_b = _dot_sel_r(a_cum, _expander(LANES))
    xdt = xs * _dot_sel_r(dt, e64)
    half = lax.broadcasted_iota(jnp.int32, (n, LANES), 1) < SSD_HEAD_DIM
    out = {
        "xdtw": xdt * jnp.exp(atot_exp - ac_exp),
        "eac": jnp.exp(ac_exp),
        "atot_exp": atot_exp,
        "ac_b": ac_b,
        "bg": [], "cg": [], "y_diag": [],
    }
    for g in range(SSD_GROUPS):
        gs = slice(g * SSD_STATE, (g + 1) * SSD_STATE)
        bg = bm[:, gs].astype(BF16)
        cg = cm[:, gs].astype(BF16)
        out["bg"].append(bg)
        out["cg"].append(cg)
        cb = _dot_nt(cg, bg)
        for k in range(g * PAIRS_PER_GROUP, (g + 1) * PAIRS_PER_GROUP):
            ps = slice(k * LANES, (k + 1) * LANES)
            ms = []
            for h in (2 * k, 2 * k + 1):
                seg = ac_b[:, h * LANES:(h + 1) * LANES] - a_cum_t[h:h + 1, :]
                ms.append(cb * jnp.exp(jnp.where(mask, seg, -jnp.inf)))
            m_cat = jnp.concatenate(ms, axis=1).astype(BF16)
            xp = xdt[:, ps]
            x_bd = jnp.concatenate([jnp.where(half, xp, 0.0), jnp.where(half, 0.0, xp)], axis=0).astype(BF16)
            out["y_diag"].append(_dot(m_cat, x_bd))
    return out


def _conv_ssd_sample_body(seq_len, xbc_ref, dt_ref, buf_ref, h0_ref, w_ref, b_ref, alog_ref, dexp_ref,
                          y_ref, hout_ref, cout_ref, xp_ref):
    n = xbc_ref.shape[0]
    n_seq = n // seq_len
    taps = CONV_WIDTH - 1
    base = 8 - taps

    x = xbc_ref[...]
    xp_ref[:, base:8, :] = buf_ref[...]
    xp_ref[:, 8:8 + seq_len, :] = x.reshape(n_seq, seq_len, CONV_DIM)
    acc = b_ref[...] + x * w_ref[taps:taps + 1, :]
    for j in range(taps):
        acc = acc + xp_ref[:, base + j:base + j + seq_len, :].reshape(n, CONV_DIM) * w_ref[j:j + 1, :]
    cout_ref[...] = xp_ref[:, 8 + seq_len - taps:8 + seq_len, :]
    conv = _silu(acc)
    xs = conv[:, :SSD_WIDTH]
    bm = conv[:, SSD_WIDTH:SSD_WIDTH + SSD_GROUPS * SSD_STATE]
    cm = conv[:, SSD_WIDTH + SSD_GROUPS * SSD_STATE:]

    r = lax.broadcasted_iota(jnp.int32, (n, n), 0)
    c = lax.broadcasted_iota(jnp.int32, (n, n), 1)
    same_seq = (r // seq_len) == (c // seq_len)
    loc = _ssd_local(xs, bm, cm, dt_ref[...], alog_ref[...], jnp.logical_and(same_seq, c <= r),
                     same_seq.astype(BF16))

    gw = PAIRS_PER_GROUP * LANES
    seq_of_col = lax.broadcasted_iota(jnp.int32, (1, n), 1) // seq_len
    decay_t = jnp.exp(loc["atot_exp"]).T
    for g in range(SSD_GROUPS):
        gr = slice(g * gw, (g + 1) * gw)
        h_prev = h0_ref[:, gr, :]
        z = _dot_nt(h_prev.reshape(n_seq * gw, SSD_STATE).astype(BF16), loc["cg"][g])
        y_off_t = jnp.zeros((gw, n), F32)
        for s in range(n_seq):
            y_off_t = y_off_t + jnp.where(seq_of_col == s, z[s * gw:(s + 1) * gw, :], 0.0)
        y_off = y_off_t.T * loc["eac"][:, gr]
        y = jnp.concatenate(loc["y_diag"][g * PAIRS_PER_GROUP:(g + 1) * PAIRS_PER_GROUP], axis=1)
        y_ref[:, gr] = y + y_off + xs[:, gr] * dexp_ref[:, gr]
        xw_t = loc["xdtw"][:, gr].T
        lhs = jnp.concatenate([jnp.where(seq_of_col == s, xw_t, 0.0) for s in range(n_seq)], axis=0)
        s_new = _dot(lhs.astype(BF16), loc["bg"][g])
        for s in range(n_seq):
            col = decay_t[gr, s * seq_len:s * seq_len + 1]
            hout_ref[s, gr, :] = h_prev[s] * col + s_new[s * gw:(s + 1) * gw, :]


def _conv_ssd_sample(xbc, dt, conv_buf, h0, conv_w, conv_b, alog, d_exp, n_seq, seq_len):
    assert seq_len == 8 and CONV_WIDTH - 1 <= seq_len
    tile = LANES
    ts = tile // seq_len
    const = lambda i: (0, 0)
    per_s = lambda i: (i, 0, 0)
    return pl.pallas_call(
        functools.partial(_conv_ssd_sample_body, seq_len),
        grid=(n_seq // ts,),
        in_specs=[
            pl.BlockSpec((tile, CONV_DIM), lambda i: (i, 0)),
            pl.BlockSpec((tile, LANES), lambda i: (i, 0)),
            pl.BlockSpec((ts, CONV_WIDTH - 1, CONV_DIM), per_s),
            pl.BlockSpec((ts, SSD_WIDTH, SSD_STATE), per_s),
            pl.BlockSpec((CONV_WIDTH, CONV_DIM), const),
            pl.BlockSpec((1, CONV_DIM), const),
            pl.BlockSpec((1, LANES), const),
            pl.BlockSpec((1, SSD_WIDTH), const),
        ],
        out_specs=[
            pl.BlockSpec((tile, SSD_WIDTH), lambda i: (i, 0)),
            pl.BlockSpec((ts, SSD_WIDTH, SSD_STATE), per_s),
            pl.BlockSpec((ts, CONV_WIDTH - 1, CONV_DIM), per_s),
        ],
        out_shape=[
            jax.ShapeDtypeStruct((n_seq * seq_len, SSD_WIDTH), F32),
            jax.ShapeDtypeStruct((n_seq, SSD_WIDTH, SSD_STATE), F32),
            jax.ShapeDtypeStruct((n_seq, CONV_WIDTH - 1, CONV_DIM), F32),
        ],
        scratch_shapes=[pltpu.VMEM((ts, 8 + seq_len, CONV_DIM), F32)],
        compiler_params=_params("parallel"),
        name="conv_ssd_sample",
    )(xbc, dt, conv_buf, h0, conv_w, conv_b, alog, d_exp)


def _conv_ssd_prompt_body(xbc_ref, dt_ref, buf_ref, h0_ref, w_ref, b_ref, alog_ref, dexp_ref,
                          y_ref, hout_ref, cout_ref, state_ref, xp_ref):
    c = pl.program_id(1)
    nc = pl.num_programs(1)
    tl = xbc_ref.shape[0]
    taps = CONV_WIDTH - 1
    base = 8 - taps

    @pl.when(c == 0)
    def _():
        state_ref[...] = h0_ref[0]
        xp_ref[base:8, :] = buf_ref[0]

    x = xbc_ref[...]
    xp_ref[8:8 + tl, :] = x
    acc = b_ref[...] + x * w_ref[taps:taps + 1, :]
    for j in range(taps):
        acc = acc + xp_ref[base + j:base + j + tl, :] * w_ref[j:j + 1, :]
    xp_ref[base:8, :] = x[tl - taps:, :]
    conv = _silu(acc)
    xs = conv[:, :SSD_WIDTH]
    bm = conv[:, SSD_WIDTH:SSD_WIDTH + SSD_GROUPS * SSD_STATE]
    cm = conv[:, SSD_WIDTH + SSD_GROUPS * SSD_STATE:]

    causal = _tri(tl)
    loc = _ssd_local(xs, bm, cm, dt_ref[...], alog_ref[...], causal, None)
    top = lax.broadcasted_iota(jnp.int32, (tl, LANES), 0) < SSD_HEAD_DIM
    for k in range(SSD_PAIRS):
        g = k // PAIRS_PER_GROUP
        ps = slice(k * LANES, (k + 1) * LANES)
        s_prev = state_ref[ps, :]
        y_off = _dot_nt(loc["cg"][g], s_prev.astype(BF16)) * loc["eac"][:, ps]
        y_ref[:, ps] = loc["y_diag"][k] + y_off + xs[:, ps] * dexp_ref[:, ps]
        cd = [jnp.exp(loc["ac_b"][tl - 1:tl, h * LANES:(h + 1) * LANES]) for h in (2 * k, 2 * k + 1)]
        state_ref[ps, :] = s_prev * jnp.where(top, cd[0], cd[1]) + _dot_tn(loc["xdtw"][:, ps].astype(BF16), loc["bg"][g])

    @pl.when(c == nc - 1)
    def _():
        hout_ref[0] = state_ref[...]
        cout_ref[0] = xp_ref[base:8, :]


def _conv_ssd_prompt(xbc, dt, conv_buf, h0, conv_w, conv_b, alog, d_exp, b, l):
    tl = SSD_CHUNK
    nc = l // tl
    const = lambda bi, c: (0, 0)
    per_b = lambda bi, c: (bi, 0, 0)
    return pl.pallas_call(
        _conv_ssd_prompt_body,
        grid=(b, nc),
        in_specs=[
            pl.BlockSpec((tl, CONV_DIM), lambda bi, c: (bi * nc + c, 0)),
            pl.BlockSpec((tl, LANES), lambda bi, c: (bi * nc + c, 0)),
            pl.BlockSpec((1, CONV_WIDTH - 1, CONV_DIM), per_b),
            pl.BlockSpec((1, SSD_WIDTH, SSD_STATE), per_b),
            pl.BlockSpec((CONV_WIDTH, CONV_DIM), const),
            pl.BlockSpec((1, CONV_DIM), const),
            pl.BlockSpec((1, LANES), const),
            pl.BlockSpec((1, SSD_WIDTH), const),
        ],
        out_specs=[
            pl.BlockSpec((tl, SSD_WIDTH), lambda bi, c: (bi * nc + c, 0)),
            pl.BlockSpec((1, SSD_WIDTH, SSD_STATE), per_b),
            pl.BlockSpec((1, CONV_WIDTH - 1, CONV_DIM), per_b),
        ],
        out_shape=[
            jax.ShapeDtypeStruct((b * l, SSD_WIDTH), F32),
            jax.ShapeDtypeStruct((b, SSD_WIDTH, SSD_STATE), F32),
            jax.ShapeDtypeStruct((b, CONV_WIDTH - 1, CONV_DIM), F32),
        ],
        scratch_shapes=[pltpu.VMEM((SSD_WIDTH, SSD_STATE), F32), pltpu.VMEM((8 + tl, CONV_DIM), F32)],
        compiler_params=_params("parallel", "arbitrary"),
        name="conv_ssd_prompt",
    )(xbc, dt, conv_buf, h0, conv_w, conv_b, alog, d_exp)


def _out_proj_body(h_ref, fox_ref, y_ref, z_ref, gs_ref, wf_ref, ws_ref, gx_ref, wq_ref, gq_ref, o_ref, q_ref):
    yn = _rms(y_ref[...] * _silu(z_ref[...]), gs_ref[...]).astype(BF16)
    h = h_ref[...] + _dot(fox_ref[...].astype(BF16), wf_ref[...]) + _dot(yn, ws_ref[...])
    o_ref[...] = h
    q = _dot(_rms(h, gx_ref[...]).astype(BF16), wq_ref[...])
    q_ref[...] = _head_norm(q, gq_ref[...]) * (HEAD_DIM ** -0.5)


def _out_proj(h, fox, y, z, g_ssd, w_fox, w_ssd, g_x, wq, gq, *, tm=256):
    t, d = h.shape
    tm = min(tm, t)
    row = lambda w: pl.BlockSpec((tm, w), lambda i: (i, 0))
    full = lambda a: pl.BlockSpec(a.shape, lambda i: (0, 0))
    consts = [g_ssd.reshape(1, SSD_WIDTH), w_fox, w_ssd, g_x.reshape(1, d), wq, gq.reshape(1, HEAD_DIM)]
    return pl.pallas_call(
        _out_proj_body,
        grid=(t // tm,),
        in_specs=[row(d), row(FOX_WIDTH), row(SSD_WIDTH), row(SSD_WIDTH)] + [full(a) for a in consts],
        out_specs=[row(d), row(XATTN_WIDTH)],
        out_shape=[jax.ShapeDtypeStruct((t, d), F32), jax.ShapeDtypeStruct((t, XATTN_WIDTH), F32)],
        compiler_params=_params("parallel"),
        name="out_proj",
    )(h, fox, y, z, *consts)


def _mem_kv_body(m_ref, g_ref, w_ref, gk_ref, k_ref, v_ref):
    kv = _dot(_rms(m_ref[...], g_ref[...]).astype(BF16), w_ref[...])
    k_ref[...] = _head_norm(kv[:, :XATTN_WIDTH], gk_ref[...])
    v_ref[...] = kv[:, XATTN_WIDTH:]


def _mem_kv(mem, g, w_kv, gk, *, tm=256):
    t, d = mem.shape
    row = lambda w: pl.BlockSpec((tm, w), lambda i: (i, 0))
    full = lambda a: pl.BlockSpec(a.shape, lambda i: (0, 0))
    consts = [g.reshape(1, d), w_kv, gk.reshape(1, HEAD_DIM)]
    return pl.pallas_call(
        _mem_kv_body,
        grid=(t // tm,),
        in_specs=[row(d)] + [full(a) for a in consts],
        out_specs=[row(XATTN_WIDTH), row(XATTN_WIDTH)],
        out_shape=[jax.ShapeDtypeStruct((t, XATTN_WIDTH), F32)] * 2,
        compiler_params=_params("parallel"),
        name="mem_kv",
    )(mem, *consts)


def _xattn_body(q_ref, k_ref, v_ref, o_ref):
    for h in range(XATTN_HEADS):
        hs = slice(h * HEAD_DIM, (h + 1) * HEAD_DIM)
        s = _dot_nt(q_ref[:, hs].astype(BF16), k_ref[0, :, hs].astype(BF16))
        p = jnp.exp(s - jnp.max(s, axis=-1, keepdims=True))
        o = _dot(p.astype(BF16), v_ref[0, :, hs].astype(BF16))
        o_ref[:, hs] = o / jnp.sum(p, axis=-1, keepdims=True)


def _xattn(q, mem_k, mem_v, b, l, *, tq=512):
    tq = min(tq, l)
    nq = l // tq
    n_mem = mem_k.shape[1]
    qspec = pl.BlockSpec((tq, XATTN_WIDTH), lambda bi, i: (bi * nq + i, 0))
    mspec = pl.BlockSpec((1, n_mem, XATTN_WIDTH), lambda bi, i: (bi, 0, 0))
    return pl.pallas_call(
        _xattn_body,
        grid=(b, nq),
        in_specs=[qspec, mspec, mspec],
        out_specs=qspec,
        out_shape=jax.ShapeDtypeStruct((b * l, XATTN_WIDTH), F32),
        compiler_params=_params("parallel", "arbitrary"),
        name="xattn",
    )(q, mem_k, mem_v)


def _xattn_rows_body(n_seq, q_ref, k_ref, v_ref, o_ref):
    tiles = q_ref.shape[0] // n_seq
    rows = tiles * 8
    keys = k_ref.shape[1] * k_ref.shape[2]
    row_head = lax.broadcasted_iota(jnp.int32, (rows, 1), 0) & (XATTN_HEADS - 1)
    key_head = lax.broadcasted_iota(jnp.int32, (1, keys), 1) & (XATTN_HEADS - 1)
    own_head = row_head == key_head
    for s in range(n_seq):
        qs = slice(s * tiles, (s + 1) * tiles)
        q = q_ref[qs].reshape(rows, HEAD_DIM).astype(BF16)
        sc = jnp.where(own_head, _dot_nt(q, k_ref[s].reshape(keys, HEAD_DIM).astype(BF16)), -jnp.inf)
        p = jnp.exp(sc - jnp.max(sc, axis=-1, keepdims=True))
        o = _dot(p.astype(BF16), v_ref[s].reshape(keys, HEAD_DIM).astype(BF16))
        o_ref[qs] = (o / jnp.sum(p, axis=-1, keepdims=True)).reshape(tiles, 8, HEAD_DIM)


def _xattn_rows(q, mem_k, mem_v, n_seq, *, ts=8):
    tiles = q.shape[0] // n_seq
    ts = math.gcd(ts, n_seq)
    qspec = pl.BlockSpec((ts * tiles, 8, HEAD_DIM), lambda i: (i, 0, 0))
    mspec = pl.BlockSpec((ts,) + mem_k.shape[1:], lambda i: (i, 0, 0, 0))
    return pl.pallas_call(
        functools.partial(_xattn_rows_body, ts),
        grid=(n_seq // ts,),
        in_specs=[qspec, mspec, mspec],
        out_specs=qspec,
        out_shape=jax.ShapeDtypeStruct(q.shape, F32),
        compiler_params=_params("parallel"),
        name="xattn_rows",
    )(q, mem_k, mem_v)


def kernel(x_prompt, x_sample, cache_fox_k, cache_fox_v, cache_fox_logf, cache_mem_k, cache_mem_v, state_ssm, state_conv, page_table, mem_prompt, ffn1_norm, ffn1_w_gate, ffn1_w_up, ffn1_w_down, mix_norm, w_in, fox_b_f, fox_q_norm, fox_k_norm, conv_w, conv_b, ssd_dt_bias, ssd_A_log, ssd_D, ssd_out_norm, w_out, xattn_norm, mem_norm, xattn_w_q, xattn_w_kv, xattn_q_norm, xattn_k_norm, xattn_w_o, ffn2_norm, ffn2_w_gate, ffn2_w_up, ffn2_w_down):
    assert x_prompt.shape[2] == D_MODEL and ffn1_norm.shape[0] == 1
    d = D_MODEL
    bp, lp = x_prompt.shape[:2]
    bs, ls = x_sample.shape[:2]
    n_mem = mem_prompt.shape[1]
    n_pool = cache_fox_k.shape[1]

    bf = lambda w: w[0].astype(BF16)
    ffn1 = (ffn1_norm[0], bf(ffn1_w_gate), bf(ffn1_w_up), bf(ffn1_w_down))
    ffn2 = (ffn2_norm[0], bf(ffn2_w_gate), bf(ffn2_w_up), bf(ffn2_w_down))
    in_w = _pack_in_proj(w_in[0], fox_b_f[0], ssd_dt_bias[0])
    w_fox, w_ssd = bf(w_out)[:FOX_WIDTH], bf(w_out)[FOX_WIDTH:]
    wq, wkv, wo = bf(xattn_w_q), bf(xattn_w_kv), bf(xattn_w_o)
    alog = jnp.pad(ssd_A_log[0], (0, LANES - SSD_HEADS)).reshape(1, LANES)
    d_exp = jnp.repeat(ssd_D[0], SSD_HEAD_DIM).reshape(1, SSD_WIDTH)
    cw, cb = conv_w[0], conv_b[0].reshape(1, CONV_DIM)

    def front(x, q_dtype, q_scale):
        h1 = _ffn(x, *ffn1)
        return h1, _in_proj(h1, mix_norm[0], *in_w, fox_q_norm[0], fox_k_norm[0], q_dtype=q_dtype, q_scale=q_scale)

    def back(h1, fox, y, z, attend):
        h2, xq = _out_proj(h1, fox, y, z, ssd_out_norm[0], w_fox, w_ssd, xattn_norm[0], wq, xattn_q_norm[0])
        return _ffn(h2, *ffn2, pre=(attend(xq), wo))

    h1, (q, k_p, kb, v_p, vb, z, xbc, logf_p, dt) = front(x_prompt.reshape(bp * lp, d), BF16, HEAD_DIM ** -0.5 * LOG2E)
    ccol, crow = _cumsum(logf_p.reshape(bp, lp, LANES))
    tk = min(FOX_KEY_BLOCK, lp)
    fox = _fox_prompt(q, kb, vb, ccol, crow.reshape(bp, FOX_HEADS, lp // tk, tk), bp, lp, tq=FOX_QUERY_BLOCK)
    y, ssm_p, conv_p = _conv_ssd_prompt(
        xbc, dt, jnp.zeros((bp, CONV_WIDTH - 1, CONV_DIM), F32), jnp.zeros((bp, SSD_WIDTH, SSD_STATE), F32),
        cw, cb, alog, d_exp, bp, lp)
    mk, mv = _mem_kv(mem_prompt.reshape(bp * n_mem, d), mem_norm[0], wkv, xattn_k_norm[0])
    y_prompt = back(h1, fox, y, z, lambda xq: _xattn(
        xq, mk.reshape(bp, n_mem, XATTN_WIDTH), mv.reshape(bp, n_mem, XATTN_WIDTH), bp, lp))

    h1, (q, k_s, _, v_s, _, z, xbc, logf_s, dt) = front(x_sample.reshape(bs * ls, d), F32, HEAD_DIM ** -0.5)
    w_pages = _page_cumsum(cache_fox_logf[0].reshape(n_pool, PAGE_SIZE * FOX_HEADS))
    c_new = _seq_cumsum(logf_s, ls)[:, :FOX_HEADS].reshape(bs, 1, ls * FOX_HEADS)
    fox = _fox_sample(
        page_table, q.reshape(bs * ls, FOX_HEADS, HEAD_DIM), k_s, v_s, c_new, cache_fox_k[0], cache_fox_v[0],
        w_pages.reshape(n_pool, 1, PAGE_SIZE * FOX_HEADS), bs, ls).reshape(bs * ls, FOX_WIDTH)
    y, ssm_s, conv_s = _conv_ssd_sample(
        xbc, dt, state_conv[0], state_ssm[0].reshape(bs, SSD_WIDTH, SSD_STATE), cw, cb, alog, d_exp, bs, ls)
    mem_tiles = lambda m: m[0].reshape(bs, n_mem * XATTN_HEADS // 8, 8, HEAD_DIM)
    y_sample = back(h1, fox, y, z, lambda xq: _xattn_rows(
        xq.reshape(bs * ls * XATTN_HEADS // 8, 8, HEAD_DIM), mem_tiles(cache_mem_k), mem_tiles(cache_mem_v), bs,
    ).reshape(bs * ls, XATTN_WIDTH))

    fox_shape = lambda b, l: (1, b, l, FOX_HEADS, HEAD_DIM)
    ssm_shape = lambda b: (1, b, SSD_HEADS, SSD_HEAD_DIM, SSD_STATE)
    mem_shape = (1, bp, n_mem, XATTN_HEADS, HEAD_DIM)
    return (
        y_prompt.reshape(bp, lp, d), y_sample.reshape(bs, ls, d),
        k_p.reshape(fox_shape(bp, lp)), v_p.reshape(fox_shape(bp, lp)),
        logf_p[:, :FOX_HEADS].reshape(1, bp, lp, FOX_HEADS),
        ssm_p.reshape(ssm_shape(bp)), conv_p[None], mk.reshape(mem_shape), mv.reshape(mem_shape),
        k_s.reshape(fox_shape(bs, ls)), v_s.reshape(fox_shape(bs, ls)),
        logf_s[:, :FOX_HEADS].reshape(1, bs, ls, FOX_HEADS),
        ssm_s.reshape(ssm_shape(bs)), conv_s[None],
    )
```

```python
import functools
import math

import jax
import jax.numpy as jnp
from jax import lax
from jax.experimental import pallas as pl
from jax.experimental.pallas import tpu as pltpu

F32 = jnp.float32
BF16 = jnp.bfloat16

EPS = 1e-6
LOG2E = math.log2(math.e)
FFN_RESIDUAL = 0.5
D_MODEL = 2048
D_FF = 5632
PAGE_SIZE = 128
FOX_HEADS = 8
HEAD_DIM = 128
FOX_WIDTH = FOX_HEADS * HEAD_DIM
SSD_HEADS = 16
SSD_HEAD_DIM = 64
SSD_WIDTH = SSD_HEADS * SSD_HEAD_DIM
SSD_GROUPS = 2
SSD_STATE = 128
SSD_CHUNK = 128
CONV_WIDTH = 4
CONV_DIM = SSD_WIDTH + 2 * SSD_GROUPS * SSD_STATE
XATTN_HEADS = 4
XATTN_WIDTH = XATTN_HEADS * HEAD_DIM
LANES = 128
VMEM_LIMIT_BYTES = 56 * 1024 * 1024

def _params(*semantics):
    return pltpu.CompilerParams(dimension_semantics=semantics, vmem_limit_bytes=VMEM_LIMIT_BYTES)


def _rms(x, g):
    return x * lax.rsqrt(jnp.mean(x * x, axis=-1, keepdims=True) + EPS) * g


def _dot(a, b):
    return jnp.dot(a, b, preferred_element_type=F32)


def _dot_nt(a, b):
    return lax.dot_general(a, b, (((1,), (1,)), ((), ())), preferred_element_type=F32)


def _dot_tn(a, b):
    return lax.dot_general(a, b, (((0,), (0,)), ((), ())), preferred_element_type=F32)


def _split3(x):
    x1 = x.astype(BF16)
    r = x - x1.astype(F32)
    x2 = r.astype(BF16)
    x3 = (r - x2.astype(F32)).astype(BF16)
    return x1, x2, x3


def _dot_sel_l(sel, x):
    x1, x2, x3 = _split3(x)
    return _dot(sel, x1) + _dot(sel, x2) + _dot(sel, x3)


def _dot_sel_r(x, sel):
    x1, x2, x3 = _split3(x)
    return _dot(x1, sel) + _dot(x2, sel) + _dot(x3, sel)


def _silu(x):
    return x * jax.nn.sigmoid(x)


def _ffn_body(pre_proj, split, *refs):
    if pre_proj:
        x_ref, a_ref, wo_ref, g_ref, *w_refs, o_ref, xn_ref = refs
    else:
        x_ref, g_ref, *w_refs, o_ref, xn_ref = refs

    @pl.when(pl.program_id(1) == 0)
    def _():
        x = x_ref[...]
        if pre_proj:
            x = x + _dot(a_ref[...].astype(BF16), wo_ref[...])
        xn_ref[...] = _rms(x, g_ref[...]).astype(BF16)
        o_ref[...] = x

    xn = xn_ref[...]
    acc = None
    for wg_ref, wu_ref, wd_ref in zip(w_refs[:split], w_refs[split:2 * split], w_refs[2 * split:]):
        gate = _dot(xn, wg_ref[...])
        up = _dot(xn, wu_ref[...])
        h = (_silu(gate) * up * FFN_RESIDUAL).astype(BF16)
        part = _dot(h, wd_ref[...])
        acc = part if acc is None else acc + part
    o_ref[...] += acc


def _ffn(x, g, wg, wu, wd, pre=None, *, tm=1024, tf=512, split=2):
    t, d = x.shape
    f = wg.shape[1]
    tm = min(tm, t)
    ts = tf // split
    grid = (t // tm, f // tf)
    row = lambda i, j: (i, 0)
    in_specs = [pl.BlockSpec((tm, d), row)]
    args = [x]
    if pre is not None:
        a, wo = pre
        in_specs += [pl.BlockSpec((tm, a.shape[1]), row), pl.BlockSpec(wo.shape, lambda i, j: (0, 0))]
        args += [a, wo]
    cols = [pl.BlockSpec((d, ts), lambda i, j, s=s: (0, j * split + s)) for s in range(split)]
    rows = [pl.BlockSpec((ts, d), lambda i, j, s=s: (j * split + s, 0)) for s in range(split)]
    in_specs += [pl.BlockSpec((1, d), lambda i, j: (0, 0))] + cols + cols + rows
    args += [g.reshape(1, d)] + [wg] * split + [wu] * split + [wd] * split
    return pl.pallas_call(
        functools.partial(_ffn_body, pre is not None, split),
        grid=grid,
        in_specs=in_specs,
        out_specs=pl.BlockSpec((tm, d), row),
        out_shape=jax.ShapeDtypeStruct((t, d), F32),
        scratch_shapes=[pltpu.VMEM((tm, d), BF16)],
        compiler_params=_params("parallel", "arbitrary"),
        name="ffn_pre" if pre is not None else "ffn",
    )(*args)


IN_QKV_TN = 1024
IN_ZX_TN = 512


def _head_norm(y, g):
    outs = []
    for c in range(y.shape[1] // HEAD_DIM):
        yc = y[:, c * HEAD_DIM:(c + 1) * HEAD_DIM]
        outs.append(yc * lax.rsqrt(jnp.mean(yc * yc, axis=-1, keepdims=True) + EPS) * g)
    return jnp.concatenate(outs, axis=1)


def _in_qkv_body(q_scale, x_ref, g_ref, w_ref, ws_ref, bs_ref, gq_ref, gk_ref,
                 q_ref, k_ref, kb_ref, v_ref, vb_ref, logf_ref, dt_ref, u_ref):
    j = pl.program_id(1)

    @pl.when(j == 0)
    def _():
        u = _rms(x_ref[...], g_ref[...]).astype(BF16)
        u_ref[...] = u
        s = _dot(u, ws_ref[...]) + bs_ref[...]
        t = jnp.log1p(jnp.exp(-jnp.abs(s)))
        logf_ref[...] = (jnp.minimum(s, 0.0) - t)[:, :LANES]
        dt_ref[...] = (jnp.maximum(s, 0.0) + t)[:, LANES:]
        q_ref[...] = (_head_norm(_dot(u, w_ref[...]), gq_ref[...]) * q_scale).astype(q_ref.dtype)

    def by_head(y, out_ref, bf_ref):
        bf_ref[...] = y.astype(BF16)
        for h in range(FOX_HEADS):
            out_ref[:, h, :] = y[:, h * HEAD_DIM:(h + 1) * HEAD_DIM]

    @pl.when(j == 1)
    def _():
        by_head(_head_norm(_dot(u_ref[...], w_ref[...]), gk_ref[...]), k_ref, kb_ref)

    @pl.when(j == 2)
    def _():
        by_head(_dot(u_ref[...], w_ref[...]), v_ref, vb_ref)


def _in_qkv(x, g, w_qkv, w_small, b_small, gq, gk, *, q_dtype, q_scale, tm=512):
    t, d = x.shape
    tm = min(tm, t)
    tn = IN_QKV_TN
    assert tn == FOX_WIDTH and w_qkv.shape[1] >= 3 * tn
    const = lambda i, j: (0, 0)
    rows = lambda w: pl.BlockSpec((tm, w), lambda i, j: (i, 0))
    by_head = pl.BlockSpec((tm, FOX_HEADS, HEAD_DIM), lambda i, j: (i, 0, 0))
    by_head_shape = jax.ShapeDtypeStruct((t, FOX_HEADS, HEAD_DIM), F32)
    flat = lambda dt, w: jax.ShapeDtypeStruct((t, w), dt)
    return pl.pallas_call(
        functools.partial(_in_qkv_body, q_scale),
        grid=(t // tm, 3),
        in_specs=[
            rows(d),
            pl.BlockSpec((1, d), const),
            pl.BlockSpec((d, tn), lambda i, j: (0, j)),
            pl.BlockSpec((d, 2 * LANES), const),
            pl.BlockSpec((1, 2 * LANES), const),
            pl.BlockSpec((1, HEAD_DIM), const),
            pl.BlockSpec((1, HEAD_DIM), const),
        ],
        out_specs=[rows(FOX_WIDTH), by_head, rows(FOX_WIDTH), by_head, rows(FOX_WIDTH), rows(LANES), rows(LANES),
                   rows(d)],
        out_shape=[flat(q_dtype, FOX_WIDTH), by_head_shape, flat(BF16, FOX_WIDTH), by_head_shape,
                   flat(BF16, FOX_WIDTH), flat(F32, LANES), flat(F32, LANES), flat(BF16, d)],
        compiler_params=_params("parallel", "arbitrary"),
        name="in_qkv",
    )(x, g.reshape(1, d), w_qkv, w_small, b_small, gq.reshape(1, HEAD_DIM), gk.reshape(1, HEAD_DIM))


def _in_zx_body(u_ref, wa_ref, wb_ref, z_ref, xbc_ref):
    j = pl.program_id(1)
    half = wa_ref.shape[1]
    tn = 2 * half
    nz = z_ref.shape[1] // tn
    for c in range(nz + xbc_ref.shape[1] // tn):
        out_ref, sub = (z_ref, c) if c < nz else (xbc_ref, c - nz)

        @pl.when(j == c)
        def _(out_ref=out_ref, sub=sub):
            out_ref[:, sub * tn:sub * tn + half] = _dot(u_ref[...], wa_ref[...])
            out_ref[:, sub * tn + half:(sub + 1) * tn] = _dot(u_ref[...], wb_ref[...])


def _in_zx(u, w_zx, *, tm=1024):
    t, d = u.shape
    tm = min(tm, t)
    tn = IN_ZX_TN
    rows = lambda w: pl.BlockSpec((tm, w), lambda i, j: (i, 0))
    return pl.pallas_call(
        _in_zx_body,
        grid=(t // tm, w_zx.shape[1] // tn),
        in_specs=[rows(d), pl.BlockSpec((d, tn // 2), lambda i, j: (0, 2 * j)),
                  pl.BlockSpec((d, tn // 2), lambda i, j: (0, 2 * j + 1))],
        out_specs=[rows(SSD_WIDTH), rows(CONV_DIM)],
        out_shape=[jax.ShapeDtypeStruct((t, SSD_WIDTH), F32), jax.ShapeDtypeStruct((t, CONV_DIM), F32)],
        compiler_params=_params("parallel", "arbitrary"),
        name="in_zx",
    )(u, w_zx, w_zx)


def _in_proj(x, g, w_qkv, w_zx, w_small, b_small, gq, gk, *, q_dtype, q_scale):
    q, k, kb, v, vb, logf, dt, u = _in_qkv(x, g, w_qkv, w_small, b_small, gq, gk, q_dtype=q_dtype, q_scale=q_scale)
    z, xbc = _in_zx(u, w_zx)
    return q, k, kb, v, vb, z, xbc, logf, dt


def _pack_in_proj(w_in, fox_b_f, ssd_dt_bias):
    fw = FOX_WIDTH
    f0 = 3 * fw
    z0 = f0 + FOX_HEADS
    x0 = z0 + SSD_WIDTH
    d0 = x0 + CONV_DIM
    w_in = w_in.astype(BF16)
    w_qkv = w_in
    w_zx = w_in[:, z0:d0]
    zeros = lambda n: jnp.zeros((w_in.shape[0], n), BF16)
    w_small = jnp.concatenate(
        [w_in[:, f0:z0], zeros(LANES - FOX_HEADS), w_in[:, d0:], zeros(LANES - SSD_HEADS)], axis=1)
    b_small = jnp.concatenate(
        [fox_b_f, jnp.zeros((LANES - FOX_HEADS,), F32), ssd_dt_bias, jnp.zeros((LANES - SSD_HEADS,), F32)]
    ).reshape(1, 2 * LANES)
    return w_qkv, w_zx, w_small, b_small


ATT_BLOCK = 256
FOX_QUERY_BLOCK = 256
FOX_KEY_BLOCK = 256


def _tri(n, *, strict=False, upper=False):
    r = lax.broadcasted_iota(jnp.int32, (n, n), 0)
    c = lax.broadcasted_iota(jnp.int32, (n, n), 1)
    if upper:
        r, c = c, r
    return (c < r) if strict else (c <= r)


def _cumsum_body(x_ref, col_ref, row_ref, carry_ref):
    @pl.when(pl.program_id(1) == 0)
    def _():
        carry_ref[...] = jnp.zeros_like(carry_ref)

    n = x_ref.shape[1]
    tril = _tri(n).astype(BF16)
    c = _dot_sel_l(tril, x_ref[0]) + carry_ref[...]
    carry_ref[...] = c[n - 1:n, :]
    c = c * LOG2E
    col_ref[0] = c
    row_ref[0] = c.T[:FOX_HEADS, :]


def _cumsum(x, *, tb=ATT_BLOCK):
    b, l, _ = x.shape
    return pl.pallas_call(
        _cumsum_body,
        grid=(b, l // tb),
        in_specs=[pl.BlockSpec((1, tb, LANES), lambda i, j: (i, j, 0))],
        out_specs=[pl.BlockSpec((1, tb, LANES), lambda i, j: (i, j, 0)),
                   pl.BlockSpec((1, FOX_HEADS, tb), lambda i, j: (i, 0, j))],
        out_shape=[jax.ShapeDtypeStruct((b, l, LANES), F32), jax.ShapeDtypeStruct((b, FOX_HEADS, l), F32)],
        scratch_shapes=[pltpu.VMEM((1, LANES), F32)],
        compiler_params=_params("parallel", "arbitrary"),
        name="logf_cumsum",
    )(x)


def _fox_prompt_body(tk, q_ref, k_ref, v_ref, ccol_ref, crow_ref, o_ref, m_ref, l_ref, cq_ref, acc_ref):
    i = pl.program_id(1)
    tq = q_ref.shape[0]
    rep = tk // LANES
    m_ref[...] = jnp.full_like(m_ref, -jnp.inf)
    l_ref[...] = jnp.zeros_like(l_ref)
    acc_ref[...] = jnp.zeros_like(acc_ref)
    for h in range(FOX_HEADS):
        cq_ref[h] = jnp.broadcast_to(ccol_ref[0, :, h:h + 1], (tq, LANES))
    row = i * tq + lax.broadcasted_iota(jnp.int32, (tq, tk), 0)
    col = lax.broadcasted_iota(jnp.int32, (tq, tk), 1)
    wide = lambda x: jnp.concatenate([x] * rep, axis=1)

    def block(j, masked):
        ks = pl.ds(pl.multiple_of(j * tk, tk), tk)
        for h in range(FOX_HEADS):
            hs = slice(h * HEAD_DIM, (h + 1) * HEAD_DIM)
            s = _dot_nt(q_ref[:, hs], k_ref[ks, hs]) + (wide(cq_ref[h]) - crow_ref[0, h, pl.ds(j, 1), :])
            if masked:
                s = jnp.where(col + j * tk <= row, s, -jnp.inf)
            m_old = m_ref[h]
            m_new = jnp.maximum(m_old, jnp.max(s, axis=-1, keepdims=True))
            alpha = jnp.exp2(m_old - m_new)
            p = jnp.exp2(s - wide(m_new))
            m_ref[h] = m_new
            l_ref[h] = alpha * l_ref[h] + jnp.sum(p, axis=-1, keepdims=True)
            acc_ref[h] = alpha * acc_ref[h] + _dot(p.astype(BF16), v_ref[ks, hs])

    first_masked = (i * tq) // tk

    def step(j, carry):
        block(j, False)
        return carry

    lax.fori_loop(0, first_masked, step, 0)
    for extra in range(max(tq // tk, 1)):
        block(first_masked + extra, True)
    for h in range(FOX_HEADS):
        o_ref[:, h * HEAD_DIM:(h + 1) * HEAD_DIM] = (acc_ref[h] / l_ref[h]).astype(o_ref.dtype)


def _fox_prompt(q, k, v, ccol, crow, b, l, *, tq=128):
    tk = crow.shape[-1]
    tq = min(tq, l)
    assert tk % tq == 0 or tq % tk == 0
    nq = l // tq
    w = FOX_WIDTH
    return pl.pallas_call(
        functools.partial(_fox_prompt_body, tk),
        grid=(b, nq),
        in_specs=[
            pl.BlockSpec((tq, w), lambda bi, i: (bi * nq + i, 0)),
            pl.BlockSpec((l, w), lambda bi, i: (bi, 0)),
            pl.BlockSpec((l, w), lambda bi, i: (bi, 0)),
            pl.BlockSpec((1, tq, LANES), lambda bi, i: (bi, i, 0)),
            pl.BlockSpec((1, FOX_HEADS, l // tk, tk), lambda bi, i: (bi, 0, 0, 0)),
        ],
        out_specs=pl.BlockSpec((tq, w), lambda bi, i: (bi * nq + i, 0)),
        out_shape=jax.ShapeDtypeStruct((b * l, w), BF16),
        scratch_shapes=[pltpu.VMEM((FOX_HEADS, tq, LANES), F32)] * 3 + [pltpu.VMEM((FOX_HEADS, tq, HEAD_DIM), F32)],
        compiler_params=_params("parallel", "arbitrary"),
        name="fox_prompt",
    )(q, k, v, ccol, crow)


def _page_cumsum_body(x_ref, w_ref, m_ref):
    n = PAGE_SIZE * FOX_HEADS

    @pl.when(pl.program_id(0) == 0)
    def _():
        r = lax.broadcasted_iota(jnp.int32, (n, n), 0)
        c = lax.broadcasted_iota(jnp.int32, (n, n), 1)
        same_head = (r & (FOX_HEADS - 1)) == (c & (FOX_HEADS - 1))
        earlier = lax.shift_right_logical(r, 3) <= lax.shift_right_logical(c, 3)
        m_ref[...] = jnp.logical_and(same_head, earlier).astype(BF16)

    w_ref[...] = _dot_sel_r(x_ref[...], m_ref[...])


def _page_cumsum(logf_pages, *, tb=256):
    n_pool, n = logf_pages.shape
    return pl.pallas_call(
        _page_cumsum_body,
        grid=(n_pool // tb,),
        in_specs=[pl.BlockSpec((tb, n), lambda i: (i, 0))],
        out_specs=pl.BlockSpec((tb, n), lambda i: (i, 0)),
        out_shape=jax.ShapeDtypeStruct((n_pool, n), F32),
        scratch_shapes=[pltpu.VMEM((n, n), BF16)],
        compiler_params=_params("arbitrary"),
        name="page_cumsum",
    )(logf_pages)


def _seq_cumsum_body(seq_len, x_ref, o_ref):
    n = x_ref.shape[0]
    r = lax.broadcasted_iota(jnp.int32, (n, n), 0)
    c = lax.broadcasted_iota(jnp.int32, (n, n), 1)
    same_seq = (r // seq_len) == (c // seq_len)
    o_ref[...] = _dot_sel_l(jnp.logical_and(same_seq, c <= r).astype(BF16), x_ref[...])


def _seq_cumsum(x, seq_len, *, tb=128):
    t = x.shape[0]
    return pl.pallas_call(
        functools.partial(_seq_cumsum_body, seq_len),
        grid=(t // tb,),
        in_specs=[pl.BlockSpec((tb, LANES), lambda i: (i, 0))],
        out_specs=pl.BlockSpec((tb, LANES), lambda i: (i, 0)),
        out_shape=jax.ShapeDtypeStruct((t, LANES), F32),
        compiler_params=_params("parallel"),
        name="seq_cumsum",
    )(x)


def _fox_sample_body(pps, pt_ref, q_ref, kn_ref, vn_ref, cn_ref, *refs):
    kp_refs, vp_refs, w_refs = refs[:pps], refs[pps:2 * pps], refs[2 * pps:3 * pps]
    o_ref, q_scr, colq_ref, toff_ref, m_ref, l_ref, acc_ref = refs[3 * pps:]
    j = pl.program_id(1)
    nq = q_ref.shape[0]
    rows = nq * FOX_HEADS
    page_keys = PAGE_SIZE * FOX_HEADS
    row_id = lax.broadcasted_iota(jnp.int32, (rows, 1), 0)
    head_of_row = row_id & (FOX_HEADS - 1)
    query_of_row = lax.shift_right_logical(row_id, 3)

    @pl.when(j == 0)
    def _():
        q = q_ref[...].reshape(rows, HEAD_DIM).astype(BF16)
        q_scr[...] = q
        cn = cn_ref[0]
        key = lax.broadcasted_iota(jnp.int32, (1, rows), 1)
        colq = jnp.sum(jnp.where(key == row_id, cn, 0.0), axis=-1, keepdims=True)
        colq_ref[...] = colq
        toff_ref[...] = jnp.zeros_like(toff_ref)
        s = _dot_nt(q, kn_ref[...].reshape(rows, HEAD_DIM).astype(BF16)) + colq - cn
        valid = jnp.logical_and((key & (FOX_HEADS - 1)) == head_of_row,
                                lax.shift_right_logical(key, 3) <= query_of_row)
        s = jnp.where(valid, s, -jnp.inf)
        m = jnp.max(s, axis=-1, keepdims=True)
        p = jnp.exp(s - m)
        m_ref[...] = m
        l_ref[...] = jnp.sum(p, axis=-1, keepdims=True)
        acc_ref[...] = _dot(p.astype(BF16), vn_ref[...].reshape(rows, HEAD_DIM).astype(BF16))

    lane = lax.broadcasted_iota(jnp.int32, (1, LANES), 1)
    own_head = (lax.broadcasted_iota(jnp.int32, (1, page_keys), 1) & (FOX_HEADS - 1)) == head_of_row
    q = q_scr[...]
    colq = colq_ref[...]
    toff = toff_ref[...]
    tiles = []
    for kp_ref, w_ref in zip(kp_refs, w_refs):
        w = w_ref[0]
        last = jnp.where(lane == LANES - FOX_HEADS + head_of_row, w[:, page_keys - LANES:], 0.0)
        toff = toff + jnp.sum(last, axis=-1, keepdims=True)
        s = _dot_nt(q, kp_ref[0].reshape(page_keys, HEAD_DIM).astype(BF16))
        tiles.append(jnp.where(own_head, s + (colq + toff) - w, -jnp.inf))
    toff_ref[...] = toff
    m_old = m_ref[...]
    m = m_old
    for s in tiles:
        m = jnp.maximum(m, jnp.max(s, axis=-1, keepdims=True))
    alpha = jnp.exp(m_old - m)
    l = alpha * l_ref[...]
    acc = alpha * acc_ref[...]
    for s, vp_ref in zip(tiles, vp_refs):
        p = jnp.exp(s - m)
        l = l + jnp.sum(p, axis=-1, keepdims=True)
        acc = acc + _dot(p.astype(BF16), vp_ref[0].reshape(page_keys, HEAD_DIM).astype(BF16))
    m_ref[...] = m
    l_ref[...] = l
    acc_ref[...] = acc

    @pl.when(j == pl.num_programs(1) - 1)
    def _():
        o_ref[...] = (acc_ref[...] / l_ref[...]).reshape(nq, FOX_HEADS, HEAD_DIM)


def _fox_sample(page_table, q, k_new, v_new, c_new, k_pages, v_pages, w_pages, n_seq, nq, *, pages_per_step=16):
    n_pages = page_table.shape[1]
    rows = nq * FOX_HEADS
    pps = math.gcd(pages_per_step, n_pages)

    def page(i, ndim):
        return lambda b, j, pt: (pt[b * n_pages + (n_pages - 1 - j * pps - i)],) + (0,) * (ndim - 1)

    seq = pl.BlockSpec((nq, FOX_HEADS, HEAD_DIM), lambda b, j, pt: (b, 0, 0))
    kv_specs = [pl.BlockSpec((1, PAGE_SIZE, FOX_HEADS, HEAD_DIM), page(i, 4)) for i in range(pps)]
    w_specs = [pl.BlockSpec((1, 1, PAGE_SIZE * FOX_HEADS), page(i, 3)) for i in range(pps)]
    grid_spec = pltpu.PrefetchScalarGridSpec(
        num_scalar_prefetch=1,
        grid=(n_seq, n_pages // pps),
        in_specs=[seq, seq, seq, pl.BlockSpec((1, 1, rows), lambda b, j, pt: (b, 0, 0))]
        + kv_specs + kv_specs + w_specs,
        out_specs=seq,
        scratch_shapes=[
            pltpu.VMEM((rows, HEAD_DIM), BF16), pltpu.VMEM((rows, 1), F32), pltpu.VMEM((rows, 1), F32),
            pltpu.VMEM((rows, 1), F32), pltpu.VMEM((rows, 1), F32), pltpu.VMEM((rows, HEAD_DIM), F32),
        ],
    )
    return pl.pallas_call(
        functools.partial(_fox_sample_body, pps),
        grid_spec=grid_spec,
        out_shape=jax.ShapeDtypeStruct((n_seq * nq, FOX_HEADS, HEAD_DIM), F32),
        compiler_params=_params("parallel", "arbitrary"),
        name="fox_sample",
    )(page_table.reshape(-1), q, k_new, v_new, c_new, *([k_pages] * pps), *([v_pages] * pps), *([w_pages] * pps))


SSD_PAIRS = SSD_HEADS // 2
PAIRS_PER_GROUP = SSD_PAIRS // SSD_GROUPS


def _expander(width):
    n = SSD_HEADS * width
    h = lax.broadcasted_iota(jnp.int32, (LANES, n), 0)
    c = lax.broadcasted_iota(jnp.int32, (LANES, n), 1)
    return (lax.shift_right_logical(c, int(math.log2(width))) == h).astype(BF16)


def _ssd_local(xs, bm, cm, dt, alog, mask, tot_sel):
    n = xs.shape[0]
    lane = lax.broadcasted_iota(jnp.int32, (1, LANES), 1)
    dta = dt * jnp.where(lane < SSD_HEADS, -jnp.exp(alog), 0.0)
    e64 = _expander(SSD_HEAD_DIM)
    a_cum = _dot_sel_l(mask.astype(BF16), dta)
    a_cum_t = a_cum.T
    ac_exp = _dot_sel_r(a_cum, e64)
    if tot_sel is None:
        atot_exp = ac_exp[n - 1:n, :]
    else:
        atot_exp = _dot_sel_r(_dot_sel_l(tot_sel, dta), e64)
    ac_b = _dot_sel_r(a_cum, _expander(LANES))
    xdt = xs * _dot_sel_r(dt, e64)
    half = lax.broadcasted_iota(jnp.int32, (n, LANES), 1) < SSD_HEAD_DIM
    out = {
        "xdtw": xdt * jnp.exp(atot_exp - ac_exp),
        "eac": jnp.exp(ac_exp),
        "atot_exp": atot_exp,
        "ac_b": ac_b,
        "bg": [], "cg": [], "y_diag": [],
    }
    for g in range(SSD_GROUPS):
        gs = slice(g * SSD_STATE, (g + 1) * SSD_STATE)
        bg = bm[:, gs].astype(BF16)
        cg = cm[:, gs].astype(BF16)
        out["bg"].append(bg)
        out["cg"].append(cg)
        cb = _dot_nt(cg, bg)
        for k in range(g * PAIRS_PER_GROUP, (g + 1) * PAIRS_PER_GROUP):
            ps = slice(k * LANES, (k + 1) * LANES)
            ms = []
            for h in (2 * k, 2 * k + 1):
                seg = ac_b[:, h * LANES:(h + 1) * LANES] - a_cum_t[h:h + 1, :]
                ms.append(cb * jnp.exp(jnp.where(mask, seg, -jnp.inf)))
            m_cat = jnp.concatenate(ms, axis=1).astype(BF16)
            xp = xdt[:, ps]
            x_bd = jnp.concatenate([jnp.where(half, xp, 0.0), jnp.where(half, 0.0, xp)], axis=0).astype(BF16)
            out["y_diag"].append(_dot(m_cat, x_bd))
    return out


def _conv_ssd_sample_body(seq_len, xbc_ref, dt_ref, buf_ref, h0_ref, w_ref, b_ref, alog_ref, dexp_ref,
                          y_ref, hout_ref, cout_ref, xp_ref):
    n = xbc_ref.shape[0]
    n_seq = n // seq_len
    taps = CONV_WIDTH - 1
    base = 8 - taps

    x = xbc_ref[...]
    xp_ref[:, base:8, :] = buf_ref[...]
    xp_ref[:, 8:8 + seq_len, :] = x.reshape(n_seq, seq_len, CONV_DIM)
    acc = b_ref[...] + x * w_ref[taps:taps + 1, :]
    for j in range(taps):
        acc = acc + xp_ref[:, base + j:base + j + seq_len, :].reshape(n, CONV_DIM) * w_ref[j:j + 1, :]
    cout_ref[...] = xp_ref[:, 8 + seq_len - taps:8 + seq_len, :]
    conv = _silu(acc)
    xs = conv[:, :SSD_WIDTH]
    bm = conv[:, SSD_WIDTH:SSD_WIDTH + SSD_GROUPS * SSD_STATE]
    cm = conv[:, SSD_WIDTH + SSD_GROUPS * SSD_STATE:]

    r = lax.broadcasted_iota(jnp.int32, (n, n), 0)
    c = lax.broadcasted_iota(jnp.int32, (n, n), 1)
    same_seq = (r // seq_len) == (c // seq_len)
    loc = _ssd_local(xs, bm, cm, dt_ref[...], alog_ref[...], jnp.logical_and(same_seq, c <= r),
                     same_seq.astype(BF16))

    gw = PAIRS_PER_GROUP * LANES
    seq_of_col = lax.broadcasted_iota(jnp.int32, (1, n), 1) // seq_len
    decay_t = jnp.exp(loc["atot_exp"]).T
    for g in range(SSD_GROUPS):
        gr = slice(g * gw, (g + 1) * gw)
        h_prev = h0_ref[:, gr, :]
        z = _dot_nt(h_prev.reshape(n_seq * gw, SSD_STATE).astype(BF16), loc["cg"][g])
        y_off_t = jnp.zeros((gw, n), F32)
        for s in range(n_seq):
            y_off_t = y_off_t + jnp.where(seq_of_col == s, z[s * gw:(s + 1) * gw, :], 0.0)
        y_off = y_off_t.T * loc["eac"][:, gr]
        y = jnp.concatenate(loc["y_diag"][g * PAIRS_PER_GROUP:(g + 1) * PAIRS_PER_GROUP], axis=1)
        y_ref[:, gr] = y + y_off + xs[:, gr] * dexp_ref[:, gr]
        xw_t = loc["xdtw"][:, gr].T
        lhs = jnp.concatenate([jnp.where(seq_of_col == s, xw_t, 0.0) for s in range(n_seq)], axis=0)
        s_new = _dot(lhs.astype(BF16), loc["bg"][g])
        for s in range(n_seq):
            col = decay_t[gr, s * seq_len:s * seq_len + 1]
            hout_ref[s, gr, :] = h_prev[s] * col + s_new[s * gw:(s + 1) * gw, :]


def _conv_ssd_sample(xbc, dt, conv_buf, h0, conv_w, conv_b, alog, d_exp, n_seq, seq_len):
    assert seq_len == 8 and CONV_WIDTH - 1 <= seq_len
    tile = LANES
    ts = tile // seq_len
    const = lambda i: (0, 0)
    per_s = lambda i: (i, 0, 0)
    return pl.pallas_call(
        functools.partial(_conv_ssd_sample_body, seq_len),
        grid=(n_seq // ts,),
        in_specs=[
            pl.BlockSpec((tile, CONV_DIM), lambda i: (i, 0)),
            pl.BlockSpec((tile, LANES), lambda i: (i, 0)),
            pl.BlockSpec((ts, CONV_WIDTH - 1, CONV_DIM), per_s),
            pl.BlockSpec((ts, SSD_WIDTH, SSD_STATE), per_s),
            pl.BlockSpec((CONV_WIDTH, CONV_DIM), const),
            pl.BlockSpec((1, CONV_DIM), const),
            pl.BlockSpec((1, LANES), const),
            pl.BlockSpec((1, SSD_WIDTH), const),
        ],
        out_specs=[
            pl.BlockSpec((tile, SSD_WIDTH), lambda i: (i, 0)),
            pl.BlockSpec((ts, SSD_WIDTH, SSD_STATE), per_s),
            pl.BlockSpec((ts, CONV_WIDTH - 1, CONV_DIM), per_s),
        ],
        out_shape=[
            jax.ShapeDtypeStruct((n_seq * seq_len, SSD_WIDTH), F32),
            jax.ShapeDtypeStruct((n_seq, SSD_WIDTH, SSD_STATE), F32),
            jax.ShapeDtypeStruct((n_seq, CONV_WIDTH - 1, CONV_DIM), F32),
        ],
        scratch_shapes=[pltpu.VMEM((ts, 8 + seq_len, CONV_DIM), F32)],
        compiler_params=_params("parallel"),
        name="conv_ssd_sample",
    )(xbc, dt, conv_buf, h0, conv_w, conv_b, alog, d_exp)


def _conv_ssd_prompt_body(xbc_ref, dt_ref, buf_ref, h0_ref, w_ref, b_ref, alog_ref, dexp_ref,
                          y_ref, hout_ref, cout_ref, state_ref, xp_ref):
    c = pl.program_id(1)
    nc = pl.num_programs(1)
    tl = xbc_ref.shape[0]
    taps = CONV_WIDTH - 1
    base = 8 - taps

    @pl.when(c == 0)
    def _():
        state_ref[...] = h0_ref[0]
        xp_ref[base:8, :] = buf_ref[0]

    x = xbc_ref[...]
    xp_ref[8:8 + tl, :] = x
    acc = b_ref[...] + x * w_ref[taps:taps + 1, :]
    for j in range(taps):
        acc = acc + xp_ref[base + j:base + j + tl, :] * w_ref[j:j + 1, :]
    xp_ref[base:8, :] = x[tl - taps:, :]
    conv = _silu(acc)
    xs = conv[:, :SSD_WIDTH]
    bm = conv[:, SSD_WIDTH:SSD_WIDTH + SSD_GROUPS * SSD_STATE]
    cm = conv[:, SSD_WIDTH + SSD_GROUPS * SSD_STATE:]

    causal = _tri(tl)
    loc = _ssd_local(xs, bm, cm, dt_ref[...], alog_ref[...], causal, None)
    top = lax.broadcasted_iota(jnp.int32, (tl, LANES), 0) < SSD_HEAD_DIM
    for k in range(SSD_PAIRS):
        g = k // PAIRS_PER_GROUP
        ps = slice(k * LANES, (k + 1) * LANES)
        s_prev = state_ref[ps, :]
        y_off = _dot_nt(loc["cg"][g], s_prev.astype(BF16)) * loc["eac"][:, ps]
        y_ref[:, ps] = loc["y_diag"][k] + y_off + xs[:, ps] * dexp_ref[:, ps]
        cd = [jnp.exp(loc["ac_b"][tl - 1:tl, h * LANES:(h + 1) * LANES]) for h in (2 * k, 2 * k + 1)]
        state_ref[ps, :] = s_prev * jnp.where(top, cd[0], cd[1]) + _dot_tn(loc["xdtw"][:, ps].astype(BF16), loc["bg"][g])

    @pl.when(c == nc - 1)
    def _():
        hout_ref[0] = state_ref[...]
        cout_ref[0] = xp_ref[base:8, :]


def _conv_ssd_prompt(xbc, dt, conv_buf, h0, conv_w, conv_b, alog, d_exp, b, l):
    tl = SSD_CHUNK
    nc = l // tl
    const = lambda bi, c: (0, 0)
    per_b = lambda bi, c: (bi, 0, 0)
    return pl.pallas_call(
        _conv_ssd_prompt_body,
        grid=(b, nc),
        in_specs=[
            pl.BlockSpec((tl, CONV_DIM), lambda bi, c: (bi * nc + c, 0)),
            pl.BlockSpec((tl, LANES), lambda bi, c: (bi * nc + c, 0)),
            pl.BlockSpec((1, CONV_WIDTH - 1, CONV_DIM), per_b),
            pl.BlockSpec((1, SSD_WIDTH, SSD_STATE), per_b),
            pl.BlockSpec((CONV_WIDTH, CONV_DIM), const),
            pl.BlockSpec((1, CONV_DIM), const),
            pl.BlockSpec((1, LANES), const),
            pl.BlockSpec((1, SSD_WIDTH), const),
        ],
        out_specs=[
            pl.BlockSpec((tl, SSD_WIDTH), lambda bi, c: (bi * nc + c, 0)),
            pl.BlockSpec((1, SSD_WIDTH, SSD_STATE), per_b),
            pl.BlockSpec((1, CONV_WIDTH - 1, CONV_DIM), per_b),
        ],
        out_shape=[
            jax.ShapeDtypeStruct((b * l, SSD_WIDTH), F32),
            jax.ShapeDtypeStruct((b, SSD_WIDTH, SSD_STATE), F32),
            jax.ShapeDtypeStruct((b, CONV_WIDTH - 1, CONV_DIM), F32),
        ],
        scratch_shapes=[pltpu.VMEM((SSD_WIDTH, SSD_STATE), F32), pltpu.VMEM((8 + tl, CONV_DIM), F32)],
        compiler_params=_params("parallel", "arbitrary"),
        name="conv_ssd_prompt",
    )(xbc, dt, conv_buf, h0, conv_w, conv_b, alog, d_exp)


def _out_proj_body(h_ref, fox_ref, y_ref, z_ref, gs_ref, w_ref, gx_ref, wq_ref, gq_ref, o_ref, q_ref):
    yn = _rms(y_ref[...] * _silu(z_ref[...]), gs_ref[...]).astype(BF16)
    fw = fox_ref.shape[1]
    h = h_ref[...] + _dot(fox_ref[...].astype(BF16), w_ref[:fw, :]) + _dot(yn, w_ref[fw:, :])
    o_ref[...] = h
    q = _dot(_rms(h, gx_ref[...]).astype(BF16), wq_ref[...])
    q_ref[...] = _head_norm(q, gq_ref[...]) * (HEAD_DIM ** -0.5)


def _out_proj(h, fox, y, z, g_ssd, w_out, g_x, wq, gq, *, tm=256):
    t, d = h.shape
    tm = min(tm, t)
    row = lambda w: pl.BlockSpec((tm, w), lambda i: (i, 0))
    full = lambda a: pl.BlockSpec(a.shape, lambda i: (0, 0))
    consts = [g_ssd.reshape(1, SSD_WIDTH), w_out, g_x.reshape(1, d), wq, gq.reshape(1, HEAD_DIM)]
    return pl.pallas_call(
        _out_proj_body,
        grid=(t // tm,),
        in_specs=[row(d), row(FOX_WIDTH), row(SSD_WIDTH), row(SSD_WIDTH)] + [full(a) for a in consts],
        out_specs=[row(d), row(XATTN_WIDTH)],
        out_shape=[jax.ShapeDtypeStruct((t, d), F32), jax.ShapeDtypeStruct((t, XATTN_WIDTH), F32)],
        compiler_params=_params("parallel"),
        name="out_proj",
    )(h, fox, y, z, *consts)


def _mem_kv_body(m_ref, g_ref, w_ref, gk_ref, k_ref, v_ref):
    kv = _dot(_rms(m_ref[...], g_ref[...]).astype(BF16), w_ref[...])
    k_ref[...] = _head_norm(kv[:, :XATTN_WIDTH], gk_ref[...])
    v_ref[...] = kv[:, XATTN_WIDTH:]


def _mem_kv(mem, g, w_kv, gk, *, tm=256):
    t, d = mem.shape
    row = lambda w: pl.BlockSpec((tm, w), lambda i: (i, 0))
    full = lambda a: pl.BlockSpec(a.shape, lambda i: (0, 0))
    consts = [g.reshape(1, d), w_kv, gk.reshape(1, HEAD_DIM)]
    return pl.pallas_call(
        _mem_kv_body,
        grid=(t // tm,),
        in_specs=[row(d)] + [full(a) for a in consts],
        out_specs=[row(XATTN_WIDTH), row(XATTN_WIDTH)],
        out_shape=[jax.ShapeDtypeStruct((t, XATTN_WIDTH), F32)] * 2,
        compiler_params=_params("parallel"),
        name="mem_kv",
    )(mem, *consts)


def _xattn_body(q_ref, k_ref, v_ref, o_ref):
    for h in range(XATTN_HEADS):
        hs = slice(h * HEAD_DIM, (h + 1) * HEAD_DIM)
        s = _dot_nt(q_ref[:, hs].astype(BF16), k_ref[0, :, hs].astype(BF16))
        p = jnp.exp(s - jnp.max(s, axis=-1, keepdims=True))
        o = _dot(p.astype(BF16), v_ref[0, :, hs].astype(BF16))
        o_ref[:, hs] = o / jnp.sum(p, axis=-1, keepdims=True)


def _xattn(q, mem_k, mem_v, b, l, *, tq=512):
    tq = min(tq, l)
    nq = l // tq
    n_mem = mem_k.shape[1]
    qspec = pl.BlockSpec((tq, XATTN_WIDTH), lambda bi, i: (bi * nq + i, 0))
    mspec = pl.BlockSpec((1, n_mem, XATTN_WIDTH), lambda bi, i: (bi, 0, 0))
    return pl.pallas_call(
        _xattn_body,
        grid=(b, nq),
        in_specs=[qspec, mspec, mspec],
        out_specs=qspec,
        out_shape=jax.ShapeDtypeStruct((b * l, XATTN_WIDTH), F32),
        compiler_params=_params("parallel", "arbitrary"),
        name="xattn",
    )(q, mem_k, mem_v)


def _xattn_rows_body(n_seq, q_ref, k_ref, v_ref, o_ref):
    tiles = q_ref.shape[0] // n_seq
    rows = tiles * 8
    keys = k_ref.shape[1] * k_ref.shape[2]
    row_head = lax.broadcasted_iota(jnp.int32, (rows, 1), 0) & (XATTN_HEADS - 1)
    key_head = lax.broadcasted_iota(jnp.int32, (1, keys), 1) & (XATTN_HEADS - 1)
    own_head = row_head == key_head
    for s in range(n_seq):
        qs = slice(s * tiles, (s + 1) * tiles)
        q = q_ref[qs].reshape(rows, HEAD_DIM).astype(BF16)
        sc = jnp.where(own_head, _dot_nt(q, k_ref[s].reshape(keys, HEAD_DIM).astype(BF16)), -jnp.inf)
        p = jnp.exp(sc - jnp.max(sc, axis=-1, keepdims=True))
        o = _dot(p.astype(BF16), v_ref[s].reshape(keys, HEAD_DIM).astype(BF16))
        o_ref[qs] = (o / jnp.sum(p, axis=-1, keepdims=True)).reshape(tiles, 8, HEAD_DIM)


def _xattn_rows(q, mem_k, mem_v, n_seq, *, ts=8):
    tiles = q.shape[0] // n_seq
    ts = math.gcd(ts, n_seq)
    qspec = pl.BlockSpec((ts * tiles, 8, HEAD_DIM), lambda i: (i, 0, 0))
    mspec = pl.BlockSpec((ts,) + mem_k.shape[1:], lambda i: (i, 0, 0, 0))
    return pl.pallas_call(
        functools.partial(_xattn_rows_body, ts),
        grid=(n_seq // ts,),
        in_specs=[qspec, mspec, mspec],
        out_specs=qspec,
        out_shape=jax.ShapeDtypeStruct(q.shape, F32),
        compiler_params=_params("parallel"),
        name="xattn_rows",
    )(q, mem_k, mem_v)


def kernel(x_prompt, x_sample, cache_fox_k, cache_fox_v, cache_fox_logf, cache_mem_k, cache_mem_v, state_ssm, state_conv, page_table, mem_prompt, ffn1_norm, ffn1_w_gate, ffn1_w_up, ffn1_w_down, mix_norm, w_in, fox_b_f, fox_q_norm, fox_k_norm, conv_w, conv_b, ssd_dt_bias, ssd_A_log, ssd_D, ssd_out_norm, w_out, xattn_norm, mem_norm, xattn_w_q, xattn_w_kv, xattn_q_norm, xattn_k_norm, xattn_w_o, ffn2_norm, ffn2_w_gate, ffn2_w_up, ffn2_w_down):
    assert x_prompt.shape[2] == D_MODEL and ffn1_norm.shape[0] == 1
    d = D_MODEL
    bp, lp = x_prompt.shape[:2]
    bs, ls = x_sample.shape[:2]
    n_mem = mem_prompt.shape[1]
    n_pool = cache_fox_k.shape[1]

    bf = lambda w: w[0].astype(BF16)
    ffn1 = (ffn1_norm[0], bf(ffn1_w_gate), bf(ffn1_w_up), bf(ffn1_w_down))
    ffn2 = (ffn2_norm[0], bf(ffn2_w_gate), bf(ffn2_w_up), bf(ffn2_w_down))
    in_w = _pack_in_proj(w_in[0], fox_b_f[0], ssd_dt_bias[0])
    wq, wkv, wo = bf(xattn_w_q), bf(xattn_w_kv), bf(xattn_w_o)
    alog = jnp.pad(ssd_A_log[0], (0, LANES - SSD_HEADS)).reshape(1, LANES)
    d_exp = jnp.repeat(ssd_D[0], SSD_HEAD_DIM).reshape(1, SSD_WIDTH)
    cw, cb = conv_w[0], conv_b[0].reshape(1, CONV_DIM)

    def front(x, q_dtype, q_scale):
        h1 = _ffn(x, *ffn1)
        return h1, _in_proj(h1, mix_norm[0], *in_w, fox_q_norm[0], fox_k_norm[0], q_dtype=q_dtype, q_scale=q_scale)

    def back(h1, fox, y, z, attend):
        h2, xq = _out_proj(h1, fox, y, z, ssd_out_norm[0], bf(w_out), xattn_norm[0], wq, xattn_q_norm[0])
        return _ffn(h2, *ffn2, pre=(attend(xq), wo), tf=256, split=1)

    h1, (q, k_p, kb, v_p, vb, z, xbc, logf_p, dt) = front(x_prompt.reshape(bp * lp, d), BF16, HEAD_DIM ** -0.5 * LOG2E)
    ccol, crow = _cumsum(logf_p.reshape(bp, lp, LANES))
    tk = min(FOX_KEY_BLOCK, lp)
    fox = _fox_prompt(q, kb, vb, ccol, crow.reshape(bp, FOX_HEADS, lp // tk, tk), bp, lp, tq=FOX_QUERY_BLOCK)
    y, ssm_p, conv_p = _conv_ssd_prompt(
        xbc, dt, jnp.zeros((bp, CONV_WIDTH - 1, CONV_DIM), F32), jnp.zeros((bp, SSD_WIDTH, SSD_STATE), F32),
        cw, cb, alog, d_exp, bp, lp)
    mk, mv = _mem_kv(mem_prompt.reshape(bp * n_mem, d), mem_norm[0], wkv, xattn_k_norm[0])
    y_prompt = back(h1, fox, y, z, lambda xq: _xattn(
        xq, mk.reshape(bp, n_mem, XATTN_WIDTH), mv.reshape(bp, n_mem, XATTN_WIDTH), bp, lp))

    h1, (q, k_s, _, v_s, _, z, xbc, logf_s, dt) = front(x_sample.reshape(bs * ls, d), F32, HEAD_DIM ** -0.5)
    w_pages = _page_cumsum(cache_fox_logf[0].reshape(n_pool, PAGE_SIZE * FOX_HEADS))
    c_new = _seq_cumsum(logf_s, ls)[:, :FOX_HEADS].reshape(bs, 1, ls * FOX_HEADS)
    fox = _fox_sample(
        page_table, q.reshape(bs * ls, FOX_HEADS, HEAD_DIM), k_s, v_s, c_new, cache_fox_k[0], cache_fox_v[0],
        w_pages.reshape(n_pool, 1, PAGE_SIZE * FOX_HEADS), bs, ls).reshape(bs * ls, FOX_WIDTH)
    y, ssm_s, conv_s = _conv_ssd_sample(
        xbc, dt, state_conv[0], state_ssm[0].reshape(bs, SSD_WIDTH, SSD_STATE), cw, cb, alog, d_exp, bs, ls)
    mem_tiles = lambda m: m[0].reshape(bs, n_mem * XATTN_HEADS // 8, 8, HEAD_DIM)
    y_sample = back(h1, fox, y, z, lambda xq: _xattn_rows(
        xq.reshape(bs * ls * XATTN_HEADS // 8, 8, HEAD_DIM), mem_tiles(cache_mem_k), mem_tiles(cache_mem_v), bs,
    ).reshape(bs * ls, XATTN_WIDTH))

    fox_shape = lambda b, l: (1, b, l, FOX_HEADS, HEAD_DIM)
    ssm_shape = lambda b: (1, b, SSD_HEADS, SSD_HEAD_DIM, SSD_STATE)
    mem_shape = (1, bp, n_mem, XATTN_HEADS, HEAD_DIM)
    return (
        y_prompt.reshape(bp, lp, d), y_sample.reshape(bs, ls, d),
        k_p.reshape(fox_shape(bp, lp)), v_p.reshape(fox_shape(bp, lp)),
        logf_p[:, :FOX_HEADS].reshape(1, bp, lp, FOX_HEADS),
        ssm_p.reshape(ssm_shape(bp)), conv_p[None], mk.reshape(mem_shape), mv.reshape(mem_shape),
        k_s.reshape(fox_shape(bs, ls)), v_s.reshape(fox_shape(bs, ls)),
        logf_s[:, :FOX_HEADS].reshape(1, bs, ls, FOX_HEADS),
        ssm_s.reshape(ssm_shape(bs)), conv_s[None],
    )
```

```python
import functools
import math

import jax
import jax.numpy as jnp
from jax import lax
from jax.experimental import pallas as pl
from jax.experimental.pallas import tpu as pltpu

F32 = jnp.float32
BF16 = jnp.bfloat16

EPS = 1e-6
LOG2E = math.log2(math.e)
FFN_RESIDUAL = 0.5
D_MODEL = 2048
D_FF = 5632
PAGE_SIZE = 128
FOX_HEADS = 8
HEAD_DIM = 128
FOX_WIDTH = FOX_HEADS * HEAD_DIM
SSD_HEADS = 16
SSD_HEAD_DIM = 64
SSD_WIDTH = SSD_HEADS * SSD_HEAD_DIM
SSD_GROUPS = 2
SSD_STATE = 128
SSD_CHUNK = 128
CONV_WIDTH = 4
CONV_DIM = SSD_WIDTH + 2 * SSD_GROUPS * SSD_STATE
XATTN_HEADS = 4
XATTN_WIDTH = XATTN_HEADS * HEAD_DIM
LANES = 128
VMEM_LIMIT_BYTES = 56 * 1024 * 1024

def _params(*semantics):
    return pltpu.CompilerParams(dimension_semantics=semantics, vmem_limit_bytes=VMEM_LIMIT_BYTES)


def _rms(x, g):
    return x * lax.rsqrt(jnp.mean(x * x, axis=-1, keepdims=True) + EPS) * g


def _dot(a, b):
    return jnp.dot(a, b, preferred_element_type=F32)


def _dot_nt(a, b):
    return lax.dot_general(a, b, (((1,), (1,)), ((), ())), preferred_element_type=F32)


def _dot_tn(a, b):
    return lax.dot_general(a, b, (((0,), (0,)), ((), ())), preferred_element_type=F32)


def _split3(x):
    x1 = x.astype(BF16)
    r = x - x1.astype(F32)
    x2 = r.astype(BF16)
    x3 = (r - x2.astype(F32)).astype(BF16)
    return x1, x2, x3


def _dot_sel_l(sel, x):
    x1, x2, x3 = _split3(x)
    return _dot(sel, x1) + _dot(sel, x2) + _dot(sel, x3)


def _dot_sel_r(x, sel):
    x1, x2, x3 = _split3(x)
    return _dot(x1, sel) + _dot(x2, sel) + _dot(x3, sel)


def _silu(x):
    return x * jax.nn.sigmoid(x)


def _ffn_body(pre_proj, split, *refs):
    if pre_proj:
        x_ref, a_ref, wo_ref, g_ref, *w_refs, o_ref, xn_ref = refs
    else:
        x_ref, g_ref, *w_refs, o_ref, xn_ref = refs

    @pl.when(pl.program_id(1) == 0)
    def _():
        x = x_ref[...]
        if pre_proj:
            x = x + _dot(a_ref[...].astype(BF16), wo_ref[...])
        xn_ref[...] = _rms(x, g_ref[...]).astype(BF16)
        o_ref[...] = x

    xn = xn_ref[...]
    acc = None
    for wg_ref, wu_ref, wd_ref in zip(w_refs[:split], w_refs[split:2 * split], w_refs[2 * split:]):
        gate = _dot(xn, wg_ref[...])
        up = _dot(xn, wu_ref[...])
        h = (_silu(gate) * up * FFN_RESIDUAL).astype(BF16)
        part = _dot(h, wd_ref[...])
        acc = part if acc is None else acc + part
    o_ref[...] += acc


def _ffn(x, g, wg, wu, wd, pre=None, *, tm=1024, tf=512, split=2):
    t, d = x.shape
    f = wg.shape[1]
    tm = min(tm, t)
    ts = tf // split
    grid = (t // tm, f // tf)
    row = lambda i, j: (i, 0)
    in_specs = [pl.BlockSpec((tm, d), row)]
    args = [x]
    if pre is not None:
        a, wo = pre
        in_specs = [pl.BlockSpec((tm, d), row, pipeline_mode=pl.Buffered(1)),
                    pl.BlockSpec((tm, a.shape[1]), row), pl.BlockSpec(wo.shape, lambda i, j: (0, 0))]
        args += [a, wo]
    cols = [pl.BlockSpec((d, ts), lambda i, j, s=s: (0, j * split + s)) for s in range(split)]
    rows = [pl.BlockSpec((ts, d), lambda i, j, s=s: (j * split + s, 0)) for s in range(split)]
    in_specs += [pl.BlockSpec((1, d), lambda i, j: (0, 0))] + cols + cols + rows
    args += [g.reshape(1, d)] + [wg] * split + [wu] * split + [wd] * split
    return pl.pallas_call(
        functools.partial(_ffn_body, pre is not None, split),
        grid=grid,
        in_specs=in_specs,
        out_specs=pl.BlockSpec((tm, d), row),
        out_shape=jax.ShapeDtypeStruct((t, d), F32),
        scratch_shapes=[pltpu.VMEM((tm, d), BF16)],
        compiler_params=_params("parallel", "arbitrary"),
        name="ffn_pre" if pre is not None else "ffn",
    )(*args)


IN_QKV_TN = 1024
IN_ZX_TN = 512


def _head_norm(y, g):
    outs = []
    for c in range(y.shape[1] // HEAD_DIM):
        yc = y[:, c * HEAD_DIM:(c + 1) * HEAD_DIM]
        outs.append(yc * lax.rsqrt(jnp.mean(yc * yc, axis=-1, keepdims=True) + EPS) * g)
    return jnp.concatenate(outs, axis=1)


def _in_qkv_body(q_scale, x_ref, g_ref, w_ref, ws_ref, bs_ref, gq_ref, gk_ref,
                 q_ref, k_ref, kb_ref, v_ref, vb_ref, logf_ref, dt_ref, u_ref):
    j = pl.program_id(1)

    @pl.when(j == 0)
    def _():
        u = _rms(x_ref[...], g_ref[...]).astype(BF16)
        u_ref[...] = u
        s = _dot(u, ws_ref[...]) + bs_ref[...]
        t = jnp.log1p(jnp.exp(-jnp.abs(s)))
        logf_ref[...] = (jnp.minimum(s, 0.0) - t)[:, :LANES]
        dt_ref[...] = (jnp.maximum(s, 0.0) + t)[:, LANES:]
        q_ref[...] = (_head_norm(_dot(u, w_ref[...]), gq_ref[...]) * q_scale).astype(q_ref.dtype)

    def by_head(y, out_ref, bf_ref):
        bf_ref[...] = y.astype(BF16)
        for h in range(FOX_HEADS):
            out_ref[:, h, :] = y[:, h * HEAD_DIM:(h + 1) * HEAD_DIM]

    @pl.when(j == 1)
    def _():
        by_head(_head_norm(_dot(u_ref[...], w_ref[...]), gk_ref[...]), k_ref, kb_ref)

    @pl.when(j == 2)
    def _():
        by_head(_dot(u_ref[...], w_ref[...]), v_ref, vb_ref)


def _in_qkv(x, g, w_qkv, w_small, b_small, gq, gk, *, q_dtype, q_scale, tm=512):
    t, d = x.shape
    tm = min(tm, t)
    tn = IN_QKV_TN
    assert tn == FOX_WIDTH and w_qkv.shape[1] >= 3 * tn
    const = lambda i, j: (0, 0)
    rows = lambda w: pl.BlockSpec((tm, w), lambda i, j: (i, 0))
    by_head = pl.BlockSpec((tm, FOX_HEADS, HEAD_DIM), lambda i, j: (i, 0, 0))
    by_head_shape = jax.ShapeDtypeStruct((t, FOX_HEADS, HEAD_DIM), F32)
    flat = lambda dt, w: jax.ShapeDtypeStruct((t, w), dt)
    return pl.pallas_call(
        functools.partial(_in_qkv_body, q_scale),
        grid=(t // tm, 3),
        in_specs=[
            rows(d),
            pl.BlockSpec((1, d), const),
            pl.BlockSpec((d, tn), lambda i, j: (0, j)),
            pl.BlockSpec((d, 2 * LANES), const),
            pl.BlockSpec((1, 2 * LANES), const),
            pl.BlockSpec((1, HEAD_DIM), const),
            pl.BlockSpec((1, HEAD_DIM), const),
        ],
        out_specs=[rows(FOX_WIDTH), by_head, rows(FOX_WIDTH), by_head, rows(FOX_WIDTH), rows(LANES), rows(LANES),
                   rows(d)],
        out_shape=[flat(q_dtype, FOX_WIDTH), by_head_shape, flat(BF16, FOX_WIDTH), by_head_shape,
                   flat(BF16, FOX_WIDTH), flat(F32, LANES), flat(F32, LANES), flat(BF16, d)],
        compiler_params=_params("parallel", "arbitrary"),
        name="in_qkv",
    )(x, g.reshape(1, d), w_qkv, w_small, b_small, gq.reshape(1, HEAD_DIM), gk.reshape(1, HEAD_DIM))


def _in_zx_body(u_ref, wa_ref, wb_ref, z_ref, xbc_ref):
    j = pl.program_id(1)
    half = wa_ref.shape[1]
    tn = 2 * half
    nz = z_ref.shape[1] // tn
    for c in range(nz + xbc_ref.shape[1] // tn):
        out_ref, sub = (z_ref, c) if c < nz else (xbc_ref, c - nz)

        @pl.when(j == c)
        def _(out_ref=out_ref, sub=sub):
            out_ref[:, sub * tn:sub * tn + half] = _dot(u_ref[...], wa_ref[...])
            out_ref[:, sub * tn + half:(sub + 1) * tn] = _dot(u_ref[...], wb_ref[...])


def _in_zx(u, w_zx, *, tm=1024):
    t, d = u.shape
    tm = min(tm, t)
    tn = IN_ZX_TN
    rows = lambda w: pl.BlockSpec((tm, w), lambda i, j: (i, 0))
    return pl.pallas_call(
        _in_zx_body,
        grid=(t // tm, w_zx.shape[1] // tn),
        in_specs=[rows(d), pl.BlockSpec((d, tn // 2), lambda i, j: (0, 2 * j)),
                  pl.BlockSpec((d, tn // 2), lambda i, j: (0, 2 * j + 1))],
        out_specs=[rows(SSD_WIDTH), rows(CONV_DIM)],
        out_shape=[jax.ShapeDtypeStruct((t, SSD_WIDTH), F32), jax.ShapeDtypeStruct((t, CONV_DIM), F32)],
        compiler_params=_params("parallel", "arbitrary"),
        name="in_zx",
    )(u, w_zx, w_zx)


def _in_proj(x, g, w_qkv, w_zx, w_small, b_small, gq, gk, *, q_dtype, q_scale):
    q, k, kb, v, vb, logf, dt, u = _in_qkv(x, g, w_qkv, w_small, b_small, gq, gk, q_dtype=q_dtype, q_scale=q_scale)
    z, xbc = _in_zx(u, w_zx)
    return q, k, kb, v, vb, z, xbc, logf, dt


def _pack_in_proj(w_in, fox_b_f, ssd_dt_bias):
    fw = FOX_WIDTH
    f0 = 3 * fw
    z0 = f0 + FOX_HEADS
    x0 = z0 + SSD_WIDTH
    d0 = x0 + CONV_DIM
    w_in = w_in.astype(BF16)
    w_qkv = w_in
    w_zx = w_in[:, z0:d0]
    zeros = lambda n: jnp.zeros((w_in.shape[0], n), BF16)
    w_small = jnp.concatenate(
        [w_in[:, f0:z0], zeros(LANES - FOX_HEADS), w_in[:, d0:], zeros(LANES - SSD_HEADS)], axis=1)
    b_small = jnp.concatenate(
        [fox_b_f, jnp.zeros((LANES - FOX_HEADS,), F32), ssd_dt_bias, jnp.zeros((LANES - SSD_HEADS,), F32)]
    ).reshape(1, 2 * LANES)
    return w_qkv, w_zx, w_small, b_small


ATT_BLOCK = 256
FOX_QUERY_BLOCK = 256
FOX_KEY_BLOCK = 256


def _tri(n, *, strict=False, upper=False):
    r = lax.broadcasted_iota(jnp.int32, (n, n), 0)
    c = lax.broadcasted_iota(jnp.int32, (n, n), 1)
    if upper:
        r, c = c, r
    return (c < r) if strict else (c <= r)


def _cumsum_body(x_ref, col_ref, row_ref, carry_ref):
    @pl.when(pl.program_id(1) == 0)
    def _():
        carry_ref[...] = jnp.zeros_like(carry_ref)

    n = x_ref.shape[1]
    tril = _tri(n).astype(BF16)
    c = _dot_sel_l(tril, x_ref[0]) + carry_ref[...]
    carry_ref[...] = c[n - 1:n, :]
    c = c * LOG2E
    col_ref[0] = c
    row_ref[0] = c.T[:FOX_HEADS, :]


def _cumsum(x, *, tb=ATT_BLOCK):
    b, l, _ = x.shape
    return pl.pallas_call(
        _cumsum_body,
        grid=(b, l // tb),
        in_specs=[pl.BlockSpec((1, tb, LANES), lambda i, j: (i, j, 0))],
        out_specs=[pl.BlockSpec((1, tb, LANES), lambda i, j: (i, j, 0)),
                   pl.BlockSpec((1, FOX_HEADS, tb), lambda i, j: (i, 0, j))],
        out_shape=[jax.ShapeDtypeStruct((b, l, LANES), F32), jax.ShapeDtypeStruct((b, FOX_HEADS, l), F32)],
        scratch_shapes=[pltpu.VMEM((1, LANES), F32)],
        compiler_params=_params("parallel", "arbitrary"),
        name="logf_cumsum",
    )(x)


def _fox_prompt_body(tk, q_ref, k_ref, v_ref, ccol_ref, crow_ref, o_ref, m_ref, l_ref, cq_ref, acc_ref):
    i = pl.program_id(1)
    tq = q_ref.shape[0]
    rep = tk // LANES
    m_ref[...] = jnp.full_like(m_ref, -jnp.inf)
    l_ref[...] = jnp.zeros_like(l_ref)
    acc_ref[...] = jnp.zeros_like(acc_ref)
    for h in range(FOX_HEADS):
        cq_ref[h] = jnp.broadcast_to(ccol_ref[0, :, h:h + 1], (tq, LANES))
    row = i * tq + lax.broadcasted_iota(jnp.int32, (tq, tk), 0)
    col = lax.broadcasted_iota(jnp.int32, (tq, tk), 1)
    wide = lambda x: jnp.concatenate([x] * rep, axis=1)

    def block(j, masked):
        ks = pl.ds(pl.multiple_of(j * tk, tk), tk)
        for h in range(FOX_HEADS):
            hs = slice(h * HEAD_DIM, (h + 1) * HEAD_DIM)
            s = _dot_nt(q_ref[:, hs], k_ref[ks, hs]) + (wide(cq_ref[h]) - crow_ref[0, h, pl.ds(j, 1), :])
            if masked:
                s = jnp.where(col + j * tk <= row, s, -jnp.inf)
            m_old = m_ref[h]
            m_new = jnp.maximum(m_old, jnp.max(s, axis=-1, keepdims=True))
            alpha = jnp.exp2(m_old - m_new)
            p = jnp.exp2(s - wide(m_new))
            m_ref[h] = m_new
            l_ref[h] = alpha * l_ref[h] + jnp.sum(p, axis=-1, keepdims=True)
            acc_ref[h] = alpha * acc_ref[h] + _dot(p.astype(BF16), v_ref[ks, hs])

    first_masked = (i * tq) // tk

    def step(j, carry):
        block(j, False)
        return carry

    lax.fori_loop(0, first_masked, step, 0)
    for extra in range(max(tq // tk, 1)):
        block(first_masked + extra, True)
    for h in range(FOX_HEADS):
        o_ref[:, h * HEAD_DIM:(h + 1) * HEAD_DIM] = (acc_ref[h] / l_ref[h]).astype(o_ref.dtype)


def _fox_prompt(q, k, v, ccol, crow, b, l, *, tq=128):
    tk = crow.shape[-1]
    tq = min(tq, l)
    assert tk % tq == 0 or tq % tk == 0
    nq = l // tq
    w = FOX_WIDTH
    return pl.pallas_call(
        functools.partial(_fox_prompt_body, tk),
        grid=(b, nq),
        in_specs=[
            pl.BlockSpec((tq, w), lambda bi, i: (bi * nq + i, 0)),
            pl.BlockSpec((l, w), lambda bi, i: (bi, 0)),
            pl.BlockSpec((l, w), lambda bi, i: (bi, 0)),
            pl.BlockSpec((1, tq, LANES), lambda bi, i: (bi, i, 0)),
            pl.BlockSpec((1, FOX_HEADS, l // tk, tk), lambda bi, i: (bi, 0, 0, 0)),
        ],
        out_specs=pl.BlockSpec((tq, w), lambda bi, i: (bi * nq + i, 0)),
        out_shape=jax.ShapeDtypeStruct((b * l, w), BF16),
        scratch_shapes=[pltpu.VMEM((FOX_HEADS, tq, LANES), F32)] * 3 + [pltpu.VMEM((FOX_HEADS, tq, HEAD_DIM), F32)],
        compiler_params=_params("parallel", "arbitrary"),
        name="fox_prompt",
    )(q, k, v, ccol, crow)


def _page_cumsum_body(x_ref, w_ref, m_ref):
    n = PAGE_SIZE * FOX_HEADS

    @pl.when(pl.program_id(0) == 0)
    def _():
        r = lax.broadcasted_iota(jnp.int32, (n, n), 0)
        c = lax.broadcasted_iota(jnp.int32, (n, n), 1)
        same_head = (r & (FOX_HEADS - 1)) == (c & (FOX_HEADS - 1))
        earlier = lax.shift_right_logical(r, 3) <= lax.shift_right_logical(c, 3)
        m_ref[...] = jnp.logical_and(same_head, earlier).astype(BF16)

    w_ref[...] = _dot_sel_r(x_ref[...], m_ref[...])


def _page_cumsum(logf_pages, *, tb=256):
    n_pool, n = logf_pages.shape
    return pl.pallas_call(
        _page_cumsum_body,
        grid=(n_pool // tb,),
        in_specs=[pl.BlockSpec((tb, n), lambda i: (i, 0))],
        out_specs=pl.BlockSpec((tb, n), lambda i: (i, 0)),
        out_shape=jax.ShapeDtypeStruct((n_pool, n), F32),
        scratch_shapes=[pltpu.VMEM((n, n), BF16)],
        compiler_params=_params("arbitrary"),
        name="page_cumsum",
    )(logf_pages)


def _seq_cumsum_body(seq_len, x_ref, o_ref):
    n = x_ref.shape[0]
    r = lax.broadcasted_iota(jnp.int32, (n, n), 0)
    c = lax.broadcasted_iota(jnp.int32, (n, n), 1)
    same_seq = (r // seq_len) == (c // seq_len)
    o_ref[...] = _dot_sel_l(jnp.logical_and(same_seq, c <= r).astype(BF16), x_ref[...])


def _seq_cumsum(x, seq_len, *, tb=128):
    t = x.shape[0]
    return pl.pallas_call(
        functools.partial(_seq_cumsum_body, seq_len),
        grid=(t // tb,),
        in_specs=[pl.BlockSpec((tb, LANES), lambda i: (i, 0))],
        out_specs=pl.BlockSpec((tb, LANES), lambda i: (i, 0)),
        out_shape=jax.ShapeDtypeStruct((t, LANES), F32),
        compiler_params=_params("parallel"),
        name="seq_cumsum",
    )(x)


def _fox_sample_body(pps, pt_ref, q_ref, kn_ref, vn_ref, cn_ref, *refs):
    kp_refs, vp_refs, w_refs = refs[:pps], refs[pps:2 * pps], refs[2 * pps:3 * pps]
    o_ref, q_scr, colq_ref, toff_ref, m_ref, l_ref, acc_ref = refs[3 * pps:]
    j = pl.program_id(1)
    nq = q_ref.shape[0]
    rows = nq * FOX_HEADS
    page_keys = PAGE_SIZE * FOX_HEADS
    row_id = lax.broadcasted_iota(jnp.int32, (rows, 1), 0)
    head_of_row = row_id & (FOX_HEADS - 1)
    query_of_row = lax.shift_right_logical(row_id, 3)

    @pl.when(j == 0)
    def _():
        q = q_ref[...].reshape(rows, HEAD_DIM).astype(BF16)
        q_scr[...] = q
        cn = cn_ref[0]
        key = lax.broadcasted_iota(jnp.int32, (1, rows), 1)
        colq = jnp.sum(jnp.where(key == row_id, cn, 0.0), axis=-1, keepdims=True)
        colq_ref[...] = colq
        toff_ref[...] = jnp.zeros_like(toff_ref)
        s = _dot_nt(q, kn_ref[...].reshape(rows, HEAD_DIM).astype(BF16)) + colq - cn
        valid = jnp.logical_and((key & (FOX_HEADS - 1)) == head_of_row,
                                lax.shift_right_logical(key, 3) <= query_of_row)
        s = jnp.where(valid, s, -jnp.inf)
        m = jnp.max(s, axis=-1, keepdims=True)
        p = jnp.exp(s - m)
        m_ref[...] = m
        l_ref[...] = jnp.sum(p, axis=-1, keepdims=True)
        acc_ref[...] = _dot(p.astype(BF16), vn_ref[...].reshape(rows, HEAD_DIM).astype(BF16))

    lane = lax.broadcasted_iota(jnp.int32, (1, LANES), 1)
    own_head = (lax.broadcasted_iota(jnp.int32, (1, page_keys), 1) & (FOX_HEADS - 1)) == head_of_row
    q = q_scr[...]
    colq = colq_ref[...]
    toff = toff_ref[...]
    tiles = []
    for kp_ref, w_ref in zip(kp_refs, w_refs):
        w = w_ref[0]
        last = jnp.where(lane == LANES - FOX_HEADS + head_of_row, w[:, page_keys - LANES:], 0.0)
        toff = toff + jnp.sum(last, axis=-1, keepdims=True)
        s = _dot_nt(q, kp_ref[0].reshape(page_keys, HEAD_DIM).astype(BF16))
        tiles.append(jnp.where(own_head, s + (colq + toff) - w, -jnp.inf))
    toff_ref[...] = toff
    m_old = m_ref[...]
    m = m_old
    for s in tiles:
        m = jnp.maximum(m, jnp.max(s, axis=-1, keepdims=True))
    alpha = jnp.exp(m_old - m)
    l = alpha * l_ref[...]
    acc = alpha * acc_ref[...]
    for s, vp_ref in zip(tiles, vp_refs):
        p = jnp.exp(s - m)
        l = l + jnp.sum(p, axis=-1, keepdims=True)
        acc = acc + _dot(p.astype(BF16), vp_ref[0].reshape(page_keys, HEAD_DIM).astype(BF16))
    m_ref[...] = m
    l_ref[...] = l
    acc_ref[...] = acc

    @pl.when(j == pl.num_programs(1) - 1)
    def _():
        o_ref[...] = (acc_ref[...] / l_ref[...]).reshape(nq, FOX_HEADS, HEAD_DIM)


def _fox_sample(page_table, q, k_new, v_new, c_new, k_pages, v_pages, w_pages, n_seq, nq, *, pages_per_step=16):
    n_pages = page_table.shape[1]
    rows = nq * FOX_HEADS
    pps = math.gcd(pages_per_step, n_pages)

    def page(i, ndim):
        return lambda b, j, pt: (pt[b * n_pages + (n_pages - 1 - j * pps - i)],) + (0,) * (ndim - 1)

    seq = pl.BlockSpec((nq, FOX_HEADS, HEAD_DIM), lambda b, j, pt: (b, 0, 0))
    kv_specs = [pl.BlockSpec((1, PAGE_SIZE, FOX_HEADS, HEAD_DIM), page(i, 4)) for i in range(pps)]
    w_specs = [pl.BlockSpec((1, 1, PAGE_SIZE * FOX_HEADS), page(i, 3)) for i in range(pps)]
    grid_spec = pltpu.PrefetchScalarGridSpec(
        num_scalar_prefetch=1,
        grid=(n_seq, n_pages // pps),
        in_specs=[seq, seq, seq, pl.BlockSpec((1, 1, rows), lambda b, j, pt: (b, 0, 0))]
        + kv_specs + kv_specs + w_specs,
        out_specs=seq,
        scratch_shapes=[
            pltpu.VMEM((rows, HEAD_DIM), BF16), pltpu.VMEM((rows, 1), F32), pltpu.VMEM((rows, 1), F32),
            pltpu.VMEM((rows, 1), F32), pltpu.VMEM((rows, 1), F32), pltpu.VMEM((rows, HEAD_DIM), F32),
        ],
    )
    return pl.pallas_call(
        functools.partial(_fox_sample_body, pps),
        grid_spec=grid_spec,
        out_shape=jax.ShapeDtypeStruct((n_seq * nq, FOX_HEADS, HEAD_DIM), F32),
        compiler_params=_params("parallel", "arbitrary"),
        name="fox_sample",
    )(page_table.reshape(-1), q, k_new, v_new, c_new, *([k_pages] * pps), *([v_pages] * pps), *([w_pages] * pps))


SSD_PAIRS = SSD_HEADS // 2
PAIRS_PER_GROUP = SSD_PAIRS // SSD_GROUPS


def _expander(width):
    n = SSD_HEADS * width
    h = lax.broadcasted_iota(jnp.int32, (LANES, n), 0)
    c = lax.broadcasted_iota(jnp.int32, (LANES, n), 1)
    return (lax.shift_right_logical(c, int(math.log2(width))) == h).astype(BF16)


def _ssd_local(xs, bm, cm, dt, alog, mask, tot_sel):
    n = xs.shape[0]
    lane = lax.broadcasted_iota(jnp.int32, (1, LANES), 1)
    dta = dt * jnp.where(lane < SSD_HEADS, -jnp.exp(alog), 0.0)
    e64 = _expander(SSD_HEAD_DIM)
    a_cum = _dot_sel_l(mask.astype(BF16), dta)
    a_cum_t = a_cum.T
    ac_exp = _dot_sel_r(a_cum, e64)
    if tot_sel is None:
        atot_exp = ac_exp[n - 1:n, :]
    else:
        atot_exp = _dot_sel_r(_dot_sel_l(tot_sel, dta), e64)
    ac_b = _dot_sel_r(a_cum, _expander(LANES))
    xdt = xs * _dot_sel_r(dt, e64)
    half = lax.broadcasted_iota(jnp.int32, (n, LANES), 1) < SSD_HEAD_DIM
    out = {
        "xdtw": xdt * jnp.exp(atot_exp - ac_exp),
        "eac": jnp.exp(ac_exp),
        "atot_exp": atot_exp,
        "ac_b": ac_b,
        "bg": [], "cg": [], "y_diag": [],
    }
    for g in range(SSD_GROUPS):
        gs = slice(g * SSD_STATE, (g + 1) * SSD_STATE)
        bg = bm[:, gs].astype(BF16)
        cg = cm[:, gs].astype(BF16)
        out["bg"].append(bg)
        out["cg"].append(cg)
        cb = _dot_nt(cg, bg)
        for k in range(g * PAIRS_PER_GROUP, (g + 1) * PAIRS_PER_GROUP):
            ps = slice(k * LANES, (k + 1) * LANES)
            ms = []
            for h in (2 * k, 2 * k + 1):
                seg = ac_b[:, h * LANES:(h + 1) * LANES] - a_cum_t[h:h + 1, :]
                ms.append(cb * jnp.exp(jnp.where(mask, seg, -jnp.inf)))
            m_cat = jnp.concatenate(ms, axis=1).astype(BF16)
            xp = xdt[:, ps]
            x_bd = jnp.concatenate([jnp.where(half, xp, 0.0), jnp.where(half, 0.0, xp)], axis=0).astype(BF16)
            out["y_diag"].append(_dot(m_cat, x_bd))
    return out


def _conv_ssd_sample_body(seq_len, xbc_ref, dt_ref, buf_ref, h0_ref, w_ref, b_ref, alog_ref, dexp_ref,
                          y_ref, hout_ref, cout_ref, xp_ref):
    n = xbc_ref.shape[0]
    n_seq = n // seq_len
    taps = CONV_WIDTH - 1
    base = 8 - taps

    x = xbc_ref[...]
    xp_ref[:, base:8, :] = buf_ref[...]
    xp_ref[:, 8:8 + seq_len, :] = x.reshape(n_seq, seq_len, CONV_DIM)
    acc = b_ref[...] + x * w_ref[taps:taps + 1, :]
    for j in range(taps):
        acc = acc + xp_ref[:, base + j:base + j + seq_len, :].reshape(n, CONV_DIM) * w_ref[j:j + 1, :]
    cout_ref[...] = xp_ref[:, 8 + seq_len - taps:8 + seq_len, :]
    conv = _silu(acc)
    xs = conv[:, :SSD_WIDTH]
    bm = conv[:, SSD_WIDTH:SSD_WIDTH + SSD_GROUPS * SSD_STATE]
    cm = conv[:, SSD_WIDTH + SSD_GROUPS * SSD_STATE:]

    r = lax.broadcasted_iota(jnp.int32, (n, n), 0)
    c = lax.broadcasted_iota(jnp.int32, (n, n), 1)
    same_seq = (r // seq_len) == (c // seq_len)
    loc = _ssd_local(xs, bm, cm, dt_ref[...], alog_ref[...], jnp.logical_and(same_seq, c <= r),
                     same_seq.astype(BF16))

    gw = PAIRS_PER_GROUP * LANES
    seq_of_col = lax.broadcasted_iota(jnp.int32, (1, n), 1) // seq_len
    decay_t = jnp.exp(loc["atot_exp"]).T
    for g in range(SSD_GROUPS):
        gr = slice(g * gw, (g + 1) * gw)
        h_prev = h0_ref[:, gr, :]
        z = _dot_nt(h_prev.reshape(n_seq * gw, SSD_STATE).astype(BF16), loc["cg"][g])
        y_off_t = jnp.zeros((gw, n), F32)
        for s in range(n_seq):
            y_off_t = y_off_t + jnp.where(seq_of_col == s, z[s * gw:(s + 1) * gw, :], 0.0)
        y_off = y_off_t.T * loc["eac"][:, gr]
        y = jnp.concatenate(loc["y_diag"][g * PAIRS_PER_GROUP:(g + 1) * PAIRS_PER_GROUP], axis=1)
        y_ref[:, gr] = y + y_off + xs[:, gr] * dexp_ref[:, gr]
        xw_t = loc["xdtw"][:, gr].T
        lhs = jnp.concatenate([jnp.where(seq_of_col == s, xw_t, 0.0) for s in range(n_seq)], axis=0)
        s_new = _dot(lhs.astype(BF16), loc["bg"][g])
        for s in range(n_seq):
            col = decay_t[gr, s * seq_len:s * seq_len + 1]
            hout_ref[s, gr, :] = h_prev[s] * col + s_new[s * gw:(s + 1) * gw, :]


def _conv_ssd_sample(xbc, dt, conv_buf, h0, conv_w, conv_b, alog, d_exp, n_seq, seq_len):
    assert seq_len == 8 and CONV_WIDTH - 1 <= seq_len
    tile = LANES
    ts = tile // seq_len
    const = lambda i: (0, 0)
    per_s = lambda i: (i, 0, 0)
    return pl.pallas_call(
        functools.partial(_conv_ssd_sample_body, seq_len),
        grid=(n_seq // ts,),
        in_specs=[
            pl.BlockSpec((tile, CONV_DIM), lambda i: (i, 0)),
            pl.BlockSpec((tile, LANES), lambda i: (i, 0)),
            pl.BlockSpec((ts, CONV_WIDTH - 1, CONV_DIM), per_s),
            pl.BlockSpec((ts, SSD_WIDTH, SSD_STATE), per_s),
            pl.BlockSpec((CONV_WIDTH, CONV_DIM), const),
            pl.BlockSpec((1, CONV_DIM), const),
            pl.BlockSpec((1, LANES), const),
            pl.BlockSpec((1, SSD_WIDTH), const),
        ],
        out_specs=[
            pl.BlockSpec((tile, SSD_WIDTH), lambda i: (i, 0)),
            pl.BlockSpec((ts, SSD_WIDTH, SSD_STATE), per_s),
            pl.BlockSpec((ts, CONV_WIDTH - 1, CONV_DIM), per_s),
        ],
        out_shape=[
            jax.ShapeDtypeStruct((n_seq * seq_len, SSD_WIDTH), F32),
            jax.ShapeDtypeStruct((n_seq, SSD_WIDTH, SSD_STATE), F32),
            jax.ShapeDtypeStruct((n_seq, CONV_WIDTH - 1, CONV_DIM), F32),
        ],
        scratch_shapes=[pltpu.VMEM((ts, 8 + seq_len, CONV_DIM), F32)],
        compiler_params=_params("parallel"),
        name="conv_ssd_sample",
    )(xbc, dt, conv_buf, h0, conv_w, conv_b, alog, d_exp)


def _conv_ssd_prompt_body(xbc_ref, dt_ref, buf_ref, h0_ref, w_ref, b_ref, alog_ref, dexp_ref,
                          y_ref, hout_ref, cout_ref, state_ref, xp_ref):
    c = pl.program_id(1)
    nc = pl.num_programs(1)
    tl = xbc_ref.shape[0]
    taps = CONV_WIDTH - 1
    base = 8 - taps

    @pl.when(c == 0)
    def _():
        state_ref[...] = h0_ref[0]
        xp_ref[base:8, :] = buf_ref[0]

    x = xbc_ref[...]
    xp_ref[8:8 + tl, :] = x
    acc = b_ref[...] + x * w_ref[taps:taps + 1, :]
    for j in range(taps):
        acc = acc + xp_ref[base + j:base + j + tl, :] * w_ref[j:j + 1, :]
    xp_ref[base:8, :] = x[tl - taps:, :]
    conv = _silu(acc)
    xs = conv[:, :SSD_WIDTH]
    bm = conv[:, SSD_WIDTH:SSD_WIDTH + SSD_GROUPS * SSD_STATE]
    cm = conv[:, SSD_WIDTH + SSD_GROUPS * SSD_STATE:]

    causal = _tri(tl)
    loc = _ssd_local(xs, bm, cm, dt_ref[...], alog_ref[...], causal, None)
    top = lax.broadcasted_iota(jnp.int32, (tl, LANES), 0) < SSD_HEAD_DIM
    for k in range(SSD_PAIRS):
        g = k // PAIRS_PER_GROUP
        ps = slice(k * LANES, (k + 1) * LANES)
        s_prev = state_ref[ps, :]
        y_off = _dot_nt(loc["cg"][g], s_prev.astype(BF16)) * loc["eac"][:, ps]
        y_ref[:, ps] = loc["y_diag"][k] + y_off + xs[:, ps] * dexp_ref[:, ps]
        cd = [jnp.exp(loc["ac_b"][tl - 1:tl, h * LANES:(h + 1) * LANES]) for h in (2 * k, 2 * k + 1)]
        state_ref[ps, :] = s_prev * jnp.where(top, cd[0], cd[1]) + _dot_tn(loc["xdtw"][:, ps].astype(BF16), loc["bg"][g])

    @pl.when(c == nc - 1)
    def _():
        hout_ref[0] = state_ref[...]
        cout_ref[0] = xp_ref[base:8, :]


def _conv_ssd_prompt(xbc, dt, conv_buf, h0, conv_w, conv_b, alog, d_exp, b, l):
    tl = SSD_CHUNK
    nc = l // tl
    const = lambda bi, c: (0, 0)
    per_b = lambda bi, c: (bi, 0, 0)
    return pl.pallas_call(
        _conv_ssd_prompt_body,
        grid=(b, nc),
        in_specs=[
            pl.BlockSpec((tl, CONV_DIM), lambda bi, c: (bi * nc + c, 0)),
            pl.BlockSpec((tl, LANES), lambda bi, c: (bi * nc + c, 0)),
            pl.BlockSpec((1, CONV_WIDTH - 1, CONV_DIM), per_b),
            pl.BlockSpec((1, SSD_WIDTH, SSD_STATE), per_b),
            pl.BlockSpec((CONV_WIDTH, CONV_DIM), const),
            pl.BlockSpec((1, CONV_DIM), const),
            pl.BlockSpec((1, LANES), const),
            pl.BlockSpec((1, SSD_WIDTH), const),
        ],
        out_specs=[
            pl.BlockSpec((tl, SSD_WIDTH), lambda bi, c: (bi * nc + c, 0)),
            pl.BlockSpec((1, SSD_WIDTH, SSD_STATE), per_b),
            pl.BlockSpec((1, CONV_WIDTH - 1, CONV_DIM), per_b),
        ],
        out_shape=[
            jax.ShapeDtypeStruct((b * l, SSD_WIDTH), F32),
            jax.ShapeDtypeStruct((b, SSD_WIDTH, SSD_STATE), F32),
            jax.ShapeDtypeStruct((b, CONV_WIDTH - 1, CONV_DIM), F32),
        ],
        scratch_shapes=[pltpu.VMEM((SSD_WIDTH, SSD_STATE), F32), pltpu.VMEM((8 + tl, CONV_DIM), F32)],
        compiler_params=_params("parallel", "arbitrary"),
        name="conv_ssd_prompt",
    )(xbc, dt, conv_buf, h0, conv_w, conv_b, alog, d_exp)


def _out_proj_body(h_ref, fox_ref, y_ref, z_ref, gs_ref, w_ref, gx_ref, wq_ref, gq_ref, o_ref, q_ref):
    yn = _rms(y_ref[...] * _silu(z_ref[...]), gs_ref[...]).astype(BF16)
    fw = fox_ref.shape[1]
    h = h_ref[...] + _dot(fox_ref[...].astype(BF16), w_ref[:fw, :]) + _dot(yn, w_ref[fw:, :])
    o_ref[...] = h
    q = _dot(_rms(h, gx_ref[...]).astype(BF16), wq_ref[...])
    q_ref[...] = _head_norm(q, gq_ref[...]) * (HEAD_DIM ** -0.5)


def _out_proj(h, fox, y, z, g_ssd, w_out, g_x, wq, gq, *, tm=512):
    t, d = h.shape
    tm = min(tm, t)
    row = lambda w: pl.BlockSpec((tm, w), lambda i: (i, 0))
    full = lambda a: pl.BlockSpec(a.shape, lambda i: (0, 0), pipeline_mode=pl.Buffered(1))
    consts = [g_ssd.reshape(1, SSD_WIDTH), w_out, g_x.reshape(1, d), wq, gq.reshape(1, HEAD_DIM)]
    return pl.pallas_call(
        _out_proj_body,
        grid=(t // tm,),
        in_specs=[row(d), row(FOX_WIDTH), row(SSD_WIDTH), row(SSD_WIDTH)] + [full(a) for a in consts],
        out_specs=[row(d), row(XATTN_WIDTH)],
        out_shape=[jax.ShapeDtypeStruct((t, d), F32), jax.ShapeDtypeStruct((t, XATTN_WIDTH), F32)],
        compiler_params=_params("parallel"),
        name="out_proj",
    )(h, fox, y, z, *consts)


def _mem_kv_body(m_ref, g_ref, w_ref, gk_ref, k_ref, v_ref):
    kv = _dot(_rms(m_ref[...], g_ref[...]).astype(BF16), w_ref[...])
    k_ref[...] = _head_norm(kv[:, :XATTN_WIDTH], gk_ref[...])
    v_ref[...] = kv[:, XATTN_WIDTH:]


def _mem_kv(mem, g, w_kv, gk, *, tm=256):
    t, d = mem.shape
    row = lambda w: pl.BlockSpec((tm, w), lambda i: (i, 0))
    full = lambda a: pl.BlockSpec(a.shape, lambda i: (0, 0))
    consts = [g.reshape(1, d), w_kv, gk.reshape(1, HEAD_DIM)]
    return pl.pallas_call(
        _mem_kv_body,
        grid=(t // tm,),
        in_specs=[row(d)] + [full(a) for a in consts],
        out_specs=[row(XATTN_WIDTH), row(XATTN_WIDTH)],
        out_shape=[jax.ShapeDtypeStruct((t, XATTN_WIDTH), F32)] * 2,
        compiler_params=_params("parallel"),
        name="mem_kv",
    )(mem, *consts)


def _xattn_body(q_ref, k_ref, v_ref, o_ref):
    for h in range(XATTN_HEADS):
        hs = slice(h * HEAD_DIM, (h + 1) * HEAD_DIM)
        s = _dot_nt(q_ref[:, hs].astype(BF16), k_ref[0, :, hs].astype(BF16))
        p = jnp.exp(s - jnp.max(s, axis=-1, keepdims=True))
        o = _dot(p.astype(BF16), v_ref[0, :, hs].astype(BF16))
        o_ref[:, hs] = o / jnp.sum(p, axis=-1, keepdims=True)


def _xattn(q, mem_k, mem_v, b, l, *, tq=512):
    tq = min(tq, l)
    nq = l // tq
    n_mem = mem_k.shape[1]
    qspec = pl.BlockSpec((tq, XATTN_WIDTH), lambda bi, i: (bi * nq + i, 0))
    mspec = pl.BlockSpec((1, n_mem, XATTN_WIDTH), lambda bi, i: (bi, 0, 0))
    return pl.pallas_call(
        _xattn_body,
        grid=(b, nq),
        in_specs=[qspec, mspec, mspec],
        out_specs=qspec,
        out_shape=jax.ShapeDtypeStruct((b * l, XATTN_WIDTH), F32),
        compiler_params=_params("parallel", "arbitrary"),
        name="xattn",
    )(q, mem_k, mem_v)


def _xattn_rows_body(n_seq, q_ref, k_ref, v_ref, o_ref):
    tiles = q_ref.shape[0] // n_seq
    rows = tiles * 8
    keys = k_ref.shape[1] * k_ref.shape[2]
    row_head = lax.broadcasted_iota(jnp.int32, (rows, 1), 0) & (XATTN_HEADS - 1)
    key_head = lax.broadcasted_iota(jnp.int32, (1, keys), 1) & (XATTN_HEADS - 1)
    own_head = row_head == key_head
    for s in range(n_seq):
        qs = slice(s * tiles, (s + 1) * tiles)
        q = q_ref[qs].reshape(rows, HEAD_DIM).astype(BF16)
        sc = jnp.where(own_head, _dot_nt(q, k_ref[s].reshape(keys, HEAD_DIM).astype(BF16)), -jnp.inf)
        p = jnp.exp(sc - jnp.max(sc, axis=-1, keepdims=True))
        o = _dot(p.astype(BF16), v_ref[s].reshape(keys, HEAD_DIM).astype(BF16))
        o_ref[qs] = (o / jnp.sum(p, axis=-1, keepdims=True)).reshape(tiles, 8, HEAD_DIM)


def _xattn_rows(q, mem_k, mem_v, n_seq, *, ts=8):
    tiles = q.shape[0] // n_seq
    ts = math.gcd(ts, n_seq)
    qspec = pl.BlockSpec((ts * tiles, 8, HEAD_DIM), lambda i: (i, 0, 0))
    mspec = pl.BlockSpec((ts,) + mem_k.shape[1:], lambda i: (i, 0, 0, 0))
    return pl.pallas_call(
        functools.partial(_xattn_rows_body, ts),
        grid=(n_seq // ts,),
        in_specs=[qspec, mspec, mspec],
        out_specs=qspec,
        out_shape=jax.ShapeDtypeStruct(q.shape, F32),
        compiler_params=_params("parallel"),
        name="xattn_rows",
    )(q, mem_k, mem_v)


def kernel(x_prompt, x_sample, cache_fox_k, cache_fox_v, cache_fox_logf, cache_mem_k, cache_mem_v, state_ssm, state_conv, page_table, mem_prompt, ffn1_norm, ffn1_w_gate, ffn1_w_up, ffn1_w_down, mix_norm, w_in, fox_b_f, fox_q_norm, fox_k_norm, conv_w, conv_b, ssd_dt_bias, ssd_A_log, ssd_D, ssd_out_norm, w_out, xattn_norm, mem_norm, xattn_w_q, xattn_w_kv, xattn_q_norm, xattn_k_norm, xattn_w_o, ffn2_norm, ffn2_w_gate, ffn2_w_up, ffn2_w_down):
    assert x_prompt.shape[2] == D_MODEL and ffn1_norm.shape[0] == 1
    d = D_MODEL
    bp, lp = x_prompt.shape[:2]
    bs, ls = x_sample.shape[:2]
    n_mem = mem_prompt.shape[1]
    n_pool = cache_fox_k.shape[1]

    bf = lambda w: w[0].astype(BF16)
    ffn1 = (ffn1_norm[0], bf(ffn1_w_gate), bf(ffn1_w_up), bf(ffn1_w_down))
    ffn2 = (ffn2_norm[0], bf(ffn2_w_gate), bf(ffn2_w_up), bf(ffn2_w_down))
    in_w = _pack_in_proj(w_in[0], fox_b_f[0], ssd_dt_bias[0])
    wq, wkv, wo = bf(xattn_w_q), bf(xattn_w_kv), bf(xattn_w_o)
    alog = jnp.pad(ssd_A_log[0], (0, LANES - SSD_HEADS)).reshape(1, LANES)
    d_exp = jnp.repeat(ssd_D[0], SSD_HEAD_DIM).reshape(1, SSD_WIDTH)
    cw, cb = conv_w[0], conv_b[0].reshape(1, CONV_DIM)

    def front(x, q_dtype, q_scale):
        h1 = _ffn(x, *ffn1)
        return h1, _in_proj(h1, mix_norm[0], *in_w, fox_q_norm[0], fox_k_norm[0], q_dtype=q_dtype, q_scale=q_scale)

    def back(h1, fox, y, z, attend):
        h2, xq = _out_proj(h1, fox, y, z, ssd_out_norm[0], bf(w_out), xattn_norm[0], wq, xattn_q_norm[0])
        return _ffn(h2, *ffn2, pre=(attend(xq), wo))

    h1, (q, k_p, kb, v_p, vb, z, xbc, logf_p, dt) = front(x_prompt.reshape(bp * lp, d), BF16, HEAD_DIM ** -0.5 * LOG2E)
    ccol, crow = _cumsum(logf_p.reshape(bp, lp, LANES))
    tk = min(FOX_KEY_BLOCK, lp)
    fox = _fox_prompt(q, kb, vb, ccol, crow.reshape(bp, FOX_HEADS, lp // tk, tk), bp, lp, tq=FOX_QUERY_BLOCK)
    y, ssm_p, conv_p = _conv_ssd_prompt(
        xbc, dt, jnp.zeros((bp, CONV_WIDTH - 1, CONV_DIM), F32), jnp.zeros((bp, SSD_WIDTH, SSD_STATE), F32),
        cw, cb, alog, d_exp, bp, lp)
    mk, mv = _mem_kv(mem_prompt.reshape(bp * n_mem, d), mem_norm[0], wkv, xattn_k_norm[0])
    y_prompt = back(h1, fox, y, z, lambda xq: _xattn(
        xq, mk.reshape(bp, n_mem, XATTN_WIDTH), mv.reshape(bp, n_mem, XATTN_WIDTH), bp, lp))

    h1, (q, k_s, _, v_s, _, z, xbc, logf_s, dt) = front(x_sample.reshape(bs * ls, d), F32, HEAD_DIM ** -0.5)
    w_pages = _page_cumsum(cache_fox_logf[0].reshape(n_pool, PAGE_SIZE * FOX_HEADS))
    c_new = _seq_cumsum(logf_s, ls)[:, :FOX_HEADS].reshape(bs, 1, ls * FOX_HEADS)
    fox = _fox_sample(
        page_table, q.reshape(bs * ls, FOX_HEADS, HEAD_DIM), k_s, v_s, c_new, cache_fox_k[0], cache_fox_v[0],
        w_pages.reshape(n_pool, 1, PAGE_SIZE * FOX_HEADS), bs, ls).reshape(bs * ls, FOX_WIDTH)
    y, ssm_s, conv_s = _conv_ssd_sample(
        xbc, dt, state_conv[0], state_ssm[0].reshape(bs, SSD_WIDTH, SSD_STATE), cw, cb, alog, d_exp, bs, ls)
    mem_tiles = lambda m: m[0].reshape(bs, n_mem * XATTN_HEADS // 8, 8, HEAD_DIM)
    y_sample = back(h1, fox, y, z, lambda xq: _xattn_rows(
        xq.reshape(bs * ls * XATTN_HEADS // 8, 8, HEAD_DIM), mem_tiles(cache_mem_k), mem_tiles(cache_mem_v), bs,
    ).reshape(bs * ls, XATTN_WIDTH))

    fox_shape = lambda b, l: (1, b, l, FOX_HEADS, HEAD_DIM)
    ssm_shape = lambda b: (1, b, SSD_HEADS, SSD_HEAD_DIM, SSD_STATE)
    mem_shape = (1, bp, n_mem, XATTN_HEADS, HEAD_DIM)
    return (
        y_prompt.reshape(bp, lp, d), y_sample.reshape(bs, ls, d),
        k_p.reshape(fox_shape(bp, lp)), v_p.reshape(fox_shape(bp, lp)),
        logf_p[:, :FOX_HEADS].reshape(1, bp, lp, FOX_HEADS),
        ssm_p.reshape(ssm_shape(bp)), conv_p[None], mk.reshape(mem_shape), mv.reshape(mem_shape),
        k_s.reshape(fox_shape(bs, ls)), v_s.reshape(fox_shape(bs, ls)),
        logf_s[:, :FOX_HEADS].reshape(1, bs, ls, FOX_HEADS),
        ssm_s.reshape(ssm_shape(bs)), conv_s[None],
    )
```

```python
import functools
import math

import jax
import jax.numpy as jnp
from jax import lax
from jax.experimental import pallas as pl
from jax.experimental.pallas import tpu as pltpu

F32 = jnp.float32
BF16 = jnp.bfloat16

EPS = 1e-6
LOG2E = math.log2(math.e)
FFN_RESIDUAL = 0.5
D_MODEL = 2048
D_FF = 5632
PAGE_SIZE = 128
FOX_HEADS = 8
HEAD_DIM = 128
FOX_WIDTH = FOX_HEADS * HEAD_DIM
SSD_HEADS = 16
SSD_HEAD_DIM = 64
SSD_WIDTH = SSD_HEADS * SSD_HEAD_DIM
SSD_GROUPS = 2
SSD_STATE = 128
SSD_CHUNK = 128
CONV_WIDTH = 4
CONV_DIM = SSD_WIDTH + 2 * SSD_GROUPS * SSD_STATE
XATTN_HEADS = 4
XATTN_WIDTH = XATTN_HEADS * HEAD_DIM
LANES = 128
VMEM_LIMIT_BYTES = 56 * 1024 * 1024

def _params(*semantics):
    return pltpu.CompilerParams(dimension_semantics=semantics, vmem_limit_bytes=VMEM_LIMIT_BYTES)


def _rms(x, g):
    return x * lax.rsqrt(jnp.mean(x * x, axis=-1, keepdims=True) + EPS) * g


def _dot(a, b):
    return jnp.dot(a, b, preferred_element_type=F32)


def _dot_nt(a, b):
    return lax.dot_general(a, b, (((1,), (1,)), ((), ())), preferred_element_type=F32)


def _dot_tn(a, b):
    return lax.dot_general(a, b, (((0,), (0,)), ((), ())), preferred_element_type=F32)


def _split3(x):
    x1 = x.astype(BF16)
    r = x - x1.astype(F32)
    x2 = r.astype(BF16)
    x3 = (r - x2.astype(F32)).astype(BF16)
    return x1, x2, x3


def _dot_sel_l(sel, x):
    x1, x2, x3 = _split3(x)
    return _dot(sel, x1) + _dot(sel, x2) + _dot(sel, x3)


def _dot_sel_r(x, sel):
    x1, x2, x3 = _split3(x)
    return _dot(x1, sel) + _dot(x2, sel) + _dot(x3, sel)


def _silu(x):
    return x * jax.nn.sigmoid(x)


def _ffn_body(pre_proj, split, *refs):
    if pre_proj:
        x_ref, a_ref, wo_ref, g_ref, *w_refs, o_ref, xn_ref = refs
    else:
        x_ref, g_ref, *w_refs, o_ref, xn_ref = refs

    @pl.when(pl.program_id(1) == 0)
    def _():
        x = x_ref[...]
        if pre_proj:
            x = x + _dot(a_ref[...].astype(BF16), wo_ref[...])
        xn_ref[...] = _rms(x, g_ref[...]).astype(BF16)
        o_ref[...] = x

    xn = xn_ref[...]
    acc = None
    for wg_ref, wu_ref, wd_ref in zip(w_refs[:split], w_refs[split:2 * split], w_refs[2 * split:]):
        gate = _dot(xn, wg_ref[...].astype(BF16))
        up = _dot(xn, wu_ref[...].astype(BF16))
        h = (_silu(gate) * up * FFN_RESIDUAL).astype(BF16)
        part = _dot(h, wd_ref[...].astype(BF16))
        acc = part if acc is None else acc + part
    o_ref[...] += acc


def _ffn(x, g, wg, wu, wd, pre=None, *, tm=1024, tf=256, split=1):
    t, d = x.shape
    f = wg.shape[1]
    tm = min(tm, t)
    ts = tf // split
    grid = (t // tm, f // tf)
    row = lambda i, j: (i, 0)
    in_specs = [pl.BlockSpec((tm, d), row)]
    args = [x]
    if pre is not None:
        a, wo = pre
        in_specs = [pl.BlockSpec((tm, d), row, pipeline_mode=pl.Buffered(1)),
                    pl.BlockSpec((tm, a.shape[1]), row), pl.BlockSpec(wo.shape, lambda i, j: (0, 0))]
        args += [a, wo]
    cols = [pl.BlockSpec((d, ts), lambda i, j, s=s: (0, j * split + s)) for s in range(split)]
    rows = [pl.BlockSpec((ts, d), lambda i, j, s=s: (j * split + s, 0)) for s in range(split)]
    in_specs += [pl.BlockSpec((1, d), lambda i, j: (0, 0))] + cols + cols + rows
    args += [g.reshape(1, d)] + [wg] * split + [wu] * split + [wd] * split
    return pl.pallas_call(
        functools.partial(_ffn_body, pre is not None, split),
        grid=grid,
        in_specs=in_specs,
        out_specs=pl.BlockSpec((tm, d), row),
        out_shape=jax.ShapeDtypeStruct((t, d), F32),
        scratch_shapes=[pltpu.VMEM((tm, d), BF16)],
        compiler_params=_params("parallel", "arbitrary"),
        name="ffn_pre" if pre is not None else "ffn",
    )(*args)


IN_QKV_TN = 1024
IN_ZX_TN = 512


def _head_norm(y, g):
    outs = []
    for c in range(y.shape[1] // HEAD_DIM):
        yc = y[:, c * HEAD_DIM:(c + 1) * HEAD_DIM]
        outs.append(yc * lax.rsqrt(jnp.mean(yc * yc, axis=-1, keepdims=True) + EPS) * g)
    return jnp.concatenate(outs, axis=1)


def _in_qkv_body(q_scale, x_ref, g_ref, w_ref, ws_ref, bs_ref, gq_ref, gk_ref,
                 q_ref, k_ref, kb_ref, v_ref, vb_ref, logf_ref, dt_ref, u_ref):
    j = pl.program_id(1)

    @pl.when(j == 0)
    def _():
        u = _rms(x_ref[...], g_ref[...]).astype(BF16)
        u_ref[...] = u
        s = _dot(u, ws_ref[...]) + bs_ref[...]
        t = jnp.log1p(jnp.exp(-jnp.abs(s)))
        logf_ref[...] = (jnp.minimum(s, 0.0) - t)[:, :LANES]
        dt_ref[...] = (jnp.maximum(s, 0.0) + t)[:, LANES:]
        q_ref[...] = (_head_norm(_dot(u, w_ref[...]), gq_ref[...]) * q_scale).astype(q_ref.dtype)

    def by_head(y, out_ref, bf_ref):
        bf_ref[...] = y.astype(BF16)
        for h in range(FOX_HEADS):
            out_ref[:, h, :] = y[:, h * HEAD_DIM:(h + 1) * HEAD_DIM]

    @pl.when(j == 1)
    def _():
        by_head(_head_norm(_dot(u_ref[...], w_ref[...]), gk_ref[...]), k_ref, kb_ref)

    @pl.when(j == 2)
    def _():
        by_head(_dot(u_ref[...], w_ref[...]), v_ref, vb_ref)


def _in_qkv(x, g, w_qkv, w_small, b_small, gq, gk, *, q_dtype, q_scale, tm=512):
    t, d = x.shape
    tm = min(tm, t)
    tn = IN_QKV_TN
    assert tn == FOX_WIDTH and w_qkv.shape[1] >= 3 * tn
    const = lambda i, j: (0, 0)
    rows = lambda w: pl.BlockSpec((tm, w), lambda i, j: (i, 0))
    by_head = pl.BlockSpec((tm, FOX_HEADS, HEAD_DIM), lambda i, j: (i, 0, 0))
    by_head_shape = jax.ShapeDtypeStruct((t, FOX_HEADS, HEAD_DIM), F32)
    flat = lambda dt, w: jax.ShapeDtypeStruct((t, w), dt)
    return pl.pallas_call(
        functools.partial(_in_qkv_body, q_scale),
        grid=(t // tm, 3),
        in_specs=[
            rows(d),
            pl.BlockSpec((1, d), const),
            pl.BlockSpec((d, tn), lambda i, j: (0, j)),
            pl.BlockSpec((d, 2 * LANES), const),
            pl.BlockSpec((1, 2 * LANES), const),
            pl.BlockSpec((1, HEAD_DIM), const),
            pl.BlockSpec((1, HEAD_DIM), const),
        ],
        out_specs=[rows(FOX_WIDTH), by_head, rows(FOX_WIDTH), by_head, rows(FOX_WIDTH), rows(LANES), rows(LANES),
                   rows(d)],
        out_shape=[flat(q_dtype, FOX_WIDTH), by_head_shape, flat(BF16, FOX_WIDTH), by_head_shape,
                   flat(BF16, FOX_WIDTH), flat(F32, LANES), flat(F32, LANES), flat(BF16, d)],
        compiler_params=_params("parallel", "arbitrary"),
        name="in_qkv",
    )(x, g.reshape(1, d), w_qkv, w_small, b_small, gq.reshape(1, HEAD_DIM), gk.reshape(1, HEAD_DIM))


def _in_zx_body(u_ref, wa_ref, wb_ref, z_ref, xbc_ref):
    j = pl.program_id(1)
    half = wa_ref.shape[1]
    tn = 2 * half
    nz = z_ref.shape[1] // tn
    for c in range(nz + xbc_ref.shape[1] // tn):
        out_ref, sub = (z_ref, c) if c < nz else (xbc_ref, c - nz)

        @pl.when(j == c)
        def _(out_ref=out_ref, sub=sub):
            out_ref[:, sub * tn:sub * tn + half] = _dot(u_ref[...], wa_ref[...])
            out_ref[:, sub * tn + half:(sub + 1) * tn] = _dot(u_ref[...], wb_ref[...])


def _in_zx(u, w_zx, *, tm=1024):
    t, d = u.shape
    tm = min(tm, t)
    tn = IN_ZX_TN
    rows = lambda w: pl.BlockSpec((tm, w), lambda i, j: (i, 0))
    return pl.pallas_call(
        _in_zx_body,
        grid=(t // tm, w_zx.shape[1] // tn),
        in_specs=[rows(d), pl.BlockSpec((d, tn // 2), lambda i, j: (0, 2 * j)),
                  pl.BlockSpec((d, tn // 2), lambda i, j: (0, 2 * j + 1))],
        out_specs=[rows(SSD_WIDTH), rows(CONV_DIM)],
        out_shape=[jax.ShapeDtypeStruct((t, SSD_WIDTH), F32), jax.ShapeDtypeStruct((t, CONV_DIM), F32)],
        compiler_params=_params("parallel", "arbitrary"),
        name="in_zx",
    )(u, w_zx, w_zx)


def _in_proj(x, g, w_qkv, w_zx, w_small, b_small, gq, gk, *, q_dtype, q_scale):
    q, k, kb, v, vb, logf, dt, u = _in_qkv(x, g, w_qkv, w_small, b_small, gq, gk, q_dtype=q_dtype, q_scale=q_scale)
    z, xbc = _in_zx(u, w_zx)
    return q, k, kb, v, vb, z, xbc, logf, dt


def _pack_in_proj(w_in, fox_b_f, ssd_dt_bias):
    fw = FOX_WIDTH
    f0 = 3 * fw
    z0 = f0 + FOX_HEADS
    x0 = z0 + SSD_WIDTH
    d0 = x0 + CONV_DIM
    w_in = w_in.astype(BF16)
    w_qkv = w_in
    w_zx = w_in[:, z0:d0]
    zeros = lambda n: jnp.zeros((w_in.shape[0], n), BF16)
    w_small = jnp.concatenate(
        [w_in[:, f0:z0], zeros(LANES - FOX_HEADS), w_in[:, d0:], zeros(LANES - SSD_HEADS)], axis=1)
    b_small = jnp.concatenate(
        [fox_b_f, jnp.zeros((LANES - FOX_HEADS,), F32), ssd_dt_bias, jnp.zeros((LANES - SSD_HEADS,), F32)]
    ).reshape(1, 2 * LANES)
    return w_qkv, w_zx, w_small, b_small


ATT_BLOCK = 256
FOX_QUERY_BLOCK = 256
FOX_KEY_BLOCK = 256


def _tri(n, *, strict=False, upper=False):
    r = lax.broadcasted_iota(jnp.int32, (n, n), 0)
    c = lax.broadcasted_iota(jnp.int32, (n, n), 1)
    if upper:
        r, c = c, r
    return (c < r) if strict else (c <= r)


def _cumsum_body(x_ref, col_ref, row_ref, carry_ref):
    @pl.when(pl.program_id(1) == 0)
    def _():
        carry_ref[...] = jnp.zeros_like(carry_ref)

    n = x_ref.shape[1]
    tril = _tri(n).astype(BF16)
    c = _dot_sel_l(tril, x_ref[0]) + carry_ref[...]
    carry_ref[...] = c[n - 1:n, :]
    c = c * LOG2E
    col_ref[0] = c
    row_ref[0] = c.T[:FOX_HEADS, :]


def _cumsum(x, *, tb=ATT_BLOCK):
    b, l, _ = x.shape
    return pl.pallas_call(
        _cumsum_body,
        grid=(b, l // tb),
        in_specs=[pl.BlockSpec((1, tb, LANES), lambda i, j: (i, j, 0))],
        out_specs=[pl.BlockSpec((1, tb, LANES), lambda i, j: (i, j, 0)),
                   pl.BlockSpec((1, FOX_HEADS, tb), lambda i, j: (i, 0, j))],
        out_shape=[jax.ShapeDtypeStruct((b, l, LANES), F32), jax.ShapeDtypeStruct((b, FOX_HEADS, l), F32)],
        scratch_shapes=[pltpu.VMEM((1, LANES), F32)],
        compiler_params=_params("parallel", "arbitrary"),
        name="logf_cumsum",
    )(x)


def _fox_prompt_body(tk, q_ref, k_ref, v_ref, ccol_ref, crow_ref, o_ref, m_ref, l_ref, cq_ref, acc_ref):
    i = pl.program_id(1)
    tq = q_ref.shape[0]
    rep = tk // LANES
    m_ref[...] = jnp.full_like(m_ref, -jnp.inf)
    l_ref[...] = jnp.zeros_like(l_ref)
    acc_ref[...] = jnp.zeros_like(acc_ref)
    for h in range(FOX_HEADS):
        cq_ref[h] = jnp.broadcast_to(ccol_ref[0, :, h:h + 1], (tq, LANES))
    row = i * tq + lax.broadcasted_iota(jnp.int32, (tq, tk), 0)
    col = lax.broadcasted_iota(jnp.int32, (tq, tk), 1)
    wide = lambda x: jnp.concatenate([x] * rep, axis=1)

    def block(j, masked):
        ks = pl.ds(pl.multiple_of(j * tk, tk), tk)
        for h in range(FOX_HEADS):
            hs = slice(h * HEAD_DIM, (h + 1) * HEAD_DIM)
            s = _dot_nt(q_ref[:, hs], k_ref[ks, hs]) + (wide(cq_ref[h]) - crow_ref[0, h, pl.ds(j, 1), :])
            if masked:
                s = jnp.where(col + j * tk <= row, s, -jnp.inf)
            m_old = m_ref[h]
            m_new = jnp.maximum(m_old, jnp.max(s, axis=-1, keepdims=True))
            alpha = jnp.exp2(m_old - m_new)
            p = jnp.exp2(s - wide(m_new))
            m_ref[h] = m_new
            l_ref[h] = alpha * l_ref[h] + jnp.sum(p, axis=-1, keepdims=True)
            acc_ref[h] = alpha * acc_ref[h] + _dot(p.astype(BF16), v_ref[ks, hs])

    first_masked = (i * tq) // tk

    def step(j, carry):
        block(j, False)
        return carry

    lax.fori_loop(0, first_masked, step, 0)
    for extra in range(max(tq // tk, 1)):
        block(first_masked + extra, True)
    for h in range(FOX_HEADS):
        o_ref[:, h * HEAD_DIM:(h + 1) * HEAD_DIM] = (acc_ref[h] / l_ref[h]).astype(o_ref.dtype)


def _fox_prompt(q, k, v, ccol, crow, b, l, *, tq=128):
    tk = crow.shape[-1]
    tq = min(tq, l)
    assert tk % tq == 0 or tq % tk == 0
    nq = l // tq
    w = FOX_WIDTH
    return pl.pallas_call(
        functools.partial(_fox_prompt_body, tk),
        grid=(b, nq),
        in_specs=[
            pl.BlockSpec((tq, w), lambda bi, i: (bi * nq + i, 0)),
            pl.BlockSpec((l, w), lambda bi, i: (bi, 0)),
            pl.BlockSpec((l, w), lambda bi, i: (bi, 0)),
            pl.BlockSpec((1, tq, LANES), lambda bi, i: (bi, i, 0)),
            pl.BlockSpec((1, FOX_HEADS, l // tk, tk), lambda bi, i: (bi, 0, 0, 0)),
        ],
        out_specs=pl.BlockSpec((tq, w), lambda bi, i: (bi * nq + i, 0)),
        out_shape=jax.ShapeDtypeStruct((b * l, w), BF16),
        scratch_shapes=[pltpu.VMEM((FOX_HEADS, tq, LANES), F32)] * 3 + [pltpu.VMEM((FOX_HEADS, tq, HEAD_DIM), F32)],
        compiler_params=_params("parallel", "arbitrary"),
        name="fox_prompt",
    )(q, k, v, ccol, crow)


def _page_cumsum_body(x_ref, w_ref, m_ref):
    n = PAGE_SIZE * FOX_HEADS

    @pl.when(pl.program_id(0) == 0)
    def _():
        r = lax.broadcasted_iota(jnp.int32, (n, n), 0)
        c = lax.broadcasted_iota(jnp.int32, (n, n), 1)
        same_head = (r & (FOX_HEADS - 1)) == (c & (FOX_HEADS - 1))
        earlier = lax.shift_right_logical(r, 3) <= lax.shift_right_logical(c, 3)
        m_ref[...] = jnp.logical_and(same_head, earlier).astype(BF16)

    w_ref[...] = _dot_sel_r(x_ref[...], m_ref[...])


def _page_cumsum(logf_pages, *, tb=256):
    n_pool, n = logf_pages.shape
    return pl.pallas_call(
        _page_cumsum_body,
        grid=(n_pool // tb,),
        in_specs=[pl.BlockSpec((tb, n), lambda i: (i, 0))],
        out_specs=pl.BlockSpec((tb, n), lambda i: (i, 0)),
        out_shape=jax.ShapeDtypeStruct((n_pool, n), F32),
        scratch_shapes=[pltpu.VMEM((n, n), BF16)],
        compiler_params=_params("arbitrary"),
        name="page_cumsum",
    )(logf_pages)


def _seq_cumsum_body(seq_len, x_ref, o_ref):
    n = x_ref.shape[0]
    r = lax.broadcasted_iota(jnp.int32, (n, n), 0)
    c = lax.broadcasted_iota(jnp.int32, (n, n), 1)
    same_seq = (r // seq_len) == (c // seq_len)
    o_ref[...] = _dot_sel_l(jnp.logical_and(same_seq, c <= r).astype(BF16), x_ref[...])


def _seq_cumsum(x, seq_len, *, tb=128):
    t = x.shape[0]
    return pl.pallas_call(
        functools.partial(_seq_cumsum_body, seq_len),
        grid=(t // tb,),
        in_specs=[pl.BlockSpec((tb, LANES), lambda i: (i, 0))],
        out_specs=pl.BlockSpec((tb, LANES), lambda i: (i, 0)),
        out_shape=jax.ShapeDtypeStruct((t, LANES), F32),
        compiler_params=_params("parallel"),
        name="seq_cumsum",
    )(x)


def _fox_sample_body(pps, pt_ref, q_ref, kn_ref, vn_ref, cn_ref, *refs):
    kp_refs, vp_refs, w_refs = refs[:pps], refs[pps:2 * pps], refs[2 * pps:3 * pps]
    o_ref, q_scr, colq_ref, toff_ref, m_ref, l_ref, acc_ref = refs[3 * pps:]
    j = pl.program_id(1)
    nq = q_ref.shape[0]
    rows = nq * FOX_HEADS
    page_keys = PAGE_SIZE * FOX_HEADS
    row_id = lax.broadcasted_iota(jnp.int32, (rows, 1), 0)
    head_of_row = row_id & (FOX_HEADS - 1)
    query_of_row = lax.shift_right_logical(row_id, 3)

    @pl.when(j == 0)
    def _():
        q = q_ref[...].reshape(rows, HEAD_DIM).astype(BF16)
        q_scr[...] = q
        cn = cn_ref[0]
        key = lax.broadcasted_iota(jnp.int32, (1, rows), 1)
        colq = jnp.sum(jnp.where(key == row_id, cn, 0.0), axis=-1, keepdims=True)
        colq_ref[...] = colq
        toff_ref[...] = jnp.zeros_like(toff_ref)
        s = _dot_nt(q, kn_ref[...].reshape(rows, HEAD_DIM).astype(BF16)) + colq - cn
        valid = jnp.logical_and((key & (FOX_HEADS - 1)) == head_of_row,
                                lax.shift_right_logical(key, 3) <= query_of_row)
        s = jnp.where(valid, s, -jnp.inf)
        m = jnp.max(s, axis=-1, keepdims=True)
        p = jnp.exp(s - m)
        m_ref[...] = m
        l_ref[...] = jnp.sum(p, axis=-1, keepdims=True)
        acc_ref[...] = _dot(p.astype(BF16), vn_ref[...].reshape(rows, HEAD_DIM).astype(BF16))

    lane = lax.broadcasted_iota(jnp.int32, (1, LANES), 1)
    own_head = (lax.broadcasted_iota(jnp.int32, (1, page_keys), 1) & (FOX_HEADS - 1)) == head_of_row
    q = q_scr[...]
    colq = colq_ref[...]
    toff = toff_ref[...]
    tiles = []
    for kp_ref, w_ref in zip(kp_refs, w_refs):
        w = w_ref[0]
        last = jnp.where(lane == LANES - FOX_HEADS + head_of_row, w[:, page_keys - LANES:], 0.0)
        toff = toff + jnp.sum(last, axis=-1, keepdims=True)
        s = _dot_nt(q, kp_ref[0].reshape(page_keys, HEAD_DIM).astype(BF16))
        tiles.append(jnp.where(own_head, s + (colq + toff) - w, -jnp.inf))
    toff_ref[...] = toff
    m_old = m_ref[...]
    m = m_old
    for s in tiles:
        m = jnp.maximum(m, jnp.max(s, axis=-1, keepdims=True))
    alpha = jnp.exp(m_old - m)
    l = alpha * l_ref[...]
    acc = alpha * acc_ref[...]
    for s, vp_ref in zip(tiles, vp_refs):
        p = jnp.exp(s - m)
        l = l + jnp.sum(p, axis=-1, keepdims=True)
        acc = acc + _dot(p.astype(BF16), vp_ref[0].reshape(page_keys, HEAD_DIM).astype(BF16))
    m_ref[...] = m
    l_ref[...] = l
    acc_ref[...] = acc

    @pl.when(j == pl.num_programs(1) - 1)
    def _():
        o_ref[...] = (acc_ref[...] / l_ref[...]).reshape(nq, FOX_HEADS, HEAD_DIM)


def _fox_sample(page_table, q, k_new, v_new, c_new, k_pages, v_pages, w_pages, n_seq, nq, *, pages_per_step=16):
    n_pages = page_table.shape[1]
    rows = nq * FOX_HEADS
    pps = math.gcd(pages_per_step, n_pages)

    def page(i, ndim):
        return lambda b, j, pt: (pt[b * n_pages + (n_pages - 1 - j * pps - i)],) + (0,) * (ndim - 1)

    seq = pl.BlockSpec((nq, FOX_HEADS, HEAD_DIM), lambda b, j, pt: (b, 0, 0))
    kv_specs = [pl.BlockSpec((1, PAGE_SIZE, FOX_HEADS, HEAD_DIM), page(i, 4)) for i in range(pps)]
    w_specs = [pl.BlockSpec((1, 1, PAGE_SIZE * FOX_HEADS), page(i, 3)) for i in range(pps)]
    grid_spec = pltpu.PrefetchScalarGridSpec(
        num_scalar_prefetch=1,
        grid=(n_seq, n_pages // pps),
        in_specs=[seq, seq, seq, pl.BlockSpec((1, 1, rows), lambda b, j, pt: (b, 0, 0))]
        + kv_specs + kv_specs + w_specs,
        out_specs=seq,
        scratch_shapes=[
            pltpu.VMEM((rows, HEAD_DIM), BF16), pltpu.VMEM((rows, 1), F32), pltpu.VMEM((rows, 1), F32),
            pltpu.VMEM((rows, 1), F32), pltpu.VMEM((rows, 1), F32), pltpu.VMEM((rows, HEAD_DIM), F32),
        ],
    )
    return pl.pallas_call(
        functools.partial(_fox_sample_body, pps),
        grid_spec=grid_spec,
        out_shape=jax.ShapeDtypeStruct((n_seq * nq, FOX_HEADS, HEAD_DIM), F32),
        compiler_params=_params("parallel", "arbitrary"),
        name="fox_sample",
    )(page_table.reshape(-1), q, k_new, v_new, c_new, *([k_pages] * pps), *([v_pages] * pps), *([w_pages] * pps))


SSD_PAIRS = SSD_HEADS // 2
PAIRS_PER_GROUP = SSD_PAIRS // SSD_GROUPS


def _expander(width):
    n = SSD_HEADS * width
    h = lax.broadcasted_iota(jnp.int32, (LANES, n), 0)
    c = lax.broadcasted_iota(jnp.int32, (LANES, n), 1)
    return (lax.shift_right_logical(c, int(math.log2(width))) == h).astype(BF16)


def _ssd_local(xs, bm, cm, dt, alog, mask, tot_sel):
    n = xs.shape[0]
    lane = lax.broadcasted_iota(jnp.int32, (1, LANES), 1)
    dta = dt * jnp.where(lane < SSD_HEADS, -jnp.exp(alog), 0.0)
    e64 = _expander(SSD_HEAD_DIM)
    a_cum = _dot_sel_l(mask.astype(BF16), dta)
    a_cum_t = a_cum.T
    ac_exp = _dot_sel_r(a_cum, e64)
    if tot_sel is None:
        atot_exp = ac_exp[n - 1:n, :]
    else:
        atot_exp = _dot_sel_r(_dot_sel_l(tot_sel, dta), e64)
    ac_b = _dot_sel_r(a_cum, _expander(LANES))
    xdt = xs * _dot_sel_r(dt, e64)
    half = lax.broadcasted_iota(jnp.int32, (n, LANES), 1) < SSD_HEAD_DIM
    out = {
        "xdtw": xdt * jnp.exp(atot_exp - ac_exp),
        "eac": jnp.exp(ac_exp),
        "atot_exp": atot_exp,
        "ac_b": ac_b,
        "bg": [], "cg": [], "y_diag": [],
    }
    for g in range(SSD_GROUPS):
        gs = slice(g * SSD_STATE, (g + 1) * SSD_STATE)
        bg = bm[:, gs].astype(BF16)
        cg = cm[:, gs].astype(BF16)
        out["bg"].append(bg)
        out["cg"].append(cg)
        cb = _dot_nt(cg, bg)
        for k in range(g * PAIRS_PER_GROUP, (g + 1) * PAIRS_PER_GROUP):
            ps = slice(k * LANES, (k + 1) * LANES)
            ms = []
            for h in (2 * k, 2 * k + 1):
                seg = ac_b[:, h * LANES:(h + 1) * LANES] - a_cum_t[h:h + 1, :]
                ms.append(cb * jnp.exp(jnp.where(mask, seg, -jnp.inf)))
            m_cat = jnp.concatenate(ms, axis=1).astype(BF16)
            xp = xdt[:, ps]
            x_bd = jnp.concatenate([jnp.where(half, xp, 0.0), jnp.where(half, 0.0, xp)], axis=0).astype(BF16)
            out["y_diag"].append(_dot(m_cat, x_bd))
    return out


def _conv_ssd_sample_body(seq_len, xbc_ref, dt_ref, buf_ref, h0_ref, w_ref, b_ref, alog_ref, dexp_ref,
                          y_ref, hout_ref, cout_ref, xp_ref):
    n = xbc_ref.shape[0]
    n_seq = n // seq_len
    taps = CONV_WIDTH - 1
    base = 8 - taps

    x = xbc_ref[...]
    xp_ref[:, base:8, :] = buf_ref[...]
    xp_ref[:, 8:8 + seq_len, :] = x.reshape(n_seq, seq_len, CONV_DIM)
    acc = b_ref[...] + x * w_ref[taps:taps + 1, :]
    for j in range(taps):
        acc = acc + xp_ref[:, base + j:base + j + seq_len, :].reshape(n, CONV_DIM) * w_ref[j:j + 1, :]
    cout_ref[...] = xp_ref[:, 8 + seq_len - taps:8 + seq_len, :]
    conv = _silu(acc)
    xs = conv[:, :SSD_WIDTH]
    bm = conv[:, SSD_WIDTH:SSD_WIDTH + SSD_GROUPS * SSD_STATE]
    cm = conv[:, SSD_WIDTH + SSD_GROUPS * SSD_STATE:]

    r = lax.broadcasted_iota(jnp.int32, (n, n), 0)
    c = lax.broadcasted_iota(jnp.int32, (n, n), 1)
    same_seq = (r // seq_len) == (c // seq_len)
    loc = _ssd_local(xs, bm, cm, dt_ref[...], alog_ref[...], jnp.logical_and(same_seq, c <= r),
                     same_seq.astype(BF16))

    gw = PAIRS_PER_GROUP * LANES
    seq_of_col = lax.broadcasted_iota(jnp.int32, (1, n), 1) // seq_len
    decay_t = jnp.exp(loc["atot_exp"]).T
    for g in range(SSD_GROUPS):
        gr = slice(g * gw, (g + 1) * gw)
        h_prev = h0_ref[:, gr, :]
        z = _dot_nt(h_prev.reshape(n_seq * gw, SSD_STATE).astype(BF16), loc["cg"][g])
        y_off_t = jnp.zeros((gw, n), F32)
        for s in range(n_seq):
            y_off_t = y_off_t + jnp.where(seq_of_col == s, z[s * gw:(s + 1) * gw, :], 0.0)
        y_off = y_off_t.T * loc["eac"][:, gr]
        y = jnp.concatenate(loc["y_diag"][g * PAIRS_PER_GROUP:(g + 1) * PAIRS_PER_GROUP], axis=1)
        y_ref[:, gr] = y + y_off + xs[:, gr] * dexp_ref[:, gr]
        xw_t = loc["xdtw"][:, gr].T
        lhs = jnp.concatenate([jnp.where(seq_of_col == s, xw_t, 0.0) for s in range(n_seq)], axis=0)
        s_new = _dot(lhs.astype(BF16), loc["bg"][g])
        for s in range(n_seq):
            col = decay_t[gr, s * seq_len:s * seq_len + 1]
            hout_ref[s, gr, :] = h_prev[s] * col + s_new[s * gw:(s + 1) * gw, :]


def _conv_ssd_sample(xbc, dt, conv_buf, h0, conv_w, conv_b, alog, d_exp, n_seq, seq_len):
    assert seq_len == 8 and CONV_WIDTH - 1 <= seq_len
    tile = LANES
    ts = tile // seq_len
    const = lambda i: (0, 0)
    per_s = lambda i: (i, 0, 0)
    return pl.pallas_call(
        functools.partial(_conv_ssd_sample_body, seq_len),
        grid=(n_seq // ts,),
        in_specs=[
            pl.BlockSpec((tile, CONV_DIM), lambda i: (i, 0)),
            pl.BlockSpec((tile, LANES), lambda i: (i, 0)),
            pl.BlockSpec((ts, CONV_WIDTH - 1, CONV_DIM), per_s),
            pl.BlockSpec((ts, SSD_WIDTH, SSD_STATE), per_s),
            pl.BlockSpec((CONV_WIDTH, CONV_DIM), const),
            pl.BlockSpec((1, CONV_DIM), const),
            pl.BlockSpec((1, LANES), const),
            pl.BlockSpec((1, SSD_WIDTH), const),
        ],
        out_specs=[
            pl.BlockSpec((tile, SSD_WIDTH), lambda i: (i, 0)),
            pl.BlockSpec((ts, SSD_WIDTH, SSD_STATE), per_s),
            pl.BlockSpec((ts, CONV_WIDTH - 1, CONV_DIM), per_s),
        ],
        out_shape=[
            jax.ShapeDtypeStruct((n_seq * seq_len, SSD_WIDTH), F32),
            jax.ShapeDtypeStruct((n_seq, SSD_WIDTH, SSD_STATE), F32),
            jax.ShapeDtypeStruct((n_seq, CONV_WIDTH - 1, CONV_DIM), F32),
        ],
        scratch_shapes=[pltpu.VMEM((ts, 8 + seq_len, CONV_DIM), F32)],
        compiler_params=_params("parallel"),
        name="conv_ssd_sample",
    )(xbc, dt, conv_buf, h0, conv_w, conv_b, alog, d_exp)


def _conv_ssd_prompt_body(xbc_ref, dt_ref, buf_ref, h0_ref, w_ref, b_ref, alog_ref, dexp_ref,
                          y_ref, hout_ref, cout_ref, state_ref, xp_ref):
    c = pl.program_id(1)
    nc = pl.num_programs(1)
    tl = xbc_ref.shape[0]
    taps = CONV_WIDTH - 1
    base = 8 - taps

    @pl.when(c == 0)
    def _():
        state_ref[...] = h0_ref[0]
        xp_ref[base:8, :] = buf_ref[0]

    x = xbc_ref[...]
    xp_ref[8:8 + tl, :] = x
    acc = b_ref[...] + x * w_ref[taps:taps + 1, :]
    for j in range(taps):
        acc = acc + xp_ref[base + j:base + j + tl, :] * w_ref[j:j + 1, :]
    xp_ref[base:8, :] = x[tl - taps:, :]
    conv = _silu(acc)
    xs = conv[:, :SSD_WIDTH]
    bm = conv[:, SSD_WIDTH:SSD_WIDTH + SSD_GROUPS * SSD_STATE]
    cm = conv[:, SSD_WIDTH + SSD_GROUPS * SSD_STATE:]

    causal = _tri(tl)
    loc = _ssd_local(xs, bm, cm, dt_ref[...], alog_ref[...], causal, None)
    top = lax.broadcasted_iota(jnp.int32, (tl, LANES), 0) < SSD_HEAD_DIM
    for k in range(SSD_PAIRS):
        g = k // PAIRS_PER_GROUP
        ps = slice(k * LANES, (k + 1) * LANES)
        s_prev = state_ref[ps, :]
        y_off = _dot_nt(loc["cg"][g], s_prev.astype(BF16)) * loc["eac"][:, ps]
        y_ref[:, ps] = loc["y_diag"][k] + y_off + xs[:, ps] * dexp_ref[:, ps]
        cd = [jnp.exp(loc["ac_b"][tl - 1:tl, h * LANES:(h + 1) * LANES]) for h in (2 * k, 2 * k + 1)]
        state_ref[ps, :] = s_prev * jnp.where(top, cd[0], cd[1]) + _dot_tn(loc["xdtw"][:, ps].astype(BF16), loc["bg"][g])

    @pl.when(c == nc - 1)
    def _():
        hout_ref[0] = state_ref[...]
        cout_ref[0] = xp_ref[base:8, :]


def _conv_ssd_prompt(xbc, dt, conv_buf, h0, conv_w, conv_b, alog, d_exp, b, l):
    tl = SSD_CHUNK
    nc = l // tl
    const = lambda bi, c: (0, 0)
    per_b = lambda bi, c: (bi, 0, 0)
    return pl.pallas_call(
        _conv_ssd_prompt_body,
        grid=(b, nc),
        in_specs=[
            pl.BlockSpec((tl, CONV_DIM), lambda bi, c: (bi * nc + c, 0)),
            pl.BlockSpec((tl, LANES), lambda bi, c: (bi * nc + c, 0)),
            pl.BlockSpec((1, CONV_WIDTH - 1, CONV_DIM), per_b),
            pl.BlockSpec((1, SSD_WIDTH, SSD_STATE), per_b),
            pl.BlockSpec((CONV_WIDTH, CONV_DIM), const),
            pl.BlockSpec((1, CONV_DIM), const),
            pl.BlockSpec((1, LANES), const),
            pl.BlockSpec((1, SSD_WIDTH), const),
        ],
        out_specs=[
            pl.BlockSpec((tl, SSD_WIDTH), lambda bi, c: (bi * nc + c, 0)),
            pl.BlockSpec((1, SSD_WIDTH, SSD_STATE), per_b),
            pl.BlockSpec((1, CONV_WIDTH - 1, CONV_DIM), per_b),
        ],
        out_shape=[
            jax.ShapeDtypeStruct((b * l, SSD_WIDTH), F32),
            jax.ShapeDtypeStruct((b, SSD_WIDTH, SSD_STATE), F32),
            jax.ShapeDtypeStruct((b, CONV_WIDTH - 1, CONV_DIM), F32),
        ],
        scratch_shapes=[pltpu.VMEM((SSD_WIDTH, SSD_STATE), F32), pltpu.VMEM((8 + tl, CONV_DIM), F32)],
        compiler_params=_params("parallel", "arbitrary"),
        name="conv_ssd_prompt",
    )(xbc, dt, conv_buf, h0, conv_w, conv_b, alog, d_exp)


def _out_proj_body(h_ref, fox_ref, y_ref, z_ref, gs_ref, w_ref, gx_ref, wq_ref, gq_ref, o_ref, q_ref):
    yn = _rms(y_ref[...] * _silu(z_ref[...]), gs_ref[...]).astype(BF16)
    fw = fox_ref.shape[1]
    h = h_ref[...] + _dot(fox_ref[...].astype(BF16), w_ref[:fw, :]) + _dot(yn, w_ref[fw:, :])
    o_ref[...] = h
    q = _dot(_rms(h, gx_ref[...]).astype(BF16), wq_ref[...])
    q_ref[...] = _head_norm(q, gq_ref[...]) * (HEAD_DIM ** -0.5)


def _out_proj(h, fox, y, z, g_ssd, w_out, g_x, wq, gq, *, tm=512):
    t, d = h.shape
    tm = min(tm, t)
    row = lambda w: pl.BlockSpec((tm, w), lambda i: (i, 0))
    full = lambda a: pl.BlockSpec(a.shape, lambda i: (0, 0), pipeline_mode=pl.Buffered(1))
    consts = [g_ssd.reshape(1, SSD_WIDTH), w_out, g_x.reshape(1, d), wq, gq.reshape(1, HEAD_DIM)]
    return pl.pallas_call(
        _out_proj_body,
        grid=(t // tm,),
        in_specs=[row(d), row(FOX_WIDTH), row(SSD_WIDTH), row(SSD_WIDTH)] + [full(a) for a in consts],
        out_specs=[row(d), row(XATTN_WIDTH)],
        out_shape=[jax.ShapeDtypeStruct((t, d), F32), jax.ShapeDtypeStruct((t, XATTN_WIDTH), F32)],
        compiler_params=_params("parallel"),
        name="out_proj",
    )(h, fox, y, z, *consts)


def _mem_kv_body(m_ref, g_ref, w_ref, gk_ref, k_ref, v_ref):
    kv = _dot(_rms(m_ref[...], g_ref[...]).astype(BF16), w_ref[...])
    k_ref[...] = _head_norm(kv[:, :XATTN_WIDTH], gk_ref[...])
    v_ref[...] = kv[:, XATTN_WIDTH:]


def _mem_kv(mem, g, w_kv, gk, *, tm=256):
    t, d = mem.shape
    row = lambda w: pl.BlockSpec((tm, w), lambda i: (i, 0))
    full = lambda a: pl.BlockSpec(a.shape, lambda i: (0, 0))
    consts = [g.reshape(1, d), w_kv, gk.reshape(1, HEAD_DIM)]
    return pl.pallas_call(
        _mem_kv_body,
        grid=(t // tm,),
        in_specs=[row(d)] + [full(a) for a in consts],
        out_specs=[row(XATTN_WIDTH), row(XATTN_WIDTH)],
        out_shape=[jax.ShapeDtypeStruct((t, XATTN_WIDTH), F32)] * 2,
        compiler_params=_params("parallel"),
        name="mem_kv",
    )(mem, *consts)


def _xattn_body(q_ref, k_ref, v_ref, o_ref):
    for h in range(XATTN_HEADS):
        hs = slice(h * HEAD_DIM, (h + 1) * HEAD_DIM)
        s = _dot_nt(q_ref[:, hs].astype(BF16), k_ref[0, :, hs].astype(BF16))
        p = jnp.exp(s - jnp.max(s, axis=-1, keepdims=True))
        o = _dot(p.astype(BF16), v_ref[0, :, hs].astype(BF16))
        o_ref[:, hs] = o / jnp.sum(p, axis=-1, keepdims=True)


def _xattn(q, mem_k, mem_v, b, l, *, tq=512):
    tq = min(tq, l)
    nq = l // tq
    n_mem = mem_k.shape[1]
    qspec = pl.BlockSpec((tq, XATTN_WIDTH), lambda bi, i: (bi * nq + i, 0))
    mspec = pl.BlockSpec((1, n_mem, XATTN_WIDTH), lambda bi, i: (bi, 0, 0))
    return pl.pallas_call(
        _xattn_body,
        grid=(b, nq),
        in_specs=[qspec, mspec, mspec],
        out_specs=qspec,
        out_shape=jax.ShapeDtypeStruct((b * l, XATTN_WIDTH), F32),
        compiler_params=_params("parallel", "arbitrary"),
        name="xattn",
    )(q, mem_k, mem_v)


def _xattn_rows_body(n_seq, q_ref, k_ref, v_ref, o_ref):
    tiles = q_ref.shape[0] // n_seq
    rows = tiles * 8
    keys = k_ref.shape[1] * k_ref.shape[2]
    row_head = lax.broadcasted_iota(jnp.int32, (rows, 1), 0) & (XATTN_HEADS - 1)
    key_head = lax.broadcasted_iota(jnp.int32, (1, keys), 1) & (XATTN_HEADS - 1)
    own_head = row_head == key_head
    for s in range(n_seq):
        qs = slice(s * tiles, (s + 1) * tiles)
        q = q_ref[qs].reshape(rows, HEAD_DIM).astype(BF16)
        sc = jnp.where(own_head, _dot_nt(q, k_ref[s].reshape(keys, HEAD_DIM).astype(BF16)), -jnp.inf)
        p = jnp.exp(sc - jnp.max(sc, axis=-1, keepdims=True))
        o = _dot(p.astype(BF16), v_ref[s].reshape(keys, HEAD_DIM).astype(BF16))
        o_ref[qs] = (o / jnp.sum(p, axis=-1, keepdims=True)).reshape(tiles, 8, HEAD_DIM)


def _xattn_rows(q, mem_k, mem_v, n_seq, *, ts=8):
    tiles = q.shape[0] // n_seq
    ts = math.gcd(ts, n_seq)
    qspec = pl.BlockSpec((ts * tiles, 8, HEAD_DIM), lambda i: (i, 0, 0))
    mspec = pl.BlockSpec((ts,) + mem_k.shape[1:], lambda i: (i, 0, 0, 0))
    return pl.pallas_call(
        functools.partial(_xattn_rows_body, ts),
        grid=(n_seq // ts,),
        in_specs=[qspec, mspec, mspec],
        out_specs=qspec,
        out_shape=jax.ShapeDtypeStruct(q.shape, F32),
        compiler_params=_params("parallel"),
        name="xattn_rows",
    )(q, mem_k, mem_v)


def kernel(x_prompt, x_sample, cache_fox_k, cache_fox_v, cache_fox_logf, cache_mem_k, cache_mem_v, state_ssm, state_conv, page_table, mem_prompt, ffn1_norm, ffn1_w_gate, ffn1_w_up, ffn1_w_down, mix_norm, w_in, fox_b_f, fox_q_norm, fox_k_norm, conv_w, conv_b, ssd_dt_bias, ssd_A_log, ssd_D, ssd_out_norm, w_out, xattn_norm, mem_norm, xattn_w_q, xattn_w_kv, xattn_q_norm, xattn_k_norm, xattn_w_o, ffn2_norm, ffn2_w_gate, ffn2_w_up, ffn2_w_down):
    assert x_prompt.shape[2] == D_MODEL and ffn1_norm.shape[0] == 1
    d = D_MODEL
    bp, lp = x_prompt.shape[:2]
    bs, ls = x_sample.shape[:2]
    n_mem = mem_prompt.shape[1]
    n_pool = cache_fox_k.shape[1]

    bf = lambda w: w[0].astype(BF16)
    ffn1 = (ffn1_norm[0], ffn1_w_gate[0], ffn1_w_up[0], ffn1_w_down[0])
    ffn2 = (ffn2_norm[0], ffn2_w_gate[0], ffn2_w_up[0], ffn2_w_down[0])
    in_w = _pack_in_proj(w_in[0], fox_b_f[0], ssd_dt_bias[0])
    wq, wkv, wo = bf(xattn_w_q), bf(xattn_w_kv), bf(xattn_w_o)
    alog = jnp.pad(ssd_A_log[0], (0, LANES - SSD_HEADS)).reshape(1, LANES)
    d_exp = jnp.repeat(ssd_D[0], SSD_HEAD_DIM).reshape(1, SSD_WIDTH)
    cw, cb = conv_w[0], conv_b[0].reshape(1, CONV_DIM)

    def front(x, q_dtype, q_scale):
        h1 = _ffn(x, *ffn1)
        return h1, _in_proj(h1, mix_norm[0], *in_w, fox_q_norm[0], fox_k_norm[0], q_dtype=q_dtype, q_scale=q_scale)

    def back(h1, fox, y, z, attend):
        h2, xq = _out_proj(h1, fox, y, z, ssd_out_norm[0], bf(w_out), xattn_norm[0], wq, xattn_q_norm[0])
        return _ffn(h2, *ffn2, pre=(attend(xq), wo))

    h1, (q, k_p, kb, v_p, vb, z, xbc, logf_p, dt) = front(x_prompt.reshape(bp * lp, d), BF16, HEAD_DIM ** -0.5 * LOG2E)
    ccol, crow = _cumsum(logf_p.reshape(bp, lp, LANES))
    tk = min(FOX_KEY_BLOCK, lp)
    fox = _fox_prompt(q, kb, vb, ccol, crow.reshape(bp, FOX_HEADS, lp // tk, tk), bp, lp, tq=FOX_QUERY_BLOCK)
    y, ssm_p, conv_p = _conv_ssd_prompt(
        xbc, dt, jnp.zeros((bp, CONV_WIDTH - 1, CONV_DIM), F32), jnp.zeros((bp, SSD_WIDTH, SSD_STATE), F32),
        cw, cb, alog, d_exp, bp, lp)
    mk, mv = _mem_kv(mem_prompt.reshape(bp * n_mem, d), mem_norm[0], wkv, xattn_k_norm[0])
    y_prompt = back(h1, fox, y, z, lambda xq: _xattn(
        xq, mk.reshape(bp, n_mem, XATTN_WIDTH), mv.reshape(bp, n_mem, XATTN_WIDTH), bp, lp))

    h1, (q, k_s, _, v_s, _, z, xbc, logf_s, dt) = front(x_sample.reshape(bs * ls, d), F32, HEAD_DIM ** -0.5)
    w_pages = _page_cumsum(cache_fox_logf[0].reshape(n_pool, PAGE_SIZE * FOX_HEADS))
    c_new = _seq_cumsum(logf_s, ls)[:, :FOX_HEADS].reshape(bs, 1, ls * FOX_HEADS)
    fox = _fox_sample(
        page_table, q.reshape(bs * ls, FOX_HEADS, HEAD_DIM), k_s, v_s, c_new, cache_fox_k[0], cache_fox_v[0],
        w_pages.reshape(n_pool, 1, PAGE_SIZE * FOX_HEADS), bs, ls).reshape(bs * ls, FOX_WIDTH)
    y, ssm_s, conv_s = _conv_ssd_sample(
        xbc, dt, state_conv[0], state_ssm[0].reshape(bs, SSD_WIDTH, SSD_STATE), cw, cb, alog, d_exp, bs, ls)
    mem_tiles = lambda m: m[0].reshape(bs, n_mem * XATTN_HEADS // 8, 8, HEAD_DIM)
    y_sample = back(h1, fox, y, z, lambda xq: _xattn_rows(
        xq.reshape(bs * ls * XATTN_HEADS // 8, 8, HEAD_DIM), mem_tiles(cache_mem_k), mem_tiles(cache_mem_v), bs,
    ).reshape(bs * ls, XATTN_WIDTH))

    fox_shape = lambda b, l: (1, b, l, FOX_HEADS, HEAD_DIM)
    ssm_shape = lambda b: (1, b, SSD_HEADS, SSD_HEAD_DIM, SSD_STATE)
    mem_shape = (1, bp, n_mem, XATTN_HEADS, HEAD_DIM)
    return (
        y_prompt.reshape(bp, lp, d), y_sample.reshape(bs, ls, d),
        k_p.reshape(fox_shape(bp, lp)), v_p.reshape(fox_shape(bp, lp)),
        logf_p[:, :FOX_HEADS].reshape(1, bp, lp, FOX_HEADS),
        ssm_p.reshape(ssm_shape(bp)), conv_p[None], mk.reshape(mem_shape), mv.reshape(mem_shape),
        k_s.reshape(fox_shape(bs, ls)), v_s.reshape(fox_shape(bs, ls)),
        logf_s[:, :FOX_HEADS].reshape(1, bs, ls, FOX_HEADS),
        ssm_s.reshape(ssm_shape(bs)), conv_s[None],
    )
```

```python
import functools
import math

import jax
import jax.numpy as jnp
from jax import lax
from jax.experimental import pallas as pl
from jax.experimental.pallas import tpu as pltpu

F32 = jnp.float32
BF16 = jnp.bfloat16

EPS = 1e-6
LOG2E = math.log2(math.e)
FFN_RESIDUAL = 0.5
D_MODEL = 2048
D_FF = 5632
PAGE_SIZE = 128
FOX_HEADS = 8
HEAD_DIM = 128
FOX_WIDTH = FOX_HEADS * HEAD_DIM
SSD_HEADS = 16
SSD_HEAD_DIM = 64
SSD_WIDTH = SSD_HEADS * SSD_HEAD_DIM
SSD_GROUPS = 2
SSD_STATE = 128
SSD_CHUNK = 128
CONV_WIDTH = 4
CONV_DIM = SSD_WIDTH + 2 * SSD_GROUPS * SSD_STATE
XATTN_HEADS = 4
XATTN_WIDTH = XATTN_HEADS * HEAD_DIM
LANES = 128
VMEM_LIMIT_BYTES = 56 * 1024 * 1024

def _params(*semantics):
    return pltpu.CompilerParams(dimension_semantics=semantics, vmem_limit_bytes=VMEM_LIMIT_BYTES)


def _rms(x, g):
    return x * lax.rsqrt(jnp.mean(x * x, axis=-1, keepdims=True) + EPS) * g


def _dot(a, b):
    return jnp.dot(a, b, preferred_element_type=F32)


def _dot_nt(a, b):
    return lax.dot_general(a, b, (((1,), (1,)), ((), ())), preferred_element_type=F32)


def _dot_tn(a, b):
    return lax.dot_general(a, b, (((0,), (0,)), ((), ())), preferred_element_type=F32)


def _split3(x):
    x1 = x.astype(BF16)
    r = x - x1.astype(F32)
    x2 = r.astype(BF16)
    x3 = (r - x2.astype(F32)).astype(BF16)
    return x1, x2, x3


def _dot_sel_l(sel, x):
    x1, x2, x3 = _split3(x)
    return _dot(sel, x1) + _dot(sel, x2) + _dot(sel, x3)


def _dot_sel_r(x, sel):
    x1, x2, x3 = _split3(x)
    return _dot(x1, sel) + _dot(x2, sel) + _dot(x3, sel)


def _silu(x):
    return x * jax.nn.sigmoid(x)


def _ffn_body(pre_proj, split, *refs):
    if pre_proj:
        x_ref, a_ref, wo_ref, g_ref, *w_refs, o_ref, xn_ref = refs
    else:
        x_ref, g_ref, *w_refs, o_ref, xn_ref = refs

    @pl.when(pl.program_id(1) == 0)
    def _():
        x = x_ref[...]
        if pre_proj:
            x = x + _dot(a_ref[...].astype(BF16), wo_ref[...])
        xn_ref[...] = _rms(x, g_ref[...]).astype(BF16)
        o_ref[...] = x

    xn = xn_ref[...]
    acc = None
    for wg_ref, wu_ref, wd_ref in zip(w_refs[:split], w_refs[split:2 * split], w_refs[2 * split:]):
        gate = _dot(xn, wg_ref[...].astype(BF16))
        up = _dot(xn, wu_ref[...].astype(BF16))
        h = (_silu(gate) * up * FFN_RESIDUAL).astype(BF16)
        part = _dot(h, wd_ref[...].astype(BF16))
        acc = part if acc is None else acc + part
    o_ref[...] += acc


def _ffn(x, g, wg, wu, wd, pre=None, *, tm=1024, tf=256, split=1):
    t, d = x.shape
    f = wg.shape[1]
    tm = min(tm, t)
    ts = tf // split
    grid = (t // tm, f // tf)
    row = lambda i, j: (i, 0)
    in_specs = [pl.BlockSpec((tm, d), row)]
    args = [x]
    if pre is not None:
        a, wo = pre
        in_specs = [pl.BlockSpec((tm, d), row, pipeline_mode=pl.Buffered(1)),
                    pl.BlockSpec((tm, a.shape[1]), row), pl.BlockSpec(wo.shape, lambda i, j: (0, 0))]
        args += [a, wo]
    cols = [pl.BlockSpec((d, ts), lambda i, j, s=s: (0, j * split + s)) for s in range(split)]
    rows = [pl.BlockSpec((ts, d), lambda i, j, s=s: (j * split + s, 0)) for s in range(split)]
    in_specs += [pl.BlockSpec((1, d), lambda i, j: (0, 0))] + cols + cols + rows
    args += [g.reshape(1, d)] + [wg] * split + [wu] * split + [wd] * split
    return pl.pallas_call(
        functools.partial(_ffn_body, pre is not None, split),
        grid=grid,
        in_specs=in_specs,
        out_specs=pl.BlockSpec((tm, d), row),
        out_shape=jax.ShapeDtypeStruct((t, d), F32),
        scratch_shapes=[pltpu.VMEM((tm, d), BF16)],
        compiler_params=_params("parallel", "arbitrary"),
        name="ffn_pre" if pre is not None else "ffn",
    )(*args)


IN_QKV_TN = 1024
IN_ZX_TN = 512


def _head_norm(y, g):
    outs = []
    for c in range(y.shape[1] // HEAD_DIM):
        yc = y[:, c * HEAD_DIM:(c + 1) * HEAD_DIM]
        outs.append(yc * lax.rsqrt(jnp.mean(yc * yc, axis=-1, keepdims=True) + EPS) * g)
    return jnp.concatenate(outs, axis=1)


def _in_qkv_body(q_scale, x_ref, g_ref, w_ref, ws_ref, bs_ref, gq_ref, gk_ref,
                 q_ref, k_ref, kb_ref, v_ref, vb_ref, logf_ref, dt_ref, u_ref):
    j = pl.program_id(1)

    @pl.when(j == 0)
    def _():
        u = _rms(x_ref[...], g_ref[...]).astype(BF16)
        u_ref[...] = u
        s = _dot(u, ws_ref[...]) + bs_ref[...]
        t = jnp.log1p(jnp.exp(-jnp.abs(s)))
        logf_ref[...] = (jnp.minimum(s, 0.0) - t)[:, :LANES]
        dt_ref[...] = (jnp.maximum(s, 0.0) + t)[:, LANES:]
        q_ref[...] = (_head_norm(_dot(u, w_ref[...]), gq_ref[...]) * q_scale).astype(q_ref.dtype)

    def by_head(y, out_ref, bf_ref):
        bf_ref[...] = y.astype(BF16)
        for h in range(FOX_HEADS):
            out_ref[:, h, :] = y[:, h * HEAD_DIM:(h + 1) * HEAD_DIM]

    @pl.when(j == 1)
    def _():
        by_head(_head_norm(_dot(u_ref[...], w_ref[...]), gk_ref[...]), k_ref, kb_ref)

    @pl.when(j == 2)
    def _():
        by_head(_dot(u_ref[...], w_ref[...]), v_ref, vb_ref)


def _in_qkv(x, g, w_qkv, w_small, b_small, gq, gk, *, q_dtype, q_scale, tm=512):
    t, d = x.shape
    tm = min(tm, t)
    tn = IN_QKV_TN
    assert tn == FOX_WIDTH and w_qkv.shape[1] >= 3 * tn
    const = lambda i, j: (0, 0)
    rows = lambda w: pl.BlockSpec((tm, w), lambda i, j: (i, 0))
    by_head = pl.BlockSpec((tm, FOX_HEADS, HEAD_DIM), lambda i, j: (i, 0, 0))
    by_head_shape = jax.ShapeDtypeStruct((t, FOX_HEADS, HEAD_DIM), F32)
    flat = lambda dt, w: jax.ShapeDtypeStruct((t, w), dt)
    return pl.pallas_call(
        functools.partial(_in_qkv_body, q_scale),
        grid=(t // tm, 3),
        in_specs=[
            rows(d),
            pl.BlockSpec((1, d), const),
            pl.BlockSpec((d, tn), lambda i, j: (0, j)),
            pl.BlockSpec((d, 2 * LANES), const),
            pl.BlockSpec((1, 2 * LANES), const),
            pl.BlockSpec((1, HEAD_DIM), const),
            pl.BlockSpec((1, HEAD_DIM), const),
        ],
        out_specs=[rows(FOX_WIDTH), by_head, rows(FOX_WIDTH), by_head, rows(FOX_WIDTH), rows(LANES), rows(LANES),
                   rows(d)],
        out_shape=[flat(q_dtype, FOX_WIDTH), by_head_shape, flat(BF16, FOX_WIDTH), by_head_shape,
                   flat(BF16, FOX_WIDTH), flat(F32, LANES), flat(F32, LANES), flat(BF16, d)],
        compiler_params=_params("parallel", "arbitrary"),
        name="in_qkv",
    )(x, g.reshape(1, d), w_qkv, w_small, b_small, gq.reshape(1, HEAD_DIM), gk.reshape(1, HEAD_DIM))


def _in_zx_body(u_ref, wa_ref, wb_ref, z_ref, xbc_ref):
    j = pl.program_id(1)
    half = wa_ref.shape[1]
    tn = 2 * half
    nz = z_ref.shape[1] // tn
    for c in range(nz + xbc_ref.shape[1] // tn):
        out_ref, sub = (z_ref, c) if c < nz else (xbc_ref, c - nz)

        @pl.when(j == c)
        def _(out_ref=out_ref, sub=sub):
            out_ref[:, sub * tn:sub * tn + half] = _dot(u_ref[...], wa_ref[...])
            out_ref[:, sub * tn + half:(sub + 1) * tn] = _dot(u_ref[...], wb_ref[...])


def _in_zx(u, w_zx, *, tm=1024):
    t, d = u.shape
    tm = min(tm, t)
    tn = IN_ZX_TN
    rows = lambda w: pl.BlockSpec((tm, w), lambda i, j: (i, 0))
    return pl.pallas_call(
        _in_zx_body,
        grid=(t // tm, w_zx.shape[1] // tn),
        in_specs=[rows(d), pl.BlockSpec((d, tn // 2), lambda i, j: (0, 2 * j)),
                  pl.BlockSpec((d, tn // 2), lambda i, j: (0, 2 * j + 1))],
        out_specs=[rows(SSD_WIDTH), rows(CONV_DIM)],
        out_shape=[jax.ShapeDtypeStruct((t, SSD_WIDTH), F32), jax.ShapeDtypeStruct((t, CONV_DIM), F32)],
        compiler_params=_params("parallel", "arbitrary"),
        name="in_zx",
    )(u, w_zx, w_zx)


def _in_proj(x, g, w_qkv, w_zx, w_small, b_small, gq, gk, *, q_dtype, q_scale):
    q, k, kb, v, vb, logf, dt, u = _in_qkv(x, g, w_qkv, w_small, b_small, gq, gk, q_dtype=q_dtype, q_scale=q_scale)
    z, xbc = _in_zx(u, w_zx)
    return q, k, kb, v, vb, z, xbc, logf, dt


def _pack_in_proj(w_in, fox_b_f, ssd_dt_bias):
    fw = FOX_WIDTH
    f0 = 3 * fw
    z0 = f0 + FOX_HEADS
    x0 = z0 + SSD_WIDTH
    d0 = x0 + CONV_DIM
    w_in = w_in.astype(BF16)
    w_qkv = w_in
    w_zx = w_in[:, z0:d0]
    zeros = lambda n: jnp.zeros((w_in.shape[0], n), BF16)
    w_small = jnp.concatenate(
        [w_in[:, f0:z0], zeros(LANES - FOX_HEADS), w_in[:, d0:], zeros(LANES - SSD_HEADS)], axis=1)
    b_small = jnp.concatenate(
        [fox_b_f, jnp.zeros((LANES - FOX_HEADS,), F32), ssd_dt_bias, jnp.zeros((LANES - SSD_HEADS,), F32)]
    ).reshape(1, 2 * LANES)
    return w_qkv, w_zx, w_small, b_small


ATT_BLOCK = 256
FOX_QUERY_BLOCK = 256
FOX_KEY_BLOCK = 256


def _tri(n, *, strict=False, upper=False):
    r = lax.broadcasted_iota(jnp.int32, (n, n), 0)
    c = lax.broadcasted_iota(jnp.int32, (n, n), 1)
    if upper:
        r, c = c, r
    return (c < r) if strict else (c <= r)


def _cumsum_body(x_ref, col_ref, row_ref, carry_ref):
    @pl.when(pl.program_id(1) == 0)
    def _():
        carry_ref[...] = jnp.zeros_like(carry_ref)

    n = x_ref.shape[1]
    tril = _tri(n).astype(BF16)
    c = _dot_sel_l(tril, x_ref[0]) + carry_ref[...]
    carry_ref[...] = c[n - 1:n, :]
    c = c * LOG2E
    col_ref[0] = c
    row_ref[0] = c.T[:FOX_HEADS, :]


def _cumsum(x, *, tb=ATT_BLOCK):
    b, l, _ = x.shape
    return pl.pallas_call(
        _cumsum_body,
        grid=(b, l // tb),
        in_specs=[pl.BlockSpec((1, tb, LANES), lambda i, j: (i, j, 0))],
        out_specs=[pl.BlockSpec((1, tb, LANES), lambda i, j: (i, j, 0)),
                   pl.BlockSpec((1, FOX_HEADS, tb), lambda i, j: (i, 0, j))],
        out_shape=[jax.ShapeDtypeStruct((b, l, LANES), F32), jax.ShapeDtypeStruct((b, FOX_HEADS, l), F32)],
        scratch_shapes=[pltpu.VMEM((1, LANES), F32)],
        compiler_params=_params("parallel", "arbitrary"),
        name="logf_cumsum",
    )(x)


def _fox_prompt_body(tk, q_ref, k_ref, v_ref, ccol_ref, crow_ref, o_ref, m_ref, l_ref, cq_ref, acc_ref):
    i = pl.program_id(1)
    tq = q_ref.shape[0]
    rep = tk // LANES
    m_ref[...] = jnp.full_like(m_ref, -jnp.inf)
    l_ref[...] = jnp.zeros_like(l_ref)
    acc_ref[...] = jnp.zeros_like(acc_ref)
    for h in range(FOX_HEADS):
        cq_ref[h] = jnp.broadcast_to(ccol_ref[0, :, h:h + 1], (tq, LANES))
    row = i * tq + lax.broadcasted_iota(jnp.int32, (tq, tk), 0)
    col = lax.broadcasted_iota(jnp.int32, (tq, tk), 1)
    wide = lambda x: jnp.concatenate([x] * rep, axis=1)

    def block(j, masked):
        ks = pl.ds(pl.multiple_of(j * tk, tk), tk)
        for h in range(FOX_HEADS):
            hs = slice(h * HEAD_DIM, (h + 1) * HEAD_DIM)
            s = _dot_nt(q_ref[:, hs], k_ref[ks, hs]) + (wide(cq_ref[h]) - crow_ref[0, h, pl.ds(j, 1), :])
            if masked:
                s = jnp.where(col + j * tk <= row, s, -jnp.inf)
            m_old = m_ref[h]
            m_new = jnp.maximum(m_old, jnp.max(s, axis=-1, keepdims=True))
            alpha = jnp.exp2(m_old - m_new)
            p = jnp.exp2(s - wide(m_new))
            m_ref[h] = m_new
            l_ref[h] = alpha * l_ref[h] + jnp.sum(p, axis=-1, keepdims=True)
            acc_ref[h] = alpha * acc_ref[h] + _dot(p.astype(BF16), v_ref[ks, hs])

    first_masked = (i * tq) // tk

    def step(j, carry):
        block(j, False)
        return carry

    lax.fori_loop(0, first_masked, step, 0)
    for extra in range(max(tq // tk, 1)):
        block(first_masked + extra, True)
    for h in range(FOX_HEADS):
        o_ref[:, h * HEAD_DIM:(h + 1) * HEAD_DIM] = (acc_ref[h] / l_ref[h]).astype(o_ref.dtype)


def _fox_prompt(q, k, v, ccol, crow, b, l, *, tq=128):
    tk = crow.shape[-1]
    tq = min(tq, l)
    assert tk % tq == 0 or tq % tk == 0
    nq = l // tq
    w = FOX_WIDTH
    return pl.pallas_call(
        functools.partial(_fox_prompt_body, tk),
        grid=(b, nq),
        in_specs=[
            pl.BlockSpec((tq, w), lambda bi, i: (bi * nq + i, 0)),
            pl.BlockSpec((l, w), lambda bi, i: (bi, 0)),
            pl.BlockSpec((l, w), lambda bi, i: (bi, 0)),
            pl.BlockSpec((1, tq, LANES), lambda bi, i: (bi, i, 0)),
            pl.BlockSpec((1, FOX_HEADS, l // tk, tk), lambda bi, i: (bi, 0, 0, 0)),
        ],
        out_specs=pl.BlockSpec((tq, w), lambda bi, i: (bi * nq + i, 0)),
        out_shape=jax.ShapeDtypeStruct((b * l, w), BF16),
        scratch_shapes=[pltpu.VMEM((FOX_HEADS, tq, LANES), F32)] * 3 + [pltpu.VMEM((FOX_HEADS, tq, HEAD_DIM), F32)],
        compiler_params=_params("parallel", "arbitrary"),
        name="fox_prompt",
    )(q, k, v, ccol, crow)


def _page_cumsum_body(x_ref, w_ref, m_ref):
    n = PAGE_SIZE * FOX_HEADS

    @pl.when(pl.program_id(0) == 0)
    def _():
        r = lax.broadcasted_iota(jnp.int32, (n, n), 0)
        c = lax.broadcasted_iota(jnp.int32, (n, n), 1)
        same_head = (r & (FOX_HEADS - 1)) == (c & (FOX_HEADS - 1))
        earlier = lax.shift_right_logical(r, 3) <= lax.shift_right_logical(c, 3)
        m_ref[...] = jnp.logical_and(same_head, earlier).astype(BF16)

    w_ref[...] = _dot_sel_r(x_ref[...], m_ref[...])


def _page_cumsum(logf_pages, *, tb=256):
    n_pool, n = logf_pages.shape
    return pl.pallas_call(
        _page_cumsum_body,
        grid=(n_pool // tb,),
        in_specs=[pl.BlockSpec((tb, n), lambda i: (i, 0))],
        out_specs=pl.BlockSpec((tb, n), lambda i: (i, 0)),
        out_shape=jax.ShapeDtypeStruct((n_pool, n), F32),
        scratch_shapes=[pltpu.VMEM((n, n), BF16)],
        compiler_params=_params("arbitrary"),
        name="page_cumsum",
    )(logf_pages)


def _seq_cumsum_body(seq_len, x_ref, o_ref):
    n = x_ref.shape[0]
    r = lax.broadcasted_iota(jnp.int32, (n, n), 0)
    c = lax.broadcasted_iota(jnp.int32, (n, n), 1)
    same_seq = (r // seq_len) == (c // seq_len)
    o_ref[...] = _dot_sel_l(jnp.logical_and(same_seq, c <= r).astype(BF16), x_ref[...])


def _seq_cumsum(x, seq_len, *, tb=128):
    t = x.shape[0]
    return pl.pallas_call(
        functools.partial(_seq_cumsum_body, seq_len),
        grid=(t // tb,),
        in_specs=[pl.BlockSpec((tb, LANES), lambda i: (i, 0))],
        out_specs=pl.BlockSpec((tb, LANES), lambda i: (i, 0)),
        out_shape=jax.ShapeDtypeStruct((t, LANES), F32),
        compiler_params=_params("parallel"),
        name="seq_cumsum",
    )(x)


def _fox_sample_body(pps, pt_ref, q_ref, kn_ref, vn_ref, cn_ref, *refs):
    kp_refs, vp_refs, w_refs = refs[:pps], refs[pps:2 * pps], refs[2 * pps:3 * pps]
    o_ref, q_scr, colq_ref, toff_ref, m_ref, l_ref, acc_ref = refs[3 * pps:]
    j = pl.program_id(1)
    nq = q_ref.shape[0]
    rows = nq * FOX_HEADS
    page_keys = PAGE_SIZE * FOX_HEADS
    row_id = lax.broadcasted_iota(jnp.int32, (rows, 1), 0)
    head_of_row = row_id & (FOX_HEADS - 1)
    query_of_row = lax.shift_right_logical(row_id, 3)

    @pl.when(j == 0)
    def _():
        q = q_ref[...].reshape(rows, HEAD_DIM).astype(BF16)
        q_scr[...] = q
        cn = cn_ref[0]
        key = lax.broadcasted_iota(jnp.int32, (1, rows), 1)
        colq = jnp.sum(jnp.where(key == row_id, cn, 0.0), axis=-1, keepdims=True)
        colq_ref[...] = colq
        toff_ref[...] = jnp.zeros_like(toff_ref)
        s = _dot_nt(q, kn_ref[...].reshape(rows, HEAD_DIM).astype(BF16)) + colq - cn
        valid = jnp.logical_and((key & (FOX_HEADS - 1)) == head_of_row,
                                lax.shift_right_logical(key, 3) <= query_of_row)
        s = jnp.where(valid, s, -jnp.inf)
        m = jnp.max(s, axis=-1, keepdims=True)
        p = jnp.exp(s - m)
        m_ref[...] = m
        l_ref[...] = jnp.sum(p, axis=-1, keepdims=True)
        acc_ref[...] = _dot(p.astype(BF16), vn_ref[...].reshape(rows, HEAD_DIM).astype(BF16))

    lane = lax.broadcasted_iota(jnp.int32, (1, LANES), 1)
    own_head = (lax.broadcasted_iota(jnp.int32, (1, page_keys), 1) & (FOX_HEADS - 1)) == head_of_row
    q = q_scr[...]
    colq = colq_ref[...]
    toff = toff_ref[...]
    tiles = []
    for kp_ref, w_ref in zip(kp_refs, w_refs):
        w = w_ref[0]
        last = jnp.where(lane == LANES - FOX_HEADS + head_of_row, w[:, page_keys - LANES:], 0.0)
        toff = toff + jnp.sum(last, axis=-1, keepdims=True)
        s = _dot_nt(q, kp_ref[0].reshape(page_keys, HEAD_DIM).astype(BF16))
        tiles.append(jnp.where(own_head, s + (colq + toff) - w, -jnp.inf))
    toff_ref[...] = toff
    m_old = m_ref[...]
    m = m_old
    for s in tiles:
        m = jnp.maximum(m, jnp.max(s, axis=-1, keepdims=True))
    alpha = jnp.exp(m_old - m)
    l = alpha * l_ref[...]
    acc = alpha * acc_ref[...]
    for s, vp_ref in zip(tiles, vp_refs):
        p = jnp.exp(s - m)
        l = l + jnp.sum(p, axis=-1, keepdims=True)
        acc = acc + _dot(p.astype(BF16), vp_ref[0].reshape(page_keys, HEAD_DIM).astype(BF16))
    m_ref[...] = m
    l_ref[...] = l
    acc_ref[...] = acc

    @pl.when(j == pl.num_programs(1) - 1)
    def _():
        o_ref[...] = (acc_ref[...] / l_ref[...]).reshape(nq, FOX_HEADS, HEAD_DIM)


def _fox_sample(page_table, q, k_new, v_new, c_new, k_pages, v_pages, w_pages, n_seq, nq, *, pages_per_step=16):
    n_pages = page_table.shape[1]
    rows = nq * FOX_HEADS
    pps = math.gcd(pages_per_step, n_pages)

    def page(i, ndim):
        return lambda b, j, pt: (pt[b * n_pages + (n_pages - 1 - j * pps - i)],) + (0,) * (ndim - 1)

    seq = pl.BlockSpec((nq, FOX_HEADS, HEAD_DIM), lambda b, j, pt: (b, 0, 0))
    kv_specs = [pl.BlockSpec((1, PAGE_SIZE, FOX_HEADS, HEAD_DIM), page(i, 4)) for i in range(pps)]
    w_specs = [pl.BlockSpec((1, 1, PAGE_SIZE * FOX_HEADS), page(i, 3)) for i in range(pps)]
    grid_spec = pltpu.PrefetchScalarGridSpec(
        num_scalar_prefetch=1,
        grid=(n_seq, n_pages // pps),
        in_specs=[seq, seq, seq, pl.BlockSpec((1, 1, rows), lambda b, j, pt: (b, 0, 0))]
        + kv_specs + kv_specs + w_specs,
        out_specs=seq,
        scratch_shapes=[
            pltpu.VMEM((rows, HEAD_DIM), BF16), pltpu.VMEM((rows, 1), F32), pltpu.VMEM((rows, 1), F32),
            pltpu.VMEM((rows, 1), F32), pltpu.VMEM((rows, 1), F32), pltpu.VMEM((rows, HEAD_DIM), F32),
        ],
    )
    return pl.pallas_call(
        functools.partial(_fox_sample_body, pps),
        grid_spec=grid_spec,
        out_shape=jax.ShapeDtypeStruct((n_seq * nq, FOX_HEADS, HEAD_DIM), F32),
        compiler_params=_params("parallel", "arbitrary"),
        name="fox_sample",
    )(page_table.reshape(-1), q, k_new, v_new, c_new, *([k_pages] * pps), *([v_pages] * pps), *([w_pages] * pps))


SSD_PAIRS = SSD_HEADS // 2
PAIRS_PER_GROUP = SSD_PAIRS // SSD_GROUPS


def _expander(width):
    n = SSD_HEADS * width
    h = lax.broadcasted_iota(jnp.int32, (LANES, n), 0)
    c = lax.broadcasted_iota(jnp.int32, (LANES, n), 1)
    return (lax.shift_right_logical(c, int(math.log2(width))) == h).astype(BF16)


def _ssd_local(xs, bm, cm, dt, alog, mask, tot_sel):
    n = xs.shape[0]
    lane = lax.broadcasted_iota(jnp.int32, (1, LANES), 1)
    dta = dt * jnp.where(lane < SSD_HEADS, -jnp.exp(alog), 0.0)
    e64 = _expander(SSD_HEAD_DIM)
    a_cum = _dot_sel_l(mask.astype(BF16), dta)
    a_cum_t = a_cum.T
    ac_exp = _dot_sel_r(a_cum, e64)
    if tot_sel is None:
        atot_exp = ac_exp[n - 1:n, :]
    else:
        atot_exp = _dot_sel_r(_dot_sel_l(tot_sel, dta), e64)
    ac_b = _dot_sel_r(a_cum, _expander(LANES))
    xdt = xs * _dot_sel_r(dt, e64)
    half = lax.broadcasted_iota(jnp.int32, (n, LANES), 1) < SSD_HEAD_DIM
    out = {
        "xdtw": xdt * jnp.exp(atot_exp - ac_exp),
        "eac": jnp.exp(ac_exp),
        "atot_exp": atot_exp,
        "ac_b": ac_b,
        "bg": [], "cg": [], "y_diag": [],
    }
    for g in range(SSD_GROUPS):
        gs = slice(g * SSD_STATE, (g + 1) * SSD_STATE)
        bg = bm[:, gs].astype(BF16)
        cg = cm[:, gs].astype(BF16)
        out["bg"].append(bg)
        out["cg"].append(cg)
        cb = _dot_nt(cg, bg)
        for k in range(g * PAIRS_PER_GROUP, (g + 1) * PAIRS_PER_GROUP):
            ps = slice(k * LANES, (k + 1) * LANES)
            ms = []
            for h in (2 * k, 2 * k + 1):
                seg = ac_b[:, h * LANES:(h + 1) * LANES] - a_cum_t[h:h + 1, :]
                ms.append(cb * jnp.exp(jnp.where(mask, seg, -jnp.inf)))
            m_cat = jnp.concatenate(ms, axis=1).astype(BF16)
            xp = xdt[:, ps]
            x_bd = jnp.concatenate([jnp.where(half, xp, 0.0), jnp.where(half, 0.0, xp)], axis=0).astype(BF16)
            out["y_diag"].append(_dot(m_cat, x_bd))
    return out


def _conv_ssd_sample_body(seq_len, xbc_ref, dt_ref, buf_ref, h0_ref, w_ref, b_ref, alog_ref, dexp_ref,
                          y_ref, hout_ref, cout_ref, xp_ref):
    n = xbc_ref.shape[0]
    n_seq = n // seq_len
    taps = CONV_WIDTH - 1
    base = 8 - taps

    x = xbc_ref[...]
    xp_ref[:, base:8, :] = buf_ref[...]
    xp_ref[:, 8:8 + seq_len, :] = x.reshape(n_seq, seq_len, CONV_DIM)
    acc = b_ref[...] + x * w_ref[taps:taps + 1, :]
    for j in range(taps):
        acc = acc + xp_ref[:, base + j:base + j + seq_len, :].reshape(n, CONV_DIM) * w_ref[j:j + 1, :]
    cout_ref[...] = xp_ref[:, 8 + seq_len - taps:8 + seq_len, :]
    conv = _silu(acc)
    xs = conv[:, :SSD_WIDTH]
    bm = conv[:, SSD_WIDTH:SSD_WIDTH + SSD_GROUPS * SSD_STATE]
    cm = conv[:, SSD_WIDTH + SSD_GROUPS * SSD_STATE:]

    r = lax.broadcasted_iota(jnp.int32, (n, n), 0)
    c = lax.broadcasted_iota(jnp.int32, (n, n), 1)
    same_seq = (r // seq_len) == (c // seq_len)
    loc = _ssd_local(xs, bm, cm, dt_ref[...], alog_ref[...], jnp.logical_and(same_seq, c <= r),
                     same_seq.astype(BF16))

    gw = PAIRS_PER_GROUP * LANES
    seq_of_col = lax.broadcasted_iota(jnp.int32, (1, n), 1) // seq_len
    decay_t = jnp.exp(loc["atot_exp"]).T
    for g in range(SSD_GROUPS):
        gr = slice(g * gw, (g + 1) * gw)
        h_prev = h0_ref[:, gr, :]
        z = _dot_nt(h_prev.reshape(n_seq * gw, SSD_STATE).astype(BF16), loc["cg"][g])
        y_off_t = jnp.zeros((gw, n), F32)
        for s in range(n_seq):
            y_off_t = y_off_t + jnp.where(seq_of_col == s, z[s * gw:(s + 1) * gw, :], 0.0)
        y_off = y_off_t.T * loc["eac"][:, gr]
        y = jnp.concatenate(loc["y_diag"][g * PAIRS_PER_GROUP:(g + 1) * PAIRS_PER_GROUP], axis=1)
        y_ref[:, gr] = y + y_off + xs[:, gr] * dexp_ref[:, gr]
        xw_t = loc["xdtw"][:, gr].T
        lhs = jnp.concatenate([jnp.where(seq_of_col == s, xw_t, 0.0) for s in range(n_seq)], axis=0)
        s_new = _dot(lhs.astype(BF16), loc["bg"][g])
        for s in range(n_seq):
            col = decay_t[gr, s * seq_len:s * seq_len + 1]
            hout_ref[s, gr, :] = h_prev[s] * col + s_new[s * gw:(s + 1) * gw, :]


def _conv_ssd_sample(xbc, dt, conv_buf, h0, conv_w, conv_b, alog, d_exp, n_seq, seq_len):
    assert seq_len == 8 and CONV_WIDTH - 1 <= seq_len
    tile = LANES
    ts = tile // seq_len
    const = lambda i: (0, 0)
    per_s = lambda i: (i, 0, 0)
    return pl.pallas_call(
        functools.partial(_conv_ssd_sample_body, seq_len),
        grid=(n_seq // ts,),
        in_specs=[
            pl.BlockSpec((tile, CONV_DIM), lambda i: (i, 0)),
            pl.BlockSpec((tile, LANES), lambda i: (i, 0)),
            pl.BlockSpec((ts, CONV_WIDTH - 1, CONV_DIM), per_s),
            pl.BlockSpec((ts, SSD_WIDTH, SSD_STATE), per_s),
            pl.BlockSpec((CONV_WIDTH, CONV_DIM), const),
            pl.BlockSpec((1, CONV_DIM), const),
            pl.BlockSpec((1, LANES), const),
            pl.BlockSpec((1, SSD_WIDTH), const),
        ],
        out_specs=[
            pl.BlockSpec((tile, SSD_WIDTH), lambda i: (i, 0)),
            pl.BlockSpec((ts, SSD_WIDTH, SSD_STATE), per_s),
            pl.BlockSpec((ts, CONV_WIDTH - 1, CONV_DIM), per_s),
        ],
        out_shape=[
            jax.ShapeDtypeStruct((n_seq * seq_len, SSD_WIDTH), F32),
            jax.ShapeDtypeStruct((n_seq, SSD_WIDTH, SSD_STATE), F32),
            jax.ShapeDtypeStruct((n_seq, CONV_WIDTH - 1, CONV_DIM), F32),
        ],
        scratch_shapes=[pltpu.VMEM((ts, 8 + seq_len, CONV_DIM), F32)],
        compiler_params=_params("parallel"),
        name="conv_ssd_sample",
    )(xbc, dt, conv_buf, h0, conv_w, conv_b, alog, d_exp)


def _conv_ssd_prompt_body(xbc_ref, dt_ref, buf_ref, h0_ref, w_ref, b_ref, alog_ref, dexp_ref,
                          y_ref, hout_ref, cout_ref, state_ref, xp_ref):
    c = pl.program_id(1)
    nc = pl.num_programs(1)
    tl = xbc_ref.shape[0]
    taps = CONV_WIDTH - 1
    base = 8 - taps

    @pl.when(c == 0)
    def _():
        state_ref[...] = h0_ref[0]
        xp_ref[base:8, :] = buf_ref[0]

    x = xbc_ref[...]
    xp_ref[8:8 + tl, :] = x
    acc = b_ref[...] + x * w_ref[taps:taps + 1, :]
    for j in range(taps):
        acc = acc + xp_ref[base + j:base + j + tl, :] * w_ref[j:j + 1, :]
    xp_ref[base:8, :] = x[tl - taps:, :]
    conv = _silu(acc)
    xs = conv[:, :SSD_WIDTH]
    bm = conv[:, SSD_WIDTH:SSD_WIDTH + SSD_GROUPS * SSD_STATE]
    cm = conv[:, SSD_WIDTH + SSD_GROUPS * SSD_STATE:]

    causal = _tri(tl)
    loc = _ssd_local(xs, bm, cm, dt_ref[...], alog_ref[...], causal, None)
    top = lax.broadcasted_iota(jnp.int32, (tl, LANES), 0) < SSD_HEAD_DIM
    for k in range(SSD_PAIRS):
        g = k // PAIRS_PER_GROUP
        ps = slice(k * LANES, (k + 1) * LANES)
        s_prev = state_ref[ps, :]
        y_off = _dot_nt(loc["cg"][g], s_prev.astype(BF16)) * loc["eac"][:, ps]
        y_ref[:, ps] = loc["y_diag"][k] + y_off + xs[:, ps] * dexp_ref[:, ps]
        cd = [jnp.exp(loc["ac_b"][tl - 1:tl, h * LANES:(h + 1) * LANES]) for h in (2 * k, 2 * k + 1)]
        state_ref[ps, :] = s_prev * jnp.where(top, cd[0], cd[1]) + _dot_tn(loc["xdtw"][:, ps].astype(BF16), loc["bg"][g])

    @pl.when(c == nc - 1)
    def _():
        hout_ref[0] = state_ref[...]
        cout_ref[0] = xp_ref[base:8, :]


def _conv_ssd_prompt(xbc, dt, conv_buf, h0, conv_w, conv_b, alog, d_exp, b, l):
    tl = SSD_CHUNK
    nc = l // tl
    const = lambda bi, c: (0, 0)
    per_b = lambda bi, c: (bi, 0, 0)
    return pl.pallas_call(
        _conv_ssd_prompt_body,
        grid=(b, nc),
        in_specs=[
            pl.BlockSpec((tl, CONV_DIM), lambda bi, c: (bi * nc + c, 0)),
            pl.BlockSpec((tl, LANES), lambda bi, c: (bi * nc + c, 0)),
            pl.BlockSpec((1, CONV_WIDTH - 1, CONV_DIM), per_b),
            pl.BlockSpec((1, SSD_WIDTH, SSD_STATE), per_b),
            pl.BlockSpec((CONV_WIDTH, CONV_DIM), const),
            pl.BlockSpec((1, CONV_DIM), const),
            pl.BlockSpec((1, LANES), const),
            pl.BlockSpec((1, SSD_WIDTH), const),
        ],
        out_specs=[
            pl.BlockSpec((tl, SSD_WIDTH), lambda bi, c: (bi * nc + c, 0)),
            pl.BlockSpec((1, SSD_WIDTH, SSD_STATE), per_b),
            pl.BlockSpec((1, CONV_WIDTH - 1, CONV_DIM), per_b),
        ],
        out_shape=[
            jax.ShapeDtypeStruct((b * l, SSD_WIDTH), F32),
            jax.ShapeDtypeStruct((b, SSD_WIDTH, SSD_STATE), F32),
            jax.ShapeDtypeStruct((b, CONV_WIDTH - 1, CONV_DIM), F32),
        ],
        scratch_shapes=[pltpu.VMEM((SSD_WIDTH, SSD_STATE), F32), pltpu.VMEM((8 + tl, CONV_DIM), F32)],
        compiler_params=_params("parallel", "arbitrary"),
        name="conv_ssd_prompt",
    )(xbc, dt, conv_buf, h0, conv_w, conv_b, alog, d_exp)


def _out_proj_body(h_ref, fox_ref, y_ref, z_ref, gs_ref, w_ref, gx_ref, wq_ref, gq_ref, o_ref, q_ref):
    yn = _rms(y_ref[...] * _silu(z_ref[...]), gs_ref[...]).astype(BF16)
    fw = fox_ref.shape[1]
    h = h_ref[...] + _dot(fox_ref[...].astype(BF16), w_ref[:fw, :]) + _dot(yn, w_ref[fw:, :])
    o_ref[...] = h
    q = _dot(_rms(h, gx_ref[...]).astype(BF16), wq_ref[...])
    q_ref[...] = _head_norm(q, gq_ref[...]) * (HEAD_DIM ** -0.5)


def _out_proj(h, fox, y, z, g_ssd, w_out, g_x, wq, gq, *, tm=512):
    t, d = h.shape
    tm = min(tm, t)
    row = lambda w: pl.BlockSpec((tm, w), lambda i: (i, 0))
    full = lambda a: pl.BlockSpec(a.shape, lambda i: (0, 0), pipeline_mode=pl.Buffered(1))
    consts = [g_ssd.reshape(1, SSD_WIDTH), w_out, g_x.reshape(1, d), wq, gq.reshape(1, HEAD_DIM)]
    return pl.pallas_call(
        _out_proj_body,
        grid=(t // tm,),
        in_specs=[row(d), row(FOX_WIDTH), row(SSD_WIDTH), row(SSD_WIDTH)] + [full(a) for a in consts],
        out_specs=[row(d), row(XATTN_WIDTH)],
        out_shape=[jax.ShapeDtypeStruct((t, d), F32), jax.ShapeDtypeStruct((t, XATTN_WIDTH), F32)],
        compiler_params=_params("parallel"),
        name="out_proj",
    )(h, fox, y, z, *consts)


def _mem_kv_body(m_ref, g_ref, w_ref, gk_ref, k_ref, v_ref):
    kv = _dot(_rms(m_ref[...], g_ref[...]).astype(BF16), w_ref[...])
    k_ref[...] = _head_norm(kv[:, :XATTN_WIDTH], gk_ref[...])
    v_ref[...] = kv[:, XATTN_WIDTH:]


def _mem_kv(mem, g, w_kv, gk, *, tm=256):
    t, d = mem.shape
    row = lambda w: pl.BlockSpec((tm, w), lambda i: (i, 0))
    full = lambda a: pl.BlockSpec(a.shape, lambda i: (0, 0))
    consts = [g.reshape(1, d), w_kv, gk.reshape(1, HEAD_DIM)]
    return pl.pallas_call(
        _mem_kv_body,
        grid=(t // tm,),
        in_specs=[row(d)] + [full(a) for a in consts],
        out_specs=[row(XATTN_WIDTH), row(XATTN_WIDTH)],
        out_shape=[jax.ShapeDtypeStruct((t, XATTN_WIDTH), F32)] * 2,
        compiler_params=_params("parallel"),
        name="mem_kv",
    )(mem, *consts)


def _xattn_body(q_ref, k_ref, v_ref, o_ref):
    for h in range(XATTN_HEADS):
        hs = slice(h * HEAD_DIM, (h + 1) * HEAD_DIM)
        s = _dot_nt(q_ref[:, hs].astype(BF16), k_ref[0, :, hs].astype(BF16))
        p = jnp.exp(s - jnp.max(s, axis=-1, keepdims=True))
        o = _dot(p.astype(BF16), v_ref[0, :, hs].astype(BF16))
        o_ref[:, hs] = o / jnp.sum(p, axis=-1, keepdims=True)


def _xattn(q, mem_k, mem_v, b, l, *, tq=512):
    tq = min(tq, l)
    nq = l // tq
    n_mem = mem_k.shape[1]
    qspec = pl.BlockSpec((tq, XATTN_WIDTH), lambda bi, i: (bi * nq + i, 0))
    mspec = pl.BlockSpec((1, n_mem, XATTN_WIDTH), lambda bi, i: (bi, 0, 0))
    return pl.pallas_call(
        _xattn_body,
        grid=(b, nq),
        in_specs=[qspec, mspec, mspec],
        out_specs=qspec,
        out_shape=jax.ShapeDtypeStruct((b * l, XATTN_WIDTH), F32),
        compiler_params=_params("parallel", "arbitrary"),
        name="xattn",
    )(q, mem_k, mem_v)


def _xattn_rows_body(n_seq, q_ref, k_ref, v_ref, o_ref):
    tiles = q_ref.shape[0] // n_seq
    rows = tiles * 8
    keys = k_ref.shape[1] * k_ref.shape[2]
    row_head = lax.broadcasted_iota(jnp.int32, (rows, 1), 0) & (XATTN_HEADS - 1)
    key_head = lax.broadcasted_iota(jnp.int32, (1, keys), 1) & (XATTN_HEADS - 1)
    own_head = row_head == key_head
    for s in range(n_seq):
        qs = slice(s * tiles, (s + 1) * tiles)
        q = q_ref[qs].reshape(rows, HEAD_DIM).astype(BF16)
        sc = jnp.where(own_head, _dot_nt(q, k_ref[s].reshape(keys, HEAD_DIM).astype(BF16)), -jnp.inf)
        p = jnp.exp(sc - jnp.max(sc, axis=-1, keepdims=True))
        o = _dot(p.astype(BF16), v_ref[s].reshape(keys, HEAD_DIM).astype(BF16))
        o_ref[qs] = (o / jnp.sum(p, axis=-1, keepdims=True)).reshape(tiles, 8, HEAD_DIM)


def _xattn_rows(q, mem_k, mem_v, n_seq, *, ts=8):
    tiles = q.shape[0] // n_seq
    ts = math.gcd(ts, n_seq)
    qspec = pl.BlockSpec((ts * tiles, 8, HEAD_DIM), lambda i: (i, 0, 0))
    mspec = pl.BlockSpec((ts,) + mem_k.shape[1:], lambda i: (i, 0, 0, 0))
    return pl.pallas_call(
        functools.partial(_xattn_rows_body, ts),
        grid=(n_seq // ts,),
        in_specs=[qspec, mspec, mspec],
        out_specs=qspec,
        out_shape=jax.ShapeDtypeStruct(q.shape, F32),
        compiler_params=_params("parallel"),
        name="xattn_rows",
    )(q, mem_k, mem_v)


def kernel(x_prompt, x_sample, cache_fox_k, cache_fox_v, cache_fox_logf, cache_mem_k, cache_mem_v, state_ssm, state_conv, page_table, mem_prompt, ffn1_norm, ffn1_w_gate, ffn1_w_up, ffn1_w_down, mix_norm, w_in, fox_b_f, fox_q_norm, fox_k_norm, conv_w, conv_b, ssd_dt_bias, ssd_A_log, ssd_D, ssd_out_norm, w_out, xattn_norm, mem_norm, xattn_w_q, xattn_w_kv, xattn_q_norm, xattn_k_norm, xattn_w_o, ffn2_norm, ffn2_w_gate, ffn2_w_up, ffn2_w_down):
    assert x_prompt.shape[2] == D_MODEL and ffn1_norm.shape[0] == 1
    d = D_MODEL
    bp, lp = x_prompt.shape[:2]
    bs, ls = x_sample.shape[:2]
    n_mem = mem_prompt.shape[1]
    n_pool = cache_fox_k.shape[1]

    bf = lambda w: w[0].astype(BF16)
    ffn1 = (ffn1_norm[0], ffn1_w_gate[0], ffn1_w_up[0], bf(ffn1_w_down))
    ffn2 = (ffn2_norm[0], ffn2_w_gate[0], ffn2_w_up[0], bf(ffn2_w_down))
    in_w = _pack_in_proj(w_in[0], fox_b_f[0], ssd_dt_bias[0])
    wq, wkv, wo = bf(xattn_w_q), bf(xattn_w_kv), bf(xattn_w_o)
    alog = jnp.pad(ssd_A_log[0], (0, LANES - SSD_HEADS)).reshape(1, LANES)
    d_exp = jnp.repeat(ssd_D[0], SSD_HEAD_DIM).reshape(1, SSD_WIDTH)
    cw, cb = conv_w[0], conv_b[0].reshape(1, CONV_DIM)

    def front(x, q_dtype, q_scale):
        h1 = _ffn(x, *ffn1)
        return h1, _in_proj(h1, mix_norm[0], *in_w, fox_q_norm[0], fox_k_norm[0], q_dtype=q_dtype, q_scale=q_scale)

    def back(h1, fox, y, z, attend):
        h2, xq = _out_proj(h1, fox, y, z, ssd_out_norm[0], bf(w_out), xattn_norm[0], wq, xattn_q_norm[0])
        return _ffn(h2, *ffn2, pre=(attend(xq), wo))

    h1, (q, k_p, kb, v_p, vb, z, xbc, logf_p, dt) = front(x_prompt.reshape(bp * lp, d), BF16, HEAD_DIM ** -0.5 * LOG2E)
    ccol, crow = _cumsum(logf_p.reshape(bp, lp, LANES))
    tk = min(FOX_KEY_BLOCK, lp)
    fox = _fox_prompt(q, kb, vb, ccol, crow.reshape(bp, FOX_HEADS, lp // tk, tk), bp, lp, tq=FOX_QUERY_BLOCK)
    y, ssm_p, conv_p = _conv_ssd_prompt(
        xbc, dt, jnp.zeros((bp, CONV_WIDTH - 1, CONV_DIM), F32), jnp.zeros((bp, SSD_WIDTH, SSD_STATE), F32),
        cw, cb, alog, d_exp, bp, lp)
    mk, mv = _mem_kv(mem_prompt.reshape(bp * n_mem, d), mem_norm[0], wkv, xattn_k_norm[0])
    y_prompt = back(h1, fox, y, z, lambda xq: _xattn(
        xq, mk.reshape(bp, n_mem, XATTN_WIDTH), mv.reshape(bp, n_mem, XATTN_WIDTH), bp, lp))

    h1, (q, k_s, _, v_s, _, z, xbc, logf_s, dt) = front(x_sample.reshape(bs * ls, d), F32, HEAD_DIM ** -0.5)
    w_pages = _page_cumsum(cache_fox_logf[0].reshape(n_pool, PAGE_SIZE * FOX_HEADS))
    c_new = _seq_cumsum(logf_s, ls)[:, :FOX_HEADS].reshape(bs, 1, ls * FOX_HEADS)
    fox = _fox_sample(
        page_table, q.reshape(bs * ls, FOX_HEADS, HEAD_DIM), k_s, v_s, c_new, cache_fox_k[0], cache_fox_v[0],
        w_pages.reshape(n_pool, 1, PAGE_SIZE * FOX_HEADS), bs, ls).reshape(bs * ls, FOX_WIDTH)
    y, ssm_s, conv_s = _conv_ssd_sample(
        xbc, dt, state_conv[0], state_ssm[0].reshape(bs, SSD_WIDTH, SSD_STATE), cw, cb, alog, d_exp, bs, ls)
    mem_tiles = lambda m: m[0].reshape(bs, n_mem * XATTN_HEADS // 8, 8, HEAD_DIM)
    y_sample = back(h1, fox, y, z, lambda xq: _xattn_rows(
        xq.reshape(bs * ls * XATTN_HEADS // 8, 8, HEAD_DIM), mem_tiles(cache_mem_k), mem_tiles(cache_mem_v), bs,
    ).reshape(bs * ls, XATTN_WIDTH))

    fox_shape = lambda b, l: (1, b, l, FOX_HEADS, HEAD_DIM)
    ssm_shape = lambda b: (1, b, SSD_HEADS, SSD_HEAD_DIM, SSD_STATE)
    mem_shape = (1, bp, n_mem, XATTN_HEADS, HEAD_DIM)
    return (
        y_prompt.reshape(bp, lp, d), y_sample.reshape(bs, ls, d),
        k_p.reshape(fox_shape(bp, lp)), v_p.reshape(fox_shape(bp, lp)),
        logf_p[:, :FOX_HEADS].reshape(1, bp, lp, FOX_HEADS),
        ssm_p.reshape(ssm_shape(bp)), conv_p[None], mk.reshape(mem_shape), mv.reshape(mem_shape),
        k_s.reshape(fox_shape(bs, ls)), v_s.reshape(fox_shape(bs, ls)),
        logf_s[:, :FOX_HEADS].reshape(1, bs, ls, FOX_HEADS),
        ssm_s.reshape(ssm_shape(bs)), conv_s[None],
    )
```

```python
import functools
import math

import jax
import jax.numpy as jnp
from jax import lax
from jax.experimental import pallas as pl
from jax.experimental.pallas import tpu as pltpu

F32 = jnp.float32
BF16 = jnp.bfloat16

EPS = 1e-6
LOG2E = math.log2(math.e)
FFN_RESIDUAL = 0.5
D_MODEL = 2048
D_FF = 5632
PAGE_SIZE = 128
FOX_HEADS = 8
HEAD_DIM = 128
FOX_WIDTH = FOX_HEADS * HEAD_DIM
SSD_HEADS = 16
SSD_HEAD_DIM = 64
SSD_WIDTH = SSD_HEADS * SSD_HEAD_DIM
SSD_GROUPS = 2
SSD_STATE = 128
SSD_CHUNK = 128
CONV_WIDTH = 4
CONV_DIM = SSD_WIDTH + 2 * SSD_GROUPS * SSD_STATE
XATTN_HEADS = 4
XATTN_WIDTH = XATTN_HEADS * HEAD_DIM
LANES = 128
VMEM_LIMIT_BYTES = 56 * 1024 * 1024

def _params(*semantics):
    return pltpu.CompilerParams(dimension_semantics=semantics, vmem_limit_bytes=VMEM_LIMIT_BYTES)


def _rms(x, g):
    return x * lax.rsqrt(jnp.mean(x * x, axis=-1, keepdims=True) + EPS) * g


def _dot(a, b):
    return jnp.dot(a, b, preferred_element_type=F32)


def _dot_nt(a, b):
    return lax.dot_general(a, b, (((1,), (1,)), ((), ())), preferred_element_type=F32)


def _dot_tn(a, b):
    return lax.dot_general(a, b, (((0,), (0,)), ((), ())), preferred_element_type=F32)


def _split3(x):
    x1 = x.astype(BF16)
    r = x - x1.astype(F32)
    x2 = r.astype(BF16)
    x3 = (r - x2.astype(F32)).astype(BF16)
    return x1, x2, x3


def _dot_sel_l(sel, x):
    x1, x2, x3 = _split3(x)
    return _dot(sel, x1) + _dot(sel, x2) + _dot(sel, x3)


def _dot_sel_r(x, sel):
    x1, x2, x3 = _split3(x)
    return _dot(x1, sel) + _dot(x2, sel) + _dot(x3, sel)


def _silu(x):
    return x * jax.nn.sigmoid(x)


def _ffn_body(pre_proj, split, *refs):
    if pre_proj:
        x_ref, a_ref, wo_ref, g_ref, *w_refs, o_ref, xn_ref = refs
    else:
        x_ref, g_ref, *w_refs, o_ref, xn_ref = refs

    @pl.when(pl.program_id(1) == 0)
    def _():
        x = x_ref[...]
        if pre_proj:
            x = x + _dot(a_ref[...].astype(BF16), wo_ref[...])
        xn_ref[...] = _rms(x, g_ref[...]).astype(BF16)
        o_ref[...] = x

    xn = xn_ref[...]
    acc = None
    for wg_ref, wu_ref, wd_ref in zip(w_refs[:split], w_refs[split:2 * split], w_refs[2 * split:]):
        gate = _dot(xn, wg_ref[...].astype(BF16))
        up = _dot(xn, wu_ref[...].astype(BF16))
        h = (_silu(gate) * up * FFN_RESIDUAL).astype(BF16)
        part = _dot(h, wd_ref[...].astype(BF16))
        acc = part if acc is None else acc + part
    o_ref[...] += acc


def _ffn(x, g, wg, wu, wd, pre=None, *, tm=1024, tf=256, split=1):
    t, d = x.shape
    f = wg.shape[1]
    tm = min(tm, t)
    ts = tf // split
    grid = (t // tm, f // tf)
    row = lambda i, j: (i, 0)
    in_specs = [pl.BlockSpec((tm, d), row)]
    args = [x]
    if pre is not None:
        a, wo = pre
        in_specs = [pl.BlockSpec((tm, d), row, pipeline_mode=pl.Buffered(1)),
                    pl.BlockSpec((tm, a.shape[1]), row), pl.BlockSpec(wo.shape, lambda i, j: (0, 0))]
        args += [a, wo]
    cols = [pl.BlockSpec((d, ts), lambda i, j, s=s: (0, j * split + s)) for s in range(split)]
    rows = [pl.BlockSpec((ts, d), lambda i, j, s=s: (j * split + s, 0)) for s in range(split)]
    in_specs += [pl.BlockSpec((1, d), lambda i, j: (0, 0))] + cols + cols + rows
    args += [g.reshape(1, d)] + [wg] * split + [wu] * split + [wd] * split
    return pl.pallas_call(
        functools.partial(_ffn_body, pre is not None, split),
        grid=grid,
        in_specs=in_specs,
        out_specs=pl.BlockSpec((tm, d), row),
        out_shape=jax.ShapeDtypeStruct((t, d), F32),
        scratch_shapes=[pltpu.VMEM((tm, d), BF16)],
        compiler_params=_params("parallel", "arbitrary"),
        name="ffn_pre" if pre is not None else "ffn",
    )(*args)


IN_QKV_TN = 1024
IN_ZX_TN = 512


def _head_norm(y, g):
    outs = []
    for c in range(y.shape[1] // HEAD_DIM):
        yc = y[:, c * HEAD_DIM:(c + 1) * HEAD_DIM]
        outs.append(yc * lax.rsqrt(jnp.mean(yc * yc, axis=-1, keepdims=True) + EPS) * g)
    return jnp.concatenate(outs, axis=1)


def _in_qkv_body(q_scale, x_ref, g_ref, w_ref, ws_ref, bs_ref, gq_ref, gk_ref,
                 q_ref, k_ref, kb_ref, v_ref, vb_ref, logf_ref, dt_ref, u_ref):
    j = pl.program_id(1)

    @pl.when(j == 0)
    def _():
        u = _rms(x_ref[...], g_ref[...]).astype(BF16)
        u_ref[...] = u
        s = _dot(u, ws_ref[...]) + bs_ref[...]
        t = jnp.log1p(jnp.exp(-jnp.abs(s)))
        logf_ref[...] = (jnp.minimum(s, 0.0) - t)[:, :LANES]
        dt_ref[...] = (jnp.maximum(s, 0.0) + t)[:, LANES:]
        q_ref[...] = (_head_norm(_dot(u, w_ref[...]), gq_ref[...]) * q_scale).astype(q_ref.dtype)

    def by_head(y, out_ref, bf_ref):
        bf_ref[...] = y.astype(BF16)
        for h in range(FOX_HEADS):
            out_ref[:, h, :] = y[:, h * HEAD_DIM:(h + 1) * HEAD_DIM]

    @pl.when(j == 1)
    def _():
        by_head(_head_norm(_dot(u_ref[...], w_ref[...]), gk_ref[...]), k_ref, kb_ref)

    @pl.when(j == 2)
    def _():
        by_head(_dot(u_ref[...], w_ref[...]), v_ref, vb_ref)


def _in_qkv(x, g, w_qkv, w_small, b_small, gq, gk, *, q_dtype, q_scale, tm=512):
    t, d = x.shape
    tm = min(tm, t)
    tn = IN_QKV_TN
    assert tn == FOX_WIDTH and w_qkv.shape[1] >= 3 * tn
    const = lambda i, j: (0, 0)
    rows = lambda w: pl.BlockSpec((tm, w), lambda i, j: (i, 0))
    by_head = pl.BlockSpec((tm, FOX_HEADS, HEAD_DIM), lambda i, j: (i, 0, 0))
    by_head_shape = jax.ShapeDtypeStruct((t, FOX_HEADS, HEAD_DIM), F32)
    flat = lambda dt, w: jax.ShapeDtypeStruct((t, w), dt)
    return pl.pallas_call(
        functools.partial(_in_qkv_body, q_scale),
        grid=(t // tm, 3),
        in_specs=[
            rows(d),
            pl.BlockSpec((1, d), const),
            pl.BlockSpec((d, tn), lambda i, j: (0, j)),
            pl.BlockSpec((d, 2 * LANES), const),
            pl.BlockSpec((1, 2 * LANES), const),
            pl.BlockSpec((1, HEAD_DIM), const),
            pl.BlockSpec((1, HEAD_DIM), const),
        ],
        out_specs=[rows(FOX_WIDTH), by_head, rows(FOX_WIDTH), by_head, rows(FOX_WIDTH), rows(LANES), rows(LANES),
                   rows(d)],
        out_shape=[flat(q_dtype, FOX_WIDTH), by_head_shape, flat(BF16, FOX_WIDTH), by_head_shape,
                   flat(BF16, FOX_WIDTH), flat(F32, LANES), flat(F32, LANES), flat(BF16, d)],
        compiler_params=_params("parallel", "arbitrary"),
        name="in_qkv",
    )(x, g.reshape(1, d), w_qkv, w_small, b_small, gq.reshape(1, HEAD_DIM), gk.reshape(1, HEAD_DIM))


def _in_zx_body(u_ref, wa_ref, wb_ref, z_ref, xbc_ref):
    j = pl.program_id(1)
    half = wa_ref.shape[1]
    tn = 2 * half
    nz = z_ref.shape[1] // tn
    for c in range(nz + xbc_ref.shape[1] // tn):
        out_ref, sub = (z_ref, c) if c < nz else (xbc_ref, c - nz)

        @pl.when(j == c)
        def _(out_ref=out_ref, sub=sub):
            out_ref[:, sub * tn:sub * tn + half] = _dot(u_ref[...], wa_ref[...])
            out_ref[:, sub * tn + half:(sub + 1) * tn] = _dot(u_ref[...], wb_ref[...])


def _in_zx(u, w_zx, *, tm=1024):
    t, d = u.shape
    tm = min(tm, t)
    tn = IN_ZX_TN
    rows = lambda w: pl.BlockSpec((tm, w), lambda i, j: (i, 0))
    return pl.pallas_call(
        _in_zx_body,
        grid=(t // tm, w_zx.shape[1] // tn),
        in_specs=[rows(d), pl.BlockSpec((d, tn // 2), lambda i, j: (0, 2 * j)),
                  pl.BlockSpec((d, tn // 2), lambda i, j: (0, 2 * j + 1))],
        out_specs=[rows(SSD_WIDTH), rows(CONV_DIM)],
        out_shape=[jax.ShapeDtypeStruct((t, SSD_WIDTH), F32), jax.ShapeDtypeStruct((t, CONV_DIM), F32)],
        compiler_params=_params("parallel", "arbitrary"),
        name="in_zx",
    )(u, w_zx, w_zx)


def _in_proj(x, g, w_qkv, w_zx, w_small, b_small, gq, gk, *, q_dtype, q_scale):
    q, k, kb, v, vb, logf, dt, u = _in_qkv(x, g, w_qkv, w_small, b_small, gq, gk, q_dtype=q_dtype, q_scale=q_scale)
    z, xbc = _in_zx(u, w_zx)
    return q, k, kb, v, vb, z, xbc, logf, dt


def _pack_in_proj(w_in, fox_b_f, ssd_dt_bias):
    fw = FOX_WIDTH
    f0 = 3 * fw
    z0 = f0 + FOX_HEADS
    x0 = z0 + SSD_WIDTH
    d0 = x0 + CONV_DIM
    w_in = w_in.astype(BF16)
    w_qkv = w_in
    w_zx = w_in[:, z0:d0]
    zeros = lambda n: jnp.zeros((w_in.shape[0], n), BF16)
    w_small = jnp.concatenate(
        [w_in[:, f0:z0], zeros(LANES - FOX_HEADS), w_in[:, d0:], zeros(LANES - SSD_HEADS)], axis=1)
    b_small = jnp.concatenate(
        [fox_b_f, jnp.zeros((LANES - FOX_HEADS,), F32), ssd_dt_bias, jnp.zeros((LANES - SSD_HEADS,), F32)]
    ).reshape(1, 2 * LANES)
    return w_qkv, w_zx, w_small, b_small


ATT_BLOCK = 256
FOX_QUERY_BLOCK = 512
FOX_KEY_BLOCK = 256


def _tri(n, *, strict=False, upper=False):
    r = lax.broadcasted_iota(jnp.int32, (n, n), 0)
    c = lax.broadcasted_iota(jnp.int32, (n, n), 1)
    if upper:
        r, c = c, r
    return (c < r) if strict else (c <= r)


def _cumsum_body(x_ref, col_ref, row_ref, carry_ref):
    @pl.when(pl.program_id(1) == 0)
    def _():
        carry_ref[...] = jnp.zeros_like(carry_ref)

    n = x_ref.shape[1]
    tril = _tri(n).astype(BF16)
    c = _dot_sel_l(tril, x_ref[0]) + carry_ref[...]
    carry_ref[...] = c[n - 1:n, :]
    c = c * LOG2E
    col_ref[0] = c
    row_ref[0] = c.T[:FOX_HEADS, :]


def _cumsum(x, *, tb=ATT_BLOCK):
    b, l, _ = x.shape
    return pl.pallas_call(
        _cumsum_body,
        grid=(b, l // tb),
        in_specs=[pl.BlockSpec((1, tb, LANES), lambda i, j: (i, j, 0))],
        out_specs=[pl.BlockSpec((1, tb, LANES), lambda i, j: (i, j, 0)),
                   pl.BlockSpec((1, FOX_HEADS, tb), lambda i, j: (i, 0, j))],
        out_shape=[jax.ShapeDtypeStruct((b, l, LANES), F32), jax.ShapeDtypeStruct((b, FOX_HEADS, l), F32)],
        scratch_shapes=[pltpu.VMEM((1, LANES), F32)],
        compiler_params=_params("parallel", "arbitrary"),
        name="logf_cumsum",
    )(x)


def _fox_prompt_body(tk, q_ref, k_ref, v_ref, ccol_ref, crow_ref, o_ref, m_ref, l_ref, cq_ref, acc_ref):
    i = pl.program_id(1)
    tq = q_ref.shape[0]
    rep = tk // LANES
    m_ref[...] = jnp.full_like(m_ref, -jnp.inf)
    l_ref[...] = jnp.zeros_like(l_ref)
    acc_ref[...] = jnp.zeros_like(acc_ref)
    for h in range(FOX_HEADS):
        cq_ref[h] = jnp.broadcast_to(ccol_ref[0, :, h:h + 1], (tq, LANES))
    row = i * tq + lax.broadcasted_iota(jnp.int32, (tq, tk), 0)
    col = lax.broadcasted_iota(jnp.int32, (tq, tk), 1)
    wide = lambda x: jnp.concatenate([x] * rep, axis=1)

    def block(j, masked):
        ks = pl.ds(pl.multiple_of(j * tk, tk), tk)
        for h in range(FOX_HEADS):
            hs = slice(h * HEAD_DIM, (h + 1) * HEAD_DIM)
            s = _dot_nt(q_ref[:, hs], k_ref[ks, hs]) + (wide(cq_ref[h]) - crow_ref[0, h, pl.ds(j, 1), :])
            if masked:
                s = jnp.where(col + j * tk <= row, s, -jnp.inf)
            m_old = m_ref[h]
            m_new = jnp.maximum(m_old, jnp.max(s, axis=-1, keepdims=True))
            alpha = jnp.exp2(m_old - m_new)
            p = jnp.exp2(s - wide(m_new))
            m_ref[h] = m_new
            l_ref[h] = alpha * l_ref[h] + jnp.sum(p, axis=-1, keepdims=True)
            acc_ref[h] = alpha * acc_ref[h] + _dot(p.astype(BF16), v_ref[ks, hs])

    first_masked = (i * tq) // tk

    def step(j, carry):
        block(j, False)
        return carry

    lax.fori_loop(0, first_masked, step, 0)
    for extra in range(max(tq // tk, 1)):
        block(first_masked + extra, True)
    for h in range(FOX_HEADS):
        o_ref[:, h * HEAD_DIM:(h + 1) * HEAD_DIM] = (acc_ref[h] / l_ref[h]).astype(o_ref.dtype)


def _fox_prompt(q, k, v, ccol, crow, b, l, *, tq=128):
    tk = crow.shape[-1]
    tq = min(tq, l)
    assert tk % tq == 0 or tq % tk == 0
    nq = l // tq
    w = FOX_WIDTH
    return pl.pallas_call(
        functools.partial(_fox_prompt_body, tk),
        grid=(b, nq),
        in_specs=[
            pl.BlockSpec((tq, w), lambda bi, i: (bi * nq + i, 0)),
            pl.BlockSpec((l, w), lambda bi, i: (bi, 0)),
            pl.BlockSpec((l, w), lambda bi, i: (bi, 0)),
            pl.BlockSpec((1, tq, LANES), lambda bi, i: (bi, i, 0)),
            pl.BlockSpec((1, FOX_HEADS, l // tk, tk), lambda bi, i: (bi, 0, 0, 0)),
        ],
        out_specs=pl.BlockSpec((tq, w), lambda bi, i: (bi * nq + i, 0)),
        out_shape=jax.ShapeDtypeStruct((b * l, w), BF16),
        scratch_shapes=[pltpu.VMEM((FOX_HEADS, tq, LANES), F32)] * 3 + [pltpu.VMEM((FOX_HEADS, tq, HEAD_DIM), F32)],
        compiler_params=_params("parallel", "arbitrary"),
        name="fox_prompt",
    )(q, k, v, ccol, crow)


def _page_cumsum_body(x_ref, w_ref, m_ref):
    n = PAGE_SIZE * FOX_HEADS

    @pl.when(pl.program_id(0) == 0)
    def _():
        r = lax.broadcasted_iota(jnp.int32, (n, n), 0)
        c = lax.broadcasted_iota(jnp.int32, (n, n), 1)
        same_head = (r & (FOX_HEADS - 1)) == (c & (FOX_HEADS - 1))
        earlier = lax.shift_right_logical(r, 3) <= lax.shift_right_logical(c, 3)
        m_ref[...] = jnp.logical_and(same_head, earlier).astype(BF16)

    w_ref[...] = _dot_sel_r(x_ref[...], m_ref[...])


def _page_cumsum(logf_pages, *, tb=256):
    n_pool, n = logf_pages.shape
    return pl.pallas_call(
        _page_cumsum_body,
        grid=(n_pool // tb,),
        in_specs=[pl.BlockSpec((tb, n), lambda i: (i, 0))],
        out_specs=pl.BlockSpec((tb, n), lambda i: (i, 0)),
        out_shape=jax.ShapeDtypeStruct((n_pool, n), F32),
        scratch_shapes=[pltpu.VMEM((n, n), BF16)],
        compiler_params=_params("arbitrary"),
        name="page_cumsum",
    )(logf_pages)


def _seq_cumsum_body(seq_len, x_ref, o_ref):
    n = x_ref.shape[0]
    r = lax.broadcasted_iota(jnp.int32, (n, n), 0)
    c = lax.broadcasted_iota(jnp.int32, (n, n), 1)
    same_seq = (r // seq_len) == (c // seq_len)
    o_ref[...] = _dot_sel_l(jnp.logical_and(same_seq, c <= r).astype(BF16), x_ref[...])


def _seq_cumsum(x, seq_len, *, tb=128):
    t = x.shape[0]
    return pl.pallas_call(
        functools.partial(_seq_cumsum_body, seq_len),
        grid=(t // tb,),
        in_specs=[pl.BlockSpec((tb, LANES), lambda i: (i, 0))],
        out_specs=pl.BlockSpec((tb, LANES), lambda i: (i, 0)),
        out_shape=jax.ShapeDtypeStruct((t, LANES), F32),
        compiler_params=_params("parallel"),
        name="seq_cumsum",
    )(x)


def _fox_sample_body(pps, pt_ref, q_ref, kn_ref, vn_ref, cn_ref, *refs):
    kp_refs, vp_refs, w_refs = refs[:pps], refs[pps:2 * pps], refs[2 * pps:3 * pps]
    o_ref, q_scr, colq_ref, toff_ref, m_ref, l_ref, acc_ref = refs[3 * pps:]
    j = pl.program_id(1)
    nq = q_ref.shape[0]
    rows = nq * FOX_HEADS
    page_keys = PAGE_SIZE * FOX_HEADS
    row_id = lax.broadcasted_iota(jnp.int32, (rows, 1), 0)
    head_of_row = row_id & (FOX_HEADS - 1)
    query_of_row = lax.shift_right_logical(row_id, 3)

    @pl.when(j == 0)
    def _():
        q = q_ref[...].reshape(rows, HEAD_DIM).astype(BF16)
        q_scr[...] = q
        cn = cn_ref[0]
        key = lax.broadcasted_iota(jnp.int32, (1, rows), 1)
        colq = jnp.sum(jnp.where(key == row_id, cn, 0.0), axis=-1, keepdims=True)
        colq_ref[...] = colq
        toff_ref[...] = jnp.zeros_like(toff_ref)
        s = _dot_nt(q, kn_ref[...].reshape(rows, HEAD_DIM).astype(BF16)) + colq - cn
        valid = jnp.logical_and((key & (FOX_HEADS - 1)) == head_of_row,
                                lax.shift_right_logical(key, 3) <= query_of_row)
        s = jnp.where(valid, s, -jnp.inf)
        m = jnp.max(s, axis=-1, keepdims=True)
        p = jnp.exp(s - m)
        m_ref[...] = m
        l_ref[...] = jnp.sum(p, axis=-1, keepdims=True)
        acc_ref[...] = _dot(p.astype(BF16), vn_ref[...].reshape(rows, HEAD_DIM).astype(BF16))

    lane = lax.broadcasted_iota(jnp.int32, (1, LANES), 1)
    own_head = (lax.broadcasted_iota(jnp.int32, (1, page_keys), 1) & (FOX_HEADS - 1)) == head_of_row
    q = q_scr[...]
    colq = colq_ref[...]
    toff = toff_ref[...]
    tiles = []
    for kp_ref, w_ref in zip(kp_refs, w_refs):
        w = w_ref[0]
        last = jnp.where(lane == LANES - FOX_HEADS + head_of_row, w[:, page_keys - LANES:], 0.0)
        toff = toff + jnp.sum(last, axis=-1, keepdims=True)
        s = _dot_nt(q, kp_ref[0].reshape(page_keys, HEAD_DIM).astype(BF16))
        tiles.append(jnp.where(own_head, s + (colq + toff) - w, -jnp.inf))
    toff_ref[...] = toff
    m_old = m_ref[...]
    m = m_old
    for s in tiles:
        m = jnp.maximum(m, jnp.max(s, axis=-1, keepdims=True))
    alpha = jnp.exp(m_old - m)
    l = alpha * l_ref[...]
    acc = alpha * acc_ref[...]
    for s, vp_ref in zip(tiles, vp_refs):
        p = jnp.exp(s - m)
        l = l + jnp.sum(p, axis=-1, keepdims=True)
        acc = acc + _dot(p.astype(BF16), vp_ref[0].reshape(page_keys, HEAD_DIM).astype(BF16))
    m_ref[...] = m
    l_ref[...] = l
    acc_ref[...] = acc

    @pl.when(j == pl.num_programs(1) - 1)
    def _():
        o_ref[...] = (acc_ref[...] / l_ref[...]).reshape(nq, FOX_HEADS, HEAD_DIM)


def _fox_sample(page_table, q, k_new, v_new, c_new, k_pages, v_pages, w_pages, n_seq, nq, *, pages_per_step=16):
    n_pages = page_table.shape[1]
    rows = nq * FOX_HEADS
    pps = math.gcd(pages_per_step, n_pages)

    def page(i, ndim):
        return lambda b, j, pt: (pt[b * n_pages + (n_pages - 1 - j * pps - i)],) + (0,) * (ndim - 1)

    seq = pl.BlockSpec((nq, FOX_HEADS, HEAD_DIM), lambda b, j, pt: (b, 0, 0))
    kv_specs = [pl.BlockSpec((1, PAGE_SIZE, FOX_HEADS, HEAD_DIM), page(i, 4)) for i in range(pps)]
    w_specs = [pl.BlockSpec((1, 1, PAGE_SIZE * FOX_HEADS), page(i, 3)) for i in range(pps)]
    grid_spec = pltpu.PrefetchScalarGridSpec(
        num_scalar_prefetch=1,
        grid=(n_seq, n_pages // pps),
        in_specs=[seq, seq, seq, pl.BlockSpec((1, 1, rows), lambda b, j, pt: (b, 0, 0))]
        + kv_specs + kv_specs + w_specs,
        out_specs=seq,
        scratch_shapes=[
            pltpu.VMEM((rows, HEAD_DIM), BF16), pltpu.VMEM((rows, 1), F32), pltpu.VMEM((rows, 1), F32),
            pltpu.VMEM((rows, 1), F32), pltpu.VMEM((rows, 1), F32), pltpu.VMEM((rows, HEAD_DIM), F32),
        ],
    )
    return pl.pallas_call(
        functools.partial(_fox_sample_body, pps),
        grid_spec=grid_spec,
        out_shape=jax.ShapeDtypeStruct((n_seq * nq, FOX_HEADS, HEAD_DIM), F32),
        compiler_params=_params("parallel", "arbitrary"),
        name="fox_sample",
    )(page_table.reshape(-1), q, k_new, v_new, c_new, *([k_pages] * pps), *([v_pages] * pps), *([w_pages] * pps))


SSD_PAIRS = SSD_HEADS // 2
PAIRS_PER_GROUP = SSD_PAIRS // SSD_GROUPS


def _expander(width):
    n = SSD_HEADS * width
    h = lax.broadcasted_iota(jnp.int32, (LANES, n), 0)
    c = lax.broadcasted_iota(jnp.int32, (LANES, n), 1)
    return (lax.shift_right_logical(c, int(math.log2(width))) == h).astype(BF16)


def _ssd_local(xs, bm, cm, dt, alog, mask, tot_sel):
    n = xs.shape[0]
    lane = lax.broadcasted_iota(jnp.int32, (1, LANES), 1)
    dta = dt * jnp.where(lane < SSD_HEADS, -jnp.exp(alog), 0.0)
    e64 = _expander(SSD_HEAD_DIM)
    a_cum = _dot_sel_l(mask.astype(BF16), dta)
    a_cum_t = a_cum.T
    ac_exp = _dot_sel_r(a_cum, e64)
    if tot_sel is None:
        atot_exp = ac_exp[n - 1:n, :]
    else:
        atot_exp = _dot_sel_r(_dot_sel_l(tot_sel, dta), e64)
    ac_b = _dot_sel_r(a_cum, _expander(LANES))
    xdt = xs * _dot_sel_r(dt, e64)
    half = lax.broadcasted_iota(jnp.int32, (n, LANES), 1) < SSD_HEAD_DIM
    out = {
        "xdtw": xdt * jnp.exp(atot_exp - ac_exp),
        "eac": jnp.exp(ac_exp),
        "atot_exp": atot_exp,
        "ac_b": ac_b,
        "bg": [], "cg": [], "y_diag": [],
    }
    for g in range(SSD_GROUPS):
        gs = slice(g * SSD_STATE, (g + 1) * SSD_STATE)
        bg = bm[:, gs].astype(BF16)
        cg = cm[:, gs].astype(BF16)
        out["bg"].append(bg)
        out["cg"].append(cg)
        cb = _dot_nt(cg, bg)
        for k in range(g * PAIRS_PER_GROUP, (g + 1) * PAIRS_PER_GROUP):
            ps = slice(k * LANES, (k + 1) * LANES)
            ms = []
            for h in (2 * k, 2 * k + 1):
                seg = ac_b[:, h * LANES:(h + 1) * LANES] - a_cum_t[h:h + 1, :]
                ms.append(cb * jnp.exp(jnp.where(mask, seg, -jnp.inf)))
            m_cat = jnp.concatenate(ms, axis=1).astype(BF16)
            xp = xdt[:, ps]
            x_bd = jnp.concatenate([jnp.where(half, xp, 0.0), jnp.where(half, 0.0, xp)], axis=0).astype(BF16)
            out["y_diag"].append(_dot(m_cat, x_bd))
    return out


def _conv_ssd_sample_body(seq_len, xbc_ref, dt_ref, buf_ref, h0_ref, w_ref, b_ref, alog_ref, dexp_ref,
                          y_ref, hout_ref, cout_ref, xp_ref):
    n = xbc_ref.shape[0]
    n_seq = n // seq_len
    taps = CONV_WIDTH - 1
    base = 8 - taps

    x = xbc_ref[...]
    xp_ref[:, base:8, :] = buf_ref[...]
    xp_ref[:, 8:8 + seq_len, :] = x.reshape(n_seq, seq_len, CONV_DIM)
    acc = b_ref[...] + x * w_ref[taps:taps + 1, :]
    for j in range(taps):
        acc = acc + xp_ref[:, base + j:base + j + seq_len, :].reshape(n, CONV_DIM) * w_ref[j:j + 1, :]
    cout_ref[...] = xp_ref[:, 8 + seq_len - taps:8 + seq_len, :]
    conv = _silu(acc)
    xs = conv[:, :SSD_WIDTH]
    bm = conv[:, SSD_WIDTH:SSD_WIDTH + SSD_GROUPS * SSD_STATE]
    cm = conv[:, SSD_WIDTH + SSD_GROUPS * SSD_STATE:]

    r = lax.broadcasted_iota(jnp.int32, (n, n), 0)
    c = lax.broadcasted_iota(jnp.int32, (n, n), 1)
    same_seq = (r // seq_len) == (c // seq_len)
    loc = _ssd_local(xs, bm, cm, dt_ref[...], alog_ref[...], jnp.logical_and(same_seq, c <= r),
                     same_seq.astype(BF16))

    gw = PAIRS_PER_GROUP * LANES
    seq_of_col = lax.broadcasted_iota(jnp.int32, (1, n), 1) // seq_len
    decay_t = jnp.exp(loc["atot_exp"]).T
    for g in range(SSD_GROUPS):
        gr = slice(g * gw, (g + 1) * gw)
        h_prev = h0_ref[:, gr, :]
        z = _dot_nt(h_prev.reshape(n_seq * gw, SSD_STATE).astype(BF16), loc["cg"][g])
        y_off_t = jnp.zeros((gw, n), F32)
        for s in range(n_seq):
            y_off_t = y_off_t + jnp.where(seq_of_col == s, z[s * gw:(s + 1) * gw, :], 0.0)
        y_off = y_off_t.T * loc["eac"][:, gr]
        y = jnp.concatenate(loc["y_diag"][g * PAIRS_PER_GROUP:(g + 1) * PAIRS_PER_GROUP], axis=1)
        y_ref[:, gr] = y + y_off + xs[:, gr] * dexp_ref[:, gr]
        xw_t = loc["xdtw"][:, gr].T
        lhs = jnp.concatenate([jnp.where(seq_of_col == s, xw_t, 0.0) for s in range(n_seq)], axis=0)
        s_new = _dot(lhs.astype(BF16), loc["bg"][g])
        for s in range(n_seq):
            col = decay_t[gr, s * seq_len:s * seq_len + 1]
            hout_ref[s, gr, :] = h_prev[s] * col + s_new[s * gw:(s + 1) * gw, :]


def _conv_ssd_sample(xbc, dt, conv_buf, h0, conv_w, conv_b, alog, d_exp, n_seq, seq_len):
    assert seq_len == 8 and CONV_WIDTH - 1 <= seq_len
    tile = LANES
    ts = tile // seq_len
    const = lambda i: (0, 0)
    per_s = lambda i: (i, 0, 0)
    return pl.pallas_call(
        functools.partial(_conv_ssd_sample_body, seq_len),
        grid=(n_seq // ts,),
        in_specs=[
            pl.BlockSpec((tile, CONV_DIM), lambda i: (i, 0)),
            pl.BlockSpec((tile, LANES), lambda i: (i, 0)),
            pl.BlockSpec((ts, CONV_WIDTH - 1, CONV_DIM), per_s),
            pl.BlockSpec((ts, SSD_WIDTH, SSD_STATE), per_s),
            pl.BlockSpec((CONV_WIDTH, CONV_DIM), const),
            pl.BlockSpec((1, CONV_DIM), const),
            pl.BlockSpec((1, LANES), const),
            pl.BlockSpec((1, SSD_WIDTH), const),
        ],
        out_specs=[
            pl.BlockSpec((tile, SSD_WIDTH), lambda i: (i, 0)),
            pl.BlockSpec((ts, SSD_WIDTH, SSD_STATE), per_s),
            pl.BlockSpec((ts, CONV_WIDTH - 1, CONV_DIM), per_s),
        ],
        out_shape=[
            jax.ShapeDtypeStruct((n_seq * seq_len, SSD_WIDTH), F32),
            jax.ShapeDtypeStruct((n_seq, SSD_WIDTH, SSD_STATE), F32),
            jax.ShapeDtypeStruct((n_seq, CONV_WIDTH - 1, CONV_DIM), F32),
        ],
        scratch_shapes=[pltpu.VMEM((ts, 8 + seq_len, CONV_DIM), F32)],
        compiler_params=_params("parallel"),
        name="conv_ssd_sample",
    )(xbc, dt, conv_buf, h0, conv_w, conv_b, alog, d_exp)


def _conv_ssd_prompt_body(xbc_ref, dt_ref, buf_ref, h0_ref, w_ref, b_ref, alog_ref, dexp_ref,
                          y_ref, hout_ref, cout_ref, state_ref, xp_ref):
    c = pl.program_id(1)
    nc = pl.num_programs(1)
    tl = xbc_ref.shape[0]
    taps = CONV_WIDTH - 1
    base = 8 - taps

    @pl.when(c == 0)
    def _():
        state_ref[...] = h0_ref[0]
        xp_ref[base:8, :] = buf_ref[0]

    x = xbc_ref[...]
    xp_ref[8:8 + tl, :] = x
    acc = b_ref[...] + x * w_ref[taps:taps + 1, :]
    for j in range(taps):
        acc = acc + xp_ref[base + j:base + j + tl, :] * w_ref[j:j + 1, :]
    xp_ref[base:8, :] = x[tl - taps:, :]
    conv = _silu(acc)
    xs = conv[:, :SSD_WIDTH]
    bm = conv[:, SSD_WIDTH:SSD_WIDTH + SSD_GROUPS * SSD_STATE]
    cm = conv[:, SSD_WIDTH + SSD_GROUPS * SSD_STATE:]

    causal = _tri(tl)
    loc = _ssd_local(xs, bm, cm, dt_ref[...], alog_ref[...], causal, None)
    top = lax.broadcasted_iota(jnp.int32, (tl, LANES), 0) < SSD_HEAD_DIM
    for k in range(SSD_PAIRS):
        g = k // PAIRS_PER_GROUP
        ps = slice(k * LANES, (k + 1) * LANES)
        s_prev = state_ref[ps, :]
        y_off = _dot_nt(loc["cg"][g], s_prev.astype(BF16)) * loc["eac"][:, ps]
        y_ref[:, ps] = loc["y_diag"][k] + y_off + xs[:, ps] * dexp_ref[:, ps]
        cd = [jnp.exp(loc["ac_b"][tl - 1:tl, h * LANES:(h + 1) * LANES]) for h in (2 * k, 2 * k + 1)]
        state_ref[ps, :] = s_prev * jnp.where(top, cd[0], cd[1]) + _dot_tn(loc["xdtw"][:, ps].astype(BF16), loc["bg"][g])

    @pl.when(c == nc - 1)
    def _():
        hout_ref[0] = state_ref[...]
        cout_ref[0] = xp_ref[base:8, :]


def _conv_ssd_prompt(xbc, dt, conv_buf, h0, conv_w, conv_b, alog, d_exp, b, l):
    tl = SSD_CHUNK
    nc = l // tl
    const = lambda bi, c: (0, 0)
    per_b = lambda bi, c: (bi, 0, 0)
    return pl.pallas_call(
        _conv_ssd_prompt_body,
        grid=(b, nc),
        in_specs=[
            pl.BlockSpec((tl, CONV_DIM), lambda bi, c: (bi * nc + c, 0)),
            pl.BlockSpec((tl, LANES), lambda bi, c: (bi * nc + c, 0)),
            pl.BlockSpec((1, CONV_WIDTH - 1, CONV_DIM), per_b),
            pl.BlockSpec((1, SSD_WIDTH, SSD_STATE), per_b),
            pl.BlockSpec((CONV_WIDTH, CONV_DIM), const),
            pl.BlockSpec((1, CONV_DIM), const),
            pl.BlockSpec((1, LANES), const),
            pl.BlockSpec((1, SSD_WIDTH), const),
        ],
        out_specs=[
            pl.BlockSpec((tl, SSD_WIDTH), lambda bi, c: (bi * nc + c, 0)),
            pl.BlockSpec((1, SSD_WIDTH, SSD_STATE), per_b),
            pl.BlockSpec((1, CONV_WIDTH - 1, CONV_DIM), per_b),
        ],
        out_shape=[
            jax.ShapeDtypeStruct((b * l, SSD_WIDTH), F32),
            jax.ShapeDtypeStruct((b, SSD_WIDTH, SSD_STATE), F32),
            jax.ShapeDtypeStruct((b, CONV_WIDTH - 1, CONV_DIM), F32),
        ],
        scratch_shapes=[pltpu.VMEM((SSD_WIDTH, SSD_STATE), F32), pltpu.VMEM((8 + tl, CONV_DIM), F32)],
        compiler_params=_params("parallel", "arbitrary"),
        name="conv_ssd_prompt",
    )(xbc, dt, conv_buf, h0, conv_w, conv_b, alog, d_exp)


def _out_proj_body(h_ref, fox_ref, y_ref, z_ref, gs_ref, w_ref, gx_ref, wq_ref, gq_ref, o_ref, q_ref):
    yn = _rms(y_ref[...] * _silu(z_ref[...]), gs_ref[...]).astype(BF16)
    fw = fox_ref.shape[1]
    h = h_ref[...] + _dot(fox_ref[...].astype(BF16), w_ref[:fw, :]) + _dot(yn, w_ref[fw:, :])
    o_ref[...] = h
    q = _dot(_rms(h, gx_ref[...]).astype(BF16), wq_ref[...])
    q_ref[...] = _head_norm(q, gq_ref[...]) * (HEAD_DIM ** -0.5)


def _out_proj(h, fox, y, z, g_ssd, w_out, g_x, wq, gq, *, tm=512):
    t, d = h.shape
    tm = min(tm, t)
    row = lambda w: pl.BlockSpec((tm, w), lambda i: (i, 0))
    full = lambda a: pl.BlockSpec(a.shape, lambda i: (0, 0), pipeline_mode=pl.Buffered(1))
    consts = [g_ssd.reshape(1, SSD_WIDTH), w_out, g_x.reshape(1, d), wq, gq.reshape(1, HEAD_DIM)]
    return pl.pallas_call(
        _out_proj_body,
        grid=(t // tm,),
        in_specs=[row(d), row(FOX_WIDTH), row(SSD_WIDTH), row(SSD_WIDTH)] + [full(a) for a in consts],
        out_specs=[row(d), row(XATTN_WIDTH)],
        out_shape=[jax.ShapeDtypeStruct((t, d), F32), jax.ShapeDtypeStruct((t, XATTN_WIDTH), F32)],
        compiler_params=_params("parallel"),
        name="out_proj",
    )(h, fox, y, z, *consts)


def _mem_kv_body(m_ref, g_ref, w_ref, gk_ref, k_ref, v_ref):
    kv = _dot(_rms(m_ref[...], g_ref[...]).astype(BF16), w_ref[...])
    k_ref[...] = _head_norm(kv[:, :XATTN_WIDTH], gk_ref[...])
    v_ref[...] = kv[:, XATTN_WIDTH:]


def _mem_kv(mem, g, w_kv, gk, *, tm=256):
    t, d = mem.shape
    row = lambda w: pl.BlockSpec((tm, w), lambda i: (i, 0))
    full = lambda a: pl.BlockSpec(a.shape, lambda i: (0, 0))
    consts = [g.reshape(1, d), w_kv, gk.reshape(1, HEAD_DIM)]
    return pl.pallas_call(
        _mem_kv_body,
        grid=(t // tm,),
        in_specs=[row(d)] + [full(a) for a in consts],
        out_specs=[row(XATTN_WIDTH), row(XATTN_WIDTH)],
        out_shape=[jax.ShapeDtypeStruct((t, XATTN_WIDTH), F32)] * 2,
        compiler_params=_params("parallel"),
        name="mem_kv",
    )(mem, *consts)


def _xattn_body(q_ref, k_ref, v_ref, o_ref):
    for h in range(XATTN_HEADS):
        hs = slice(h * HEAD_DIM, (h + 1) * HEAD_DIM)
        s = _dot_nt(q_ref[:, hs].astype(BF16), k_ref[0, :, hs].astype(BF16))
        p = jnp.exp(s - jnp.max(s, axis=-1, keepdims=True))
        o = _dot(p.astype(BF16), v_ref[0, :, hs].astype(BF16))
        o_ref[:, hs] = o / jnp.sum(p, axis=-1, keepdims=True)


def _xattn(q, mem_k, mem_v, b, l, *, tq=512):
    tq = min(tq, l)
    nq = l // tq
    n_mem = mem_k.shape[1]
    qspec = pl.BlockSpec((tq, XATTN_WIDTH), lambda bi, i: (bi * nq + i, 0))
    mspec = pl.BlockSpec((1, n_mem, XATTN_WIDTH), lambda bi, i: (bi, 0, 0))
    return pl.pallas_call(
        _xattn_body,
        grid=(b, nq),
        in_specs=[qspec, mspec, mspec],
        out_specs=qspec,
        out_shape=jax.ShapeDtypeStruct((b * l, XATTN_WIDTH), F32),
        compiler_params=_params("parallel", "arbitrary"),
        name="xattn",
    )(q, mem_k, mem_v)


def _xattn_rows_body(n_seq, q_ref, k_ref, v_ref, o_ref):
    tiles = q_ref.shape[0] // n_seq
    rows = tiles * 8
    keys = k_ref.shape[1] * k_ref.shape[2]
    row_head = lax.broadcasted_iota(jnp.int32, (rows, 1), 0) & (XATTN_HEADS - 1)
    key_head = lax.broadcasted_iota(jnp.int32, (1, keys), 1) & (XATTN_HEADS - 1)
    own_head = row_head == key_head
    for s in range(n_seq):
        qs = slice(s * tiles, (s + 1) * tiles)
        q = q_ref[qs].reshape(rows, HEAD_DIM).astype(BF16)
        sc = jnp.where(own_head, _dot_nt(q, k_ref[s].reshape(keys, HEAD_DIM).astype(BF16)), -jnp.inf)
        p = jnp.exp(sc - jnp.max(sc, axis=-1, keepdims=True))
        o = _dot(p.astype(BF16), v_ref[s].reshape(keys, HEAD_DIM).astype(BF16))
        o_ref[qs] = (o / jnp.sum(p, axis=-1, keepdims=True)).reshape(tiles, 8, HEAD_DIM)


def _xattn_rows(q, mem_k, mem_v, n_seq, *, ts=8):
    tiles = q.shape[0] // n_seq
    ts = math.gcd(ts, n_seq)
    qspec = pl.BlockSpec((ts * tiles, 8, HEAD_DIM), lambda i: (i, 0, 0))
    mspec = pl.BlockSpec((ts,) + mem_k.shape[1:], lambda i: (i, 0, 0, 0))
    return pl.pallas_call(
        functools.partial(_xattn_rows_body, ts),
        grid=(n_seq // ts,),
        in_specs=[qspec, mspec, mspec],
        out_specs=qspec,
        out_shape=jax.ShapeDtypeStruct(q.shape, F32),
        compiler_params=_params("parallel"),
        name="xattn_rows",
    )(q, mem_k, mem_v)


def kernel(x_prompt, x_sample, cache_fox_k, cache_fox_v, cache_fox_logf, cache_mem_k, cache_mem_v, state_ssm, state_conv, page_table, mem_prompt, ffn1_norm, ffn1_w_gate, ffn1_w_up, ffn1_w_down, mix_norm, w_in, fox_b_f, fox_q_norm, fox_k_norm, conv_w, conv_b, ssd_dt_bias, ssd_A_log, ssd_D, ssd_out_norm, w_out, xattn_norm, mem_norm, xattn_w_q, xattn_w_kv, xattn_q_norm, xattn_k_norm, xattn_w_o, ffn2_norm, ffn2_w_gate, ffn2_w_up, ffn2_w_down):
    assert x_prompt.shape[2] == D_MODEL and ffn1_norm.shape[0] == 1
    d = D_MODEL
    bp, lp = x_prompt.shape[:2]
    bs, ls = x_sample.shape[:2]
    n_mem = mem_prompt.shape[1]
    n_pool = cache_fox_k.shape[1]

    bf = lambda w: w[0].astype(BF16)
    ffn1 = (ffn1_norm[0], ffn1_w_gate[0], ffn1_w_up[0], ffn1_w_down[0])
    ffn2 = (ffn2_norm[0], ffn2_w_gate[0], ffn2_w_up[0], ffn2_w_down[0])
    in_w = _pack_in_proj(w_in[0], fox_b_f[0], ssd_dt_bias[0])
    wq, wkv, wo = bf(xattn_w_q), bf(xattn_w_kv), bf(xattn_w_o)
    alog = jnp.pad(ssd_A_log[0], (0, LANES - SSD_HEADS)).reshape(1, LANES)
    d_exp = jnp.repeat(ssd_D[0], SSD_HEAD_DIM).reshape(1, SSD_WIDTH)
    cw, cb = conv_w[0], conv_b[0].reshape(1, CONV_DIM)

    def front(x, q_dtype, q_scale):
        h1 = _ffn(x, *ffn1)
        return h1, _in_proj(h1, mix_norm[0], *in_w, fox_q_norm[0], fox_k_norm[0], q_dtype=q_dtype, q_scale=q_scale)

    def back(h1, fox, y, z, attend):
        h2, xq = _out_proj(h1, fox, y, z, ssd_out_norm[0], bf(w_out), xattn_norm[0], wq, xattn_q_norm[0])
        return _ffn(h2, *ffn2, pre=(attend(xq), wo))

    h1, (q, k_p, kb, v_p, vb, z, xbc, logf_p, dt) = front(x_prompt.reshape(bp * lp, d), BF16, HEAD_DIM ** -0.5 * LOG2E)
    ccol, crow = _cumsum(logf_p.reshape(bp, lp, LANES))
    tk = min(FOX_KEY_BLOCK, lp)
    fox = _fox_prompt(q, kb, vb, ccol, crow.reshape(bp, FOX_HEADS, lp // tk, tk), bp, lp, tq=FOX_QUERY_BLOCK)
    y, ssm_p, conv_p = _conv_ssd_prompt(
        xbc, dt, jnp.zeros((bp, CONV_WIDTH - 1, CONV_DIM), F32), jnp.zeros((bp, SSD_WIDTH, SSD_STATE), F32),
        cw, cb, alog, d_exp, bp, lp)
    mk, mv = _mem_kv(mem_prompt.reshape(bp * n_mem, d), mem_norm[0], wkv, xattn_k_norm[0])
    y_prompt = back(h1, fox, y, z, lambda xq: _xattn(
        xq, mk.reshape(bp, n_mem, XATTN_WIDTH), mv.reshape(bp, n_mem, XATTN_WIDTH), bp, lp))

    h1, (q, k_s, _, v_s, _, z, xbc, logf_s, dt) = front(x_sample.reshape(bs * ls, d), F32, HEAD_DIM ** -0.5)
    w_pages = _page_cumsum(cache_fox_logf[0].reshape(n_pool, PAGE_SIZE * FOX_HEADS))
    c_new = _seq_cumsum(logf_s, ls)[:, :FOX_HEADS].reshape(bs, 1, ls * FOX_HEADS)
    fox = _fox_sample(
        page_table, q.reshape(bs * ls, FOX_HEADS, HEAD_DIM), k_s, v_s, c_new, cache_fox_k[0], cache_fox_v[0],
        w_pages.reshape(n_pool, 1, PAGE_SIZE * FOX_HEADS), bs, ls).reshape(bs * ls, FOX_WIDTH)
    y, ssm_s, conv_s = _conv_ssd_sample(
        xbc, dt, state_conv[0], state_ssm[0].reshape(bs, SSD_WIDTH, SSD_STATE), cw, cb, alog, d_exp, bs, ls)
    mem_tiles = lambda m: m[0].reshape(bs, n_mem * XATTN_HEADS // 8, 8, HEAD_DIM)
    y_sample = back(h1, fox, y, z, lambda xq: _xattn_rows(
        xq.reshape(bs * ls * XATTN_HEADS // 8, 8, HEAD_DIM), mem_tiles(cache_mem_k), mem_tiles(cache_mem_v), bs,
    ).reshape(bs * ls, XATTN_WIDTH))

    fox_shape = lambda b, l: (1, b, l, FOX_HEADS, HEAD_DIM)
    ssm_shape = lambda b: (1, b, SSD_HEADS, SSD_HEAD_DIM, SSD_STATE)
    mem_shape = (1, bp, n_mem, XATTN_HEADS, HEAD_DIM)
    return (
        y_prompt.reshape(bp, lp, d), y_sample.reshape(bs, ls, d),
        k_p.reshape(fox_shape(bp, lp)), v_p.reshape(fox_shape(bp, lp)),
        logf_p[:, :FOX_HEADS].reshape(1, bp, lp, FOX_HEADS),
        ssm_p.reshape(ssm_shape(bp)), conv_p[None], mk.reshape(mem_shape), mv.reshape(mem_shape),
        k_s.reshape(fox_shape(bs, ls)), v_s.reshape(fox_shape(bs, ls)),
        logf_s[:, :FOX_HEADS].reshape(1, bs, ls, FOX_HEADS),
        ssm_s.reshape(ssm_shape(bs)), conv_s[None],
    )
```

```python
import functools
import math

import jax
import jax.numpy as jnp
from jax import lax
from jax.experimental import pallas as pl
from jax.experimental.pallas import tpu as pltpu

F32 = jnp.float32
BF16 = jnp.bfloat16

EPS = 1e-6
LOG2E = math.log2(math.e)
FFN_RESIDUAL = 0.5
D_MODEL = 2048
D_FF = 5632
PAGE_SIZE = 128
FOX_HEADS = 8
HEAD_DIM = 128
FOX_WIDTH = FOX_HEADS * HEAD_DIM
SSD_HEADS = 16
SSD_HEAD_DIM = 64
SSD_WIDTH = SSD_HEADS * SSD_HEAD_DIM
SSD_GROUPS = 2
SSD_STATE = 128
SSD_CHUNK = 128
CONV_WIDTH = 4
CONV_DIM = SSD_WIDTH + 2 * SSD_GROUPS * SSD_STATE
XATTN_HEADS = 4
XATTN_WIDTH = XATTN_HEADS * HEAD_DIM
LANES = 128
VMEM_LIMIT_BYTES = 56 * 1024 * 1024

def _params(*semantics):
    return pltpu.CompilerParams(dimension_semantics=semantics, vmem_limit_bytes=VMEM_LIMIT_BYTES)


def _rms(x, g):
    return x * lax.rsqrt(jnp.mean(x * x, axis=-1, keepdims=True) + EPS) * g


def _dot(a, b):
    return jnp.dot(a, b, preferred_element_type=F32)


def _dot_nt(a, b):
    return lax.dot_general(a, b, (((1,), (1,)), ((), ())), preferred_element_type=F32)


def _dot_tn(a, b):
    return lax.dot_general(a, b, (((0,), (0,)), ((), ())), preferred_element_type=F32)


def _split3(x):
    x1 = x.astype(BF16)
    r = x - x1.astype(F32)
    x2 = r.astype(BF16)
    x3 = (r - x2.astype(F32)).astype(BF16)
    return x1, x2, x3


def _dot_sel_l(sel, x):
    x1, x2, x3 = _split3(x)
    return _dot(sel, x1) + _dot(sel, x2) + _dot(sel, x3)


def _dot_sel_r(x, sel):
    x1, x2, x3 = _split3(x)
    return _dot(x1, sel) + _dot(x2, sel) + _dot(x3, sel)


def _silu(x):
    return x * jax.nn.sigmoid(x)


def _ffn_body(pre_proj, split, *refs):
    if pre_proj:
        x_ref, a_ref, wo_ref, g_ref, *w_refs, o_ref, xn_ref = refs
    else:
        x_ref, g_ref, *w_refs, o_ref, xn_ref = refs

    @pl.when(pl.program_id(1) == 0)
    def _():
        x = x_ref[...]
        if pre_proj:
            x = x + _dot(a_ref[...].astype(BF16), wo_ref[...])
        xn_ref[...] = _rms(x, g_ref[...]).astype(BF16)
        o_ref[...] = x

    xn = xn_ref[...]
    acc = None
    for wg_ref, wu_ref, wd_ref in zip(w_refs[:split], w_refs[split:2 * split], w_refs[2 * split:]):
        gate = _dot(xn, wg_ref[...].astype(BF16))
        up = _dot(xn, wu_ref[...].astype(BF16))
        h = (_silu(gate) * up * FFN_RESIDUAL).astype(BF16)
        part = _dot(h, wd_ref[...].astype(BF16))
        acc = part if acc is None else acc + part
    o_ref[...] += acc


def _ffn(x, g, wg, wu, wd, pre=None, *, tm=1024, tf=256, split=1):
    t, d = x.shape
    f = wg.shape[1]
    tm = min(tm, t)
    ts = tf // split
    grid = (t // tm, f // tf)
    row = lambda i, j: (i, 0)
    in_specs = [pl.BlockSpec((tm, d), row)]
    args = [x]
    if pre is not None:
        a, wo = pre
        in_specs = [pl.BlockSpec((tm, d), row, pipeline_mode=pl.Buffered(1)),
                    pl.BlockSpec((tm, a.shape[1]), row), pl.BlockSpec(wo.shape, lambda i, j: (0, 0))]
        args += [a, wo]
    cols = [pl.BlockSpec((d, ts), lambda i, j, s=s: (0, j * split + s)) for s in range(split)]
    rows = [pl.BlockSpec((ts, d), lambda i, j, s=s: (j * split + s, 0)) for s in range(split)]
    in_specs += [pl.BlockSpec((1, d), lambda i, j: (0, 0))] + cols + cols + rows
    args += [g.reshape(1, d)] + [wg] * split + [wu] * split + [wd] * split
    return pl.pallas_call(
        functools.partial(_ffn_body, pre is not None, split),
        grid=grid,
        in_specs=in_specs,
        out_specs=pl.BlockSpec((tm, d), row),
        out_shape=jax.ShapeDtypeStruct((t, d), F32),
        scratch_shapes=[pltpu.VMEM((tm, d), BF16)],
        compiler_params=_params("parallel", "arbitrary"),
        name="ffn_pre" if pre is not None else "ffn",
    )(*args)


IN_QKV_TN = 1024
IN_ZX_TN = 512


def _head_norm(y, g):
    outs = []
    for c in range(y.shape[1] // HEAD_DIM):
        yc = y[:, c * HEAD_DIM:(c + 1) * HEAD_DIM]
        outs.append(yc * lax.rsqrt(jnp.mean(yc * yc, axis=-1, keepdims=True) + EPS) * g)
    return jnp.concatenate(outs, axis=1)


def _in_qkv_body(q_scale, x_ref, g_ref, w_ref, ws_ref, bs_ref, gq_ref, gk_ref,
                 q_ref, k_ref, kb_ref, v_ref, vb_ref, logf_ref, dt_ref, u_ref):
    j = pl.program_id(1)

    @pl.when(j == 0)
    def _():
        u = _rms(x_ref[...], g_ref[...]).astype(BF16)
        u_ref[...] = u
        s = _dot(u, ws_ref[...]) + bs_ref[...]
        t = jnp.log1p(jnp.exp(-jnp.abs(s)))
        logf_ref[...] = (jnp.minimum(s, 0.0) - t)[:, :LANES]
        dt_ref[...] = (jnp.maximum(s, 0.0) + t)[:, LANES:]
        q_ref[...] = (_head_norm(_dot(u, w_ref[...]), gq_ref[...]) * q_scale).astype(q_ref.dtype)

    def by_head(y, out_ref, bf_ref):
        bf_ref[...] = y.astype(BF16)
        for h in range(FOX_HEADS):
            out_ref[:, h, :] = y[:, h * HEAD_DIM:(h + 1) * HEAD_DIM]

    @pl.when(j == 1)
    def _():
        by_head(_head_norm(_dot(u_ref[...], w_ref[...]), gk_ref[...]), k_ref, kb_ref)

    @pl.when(j == 2)
    def _():
        by_head(_dot(u_ref[...], w_ref[...]), v_ref, vb_ref)


def _in_qkv(x, g, w_qkv, w_small, b_small, gq, gk, *, q_dtype, q_scale, tm=512):
    t, d = x.shape
    tm = min(tm, t)
    tn = IN_QKV_TN
    assert tn == FOX_WIDTH and w_qkv.shape[1] >= 3 * tn
    const = lambda i, j: (0, 0)
    rows = lambda w: pl.BlockSpec((tm, w), lambda i, j: (i, 0))
    by_head = pl.BlockSpec((tm, FOX_HEADS, HEAD_DIM), lambda i, j: (i, 0, 0))
    by_head_shape = jax.ShapeDtypeStruct((t, FOX_HEADS, HEAD_DIM), F32)
    flat = lambda dt, w: jax.ShapeDtypeStruct((t, w), dt)
    return pl.pallas_call(
        functools.partial(_in_qkv_body, q_scale),
        grid=(t // tm, 3),
        in_specs=[
            rows(d),
            pl.BlockSpec((1, d), const),
            pl.BlockSpec((d, tn), lambda i, j: (0, j)),
            pl.BlockSpec((d, 2 * LANES), const),
            pl.BlockSpec((1, 2 * LANES), const),
            pl.BlockSpec((1, HEAD_DIM), const),
            pl.BlockSpec((1, HEAD_DIM), const),
        ],
        out_specs=[rows(FOX_WIDTH), by_head, rows(FOX_WIDTH), by_head, rows(FOX_WIDTH), rows(LANES), rows(LANES),
                   rows(d)],
        out_shape=[flat(q_dtype, FOX_WIDTH), by_head_shape, flat(BF16, FOX_WIDTH), by_head_shape,
                   flat(BF16, FOX_WIDTH), flat(F32, LANES), flat(F32, LANES), flat(BF16, d)],
        compiler_params=_params("parallel", "arbitrary"),
        name="in_qkv",
    )(x, g.reshape(1, d), w_qkv, w_small, b_small, gq.reshape(1, HEAD_DIM), gk.reshape(1, HEAD_DIM))


def _in_zx_body(u_ref, wa_ref, wb_ref, z_ref, xbc_ref):
    j = pl.program_id(1)
    half = wa_ref.shape[1]
    tn = 2 * half
    nz = z_ref.shape[1] // tn
    for c in range(nz + xbc_ref.shape[1] // tn):
        out_ref, sub = (z_ref, c) if c < nz else (xbc_ref, c - nz)

        @pl.when(j == c)
        def _(out_ref=out_ref, sub=sub):
            out_ref[:, sub * tn:sub * tn + half] = _dot(u_ref[...], wa_ref[...])
            out_ref[:, sub * tn + half:(sub + 1) * tn] = _dot(u_ref[...], wb_ref[...])


def _in_zx(u, w_zx, *, tm=1024):
    t, d = u.shape
    tm = min(tm, t)
    tn = IN_ZX_TN
    rows = lambda w: pl.BlockSpec((tm, w), lambda i, j: (i, 0))
    return pl.pallas_call(
        _in_zx_body,
        grid=(t // tm, w_zx.shape[1] // tn),
        in_specs=[rows(d), pl.BlockSpec((d, tn // 2), lambda i, j: (0, 2 * j)),
                  pl.BlockSpec((d, tn // 2), lambda i, j: (0, 2 * j + 1))],
        out_specs=[rows(SSD_WIDTH), rows(CONV_DIM)],
        out_shape=[jax.ShapeDtypeStruct((t, SSD_WIDTH), F32), jax.ShapeDtypeStruct((t, CONV_DIM), F32)],
        compiler_params=_params("parallel", "arbitrary"),
        name="in_zx",
    )(u, w_zx, w_zx)


def _in_proj(x, g, w_qkv, w_zx, w_small, b_small, gq, gk, *, q_dtype, q_scale):
    q, k, kb, v, vb, logf, dt, u = _in_qkv(x, g, w_qkv, w_small, b_small, gq, gk, q_dtype=q_dtype, q_scale=q_scale)
    z, xbc = _in_zx(u, w_zx)
    return q, k, kb, v, vb, z, xbc, logf, dt


def _pack_in_proj(w_in, fox_b_f, ssd_dt_bias):
    fw = FOX_WIDTH
    f0 = 3 * fw
    z0 = f0 + FOX_HEADS
    x0 = z0 + SSD_WIDTH
    d0 = x0 + CONV_DIM
    w_in = w_in.astype(BF16)
    w_qkv = w_in
    w_zx = w_in[:, z0:d0]
    zeros = lambda n: jnp.zeros((w_in.shape[0], n), BF16)
    w_small = jnp.concatenate(
        [w_in[:, f0:z0], zeros(LANES - FOX_HEADS), w_in[:, d0:], zeros(LANES - SSD_HEADS)], axis=1)
    b_small = jnp.concatenate(
        [fox_b_f, jnp.zeros((LANES - FOX_HEADS,), F32), ssd_dt_bias, jnp.zeros((LANES - SSD_HEADS,), F32)]
    ).reshape(1, 2 * LANES)
    return w_qkv, w_zx, w_small, b_small


ATT_BLOCK = 512
FOX_QUERY_BLOCK = 512
FOX_KEY_BLOCK = 256


def _tri(n, *, strict=False, upper=False):
    r = lax.broadcasted_iota(jnp.int32, (n, n), 0)
    c = lax.broadcasted_iota(jnp.int32, (n, n), 1)
    if upper:
        r, c = c, r
    return (c < r) if strict else (c <= r)


def _cumsum_body(x_ref, col_ref, row_ref, carry_ref):
    @pl.when(pl.program_id(1) == 0)
    def _():
        carry_ref[...] = jnp.zeros_like(carry_ref)

    n = x_ref.shape[1]
    tril = _tri(n).astype(BF16)
    c = _dot_sel_l(tril, x_ref[0]) + carry_ref[...]
    carry_ref[...] = c[n - 1:n, :]
    c = c * LOG2E
    col_ref[0] = c
    row_ref[0] = c.T[:FOX_HEADS, :]


def _cumsum(x, *, tb=ATT_BLOCK):
    b, l, _ = x.shape
    return pl.pallas_call(
        _cumsum_body,
        grid=(b, l // tb),
        in_specs=[pl.BlockSpec((1, tb, LANES), lambda i, j: (i, j, 0))],
        out_specs=[pl.BlockSpec((1, tb, LANES), lambda i, j: (i, j, 0)),
                   pl.BlockSpec((1, FOX_HEADS, tb), lambda i, j: (i, 0, j))],
        out_shape=[jax.ShapeDtypeStruct((b, l, LANES), F32), jax.ShapeDtypeStruct((b, FOX_HEADS, l), F32)],
        scratch_shapes=[pltpu.VMEM((1, LANES), F32)],
        compiler_params=_params("parallel", "arbitrary"),
        name="logf_cumsum",
    )(x)


def _fox_prompt_body(tk, q_ref, k_ref, v_ref, ccol_ref, crow_ref, o_ref, m_ref, l_ref, cq_ref, acc_ref):
    i = pl.program_id(1)
    tq = q_ref.shape[0]
    rep = tk // LANES
    m_ref[...] = jnp.full_like(m_ref, -jnp.inf)
    l_ref[...] = jnp.zeros_like(l_ref)
    acc_ref[...] = jnp.zeros_like(acc_ref)
    for h in range(FOX_HEADS):
        cq_ref[h] = jnp.broadcast_to(ccol_ref[0, :, h:h + 1], (tq, LANES))
    row = i * tq + lax.broadcasted_iota(jnp.int32, (tq, tk), 0)
    col = lax.broadcasted_iota(jnp.int32, (tq, tk), 1)
    wide = lambda x: jnp.concatenate([x] * rep, axis=1)

    def block(j, masked):
        ks = pl.ds(pl.multiple_of(j * tk, tk), tk)
        for h in range(FOX_HEADS):
            hs = slice(h * HEAD_DIM, (h + 1) * HEAD_DIM)
            s = _dot_nt(q_ref[:, hs], k_ref[ks, hs]) + (wide(cq_ref[h]) - crow_ref[0, h, pl.ds(j, 1), :])
            if masked:
                s = jnp.where(col + j * tk <= row, s, -jnp.inf)
            m_old = m_ref[h]
            m_new = jnp.maximum(m_old, jnp.max(s, axis=-1, keepdims=True))
            alpha = jnp.exp2(m_old - m_new)
            p = jnp.exp2(s - wide(m_new))
            m_ref[h] = m_new
            l_ref[h] = alpha * l_ref[h] + jnp.sum(p, axis=-1, keepdims=True)
            acc_ref[h] = alpha * acc_ref[h] + _dot(p.astype(BF16), v_ref[ks, hs])

    first_masked = (i * tq) // tk

    def step(j, carry):
        block(j, False)
        return carry

    lax.fori_loop(0, first_masked, step, 0)
    for extra in range(max(tq // tk, 1)):
        block(first_masked + extra, True)
    for h in range(FOX_HEADS):
        o_ref[:, h * HEAD_DIM:(h + 1) * HEAD_DIM] = (acc_ref[h] / l_ref[h]).astype(o_ref.dtype)


def _fox_prompt(q, k, v, ccol, crow, b, l, *, tq=128):
    tk = crow.shape[-1]
    tq = min(tq, l)
    assert tk % tq == 0 or tq % tk == 0
    nq = l // tq
    w = FOX_WIDTH
    return pl.pallas_call(
        functools.partial(_fox_prompt_body, tk),
        grid=(b, nq),
        in_specs=[
            pl.BlockSpec((tq, w), lambda bi, i: (bi * nq + i, 0)),
            pl.BlockSpec((l, w), lambda bi, i: (bi, 0)),
            pl.BlockSpec((l, w), lambda bi, i: (bi, 0)),
            pl.BlockSpec((1, tq, LANES), lambda bi, i: (bi, i, 0)),
            pl.BlockSpec((1, FOX_HEADS, l // tk, tk), lambda bi, i: (bi, 0, 0, 0)),
        ],
        out_specs=pl.BlockSpec((tq, w), lambda bi, i: (bi * nq + i, 0)),
        out_shape=jax.ShapeDtypeStruct((b * l, w), BF16),
        scratch_shapes=[pltpu.VMEM((FOX_HEADS, tq, LANES), F32)] * 3 + [pltpu.VMEM((FOX_HEADS, tq, HEAD_DIM), F32)],
        compiler_params=_params("parallel", "arbitrary"),
        name="fox_prompt",
    )(q, k, v, ccol, crow)


def _page_cumsum_body(x_ref, w_ref, m_ref):
    n = PAGE_SIZE * FOX_HEADS

    @pl.when(pl.program_id(0) == 0)
    def _():
        r = lax.broadcasted_iota(jnp.int32, (n, n), 0)
        c = lax.broadcasted_iota(jnp.int32, (n, n), 1)
        same_head = (r & (FOX_HEADS - 1)) == (c & (FOX_HEADS - 1))
        earlier = lax.shift_right_logical(r, 3) <= lax.shift_right_logical(c, 3)
        m_ref[...] = jnp.logical_and(same_head, earlier).astype(BF16)

    w_ref[...] = _dot_sel_r(x_ref[...], m_ref[...])


def _page_cumsum(logf_pages, *, tb=256):
    n_pool, n = logf_pages.shape
    return pl.pallas_call(
        _page_cumsum_body,
        grid=(n_pool // tb,),
        in_specs=[pl.BlockSpec((tb, n), lambda i: (i, 0))],
        out_specs=pl.BlockSpec((tb, n), lambda i: (i, 0)),
        out_shape=jax.ShapeDtypeStruct((n_pool, n), F32),
        scratch_shapes=[pltpu.VMEM((n, n), BF16)],
        compiler_params=_params("arbitrary"),
        name="page_cumsum",
    )(logf_pages)


def _seq_cumsum_body(seq_len, x_ref, o_ref):
    n = x_ref.shape[0]
    r = lax.broadcasted_iota(jnp.int32, (n, n), 0)
    c = lax.broadcasted_iota(jnp.int32, (n, n), 1)
    same_seq = (r // seq_len) == (c // seq_len)
    o_ref[...] = _dot_sel_l(jnp.logical_and(same_seq, c <= r).astype(BF16), x_ref[...])


def _seq_cumsum(x, seq_len, *, tb=128):
    t = x.shape[0]
    return pl.pallas_call(
        functools.partial(_seq_cumsum_body, seq_len),
        grid=(t // tb,),
        in_specs=[pl.BlockSpec((tb, LANES), lambda i: (i, 0))],
        out_specs=pl.BlockSpec((tb, LANES), lambda i: (i, 0)),
        out_shape=jax.ShapeDtypeStruct((t, LANES), F32),
        compiler_params=_params("parallel"),
        name="seq_cumsum",
    )(x)


def _fox_sample_body(pps, pt_ref, q_ref, kn_ref, vn_ref, cn_ref, *refs):
    kp_refs, vp_refs, w_refs = refs[:pps], refs[pps:2 * pps], refs[2 * pps:3 * pps]
    o_ref, q_scr, colq_ref, toff_ref, m_ref, l_ref, acc_ref = refs[3 * pps:]
    j = pl.program_id(1)
    nq = q_ref.shape[0]
    rows = nq * FOX_HEADS
    page_keys = PAGE_SIZE * FOX_HEADS
    row_id = lax.broadcasted_iota(jnp.int32, (rows, 1), 0)
    head_of_row = row_id & (FOX_HEADS - 1)
    query_of_row = lax.shift_right_logical(row_id, 3)

    @pl.when(j == 0)
    def _():
        q = q_ref[...].reshape(rows, HEAD_DIM).astype(BF16)
        q_scr[...] = q
        cn = cn_ref[0]
        key = lax.broadcasted_iota(jnp.int32, (1, rows), 1)
        colq = jnp.sum(jnp.where(key == row_id, cn, 0.0), axis=-1, keepdims=True)
        colq_ref[...] = colq
        toff_ref[...] = jnp.zeros_like(toff_ref)
        s = _dot_nt(q, kn_ref[...].reshape(rows, HEAD_DIM).astype(BF16)) + colq - cn
        valid = jnp.logical_and((key & (FOX_HEADS - 1)) == head_of_row,
                                lax.shift_right_logical(key, 3) <= query_of_row)
        s = jnp.where(valid, s, -jnp.inf)
        m = jnp.max(s, axis=-1, keepdims=True)
        p = jnp.exp(s - m)
        m_ref[...] = m
        l_ref[...] = jnp.sum(p, axis=-1, keepdims=True)
        acc_ref[...] = _dot(p.astype(BF16), vn_ref[...].reshape(rows, HEAD_DIM).astype(BF16))

    lane = lax.broadcasted_iota(jnp.int32, (1, LANES), 1)
    own_head = (lax.broadcasted_iota(jnp.int32, (1, page_keys), 1) & (FOX_HEADS - 1)) == head_of_row
    q = q_scr[...]
    colq = colq_ref[...]
    toff = toff_ref[...]
    tiles = []
    for kp_ref, w_ref in zip(kp_refs, w_refs):
        w = w_ref[0]
        last = jnp.where(lane == LANES - FOX_HEADS + head_of_row, w[:, page_keys - LANES:], 0.0)
        toff = toff + jnp.sum(last, axis=-1, keepdims=True)
        s = _dot_nt(q, kp_ref[0].reshape(page_keys, HEAD_DIM).astype(BF16))
        tiles.append(jnp.where(own_head, s + (colq + toff) - w, -jnp.inf))
    toff_ref[...] = toff
    m_old = m_ref[...]
    m = m_old
    for s in tiles:
        m = jnp.maximum(m, jnp.max(s, axis=-1, keepdims=True))
    alpha = jnp.exp(m_old - m)
    l = alpha * l_ref[...]
    acc = alpha * acc_ref[...]
    for s, vp_ref in zip(tiles, vp_refs):
        p = jnp.exp(s - m)
        l = l + jnp.sum(p, axis=-1, keepdims=True)
        acc = acc + _dot(p.astype(BF16), vp_ref[0].reshape(page_keys, HEAD_DIM).astype(BF16))
    m_ref[...] = m
    l_ref[...] = l
    acc_ref[...] = acc

    @pl.when(j == pl.num_programs(1) - 1)
    def _():
        o_ref[...] = (acc_ref[...] / l_ref[...]).reshape(nq, FOX_HEADS, HEAD_DIM)


def _fox_sample(page_table, q, k_new, v_new, c_new, k_pages, v_pages, w_pages, n_seq, nq, *, pages_per_step=16):
    n_pages = page_table.shape[1]
    rows = nq * FOX_HEADS
    pps = math.gcd(pages_per_step, n_pages)

    def page(i, ndim):
        return lambda b, j, pt: (pt[b * n_pages + (n_pages - 1 - j * pps - i)],) + (0,) * (ndim - 1)

    seq = pl.BlockSpec((nq, FOX_HEADS, HEAD_DIM), lambda b, j, pt: (b, 0, 0))
    kv_specs = [pl.BlockSpec((1, PAGE_SIZE, FOX_HEADS, HEAD_DIM), page(i, 4)) for i in range(pps)]
    w_specs = [pl.BlockSpec((1, 1, PAGE_SIZE * FOX_HEADS), page(i, 3)) for i in range(pps)]
    grid_spec = pltpu.PrefetchScalarGridSpec(
        num_scalar_prefetch=1,
        grid=(n_seq, n_pages // pps),
        in_specs=[seq, seq, seq, pl.BlockSpec((1, 1, rows), lambda b, j, pt: (b, 0, 0))]
        + kv_specs + kv_specs + w_specs,
        out_specs=seq,
        scratch_shapes=[
            pltpu.VMEM((rows, HEAD_DIM), BF16), pltpu.VMEM((rows, 1), F32), pltpu.VMEM((rows, 1), F32),
            pltpu.VMEM((rows, 1), F32), pltpu.VMEM((rows, 1), F32), pltpu.VMEM((rows, HEAD_DIM), F32),
        ],
    )
    return pl.pallas_call(
        functools.partial(_fox_sample_body, pps),
        grid_spec=grid_spec,
        out_shape=jax.ShapeDtypeStruct((n_seq * nq, FOX_HEADS, HEAD_DIM), F32),
        compiler_params=_params("parallel", "arbitrary"),
        name="fox_sample",
    )(page_table.reshape(-1), q, k_new, v_new, c_new, *([k_pages] * pps), *([v_pages] * pps), *([w_pages] * pps))


SSD_PAIRS = SSD_HEADS // 2
PAIRS_PER_GROUP = SSD_PAIRS // SSD_GROUPS


def _expander(width):
    n = SSD_HEADS * width
    h = lax.broadcasted_iota(jnp.int32, (LANES, n), 0)
    c = lax.broadcasted_iota(jnp.int32, (LANES, n), 1)
    return (lax.shift_right_logical(c, int(math.log2(width))) == h).astype(BF16)


def _ssd_local(xs, bm, cm, dt, alog, mask, tot_sel):
    n = xs.shape[0]
    lane = lax.broadcasted_iota(jnp.int32, (1, LANES), 1)
    dta = dt * jnp.where(lane < SSD_HEADS, -jnp.exp(alog), 0.0)
    e64 = _expander(SSD_HEAD_DIM)
    a_cum = _dot_sel_l(mask.astype(BF16), dta)
    a_cum_t = a_cum.T
    ac_exp = _dot_sel_r(a_cum, e64)
    if tot_sel is None:
        atot_exp = ac_exp[n - 1:n, :]
    else:
        atot_exp = _dot_sel_r(_dot_sel_l(tot_sel, dta), e64)
    ac_b = _dot_sel_r(a_cum, _expander(LANES))
    xdt = xs * _dot_sel_r(dt, e64)
    half = lax.broadcasted_iota(jnp.int32, (n, LANES), 1) < SSD_HEAD_DIM
    out = {
        "xdtw": xdt * jnp.exp(atot_exp - ac_exp),
        "eac": jnp.exp(ac_exp),
        "atot_exp": atot_exp,
        "ac_b": ac_b,
        "bg": [], "cg": [], "y_diag": [],
    }
    for g in range(SSD_GROUPS):
        gs = slice(g * SSD_STATE, (g + 1) * SSD_STATE)
        bg = bm[:, gs].astype(BF16)
        cg = cm[:, gs].astype(BF16)
        out["bg"].append(bg)
        out["cg"].append(cg)
        cb = _dot_nt(cg, bg)
        for k in range(g * PAIRS_PER_GROUP, (g + 1) * PAIRS_PER_GROUP):
            ps = slice(k * LANES, (k + 1) * LANES)
            ms = []
            for h in (2 * k, 2 * k + 1):
                seg = ac_b[:, h * LANES:(h + 1) * LANES] - a_cum_t[h:h + 1, :]
                ms.append(cb * jnp.exp(jnp.where(mask, seg, -jnp.inf)))
            m_cat = jnp.concatenate(ms, axis=1).astype(BF16)
            xp = xdt[:, ps]
            x_bd = jnp.concatenate([jnp.where(half, xp, 0.0), jnp.where(half, 0.0, xp)], axis=0).astype(BF16)
            out["y_diag"].append(_dot(m_cat, x_bd))
    return out


def _conv_ssd_sample_body(seq_len, xbc_ref, dt_ref, buf_ref, h0_ref, w_ref, b_ref, alog_ref, dexp_ref,
                          y_ref, hout_ref, cout_ref, xp_ref):
    n = xbc_ref.shape[0]
    n_seq = n // seq_len
    taps = CONV_WIDTH - 1
    base = 8 - taps

    x = xbc_ref[...]
    xp_ref[:, base:8, :] = buf_ref[...]
    xp_ref[:, 8:8 + seq_len, :] = x.reshape(n_seq, seq_len, CONV_DIM)
    acc = b_ref[...] + x * w_ref[taps:taps + 1, :]
    for j in range(taps):
        acc = acc + xp_ref[:, base + j:base + j + seq_len, :].reshape(n, CONV_DIM) * w_ref[j:j + 1, :]
    cout_ref[...] = xp_ref[:, 8 + seq_len - taps:8 + seq_len, :]
    conv = _silu(acc)
    xs = conv[:, :SSD_WIDTH]
    bm = conv[:, SSD_WIDTH:SSD_WIDTH + SSD_GROUPS * SSD_STATE]
    cm = conv[:, SSD_WIDTH + SSD_GROUPS * SSD_STATE:]

    r = lax.broadcasted_iota(jnp.int32, (n, n), 0)
    c = lax.broadcasted_iota(jnp.int32, (n, n), 1)
    same_seq = (r // seq_len) == (c // seq_len)
    loc = _ssd_local(xs, bm, cm, dt_ref[...], alog_ref[...], jnp.logical_and(same_seq, c <= r),
                     same_seq.astype(BF16))

    gw = PAIRS_PER_GROUP * LANES
    seq_of_col = lax.broadcasted_iota(jnp.int32, (1, n), 1) // seq_len
    decay_t = jnp.exp(loc["atot_exp"]).T
    for g in range(SSD_GROUPS):
        gr = slice(g * gw, (g + 1) * gw)
        h_prev = h0_ref[:, gr, :]
        z = _dot_nt(h_prev.reshape(n_seq * gw, SSD_STATE).astype(BF16), loc["cg"][g])
        y_off_t = jnp.zeros((gw, n), F32)
        for s in range(n_seq):
            y_off_t = y_off_t + jnp.where(seq_of_col == s, z[s * gw:(s + 1) * gw, :], 0.0)
        y_off = y_off_t.T * loc["eac"][:, gr]
        y = jnp.concatenate(loc["y_diag"][g * PAIRS_PER_GROUP:(g + 1) * PAIRS_PER_GROUP], axis=1)
        y_ref[:, gr] = y + y_off + xs[:, gr] * dexp_ref[:, gr]
        xw_t = loc["xdtw"][:, gr].T
        lhs = jnp.concatenate([jnp.where(seq_of_col == s, xw_t, 0.0) for s in range(n_seq)], axis=0)
        s_new = _dot(lhs.astype(BF16), loc["bg"][g])
        for s in range(n_seq):
            col = decay_t[gr, s * seq_len:s * seq_len + 1]
            hout_ref[s, gr, :] = h_prev[s] * col + s_new[s * gw:(s + 1) * gw, :]


def _conv_ssd_sample(xbc, dt, conv_buf, h0, conv_w, conv_b, alog, d_exp, n_seq, seq_len):
    assert seq_len == 8 and CONV_WIDTH - 1 <= seq_len
    tile = LANES
    ts = tile // seq_len
    const = lambda i: (0, 0)
    per_s = lambda i: (i, 0, 0)
    return pl.pallas_call(
        functools.partial(_conv_ssd_sample_body, seq_len),
        grid=(n_seq // ts,),
        in_specs=[
            pl.BlockSpec((tile, CONV_DIM), lambda i: (i, 0)),
            pl.BlockSpec((tile, LANES), lambda i: (i, 0)),
            pl.BlockSpec((ts, CONV_WIDTH - 1, CONV_DIM), per_s),
            pl.BlockSpec((ts, SSD_WIDTH, SSD_STATE), per_s),
            pl.BlockSpec((CONV_WIDTH, CONV_DIM), const),
            pl.BlockSpec((1, CONV_DIM), const),
            pl.BlockSpec((1, LANES), const),
            pl.BlockSpec((1, SSD_WIDTH), const),
        ],
        out_specs=[
            pl.BlockSpec((tile, SSD_WIDTH), lambda i: (i, 0)),
            pl.BlockSpec((ts, SSD_WIDTH, SSD_STATE), per_s),
            pl.BlockSpec((ts, CONV_WIDTH - 1, CONV_DIM), per_s),
        ],
        out_shape=[
            jax.ShapeDtypeStruct((n_seq * seq_len, SSD_WIDTH), F32),
            jax.ShapeDtypeStruct((n_seq, SSD_WIDTH, SSD_STATE), F32),
            jax.ShapeDtypeStruct((n_seq, CONV_WIDTH - 1, CONV_DIM), F32),
        ],
        scratch_shapes=[pltpu.VMEM((ts, 8 + seq_len, CONV_DIM), F32)],
        compiler_params=_params("parallel"),
        name="conv_ssd_sample",
    )(xbc, dt, conv_buf, h0, conv_w, conv_b, alog, d_exp)


def _conv_ssd_prompt_body(xbc_ref, dt_ref, buf_ref, h0_ref, w_ref, b_ref, alog_ref, dexp_ref,
                          y_ref, hout_ref, cout_ref, state_ref, xp_ref):
    c = pl.program_id(1)
    nc = pl.num_programs(1)
    tl = xbc_ref.shape[0]
    taps = CONV_WIDTH - 1
    base = 8 - taps

    @pl.when(c == 0)
    def _():
        state_ref[...] = h0_ref[0]
        xp_ref[base:8, :] = buf_ref[0]

    x = xbc_ref[...]
    xp_ref[8:8 + tl, :] = x
    acc = b_ref[...] + x * w_ref[taps:taps + 1, :]
    for j in range(taps):
        acc = acc + xp_ref[base + j:base + j + tl, :] * w_ref[j:j + 1, :]
    xp_ref[base:8, :] = x[tl - taps:, :]
    conv = _silu(acc)
    xs = conv[:, :SSD_WIDTH]
    bm = conv[:, SSD_WIDTH:SSD_WIDTH + SSD_GROUPS * SSD_STATE]
    cm = conv[:, SSD_WIDTH + SSD_GROUPS * SSD_STATE:]

    causal = _tri(tl)
    loc = _ssd_local(xs, bm, cm, dt_ref[...], alog_ref[...], causal, None)
    top = lax.broadcasted_iota(jnp.int32, (tl, LANES), 0) < SSD_HEAD_DIM
    for k in range(SSD_PAIRS):
        g = k // PAIRS_PER_GROUP
        ps = slice(k * LANES, (k + 1) * LANES)
        s_prev = state_ref[ps, :]
        y_off = _dot_nt(loc["cg"][g], s_prev.astype(BF16)) * loc["eac"][:, ps]
        y_ref[:, ps] = loc["y_diag"][k] + y_off + xs[:, ps] * dexp_ref[:, ps]
        cd = [jnp.exp(loc["ac_b"][tl - 1:tl, h * LANES:(h + 1) * LANES]) for h in (2 * k, 2 * k + 1)]
        state_ref[ps, :] = s_prev * jnp.where(top, cd[0], cd[1]) + _dot_tn(loc["xdtw"][:, ps].astype(BF16), loc["bg"][g])

    @pl.when(c == nc - 1)
    def _():
        hout_ref[0] = state_ref[...]
        cout_ref[0] = xp_ref[base:8, :]


def _conv_ssd_prompt(xbc, dt, conv_buf, h0, conv_w, conv_b, alog, d_exp, b, l):
    tl = SSD_CHUNK
    nc = l // tl
    const = lambda bi, c: (0, 0)
    per_b = lambda bi, c: (bi, 0, 0)
    return pl.pallas_call(
        _conv_ssd_prompt_body,
        grid=(b, nc),
        in_specs=[
            pl.BlockSpec((tl, CONV_DIM), lambda bi, c: (bi * nc + c, 0)),
            pl.BlockSpec((tl, LANES), lambda bi, c: (bi * nc + c, 0)),
            pl.BlockSpec((1, CONV_WIDTH - 1, CONV_DIM), per_b),
            pl.BlockSpec((1, SSD_WIDTH, SSD_STATE), per_b),
            pl.BlockSpec((CONV_WIDTH, CONV_DIM), const),
            pl.BlockSpec((1, CONV_DIM), const),
            pl.BlockSpec((1, LANES), const),
            pl.BlockSpec((1, SSD_WIDTH), const),
        ],
        out_specs=[
            pl.BlockSpec((tl, SSD_WIDTH), lambda bi, c: (bi * nc + c, 0)),
            pl.BlockSpec((1, SSD_WIDTH, SSD_STATE), per_b),
            pl.BlockSpec((1, CONV_WIDTH - 1, CONV_DIM), per_b),
        ],
        out_shape=[
            jax.ShapeDtypeStruct((b * l, SSD_WIDTH), F32),
            jax.ShapeDtypeStruct((b, SSD_WIDTH, SSD_STATE), F32),
            jax.ShapeDtypeStruct((b, CONV_WIDTH - 1, CONV_DIM), F32),
        ],
        scratch_shapes=[pltpu.VMEM((SSD_WIDTH, SSD_STATE), F32), pltpu.VMEM((8 + tl, CONV_DIM), F32)],
        compiler_params=_params("parallel", "arbitrary"),
        name="conv_ssd_prompt",
    )(xbc, dt, conv_buf, h0, conv_w, conv_b, alog, d_exp)


def _out_proj_body(h_ref, fox_ref, y_ref, z_ref, gs_ref, w_ref, gx_ref, wq_ref, gq_ref, o_ref, q_ref):
    yn = _rms(y_ref[...] * _silu(z_ref[...]), gs_ref[...]).astype(BF16)
    fw = fox_ref.shape[1]
    h = h_ref[...] + _dot(fox_ref[...].astype(BF16), w_ref[:fw, :]) + _dot(yn, w_ref[fw:, :])
    o_ref[...] = h
    q = _dot(_rms(h, gx_ref[...]).astype(BF16), wq_ref[...])
    q_ref[...] = _head_norm(q, gq_ref[...]) * (HEAD_DIM ** -0.5)


def _out_proj(h, fox, y, z, g_ssd, w_out, g_x, wq, gq, *, tm=512):
    t, d = h.shape
    tm = min(tm, t)
    row = lambda w: pl.BlockSpec((tm, w), lambda i: (i, 0))
    full = lambda a: pl.BlockSpec(a.shape, lambda i: (0, 0), pipeline_mode=pl.Buffered(1))
    consts = [g_ssd.reshape(1, SSD_WIDTH), w_out, g_x.reshape(1, d), wq, gq.reshape(1, HEAD_DIM)]
    return pl.pallas_call(
        _out_proj_body,
        grid=(t // tm,),
        in_specs=[row(d), row(FOX_WIDTH), row(SSD_WIDTH), row(SSD_WIDTH)] + [full(a) for a in consts],
        out_specs=[row(d), row(XATTN_WIDTH)],
        out_shape=[jax.ShapeDtypeStruct((t, d), F32), jax.ShapeDtypeStruct((t, XATTN_WIDTH), F32)],
        compiler_params=_params("parallel"),
        name="out_proj",
    )(h, fox, y, z, *consts)


def _mem_kv_body(m_ref, g_ref, w_ref, gk_ref, k_ref, v_ref):
    kv = _dot(_rms(m_ref[...], g_ref[...]).astype(BF16), w_ref[...])
    k_ref[...] = _head_norm(kv[:, :XATTN_WIDTH], gk_ref[...])
    v_ref[...] = kv[:, XATTN_WIDTH:]


def _mem_kv(mem, g, w_kv, gk, *, tm=256):
    t, d = mem.shape
    row = lambda w: pl.BlockSpec((tm, w), lambda i: (i, 0))
    full = lambda a: pl.BlockSpec(a.shape, lambda i: (0, 0))
    consts = [g.reshape(1, d), w_kv, gk.reshape(1, HEAD_DIM)]
    return pl.pallas_call(
        _mem_kv_body,
        grid=(t // tm,),
        in_specs=[row(d)] + [full(a) for a in consts],
        out_specs=[row(XATTN_WIDTH), row(XATTN_WIDTH)],
        out_shape=[jax.ShapeDtypeStruct((t, XATTN_WIDTH), F32)] * 2,
        compiler_params=_params("parallel"),
        name="mem_kv",
    )(mem, *consts)


def _xattn_body(q_ref, k_ref, v_ref, o_ref):
    for h in range(XATTN_HEADS):
        hs = slice(h * HEAD_DIM, (h + 1) * HEAD_DIM)
        s = _dot_nt(q_ref[:, hs].astype(BF16), k_ref[0, :, hs].astype(BF16))
        p = jnp.exp(s - jnp.max(s, axis=-1, keepdims=True))
        o = _dot(p.astype(BF16), v_ref[0, :, hs].astype(BF16))
        o_ref[:, hs] = o / jnp.sum(p, axis=-1, keepdims=True)


def _xattn(q, mem_k, mem_v, b, l, *, tq=512):
    tq = min(tq, l)
    nq = l // tq
    n_mem = mem_k.shape[1]
    qspec = pl.BlockSpec((tq, XATTN_WIDTH), lambda bi, i: (bi * nq + i, 0))
    mspec = pl.BlockSpec((1, n_mem, XATTN_WIDTH), lambda bi, i: (bi, 0, 0))
    return pl.pallas_call(
        _xattn_body,
        grid=(b, nq),
        in_specs=[qspec, mspec, mspec],
        out_specs=qspec,
        out_shape=jax.ShapeDtypeStruct((b * l, XATTN_WIDTH), F32),
        compiler_params=_params("parallel", "arbitrary"),
        name="xattn",
    )(q, mem_k, mem_v)


def _xattn_rows_body(n_seq, q_ref, k_ref, v_ref, o_ref):
    tiles = q_ref.shape[0] // n_seq
    rows = tiles * 8
    keys = k_ref.shape[1] * k_ref.shape[2]
    row_head = lax.broadcasted_iota(jnp.int32, (rows, 1), 0) & (XATTN_HEADS - 1)
    key_head = lax.broadcasted_iota(jnp.int32, (1, keys), 1) & (XATTN_HEADS - 1)
    own_head = row_head == key_head
    for s in range(n_seq):
        qs = slice(s * tiles, (s + 1) * tiles)
        q = q_ref[qs].reshape(rows, HEAD_DIM).astype(BF16)
        sc = jnp.where(own_head, _dot_nt(q, k_ref[s].reshape(keys, HEAD_DIM).astype(BF16)), -jnp.inf)
        p = jnp.exp(sc - jnp.max(sc, axis=-1, keepdims=True))
        o = _dot(p.astype(BF16), v_ref[s].reshape(keys, HEAD_DIM).astype(BF16))
        o_ref[qs] = (o / jnp.sum(p, axis=-1, keepdims=True)).reshape(tiles, 8, HEAD_DIM)


def _xattn_rows(q, mem_k, mem_v, n_seq, *, ts=16):
    tiles = q.shape[0] // n_seq
    ts = math.gcd(ts, n_seq)
    qspec = pl.BlockSpec((ts * tiles, 8, HEAD_DIM), lambda i: (i, 0, 0))
    mspec = pl.BlockSpec((ts,) + mem_k.shape[1:], lambda i: (i, 0, 0, 0))
    return pl.pallas_call(
        functools.partial(_xattn_rows_body, ts),
        grid=(n_seq // ts,),
        in_specs=[qspec, mspec, mspec],
        out_specs=qspec,
        out_shape=jax.ShapeDtypeStruct(q.shape, F32),
        compiler_params=_params("parallel"),
        name="xattn_rows",
    )(q, mem_k, mem_v)


def kernel(x_prompt, x_sample, cache_fox_k, cache_fox_v, cache_fox_logf, cache_mem_k, cache_mem_v, state_ssm, state_conv, page_table, mem_prompt, ffn1_norm, ffn1_w_gate, ffn1_w_up, ffn1_w_down, mix_norm, w_in, fox_b_f, fox_q_norm, fox_k_norm, conv_w, conv_b, ssd_dt_bias, ssd_A_log, ssd_D, ssd_out_norm, w_out, xattn_norm, mem_norm, xattn_w_q, xattn_w_kv, xattn_q_norm, xattn_k_norm, xattn_w_o, ffn2_norm, ffn2_w_gate, ffn2_w_up, ffn2_w_down):
    assert x_prompt.shape[2] == D_MODEL and ffn1_norm.shape[0] == 1
    d = D_MODEL
    bp, lp = x_prompt.shape[:2]
    bs, ls = x_sample.shape[:2]
    n_mem = mem_prompt.shape[1]
    n_pool = cache_fox_k.shape[1]

    bf = lambda w: w[0].astype(BF16)
    ffn1 = (ffn1_norm[0], ffn1_w_gate[0], ffn1_w_up[0], ffn1_w_down[0])
    ffn2 = (ffn2_norm[0], ffn2_w_gate[0], ffn2_w_up[0], ffn2_w_down[0])
    in_w = _pack_in_proj(w_in[0], fox_b_f[0], ssd_dt_bias[0])
    wq, wkv, wo = bf(xattn_w_q), bf(xattn_w_kv), bf(xattn_w_o)
    alog = jnp.pad(ssd_A_log[0], (0, LANES - SSD_HEADS)).reshape(1, LANES)
    d_exp = jnp.repeat(ssd_D[0], SSD_HEAD_DIM).reshape(1, SSD_WIDTH)
    cw, cb = conv_w[0], conv_b[0].reshape(1, CONV_DIM)

    def front(x, q_dtype, q_scale):
        h1 = _ffn(x, *ffn1)
        return h1, _in_proj(h1, mix_norm[0], *in_w, fox_q_norm[0], fox_k_norm[0], q_dtype=q_dtype, q_scale=q_scale)

    def back(h1, fox, y, z, attend):
        h2, xq = _out_proj(h1, fox, y, z, ssd_out_norm[0], bf(w_out), xattn_norm[0], wq, xattn_q_norm[0])
        return _ffn(h2, *ffn2, pre=(attend(xq), wo))

    h1, (q, k_p, kb, v_p, vb, z, xbc, logf_p, dt) = front(x_prompt.reshape(bp * lp, d), BF16, HEAD_DIM ** -0.5 * LOG2E)
    ccol, crow = _cumsum(logf_p.reshape(bp, lp, LANES))
    tk = min(FOX_KEY_BLOCK, lp)
    fox = _fox_prompt(q, kb, vb, ccol, crow.reshape(bp, FOX_HEADS, lp // tk, tk), bp, lp, tq=FOX_QUERY_BLOCK)
    y, ssm_p, conv_p = _conv_ssd_prompt(
        xbc, dt, jnp.zeros((bp, CONV_WIDTH - 1, CONV_DIM), F32), jnp.zeros((bp, SSD_WIDTH, SSD_STATE), F32),
        cw, cb, alog, d_exp, bp, lp)
    mk, mv = _mem_kv(mem_prompt.reshape(bp * n_mem, d), mem_norm[0], wkv, xattn_k_norm[0])
    y_prompt = back(h1, fox, y, z, lambda xq: _xattn(
        xq, mk.reshape(bp, n_mem, XATTN_WIDTH), mv.reshape(bp, n_mem, XATTN_WIDTH), bp, lp))

    h1, (q, k_s, _, v_s, _, z, xbc, logf_s, dt) = front(x_sample.reshape(bs * ls, d), F32, HEAD_DIM ** -0.5)
    w_pages = _page_cumsum(cache_fox_logf[0].reshape(n_pool, PAGE_SIZE * FOX_HEADS))
    c_new = _seq_cumsum(logf_s, ls)[:, :FOX_HEADS].reshape(bs, 1, ls * FOX_HEADS)
    fox = _fox_sample(
        page_table, q.reshape(bs * ls, FOX_HEADS, HEAD_DIM), k_s, v_s, c_new, cache_fox_k[0], cache_fox_v[0],
        w_pages.reshape(n_pool, 1, PAGE_SIZE * FOX_HEADS), bs, ls).reshape(bs * ls, FOX_WIDTH)
    y, ssm_s, conv_s = _conv_ssd_sample(
        xbc, dt, state_conv[0], state_ssm[0].reshape(bs, SSD_WIDTH, SSD_STATE), cw, cb, alog, d_exp, bs, ls)
    mem_tiles = lambda m: m[0].reshape(bs, n_mem * XATTN_HEADS // 8, 8, HEAD_DIM)
    y_sample = back(h1, fox, y, z, lambda xq: _xattn_rows(
        xq.reshape(bs * ls * XATTN_HEADS // 8, 8, HEAD_DIM), mem_tiles(cache_mem_k), mem_tiles(cache_mem_v), bs,
    ).reshape(bs * ls, XATTN_WIDTH))

    fox_shape = lambda b, l: (1, b, l, FOX_HEADS, HEAD_DIM)
    ssm_shape = lambda b: (1, b, SSD_HEADS, SSD_HEAD_DIM, SSD_STATE)
    mem_shape = (1, bp, n_mem, XATTN_HEADS, HEAD_DIM)
    return (
        y_prompt.reshape(bp, lp, d), y_sample.reshape(bs, ls, d),
        k_p.reshape(fox_shape(bp, lp)), v_p.reshape(fox_shape(bp, lp)),
        logf_p[:, :FOX_HEADS].reshape(1, bp, lp, FOX_HEADS),
        ssm_p.reshape(ssm_shape(bp)), conv_p[None], mk.reshape(mem_shape), mv.reshape(mem_shape),
        k_s.reshape(fox_shape(bs, ls)), v_s.reshape(fox_shape(bs, ls)),
        logf_s[:, :FOX_HEADS].reshape(1, bs, ls, FOX_HEADS),
        ssm_s.reshape(ssm_shape(bs)), conv_s[None],
    )
```

```python
import functools
import math

import jax
import jax.numpy as jnp
from jax import lax
from jax.experimental import pallas as pl
from jax.experimental.pallas import tpu as pltpu

F32 = jnp.float32
BF16 = jnp.bfloat16

EPS = 1e-6
LOG2E = math.log2(math.e)
FFN_RESIDUAL = 0.5
D_MODEL = 2048
D_FF = 5632
PAGE_SIZE = 128
FOX_HEADS = 8
HEAD_DIM = 128
FOX_WIDTH = FOX_HEADS * HEAD_DIM
SSD_HEADS = 16
SSD_HEAD_DIM = 64
SSD_WIDTH = SSD_HEADS * SSD_HEAD_DIM
SSD_GROUPS = 2
SSD_STATE = 128
SSD_CHUNK = 128
CONV_WIDTH = 4
CONV_DIM = SSD_WIDTH + 2 * SSD_GROUPS * SSD_STATE
XATTN_HEADS = 4
XATTN_WIDTH = XATTN_HEADS * HEAD_DIM
LANES = 128
VMEM_LIMIT_BYTES = 56 * 1024 * 1024

def _params(*semantics):
    return pltpu.CompilerParams(dimension_semantics=semantics, vmem_limit_bytes=VMEM_LIMIT_BYTES)


def _rms(x, g):
    return x * lax.rsqrt(jnp.mean(x * x, axis=-1, keepdims=True) + EPS) * g


def _dot(a, b):
    return jnp.dot(a, b, preferred_element_type=F32)


def _dot_nt(a, b):
    return lax.dot_general(a, b, (((1,), (1,)), ((), ())), preferred_element_type=F32)


def _dot_tn(a, b):
    return lax.dot_general(a, b, (((0,), (0,)), ((), ())), preferred_element_type=F32)


def _split3(x):
    x1 = x.astype(BF16)
    r = x - x1.astype(F32)
    x2 = r.astype(BF16)
    x3 = (r - x2.astype(F32)).astype(BF16)
    return x1, x2, x3


def _dot_sel_l(sel, x):
    x1, x2, x3 = _split3(x)
    return _dot(sel, x1) + _dot(sel, x2) + _dot(sel, x3)


def _dot_sel_r(x, sel):
    x1, x2, x3 = _split3(x)
    return _dot(x1, sel) + _dot(x2, sel) + _dot(x3, sel)


def _silu(x):
    return x * jax.nn.sigmoid(x)


def _ffn_body(pre_proj, split, *refs):
    if pre_proj:
        x_ref, a_ref, wo_ref, g_ref, *w_refs, o_ref, xn_ref = refs
    else:
        x_ref, g_ref, *w_refs, o_ref, xn_ref = refs

    @pl.when(pl.program_id(1) == 0)
    def _():
        x = x_ref[...]
        if pre_proj:
            x = x + _dot(a_ref[...].astype(BF16), wo_ref[...])
        xn_ref[...] = _rms(x, g_ref[...]).astype(BF16)
        o_ref[...] = x

    xn = xn_ref[...]
    acc = None
    for wg_ref, wu_ref, wd_ref in zip(w_refs[:split], w_refs[split:2 * split], w_refs[2 * split:]):
        gate = _dot(xn, wg_ref[...].astype(BF16))
        up = _dot(xn, wu_ref[...].astype(BF16))
        h = (_silu(gate) * up * FFN_RESIDUAL).astype(BF16)
        part = _dot(h, wd_ref[...].astype(BF16))
        acc = part if acc is None else acc + part
    o_ref[...] += acc


def _ffn(x, g, wg, wu, wd, pre=None, *, tm=1024, tf=256, split=1):
    t, d = x.shape
    f = wg.shape[1]
    tm = min(tm, t)
    ts = tf // split
    grid = (t // tm, f // tf)
    row = lambda i, j: (i, 0)
    in_specs = [pl.BlockSpec((tm, d), row)]
    args = [x]
    if pre is not None:
        a, wo = pre
        in_specs += [pl.BlockSpec((tm, a.shape[1]), row),
                     pl.BlockSpec(wo.shape, lambda i, j: (0, 0), pipeline_mode=pl.Buffered(1))]
        args += [a, wo]
    cols = [pl.BlockSpec((d, ts), lambda i, j, s=s: (0, j * split + s)) for s in range(split)]
    rows = [pl.BlockSpec((ts, d), lambda i, j, s=s: (j * split + s, 0)) for s in range(split)]
    in_specs += [pl.BlockSpec((1, d), lambda i, j: (0, 0))] + cols + cols + rows
    args += [g.reshape(1, d)] + [wg] * split + [wu] * split + [wd] * split
    return pl.pallas_call(
        functools.partial(_ffn_body, pre is not None, split),
        grid=grid,
        in_specs=in_specs,
        out_specs=pl.BlockSpec((tm, d), row),
        out_shape=jax.ShapeDtypeStruct((t, d), F32),
        scratch_shapes=[pltpu.VMEM((tm, d), BF16)],
        compiler_params=_params("parallel", "arbitrary"),
        name="ffn_pre" if pre is not None else "ffn",
    )(*args)


IN_QKV_TN = 1024
IN_ZX_TN = 512


def _head_norm(y, g):
    outs = []
    for c in range(y.shape[1] // HEAD_DIM):
        yc = y[:, c * HEAD_DIM:(c + 1) * HEAD_DIM]
        outs.append(yc * lax.rsqrt(jnp.mean(yc * yc, axis=-1, keepdims=True) + EPS) * g)
    return jnp.concatenate(outs, axis=1)


def _in_qkv_body(q_scale, x_ref, g_ref, w_ref, ws_ref, bs_ref, gq_ref, gk_ref,
                 q_ref, k_ref, kb_ref, v_ref, vb_ref, logf_ref, dt_ref, u_ref):
    j = pl.program_id(1)

    @pl.when(j == 0)
    def _():
        u = _rms(x_ref[...], g_ref[...]).astype(BF16)
        u_ref[...] = u
        s = _dot(u, ws_ref[...]) + bs_ref[...]
        t = jnp.log1p(jnp.exp(-jnp.abs(s)))
        logf_ref[...] = (jnp.minimum(s, 0.0) - t)[:, :LANES]
        dt_ref[...] = (jnp.maximum(s, 0.0) + t)[:, LANES:]
        q_ref[...] = (_head_norm(_dot(u, w_ref[...]), gq_ref[...]) * q_scale).astype(q_ref.dtype)

    def by_head(y, out_ref, bf_ref):
        bf_ref[...] = y.astype(BF16)
        for h in range(FOX_HEADS):
            out_ref[:, h, :] = y[:, h * HEAD_DIM:(h + 1) * HEAD_DIM]

    @pl.when(j == 1)
    def _():
        by_head(_head_norm(_dot(u_ref[...], w_ref[...]), gk_ref[...]), k_ref, kb_ref)

    @pl.when(j == 2)
    def _():
        by_head(_dot(u_ref[...], w_ref[...]), v_ref, vb_ref)


def _in_qkv(x, g, w_qkv, w_small, b_small, gq, gk, *, q_dtype, q_scale, tm=512):
    t, d = x.shape
    tm = min(tm, t)
    tn = IN_QKV_TN
    assert tn == FOX_WIDTH and w_qkv.shape[1] >= 3 * tn
    const = lambda i, j: (0, 0)
    rows = lambda w: pl.BlockSpec((tm, w), lambda i, j: (i, 0))
    by_head = pl.BlockSpec((tm, FOX_HEADS, HEAD_DIM), lambda i, j: (i, 0, 0))
    by_head_shape = jax.ShapeDtypeStruct((t, FOX_HEADS, HEAD_DIM), F32)
    flat = lambda dt, w: jax.ShapeDtypeStruct((t, w), dt)
    return pl.pallas_call(
        functools.partial(_in_qkv_body, q_scale),
        grid=(t // tm, 3),
        in_specs=[
            rows(d),
            pl.BlockSpec((1, d), const),
            pl.BlockSpec((d, tn), lambda i, j: (0, j)),
            pl.BlockSpec((d, 2 * LANES), const),
            pl.BlockSpec((1, 2 * LANES), const),
            pl.BlockSpec((1, HEAD_DIM), const),
            pl.BlockSpec((1, HEAD_DIM), const),
        ],
        out_specs=[rows(FOX_WIDTH), by_head, rows(FOX_WIDTH), by_head, rows(FOX_WIDTH), rows(LANES), rows(LANES),
                   rows(d)],
        out_shape=[flat(q_dtype, FOX_WIDTH), by_head_shape, flat(BF16, FOX_WIDTH), by_head_shape,
                   flat(BF16, FOX_WIDTH), flat(F32, LANES), flat(F32, LANES), flat(BF16, d)],
        compiler_params=_params("parallel", "arbitrary"),
        name="in_qkv",
    )(x, g.reshape(1, d), w_qkv, w_small, b_small, gq.reshape(1, HEAD_DIM), gk.reshape(1, HEAD_DIM))


def _in_zx_body(u_ref, wa_ref, wb_ref, z_ref, xbc_ref):
    j = pl.program_id(1)
    half = wa_ref.shape[1]
    tn = 2 * half
    nz = z_ref.shape[1] // tn
    for c in range(nz + xbc_ref.shape[1] // tn):
        out_ref, sub = (z_ref, c) if c < nz else (xbc_ref, c - nz)

        @pl.when(j == c)
        def _(out_ref=out_ref, sub=sub):
            out_ref[:, sub * tn:sub * tn + half] = _dot(u_ref[...], wa_ref[...])
            out_ref[:, sub * tn + half:(sub + 1) * tn] = _dot(u_ref[...], wb_ref[...])


def _in_zx(u, w_zx, *, tm=1024):
    t, d = u.shape
    tm = min(tm, t)
    tn = IN_ZX_TN
    rows = lambda w: pl.BlockSpec((tm, w), lambda i, j: (i, 0))
    return pl.pallas_call(
        _in_zx_body,
        grid=(t // tm, w_zx.shape[1] // tn),
        in_specs=[rows(d), pl.BlockSpec((d, tn // 2), lambda i, j: (0, 2 * j)),
                  pl.BlockSpec((d, tn // 2), lambda i, j: (0, 2 * j + 1))],
        out_specs=[rows(SSD_WIDTH), rows(CONV_DIM)],
        out_shape=[jax.ShapeDtypeStruct((t, SSD_WIDTH), F32), jax.ShapeDtypeStruct((t, CONV_DIM), F32)],
        compiler_params=_params("parallel", "arbitrary"),
        name="in_zx",
    )(u, w_zx, w_zx)


def _in_proj(x, g, w_qkv, w_zx, w_small, b_small, gq, gk, *, q_dtype, q_scale):
    q, k, kb, v, vb, logf, dt, u = _in_qkv(x, g, w_qkv, w_small, b_small, gq, gk, q_dtype=q_dtype, q_scale=q_scale)
    z, xbc = _in_zx(u, w_zx)
    return q, k, kb, v, vb, z, xbc, logf, dt


def _pack_in_proj(w_in, fox_b_f, ssd_dt_bias):
    fw = FOX_WIDTH
    f0 = 3 * fw
    z0 = f0 + FOX_HEADS
    x0 = z0 + SSD_WIDTH
    d0 = x0 + CONV_DIM
    w_in = w_in.astype(BF16)
    w_qkv = w_in
    w_zx = w_in[:, z0:d0]
    zeros = lambda n: jnp.zeros((w_in.shape[0], n), BF16)
    w_small = jnp.concatenate(
        [w_in[:, f0:z0], zeros(LANES - FOX_HEADS), w_in[:, d0:], zeros(LANES - SSD_HEADS)], axis=1)
    b_small = jnp.concatenate(
        [fox_b_f, jnp.zeros((LANES - FOX_HEADS,), F32), ssd_dt_bias, jnp.zeros((LANES - SSD_HEADS,), F32)]
    ).reshape(1, 2 * LANES)
    return w_qkv, w_zx, w_small, b_small


ATT_BLOCK = 512
FOX_QUERY_BLOCK = 512
FOX_KEY_BLOCK = 256


def _tri(n, *, strict=False, upper=False):
    r = lax.broadcasted_iota(jnp.int32, (n, n), 0)
    c = lax.broadcasted_iota(jnp.int32, (n, n), 1)
    if upper:
        r, c = c, r
    return (c < r) if strict else (c <= r)


def _cumsum_body(x_ref, col_ref, row_ref, carry_ref):
    @pl.when(pl.program_id(1) == 0)
    def _():
        carry_ref[...] = jnp.zeros_like(carry_ref)

    n = x_ref.shape[1]
    tril = _tri(n).astype(BF16)
    c = _dot_sel_l(tril, x_ref[0]) + carry_ref[...]
    carry_ref[...] = c[n - 1:n, :]
    c = c * LOG2E
    col_ref[0] = c
    row_ref[0] = c.T[:FOX_HEADS, :]


def _cumsum(x, *, tb=ATT_BLOCK):
    b, l, _ = x.shape
    return pl.pallas_call(
        _cumsum_body,
        grid=(b, l // tb),
        in_specs=[pl.BlockSpec((1, tb, LANES), lambda i, j: (i, j, 0))],
        out_specs=[pl.BlockSpec((1, tb, LANES), lambda i, j: (i, j, 0)),
                   pl.BlockSpec((1, FOX_HEADS, tb), lambda i, j: (i, 0, j))],
        out_shape=[jax.ShapeDtypeStruct((b, l, LANES), F32), jax.ShapeDtypeStruct((b, FOX_HEADS, l), F32)],
        scratch_shapes=[pltpu.VMEM((1, LANES), F32)],
        compiler_params=_params("parallel", "arbitrary"),
        name="logf_cumsum",
    )(x)


def _fox_prompt_body(tk, q_ref, k_ref, v_ref, ccol_ref, crow_ref, o_ref, m_ref, l_ref, cq_ref, acc_ref):
    i = pl.program_id(1)
    tq = q_ref.shape[0]
    rep = tk // LANES
    m_ref[...] = jnp.full_like(m_ref, -jnp.inf)
    l_ref[...] = jnp.zeros_like(l_ref)
    acc_ref[...] = jnp.zeros_like(acc_ref)
    for h in range(FOX_HEADS):
        cq_ref[h] = jnp.broadcast_to(ccol_ref[0, :, h:h + 1], (tq, LANES))
    row = i * tq + lax.broadcasted_iota(jnp.int32, (tq, tk), 0)
    col = lax.broadcasted_iota(jnp.int32, (tq, tk), 1)
    wide = lambda x: jnp.concatenate([x] * rep, axis=1)

    def block(j, masked):
        ks = pl.ds(pl.multiple_of(j * tk, tk), tk)
        for h in range(FOX_HEADS):
            hs = slice(h * HEAD_DIM, (h + 1) * HEAD_DIM)
            s = _dot_nt(q_ref[:, hs], k_ref[ks, hs]) + (wide(cq_ref[h]) - crow_ref[0, h, pl.ds(j, 1), :])
            if masked:
                s = jnp.where(col + j * tk <= row, s, -jnp.inf)
            m_old = m_ref[h]
            m_new = jnp.maximum(m_old, jnp.max(s, axis=-1, keepdims=True))
            alpha = jnp.exp2(m_old - m_new)
            p = jnp.exp2(s - wide(m_new))
            m_ref[h] = m_new
            l_ref[h] = alpha * l_ref[h] + jnp.sum(p, axis=-1, keepdims=True)
            acc_ref[h] = alpha * acc_ref[h] + _dot(p.astype(BF16), v_ref[ks, hs])

    first_masked = (i * tq) // tk

    def step(j, carry):
        block(j, False)
        return carry

    lax.fori_loop(0, first_masked, step, 0)
    for extra in range(max(tq // tk, 1)):
        block(first_masked + extra, True)
    for h in range(FOX_HEADS):
        o_ref[:, h * HEAD_DIM:(h + 1) * HEAD_DIM] = (acc_ref[h] / l_ref[h]).astype(o_ref.dtype)


def _fox_prompt(q, k, v, ccol, crow, b, l, *, tq=128):
    tk = crow.shape[-1]
    tq = min(tq, l)
    assert tk % tq == 0 or tq % tk == 0
    nq = l // tq
    w = FOX_WIDTH
    return pl.pallas_call(
        functools.partial(_fox_prompt_body, tk),
        grid=(b, nq),
        in_specs=[
            pl.BlockSpec((tq, w), lambda bi, i: (bi * nq + i, 0)),
            pl.BlockSpec((l, w), lambda bi, i: (bi, 0)),
            pl.BlockSpec((l, w), lambda bi, i: (bi, 0)),
            pl.BlockSpec((1, tq, LANES), lambda bi, i: (bi, i, 0)),
            pl.BlockSpec((1, FOX_HEADS, l // tk, tk), lambda bi, i: (bi, 0, 0, 0)),
        ],
        out_specs=pl.BlockSpec((tq, w), lambda bi, i: (bi * nq + i, 0)),
        out_shape=jax.ShapeDtypeStruct((b * l, w), BF16),
        scratch_shapes=[pltpu.VMEM((FOX_HEADS, tq, LANES), F32)] * 3 + [pltpu.VMEM((FOX_HEADS, tq, HEAD_DIM), F32)],
        compiler_params=_params("parallel", "arbitrary"),
        name="fox_prompt",
    )(q, k, v, ccol, crow)


def _page_cumsum_body(x_ref, w_ref, m_ref):
    n = PAGE_SIZE * FOX_HEADS

    @pl.when(pl.program_id(0) == 0)
    def _():
        r = lax.broadcasted_iota(jnp.int32, (n, n), 0)
        c = lax.broadcasted_iota(jnp.int32, (n, n), 1)
        same_head = (r & (FOX_HEADS - 1)) == (c & (FOX_HEADS - 1))
        earlier = lax.shift_right_logical(r, 3) <= lax.shift_right_logical(c, 3)
        m_ref[...] = jnp.logical_and(same_head, earlier).astype(BF16)

    w_ref[...] = _dot_sel_r(x_ref[...], m_ref[...])


def _page_cumsum(logf_pages, *, tb=256):
    n_pool, n = logf_pages.shape
    return pl.pallas_call(
        _page_cumsum_body,
        grid=(n_pool // tb,),
        in_specs=[pl.BlockSpec((tb, n), lambda i: (i, 0))],
        out_specs=pl.BlockSpec((tb, n), lambda i: (i, 0)),
        out_shape=jax.ShapeDtypeStruct((n_pool, n), F32),
        scratch_shapes=[pltpu.VMEM((n, n), BF16)],
        compiler_params=_params("arbitrary"),
        name="page_cumsum",
    )(logf_pages)


def _seq_cumsum_body(seq_len, x_ref, o_ref):
    n = x_ref.shape[0]
    r = lax.broadcasted_iota(jnp.int32, (n, n), 0)
    c = lax.broadcasted_iota(jnp.int32, (n, n), 1)
    same_seq = (r // seq_len) == (c // seq_len)
    o_ref[...] = _dot_sel_l(jnp.logical_and(same_seq, c <= r).astype(BF16), x_ref[...])


def _seq_cumsum(x, seq_len, *, tb=128):
    t = x.shape[0]
    return pl.pallas_call(
        functools.partial(_seq_cumsum_body, seq_len),
        grid=(t // tb,),
        in_specs=[pl.BlockSpec((tb, LANES), lambda i: (i, 0))],
        out_specs=pl.BlockSpec((tb, LANES), lambda i: (i, 0)),
        out_shape=jax.ShapeDtypeStruct((t, LANES), F32),
        compiler_params=_params("parallel"),
        name="seq_cumsum",
    )(x)


def _fox_sample_body(pps, pt_ref, q_ref, kn_ref, vn_ref, cn_ref, *refs):
    kp_refs, vp_refs, w_refs = refs[:pps], refs[pps:2 * pps], refs[2 * pps:3 * pps]
    o_ref, q_scr, colq_ref, toff_ref, m_ref, l_ref, acc_ref = refs[3 * pps:]
    j = pl.program_id(1)
    nq = q_ref.shape[0]
    rows = nq * FOX_HEADS
    page_keys = PAGE_SIZE * FOX_HEADS
    row_id = lax.broadcasted_iota(jnp.int32, (rows, 1), 0)
    head_of_row = row_id & (FOX_HEADS - 1)
    query_of_row = lax.shift_right_logical(row_id, 3)

    @pl.when(j == 0)
    def _():
        q = q_ref[...].reshape(rows, HEAD_DIM).astype(BF16)
        q_scr[...] = q
        cn = cn_ref[0]
        key = lax.broadcasted_iota(jnp.int32, (1, rows), 1)
        colq = jnp.sum(jnp.where(key == row_id, cn, 0.0), axis=-1, keepdims=True)
        colq_ref[...] = colq
        toff_ref[...] = jnp.zeros_like(toff_ref)
        s = _dot_nt(q, kn_ref[...].reshape(rows, HEAD_DIM).astype(BF16)) + colq - cn
        valid = jnp.logical_and((key & (FOX_HEADS - 1)) == head_of_row,
                                lax.shift_right_logical(key, 3) <= query_of_row)
        s = jnp.where(valid, s, -jnp.inf)
        m = jnp.max(s, axis=-1, keepdims=True)
        p = jnp.exp(s - m)
        m_ref[...] = m
        l_ref[...] = jnp.sum(p, axis=-1, keepdims=True)
        acc_ref[...] = _dot(p.astype(BF16), vn_ref[...].reshape(rows, HEAD_DIM).astype(BF16))

    lane = lax.broadcasted_iota(jnp.int32, (1, LANES), 1)
    own_head = (lax.broadcasted_iota(jnp.int32, (1, page_keys), 1) & (FOX_HEADS - 1)) == head_of_row
    q = q_scr[...]
    colq = colq_ref[...]
    toff = toff_ref[...]
    tiles = []
    for kp_ref, w_ref in zip(kp_refs, w_refs):
        w = w_ref[0]
        last = jnp.where(lane == LANES - FOX_HEADS + head_of_row, w[:, page_keys - LANES:], 0.0)
        toff = toff + jnp.sum(last, axis=-1, keepdims=True)
        s = _dot_nt(q, kp_ref[0].reshape(page_keys, HEAD_DIM).astype(BF16))
        tiles.append(jnp.where(own_head, s + (colq + toff) - w, -jnp.inf))
    toff_ref[...] = toff
    m_old = m_ref[...]
    m = m_old
    for s in tiles:
        m = jnp.maximum(m, jnp.max(s, axis=-1, keepdims=True))
    alpha = jnp.exp(m_old - m)
    l = alpha * l_ref[...]
    acc = alpha * acc_ref[...]
    for s, vp_ref in zip(tiles, vp_refs):
        p = jnp.exp(s - m)
        l = l + jnp.sum(p, axis=-1, keepdims=True)
        acc = acc + _dot(p.astype(BF16), vp_ref[0].reshape(page_keys, HEAD_DIM).astype(BF16))
    m_ref[...] = m
    l_ref[...] = l
    acc_ref[...] = acc

    @pl.when(j == pl.num_programs(1) - 1)
    def _():
        o_ref[...] = (acc_ref[...] / l_ref[...]).reshape(nq, FOX_HEADS, HEAD_DIM)


def _fox_sample(page_table, q, k_new, v_new, c_new, k_pages, v_pages, w_pages, n_seq, nq, *, pages_per_step=16):
    n_pages = page_table.shape[1]
    rows = nq * FOX_HEADS
    pps = math.gcd(pages_per_step, n_pages)

    def page(i, ndim):
        return lambda b, j, pt: (pt[b * n_pages + (n_pages - 1 - j * pps - i)],) + (0,) * (ndim - 1)

    seq = pl.BlockSpec((nq, FOX_HEADS, HEAD_DIM), lambda b, j, pt: (b, 0, 0))
    kv_specs = [pl.BlockSpec((1, PAGE_SIZE, FOX_HEADS, HEAD_DIM), page(i, 4)) for i in range(pps)]
    w_specs = [pl.BlockSpec((1, 1, PAGE_SIZE * FOX_HEADS), page(i, 3)) for i in range(pps)]
    grid_spec = pltpu.PrefetchScalarGridSpec(
        num_scalar_prefetch=1,
        grid=(n_seq, n_pages // pps),
        in_specs=[seq, seq, seq, pl.BlockSpec((1, 1, rows), lambda b, j, pt: (b, 0, 0))]
        + kv_specs + kv_specs + w_specs,
        out_specs=seq,
        scratch_shapes=[
            pltpu.VMEM((rows, HEAD_DIM), BF16), pltpu.VMEM((rows, 1), F32), pltpu.VMEM((rows, 1), F32),
            pltpu.VMEM((rows, 1), F32), pltpu.VMEM((rows, 1), F32), pltpu.VMEM((rows, HEAD_DIM), F32),
        ],
    )
    return pl.pallas_call(
        functools.partial(_fox_sample_body, pps),
        grid_spec=grid_spec,
        out_shape=jax.ShapeDtypeStruct((n_seq * nq, FOX_HEADS, HEAD_DIM), F32),
        compiler_params=_params("parallel", "arbitrary"),
        name="fox_sample",
    )(page_table.reshape(-1), q, k_new, v_new, c_new, *([k_pages] * pps), *([v_pages] * pps), *([w_pages] * pps))


SSD_PAIRS = SSD_HEADS // 2
PAIRS_PER_GROUP = SSD_PAIRS // SSD_GROUPS


def _expander(width):
    n = SSD_HEADS * width
    h = lax.broadcasted_iota(jnp.int32, (LANES, n), 0)
    c = lax.broadcasted_iota(jnp.int32, (LANES, n), 1)
    return (lax.shift_right_logical(c, int(math.log2(width))) == h).astype(BF16)


def _ssd_local(xs, bm, cm, dt, alog, mask, tot_sel):
    n = xs.shape[0]
    lane = lax.broadcasted_iota(jnp.int32, (1, LANES), 1)
    dta = dt * jnp.where(lane < SSD_HEADS, -jnp.exp(alog), 0.0)
    e64 = _expander(SSD_HEAD_DIM)
    a_cum = _dot_sel_l(mask.astype(BF16), dta)
    a_cum_t = a_cum.T
    ac_exp = _dot_sel_r(a_cum, e64)
    if tot_sel is None:
        atot_exp = ac_exp[n - 1:n, :]
    else:
        atot_exp = _dot_sel_r(_dot_sel_l(tot_sel, dta), e64)
    ac_b = _dot_sel_r(a_cum, _expander(LANES))
    xdt = xs * _dot_sel_r(dt, e64)
    half = lax.broadcasted_iota(jnp.int32, (n, LANES), 1) < SSD_HEAD_DIM
    out = {
        "xdtw": xdt * jnp.exp(atot_exp - ac_exp),
        "eac": jnp.exp(ac_exp),
        "atot_exp": atot_exp,
        "ac_b": ac_b,
        "bg": [], "cg": [], "y_diag": [],
    }
    for g in range(SSD_GROUPS):
        gs = slice(g * SSD_STATE, (g + 1) * SSD_STATE)
        bg = bm[:, gs].astype(BF16)
        cg = cm[:, gs].astype(BF16)
        out["bg"].append(bg)
        out["cg"].append(cg)
        cb = _dot_nt(cg, bg)
        for k in range(g * PAIRS_PER_GROUP, (g + 1) * PAIRS_PER_GROUP):
            ps = slice(k * LANES, (k + 1) * LANES)
            ms = []
            for h in (2 * k, 2 * k + 1):
                seg = ac_b[:, h * LANES:(h + 1) * LANES] - a_cum_t[h:h + 1, :]
                ms.append(cb * jnp.exp(jnp.where(mask, seg, -jnp.inf)))
            m_cat = jnp.concatenate(ms, axis=1).astype(BF16)
            xp = xdt[:, ps]
            x_bd = jnp.concatenate([jnp.where(half, xp, 0.0), jnp.where(half, 0.0, xp)], axis=0).astype(BF16)
            out["y_diag"].append(_dot(m_cat, x_bd))
    return out


def _conv_ssd_sample_body(seq_len, xbc_ref, dt_ref, buf_ref, h0_ref, w_ref, b_ref, alog_ref, dexp_ref,
                          y_ref, hout_ref, cout_ref, xp_ref):
    n = xbc_ref.shape[0]
    n_seq = n // seq_len
    taps = CONV_WIDTH - 1
    base = 8 - taps

    x = xbc_ref[...]
    xp_ref[:, base:8, :] = buf_ref[...]
    xp_ref[:, 8:8 + seq_len, :] = x.reshape(n_seq, seq_len, CONV_DIM)
    acc = b_ref[...] + x * w_ref[taps:taps + 1, :]
    for j in range(taps):
        acc = acc + xp_ref[:, base + j:base + j + seq_len, :].reshape(n, CONV_DIM) * w_ref[j:j + 1, :]
    cout_ref[...] = xp_ref[:, 8 + seq_len - taps:8 + seq_len, :]
    conv = _silu(acc)
    xs = conv[:, :SSD_WIDTH]
    bm = conv[:, SSD_WIDTH:SSD_WIDTH + SSD_GROUPS * SSD_STATE]
    cm = conv[:, SSD_WIDTH + SSD_GROUPS * SSD_STATE:]

    r = lax.broadcasted_iota(jnp.int32, (n, n), 0)
    c = lax.broadcasted_iota(jnp.int32, (n, n), 1)
    same_seq = (r // seq_len) == (c // seq_len)
    loc = _ssd_local(xs, bm, cm, dt_ref[...], alog_ref[...], jnp.logical_and(same_seq, c <= r),
                     same_seq.astype(BF16))

    gw = PAIRS_PER_GROUP * LANES
    seq_of_col = lax.broadcasted_iota(jnp.int32, (1, n), 1) // seq_len
    decay_t = jnp.exp(loc["atot_exp"]).T
    for g in range(SSD_GROUPS):
        gr = slice(g * gw, (g + 1) * gw)
        h_prev = h0_ref[:, gr, :]
        z = _dot_nt(h_prev.reshape(n_seq * gw, SSD_STATE).astype(BF16), loc["cg"][g])
        y_off_t = jnp.zeros((gw, n), F32)
        for s in range(n_seq):
            y_off_t = y_off_t + jnp.where(seq_of_col == s, z[s * gw:(s + 1) * gw, :], 0.0)
        y_off = y_off_t.T * loc["eac"][:, gr]
        y = jnp.concatenate(loc["y_diag"][g * PAIRS_PER_GROUP:(g + 1) * PAIRS_PER_GROUP], axis=1)
        y_ref[:, gr] = y + y_off + xs[:, gr] * dexp_ref[:, gr]
        xw_t = loc["xdtw"][:, gr].T
        lhs = jnp.concatenate([jnp.where(seq_of_col == s, xw_t, 0.0) for s in range(n_seq)], axis=0)
        s_new = _dot(lhs.astype(BF16), loc["bg"][g])
        for s in range(n_seq):
            col = decay_t[gr, s * seq_len:s * seq_len + 1]
            hout_ref[s, gr, :] = h_prev[s] * col + s_new[s * gw:(s + 1) * gw, :]


def _conv_ssd_sample(xbc, dt, conv_buf, h0, conv_w, conv_b, alog, d_exp, n_seq, seq_len):
    assert seq_len == 8 and CONV_WIDTH - 1 <= seq_len
    tile = LANES
    ts = tile // seq_len
    const = lambda i: (0, 0)
    per_s = lambda i: (i, 0, 0)
    return pl.pallas_call(
        functools.partial(_conv_ssd_sample_body, seq_len),
        grid=(n_seq // ts,),
        in_specs=[
            pl.BlockSpec((tile, CONV_DIM), lambda i: (i, 0)),
            pl.BlockSpec((tile, LANES), lambda i: (i, 0)),
            pl.BlockSpec((ts, CONV_WIDTH - 1, CONV_DIM), per_s),
            pl.BlockSpec((ts, SSD_WIDTH, SSD_STATE), per_s),
            pl.BlockSpec((CONV_WIDTH, CONV_DIM), const),
            pl.BlockSpec((1, CONV_DIM), const),
            pl.BlockSpec((1, LANES), const),
            pl.BlockSpec((1, SSD_WIDTH), const),
        ],
        out_specs=[
            pl.BlockSpec((tile, SSD_WIDTH), lambda i: (i, 0)),
            pl.BlockSpec((ts, SSD_WIDTH, SSD_STATE), per_s),
            pl.BlockSpec((ts, CONV_WIDTH - 1, CONV_DIM), per_s),
        ],
        out_shape=[
            jax.ShapeDtypeStruct((n_seq * seq_len, SSD_WIDTH), F32),
            jax.ShapeDtypeStruct((n_seq, SSD_WIDTH, SSD_STATE), F32),
            jax.ShapeDtypeStruct((n_seq, CONV_WIDTH - 1, CONV_DIM), F32),
        ],
        scratch_shapes=[pltpu.VMEM((ts, 8 + seq_len, CONV_DIM), F32)],
        compiler_params=_params("parallel"),
        name="conv_ssd_sample",
    )(xbc, dt, conv_buf, h0, conv_w, conv_b, alog, d_exp)


def _conv_ssd_prompt_body(xbc_ref, dt_ref, buf_ref, h0_ref, w_ref, b_ref, alog_ref, dexp_ref,
                          y_ref, hout_ref, cout_ref, state_ref, xp_ref):
    c = pl.program_id(1)
    nc = pl.num_programs(1)
    tl = xbc_ref.shape[0]
    taps = CONV_WIDTH - 1
    base = 8 - taps

    @pl.when(c == 0)
    def _():
        state_ref[...] = h0_ref[0]
        xp_ref[base:8, :] = buf_ref[0]

    x = xbc_ref[...]
    xp_ref[8:8 + tl, :] = x
    acc = b_ref[...] + x * w_ref[taps:taps + 1, :]
    for j in range(taps):
        acc = acc + xp_ref[base + j:base + j + tl, :] * w_ref[j:j + 1, :]
    xp_ref[base:8, :] = x[tl - taps:, :]
    conv = _silu(acc)
    xs = conv[:, :SSD_WIDTH]
    bm = conv[:, SSD_WIDTH:SSD_WIDTH + SSD_GROUPS * SSD_STATE]
    cm = conv[:, SSD_WIDTH + SSD_GROUPS * SSD_STATE:]

    causal = _tri(tl)
    loc = _ssd_local(xs, bm, cm, dt_ref[...], alog_ref[...], causal, None)
    top = lax.broadcasted_iota(jnp.int32, (tl, LANES), 0) < SSD_HEAD_DIM
    for k in range(SSD_PAIRS):
        g = k // PAIRS_PER_GROUP
        ps = slice(k * LANES, (k + 1) * LANES)
        s_prev = state_ref[ps, :]
        y_off = _dot_nt(loc["cg"][g], s_prev.astype(BF16)) * loc["eac"][:, ps]
        y_ref[:, ps] = loc["y_diag"][k] + y_off + xs[:, ps] * dexp_ref[:, ps]
        cd = [jnp.exp(loc["ac_b"][tl - 1:tl, h * LANES:(h + 1) * LANES]) for h in (2 * k, 2 * k + 1)]
        state_ref[ps, :] = s_prev * jnp.where(top, cd[0], cd[1]) + _dot_tn(loc["xdtw"][:, ps].astype(BF16), loc["bg"][g])

    @pl.when(c == nc - 1)
    def _():
        hout_ref[0] = state_ref[...]
        cout_ref[0] = xp_ref[base:8, :]


def _conv_ssd_prompt(xbc, dt, conv_buf, h0, conv_w, conv_b, alog, d_exp, b, l):
    tl = SSD_CHUNK
    nc = l // tl
    const = lambda bi, c: (0, 0)
    per_b = lambda bi, c: (bi, 0, 0)
    return pl.pallas_call(
        _conv_ssd_prompt_body,
        grid=(b, nc),
        in_specs=[
            pl.BlockSpec((tl, CONV_DIM), lambda bi, c: (bi * nc + c, 0)),
            pl.BlockSpec((tl, LANES), lambda bi, c: (bi * nc + c, 0)),
            pl.BlockSpec((1, CONV_WIDTH - 1, CONV_DIM), per_b),
            pl.BlockSpec((1, SSD_WIDTH, SSD_STATE), per_b),
            pl.BlockSpec((CONV_WIDTH, CONV_DIM), const),
            pl.BlockSpec((1, CONV_DIM), const),
            pl.BlockSpec((1, LANES), const),
            pl.BlockSpec((1, SSD_WIDTH), const),
        ],
        out_specs=[
            pl.BlockSpec((tl, SSD_WIDTH), lambda bi, c: (bi * nc + c, 0)),
            pl.BlockSpec((1, SSD_WIDTH, SSD_STATE), per_b),
            pl.BlockSpec((1, CONV_WIDTH - 1, CONV_DIM), per_b),
        ],
        out_shape=[
            jax.ShapeDtypeStruct((b * l, SSD_WIDTH), F32),
            jax.ShapeDtypeStruct((b, SSD_WIDTH, SSD_STATE), F32),
            jax.ShapeDtypeStruct((b, CONV_WIDTH - 1, CONV_DIM), F32),
        ],
        scratch_shapes=[pltpu.VMEM((SSD_WIDTH, SSD_STATE), F32), pltpu.VMEM((8 + tl, CONV_DIM), F32)],
        compiler_params=_params("parallel", "arbitrary"),
        name="conv_ssd_prompt",
    )(xbc, dt, conv_buf, h0, conv_w, conv_b, alog, d_exp)


def _out_proj_body(h_ref, fox_ref, y_ref, z_ref, gs_ref, w_ref, gx_ref, wq_ref, gq_ref, o_ref, q_ref):
    yn = _rms(y_ref[...] * _silu(z_ref[...]), gs_ref[...]).astype(BF16)
    fw = fox_ref.shape[1]
    h = h_ref[...] + _dot(fox_ref[...].astype(BF16), w_ref[:fw, :]) + _dot(yn, w_ref[fw:, :])
    o_ref[...] = h
    q = _dot(_rms(h, gx_ref[...]).astype(BF16), wq_ref[...])
    q_ref[...] = _head_norm(q, gq_ref[...]) * (HEAD_DIM ** -0.5)


def _out_proj(h, fox, y, z, g_ssd, w_out, g_x, wq, gq, *, tm=512):
    t, d = h.shape
    tm = min(tm, t)
    row = lambda w: pl.BlockSpec((tm, w), lambda i: (i, 0))
    full = lambda a: pl.BlockSpec(a.shape, lambda i: (0, 0), pipeline_mode=pl.Buffered(1))
    consts = [g_ssd.reshape(1, SSD_WIDTH), w_out, g_x.reshape(1, d), wq, gq.reshape(1, HEAD_DIM)]
    return pl.pallas_call(
        _out_proj_body,
        grid=(t // tm,),
        in_specs=[row(d), row(FOX_WIDTH), row(SSD_WIDTH), row(SSD_WIDTH)] + [full(a) for a in consts],
        out_specs=[row(d), row(XATTN_WIDTH)],
        out_shape=[jax.ShapeDtypeStruct((t, d), F32), jax.ShapeDtypeStruct((t, XATTN_WIDTH), F32)],
        compiler_params=_params("parallel"),
        name="out_proj",
    )(h, fox, y, z, *consts)


def _mem_kv_body(m_ref, g_ref, w_ref, gk_ref, k_ref, v_ref):
    kv = _dot(_rms(m_ref[...], g_ref[...]).astype(BF16), w_ref[...])
    k_ref[...] = _head_norm(kv[:, :XATTN_WIDTH], gk_ref[...])
    v_ref[...] = kv[:, XATTN_WIDTH:]


def _mem_kv(mem, g, w_kv, gk, *, tm=256):
    t, d = mem.shape
    row = lambda w: pl.BlockSpec((tm, w), lambda i: (i, 0))
    full = lambda a: pl.BlockSpec(a.shape, lambda i: (0, 0))
    consts = [g.reshape(1, d), w_kv, gk.reshape(1, HEAD_DIM)]
    return pl.pallas_call(
        _mem_kv_body,
        grid=(t // tm,),
        in_specs=[row(d)] + [full(a) for a in consts],
        out_specs=[row(XATTN_WIDTH), row(XATTN_WIDTH)],
        out_shape=[jax.ShapeDtypeStruct((t, XATTN_WIDTH), F32)] * 2,
        compiler_params=_params("parallel"),
        name="mem_kv",
    )(mem, *consts)


def _xattn_body(q_ref, k_ref, v_ref, o_ref):
    for h in range(XATTN_HEADS):
        hs = slice(h * HEAD_DIM, (h + 1) * HEAD_DIM)
        s = _dot_nt(q_ref[:, hs].astype(BF16), k_ref[0, :, hs].astype(BF16))
        p = jnp.exp(s - jnp.max(s, axis=-1, keepdims=True))
        o = _dot(p.astype(BF16), v_ref[0, :, hs].astype(BF16))
        o_ref[:, hs] = o / jnp.sum(p, axis=-1, keepdims=True)


def _xattn(q, mem_k, mem_v, b, l, *, tq=512):
    tq = min(tq, l)
    nq = l // tq
    n_mem = mem_k.shape[1]
    qspec = pl.BlockSpec((tq, XATTN_WIDTH), lambda bi, i: (bi * nq + i, 0))
    mspec = pl.BlockSpec((1, n_mem, XATTN_WIDTH), lambda bi, i: (bi, 0, 0))
    return pl.pallas_call(
        _xattn_body,
        grid=(b, nq),
        in_specs=[qspec, mspec, mspec],
        out_specs=qspec,
        out_shape=jax.ShapeDtypeStruct((b * l, XATTN_WIDTH), F32),
        compiler_params=_params("parallel", "arbitrary"),
        name="xattn",
    )(q, mem_k, mem_v)


def _xattn_rows_body(n_seq, q_ref, k_ref, v_ref, o_ref):
    tiles = q_ref.shape[0] // n_seq
    rows = tiles * 8
    keys = k_ref.shape[1] * k_ref.shape[2]
    row_head = lax.broadcasted_iota(jnp.int32, (rows, 1), 0) & (XATTN_HEADS - 1)
    key_head = lax.broadcasted_iota(jnp.int32, (1, keys), 1) & (XATTN_HEADS - 1)
    own_head = row_head == key_head
    for s in range(n_seq):
        qs = slice(s * tiles, (s + 1) * tiles)
        q = q_ref[qs].reshape(rows, HEAD_DIM).astype(BF16)
        sc = jnp.where(own_head, _dot_nt(q, k_ref[s].reshape(keys, HEAD_DIM).astype(BF16)), -jnp.inf)
        p = jnp.exp(sc - jnp.max(sc, axis=-1, keepdims=True))
        o = _dot(p.astype(BF16), v_ref[s].reshape(keys, HEAD_DIM).astype(BF16))
        o_ref[qs] = (o / jnp.sum(p, axis=-1, keepdims=True)).reshape(tiles, 8, HEAD_DIM)


def _xattn_rows(q, mem_k, mem_v, n_seq, *, ts=16):
    tiles = q.shape[0] // n_seq
    ts = math.gcd(ts, n_seq)
    qspec = pl.BlockSpec((ts * tiles, 8, HEAD_DIM), lambda i: (i, 0, 0))
    mspec = pl.BlockSpec((ts,) + mem_k.shape[1:], lambda i: (i, 0, 0, 0))
    return pl.pallas_call(
        functools.partial(_xattn_rows_body, ts),
        grid=(n_seq // ts,),
        in_specs=[qspec, mspec, mspec],
        out_specs=qspec,
        out_shape=jax.ShapeDtypeStruct(q.shape, F32),
        compiler_params=_params("parallel"),
        name="xattn_rows",
    )(q, mem_k, mem_v)


def kernel(x_prompt, x_sample, cache_fox_k, cache_fox_v, cache_fox_logf, cache_mem_k, cache_mem_v, state_ssm, state_conv, page_table, mem_prompt, ffn1_norm, ffn1_w_gate, ffn1_w_up, ffn1_w_down, mix_norm, w_in, fox_b_f, fox_q_norm, fox_k_norm, conv_w, conv_b, ssd_dt_bias, ssd_A_log, ssd_D, ssd_out_norm, w_out, xattn_norm, mem_norm, xattn_w_q, xattn_w_kv, xattn_q_norm, xattn_k_norm, xattn_w_o, ffn2_norm, ffn2_w_gate, ffn2_w_up, ffn2_w_down):
    assert x_prompt.shape[2] == D_MODEL and ffn1_norm.shape[0] == 1
    d = D_MODEL
    bp, lp = x_prompt.shape[:2]
    bs, ls = x_sample.shape[:2]
    n_mem = mem_prompt.shape[1]
    n_pool = cache_fox_k.shape[1]

    bf = lambda w: w[0].astype(BF16)
    ffn1 = (ffn1_norm[0], ffn1_w_gate[0], ffn1_w_up[0], ffn1_w_down[0])
    ffn2 = (ffn2_norm[0], ffn2_w_gate[0], ffn2_w_up[0], ffn2_w_down[0])
    in_w = _pack_in_proj(w_in[0], fox_b_f[0], ssd_dt_bias[0])
    wq, wkv, wo = bf(xattn_w_q), bf(xattn_w_kv), bf(xattn_w_o)
    alog = jnp.pad(ssd_A_log[0], (0, LANES - SSD_HEADS)).reshape(1, LANES)
    d_exp = jnp.repeat(ssd_D[0], SSD_HEAD_DIM).reshape(1, SSD_WIDTH)
    cw, cb = conv_w[0], conv_b[0].reshape(1, CONV_DIM)

    def front(x, q_dtype, q_scale):
        h1 = _ffn(x, *ffn1)
        return h1, _in_proj(h1, mix_norm[0], *in_w, fox_q_norm[0], fox_k_norm[0], q_dtype=q_dtype, q_scale=q_scale)

    def back(h1, fox, y, z, attend):
        h2, xq = _out_proj(h1, fox, y, z, ssd_out_norm[0], bf(w_out), xattn_norm[0], wq, xattn_q_norm[0])
        return _ffn(h2, *ffn2, pre=(attend(xq), wo))

    h1, (q, k_p, kb, v_p, vb, z, xbc, logf_p, dt) = front(x_prompt.reshape(bp * lp, d), BF16, HEAD_DIM ** -0.5 * LOG2E)
    ccol, crow = _cumsum(logf_p.reshape(bp, lp, LANES))
    tk = min(FOX_KEY_BLOCK, lp)
    fox = _fox_prompt(q, kb, vb, ccol, crow.reshape(bp, FOX_HEADS, lp // tk, tk), bp, lp, tq=FOX_QUERY_BLOCK)
    y, ssm_p, conv_p = _conv_ssd_prompt(
        xbc, dt, jnp.zeros((bp, CONV_WIDTH - 1, CONV_DIM), F32), jnp.zeros((bp, SSD_WIDTH, SSD_STATE), F32),
        cw, cb, alog, d_exp, bp, lp)
    mk, mv = _mem_kv(mem_prompt.reshape(bp * n_mem, d), mem_norm[0], wkv, xattn_k_norm[0])
    y_prompt = back(h1, fox, y, z, lambda xq: _xattn(
        xq, mk.reshape(bp, n_mem, XATTN_WIDTH), mv.reshape(bp, n_mem, XATTN_WIDTH), bp, lp))

    h1, (q, k_s, _, v_s, _, z, xbc, logf_s, dt) = front(x_sample.reshape(bs * ls, d), F32, HEAD_DIM ** -0.5)
    w_pages = _page_cumsum(cache_fox_logf[0].reshape(n_pool, PAGE_SIZE * FOX_HEADS))
    c_new = _seq_cumsum(logf_s, ls)[:, :FOX_HEADS].reshape(bs, 1, ls * FOX_HEADS)
    fox = _fox_sample(
        page_table, q.reshape(bs * ls, FOX_HEADS, HEAD_DIM), k_s, v_s, c_new, cache_fox_k[0], cache_fox_v[0],
        w_pages.reshape(n_pool, 1, PAGE_SIZE * FOX_HEADS), bs, ls).reshape(bs * ls, FOX_WIDTH)
    y, ssm_s, conv_s = _conv_ssd_sample(
        xbc, dt, state_conv[0], state_ssm[0].reshape(bs, SSD_WIDTH, SSD_STATE), cw, cb, alog, d_exp, bs, ls)
    mem_tiles = lambda m: m[0].reshape(bs, n_mem * XATTN_HEADS // 8, 8, HEAD_DIM)
    y_sample = back(h1, fox, y, z, lambda xq: _xattn_rows(
        xq.reshape(bs * ls * XATTN_HEADS // 8, 8, HEAD_DIM), mem_tiles(cache_mem_k), mem_tiles(cache_mem_v), bs,
    ).reshape(bs * ls, XATTN_WIDTH))

    fox_shape = lambda b, l: (1, b, l, FOX_HEADS, HEAD_DIM)
    ssm_shape = lambda b: (1, b, SSD_HEADS, SSD_HEAD_DIM, SSD_STATE)
    mem_shape = (1, bp, n_mem, XATTN_HEADS, HEAD_DIM)
    return (
        y_prompt.reshape(bp, lp, d), y_sample.reshape(bs, ls, d),
        k_p.reshape(fox_shape(bp, lp)), v_p.reshape(fox_shape(bp, lp)),
        logf_p[:, :FOX_HEADS].reshape(1, bp, lp, FOX_HEADS),
        ssm_p.reshape(ssm_shape(bp)), conv_p[None], mk.reshape(mem_shape), mv.reshape(mem_shape),
        k_s.reshape(fox_shape(bs, ls)), v_s.reshape(fox_shape(bs, ls)),
        logf_s[:, :FOX_HEADS].reshape(1, bs, ls, FOX_HEADS),
        ssm_s.reshape(ssm_shape(bs)), conv_s[None],
    )
```
